```python
import jax, jax.numpy as jnp
from jax import lax
import numpy as np

D_MODEL = 1024
BATCH = 8
SEQ = 2048
DEPTH = 2
DEC_BATCH = 128
DEC_SEQ = 1
PAST_LEN = 16384
PAGE_SIZE = 128

D_MIX = D_MODEL
D_CONV = D_MIX // 2
D_LRU = D_MIX - D_CONV
N_LRU_HEADS = 8
LRU_HEAD_DIM = D_LRU // N_LRU_HEADS
SHORT_CONV_W = 3
LRU_CONV_W = 4
LRU_C = 8.0
D_IN = 3 * D_CONV + 2 * D_LRU
N_MEM = 256
N_MEM_HEADS = 4
MEM_HEAD_DIM = D_MODEL // N_MEM_HEADS
D_FF = 2816
N_EXPERTS = 8
TOP_K = 2
D_FF_EXPERT = 1408
N_DENSE = (DEPTH + 1) // 2
N_MOE = DEPTH // 2
EPS = 1e-6

kernel_name = 'hybrid_shortconv_rglru_xattn_moe_step'


def rms_norm(x, g):
    xf = x.astype(jnp.float32)
    y = xf * lax.rsqrt(jnp.mean(xf * xf, axis=-1, keepdims=True) + EPS)
    return (y * g.astype(jnp.float32)).astype(x.dtype)


def causal_dwconv(x, prev, w):
    width = w.shape[0]
    t_len = x.shape[1]
    xp = jnp.concatenate([prev.astype(x.dtype), x], axis=1)
    y = xp[:, 0:t_len] * w[0]
    for k in range(1, width):
        y = y + xp[:, k:k + t_len] * w[k]
    return y, xp[:, t_len:]


def rg_lru(x, h0, w_a, b_a, w_x, b_x, lam):
    bsz, t_len, _ = x.shape
    xh = x.reshape(bsz, t_len, N_LRU_HEADS, LRU_HEAD_DIM)
    ga = jnp.einsum('bthi,hij->bthj', xh, w_a).reshape(bsz, t_len, D_LRU) + b_a
    gx = jnp.einsum('bthi,hij->bthj', xh, w_x).reshape(bsz, t_len, D_LRU) + b_x
    r = jax.nn.sigmoid(ga.astype(jnp.float32))
    i = jax.nn.sigmoid(gx.astype(jnp.float32))
    log_a = -LRU_C * r * jax.nn.softplus(-lam.astype(jnp.float32))
    a = jnp.exp(log_a)
    u = jnp.sqrt(-jnp.expm1(2.0 * log_a)) * (i * x.astype(jnp.float32))

    def step(h, au):
        a_t, u_t = au
        h = a_t * h + u_t
        return h, h

    h_last, hs = lax.scan(step, h0.astype(jnp.float32), (jnp.swapaxes(a, 0, 1), jnp.swapaxes(u, 0, 1)))
    return jnp.swapaxes(hs, 0, 1).astype(x.dtype), h_last.astype(h0.dtype)


def mixer(h, conv_buf, lru_buf, lru_h, p, l):
    z = h @ p['w_in'][l]
    gb, gc, xa, xb, gate = jnp.split(z, [D_CONV, 2 * D_CONV, 3 * D_CONV, 3 * D_CONV + D_LRU], axis=-1)
    ya, new_conv_buf = causal_dwconv(gc * xa, conv_buf, p['w_short_conv'][l])
    ya = gb * ya
    xc, new_lru_buf = causal_dwconv(xb, lru_buf, p['w_lru_conv'][l])
    xc = xc + p['b_lru_conv'][l]
    yb, new_h = rg_lru(xc, lru_h, p['w_lru_a'][l], p['b_lru_a'][l], p['w_lru_x'][l], p['b_lru_x'][l], p['lru_lambda'][l])
    yb = yb * jax.nn.gelu(gate)
    y = jnp.concatenate([rms_norm(ya, p['norm_conv_out'][l]), rms_norm(yb, p['norm_lru_out'][l])], axis=-1)
    return y @ p['w_mix_out'][l], new_conv_buf, new_lru_buf, new_h


def memory_kv(mem, g_mem, w_k, w_v):
    bsz = mem.shape[0]
    m = rms_norm(mem, g_mem)
    k = (m @ w_k).reshape(bsz, N_MEM, N_MEM_HEADS, MEM_HEAD_DIM)
    v = (m @ w_v).reshape(bsz, N_MEM, N_MEM_HEADS, MEM_HEAD_DIM)
    return k, v


def cross_attn(h, k, v, w_q, w_o):
    bsz, t_len, _ = h.shape
    q = (h @ w_q).reshape(bsz, t_len, N_MEM_HEADS, MEM_HEAD_DIM)
    s = jnp.einsum('bthd,bmhd->bhtm', q, k.astype(q.dtype), preferred_element_type=jnp.float32) * (MEM_HEAD_DIM ** -0.5)
    pr = jax.nn.softmax(s, axis=-1).astype(h.dtype)
    o = jnp.einsum('bhtm,bmhd->bthd', pr, v.astype(h.dtype)).reshape(bsz, t_len, D_MODEL)
    return o @ w_o


def swiglu(h, w_gate, w_up, w_down):
    return (jax.nn.silu(h @ w_gate) * (h @ w_up)) @ w_down


def moe_swiglu(h, w_router, w_gate, w_up, w_down):
    logits = (h @ w_router).astype(jnp.float32)
    top_v, top_i = lax.top_k(logits, TOP_K)
    gates = jax.nn.softmax(top_v, axis=-1)
    combine = jnp.sum(jax.nn.one_hot(top_i, N_EXPERTS, dtype=jnp.float32) * gates[..., None], axis=-2)
    out = jnp.zeros_like(h)
    for e in range(N_EXPERTS):
        out = out + combine[..., e:e + 1].astype(h.dtype) * swiglu(h, w_gate[e], w_up[e], w_down[e])
    return out


def layer(l, x, k, v, conv_buf, lru_buf, lru_h, p):
    y, conv_buf, lru_buf, lru_h = mixer(rms_norm(x, p['norm_mix'][l]), conv_buf, lru_buf, lru_h, p, l)
    x = x + y
    x = x + cross_attn(rms_norm(x, p['norm_xattn'][l]), k, v, p['w_q'][l], p['w_o'][l])
    h = rms_norm(x, p['norm_ffn'][l])
    j = l // 2
    if l % 2 == 0:
        f = swiglu(h, p['w_ff_gate'][j], p['w_ff_up'][j], p['w_ff_down'][j])
    else:
        f = moe_swiglu(h, p['w_router'][j], p['w_moe_gate'][j], p['w_moe_up'][j], p['w_moe_down'][j])
    return x + f, conv_buf, lru_buf, lru_h


def setup_inputs(seed: int = 0) -> dict:
    key = jax.random.key(seed)
    ks = iter(jax.random.split(key, 64))
    f32 = jnp.float32

    def nrm(shape, scale):
        return scale * jax.random.normal(next(ks), shape, f32)

    def gain(shape):
        return 1.0 + 0.05 * jax.random.normal(next(ks), shape, f32)

    u = jax.random.uniform(next(ks), (DEPTH, D_LRU), f32, 0.9, 0.999)
    a_base = u ** (1.0 / LRU_C)
    lru_lambda = jnp.log(a_base) - jnp.log1p(-a_base)
    return {
        'x_prompt': nrm((BATCH, SEQ, D_MODEL), 1.0),
        'x_sample': nrm((DEC_BATCH, DEC_SEQ, D_MODEL), 1.0),
        'state_shortconv': nrm((DEPTH, DEC_BATCH, SHORT_CONV_W - 1, D_CONV), 1.0),
        'state_lru_conv': nrm((DEPTH, DEC_BATCH, LRU_CONV_W - 1, D_LRU), 1.0),
        'state_lru_h': nrm((DEPTH, DEC_BATCH, D_LRU), 0.5),
        'cache_mem_k': nrm((DEPTH, DEC_BATCH, N_MEM, N_MEM_HEADS, MEM_HEAD_DIM), 1.0),
        'cache_mem_v': nrm((DEPTH, DEC_BATCH, N_MEM, N_MEM_HEADS, MEM_HEAD_DIM), 1.0),
        'mem_prompt': nrm((BATCH, N_MEM, D_MODEL), 1.0),
        'norm_mix': gain((DEPTH, D_MODEL)),
        'w_in': nrm((DEPTH, D_MODEL, D_IN), D_MODEL ** -0.5),
        'w_short_conv': nrm((DEPTH, SHORT_CONV_W, D_CONV), SHORT_CONV_W ** -0.5),
        'w_lru_conv': nrm((DEPTH, LRU_CONV_W, D_LRU), LRU_CONV_W ** -0.5),
        'b_lru_conv': nrm((DEPTH, D_LRU), 0.02),
        'w_lru_a': nrm((DEPTH, N_LRU_HEADS, LRU_HEAD_DIM, LRU_HEAD_DIM), LRU_HEAD_DIM ** -0.5),
        'b_lru_a': nrm((DEPTH, D_LRU), 0.1),
        'w_lru_x': nrm((DEPTH, N_LRU_HEADS, LRU_HEAD_DIM, LRU_HEAD_DIM), LRU_HEAD_DIM ** -0.5),
        'b_lru_x': nrm((DEPTH, D_LRU), 0.1),
        'lru_lambda': lru_lambda,
        'norm_conv_out': gain((DEPTH, D_CONV)),
        'norm_lru_out': gain((DEPTH, D_LRU)),
        'w_mix_out': nrm((DEPTH, D_MIX, D_MODEL), D_MIX ** -0.5),
        'norm_xattn': gain((DEPTH, D_MODEL)),
        'norm_mem': gain((DEPTH, D_MODEL)),
        'w_q': nrm((DEPTH, D_MODEL, D_MODEL), D_MODEL ** -0.5),
        'w_k': nrm((DEPTH, D_MODEL, D_MODEL), D_MODEL ** -0.5),
        'w_v': nrm((DEPTH, D_MODEL, D_MODEL), D_MODEL ** -0.5),
        'w_o': nrm((DEPTH, D_MODEL, D_MODEL), D_MODEL ** -0.5),
        'norm_ffn': gain((DEPTH, D_MODEL)),
        'w_ff_gate': nrm((N_DENSE, D_MODEL, D_FF), D_MODEL ** -0.5),
        'w_ff_up': nrm((N_DENSE, D_MODEL, D_FF), D_MODEL ** -0.5),
        'w_ff_down': nrm((N_DENSE, D_FF, D_MODEL), D_FF ** -0.5),
        'w_router': nrm((N_MOE, D_MODEL, N_EXPERTS), D_MODEL ** -0.5),
        'w_moe_gate': nrm((N_MOE, N_EXPERTS, D_MODEL, D_FF_EXPERT), D_MODEL ** -0.5),
        'w_moe_up': nrm((N_MOE, N_EXPERTS, D_MODEL, D_FF_EXPERT), D_MODEL ** -0.5),
        'w_moe_down': nrm((N_MOE, N_EXPERTS, D_FF_EXPERT, D_MODEL), D_FF_EXPERT ** -0.5),
        'norm_final': gain((D_MODEL,)),
    }


def reference(x_prompt, x_sample, state_shortconv, state_lru_conv, state_lru_h, cache_mem_k, cache_mem_v,
              mem_prompt, norm_mix, w_in, w_short_conv, w_lru_conv, b_lru_conv, w_lru_a, b_lru_a,
              w_lru_x, b_lru_x, lru_lambda, norm_conv_out, norm_lru_out, w_mix_out, norm_xattn, norm_mem,
              w_q, w_k, w_v, w_o, norm_ffn, w_ff_gate, w_ff_up, w_ff_down, w_router, w_moe_gate,
              w_moe_up, w_moe_down, norm_final):
    p = dict(norm_mix=norm_mix, w_in=w_in, w_short_conv=w_short_conv, w_lru_conv=w_lru_conv,
             b_lru_conv=b_lru_conv, w_lru_a=w_lru_a, b_lru_a=b_lru_a, w_lru_x=w_lru_x, b_lru_x=b_lru_x,
             lru_lambda=lru_lambda, norm_conv_out=norm_conv_out, norm_lru_out=norm_lru_out,
             w_mix_out=w_mix_out, norm_xattn=norm_xattn, w_q=w_q, w_o=w_o, norm_ffn=norm_ffn,
             w_ff_gate=w_ff_gate, w_ff_up=w_ff_up, w_ff_down=w_ff_down, w_router=w_router,
             w_moe_gate=w_moe_gate, w_moe_up=w_moe_up, w_moe_down=w_moe_down)

    x = x_prompt
    bp = x.shape[0]
    pc, pl, ph, pk, pv = [], [], [], [], []
    for l in range(DEPTH):
        k, v = memory_kv(mem_prompt, norm_mem[l], w_k[l], w_v[l])
        x, cb, lb, hh = layer(l, x, k, v,
                              jnp.zeros((bp, SHORT_CONV_W - 1, D_CONV), x.dtype),
                              jnp.zeros((bp, LRU_CONV_W - 1, D_LRU), x.dtype),
                              jnp.zeros((bp, D_LRU), x.dtype), p)
        pc.append(cb); pl.append(lb); ph.append(hh); pk.append(k); pv.append(v)
    y_prompt = rms_norm(x, norm_final)

    x = x_sample
    sc, sl, sh = [], [], []
    for l in range(DEPTH):
        x, cb, lb, hh = layer(l, x, cache_mem_k[l], cache_mem_v[l],
                              state_shortconv[l], state_lru_conv[l], state_lru_h[l], p)
        sc.append(cb); sl.append(lb); sh.append(hh)
    y_sample = rms_norm(x, norm_final)

    return (y_prompt, y_sample, jnp.stack(pc), jnp.stack(pl), jnp.stack(ph), jnp.stack(pk), jnp.stack(pv),
            jnp.stack(sc), jnp.stack(sl), jnp.stack(sh))
```

```python
import functools

import jax
import jax.numpy as jnp
from jax import lax
from jax.experimental import pallas as pl
from jax.experimental.pallas import tpu as pltpu

F32 = jnp.float32
BF16 = jnp.bfloat16

EPS = 1e-6
LRU_C = 8.0
D_CONV = 512
D_LRU = 512
N_LRU_HEADS = 8
LRU_HEAD_DIM = 64
LRU_GROUP = 256
N_MEM_HEADS = 4
MEM_HEAD_DIM = 256
N_EXPERTS = 8
SUBLANES = 8
LANES = 128
VMEM_LIMIT_BYTES = 56 * 1024 * 1024


def _cparams(*sem):
    return pltpu.CompilerParams(dimension_semantics=sem, vmem_limit_bytes=VMEM_LIMIT_BYTES)


def _rms(x, g):
    return x * lax.rsqrt(jnp.mean(x * x, axis=-1, keepdims=True) + EPS) * g


def _dot(a, b):
    return jnp.dot(a, b, preferred_element_type=F32)


def _lru_gates(xc, wg_ref, ba, bx, lam):
    neg_lam = -lam
    softplus = jnp.maximum(neg_lam, 0.0) + jnp.log1p(jnp.exp(-jnp.abs(neg_lam)))
    a_parts, u_parts = [], []
    for g in range(D_LRU // LRU_GROUP):
        sl = slice(g * LRU_GROUP, (g + 1) * LRU_GROUP)
        xg = xc[:, sl]
        gates = _dot(xg.astype(BF16), wg_ref[g])
        r = jax.nn.sigmoid(gates[:, :LRU_GROUP] + ba[:, sl])
        i = jax.nn.sigmoid(gates[:, LRU_GROUP:] + bx[:, sl])
        log_a = (-LRU_C) * r * softplus[:, sl]
        th = jnp.tanh(log_a)
        one_minus_a2 = (-2.0 * th) / (1.0 - th)
        a_parts.append(jnp.exp(log_a))
        u_parts.append(jnp.sqrt(one_minus_a2) * (i * xg))
    return jnp.concatenate(a_parts, axis=-1), jnp.concatenate(u_parts, axis=-1)


def _mix_out(x, ya, yb, gco, glo, wout_ref):
    y = jnp.concatenate([_rms(ya, gco), _rms(yb, glo)], axis=-1).astype(BF16)
    return x + _dot(y, wout_ref[...])


def _conv_bulk(v, w_ref):
    width = w_ref.shape[0]
    acc = v * w_ref[width - 1:width, :]
    for s in range(1, width):
        acc = acc + pltpu.roll(v, s, axis=0) * w_ref[width - 1 - s:width - s, :]
    return acc


def _conv_head(carry, head, w_ref):
    width = w_ref.shape[0]
    ext = jnp.concatenate([carry, head], axis=0)
    acc = head * w_ref[width - 1:width, :]
    for s in range(1, width):
        acc = acc + pltpu.roll(ext, s, axis=0)[SUBLANES:] * w_ref[width - 1 - s:width - s, :]
    return acc


def _mixer_prompt_kernel(x_ref, gmix_ref, win_ref, wsc_ref, wlc_ref, blc_ref, wg_ref, ba_ref, bx_ref,
                         lam_ref, gco_ref, glo_ref, wout_ref,
                         xo_ref, tailp_ref, tailx_ref, hout_ref,
                         cp_ref, cx_ref, h_ref, ya_ref, xc_ref, a_ref, u_ref):
    nb, tt, d = x_ref.shape

    @pl.when(pl.program_id(0) == 0)
    def _():
        cp_ref[...] = jnp.zeros_like(cp_ref)
        cx_ref[...] = jnp.zeros_like(cx_ref)
        h_ref[...] = jnp.zeros_like(h_ref)

    x = x_ref[...].reshape(nb * tt, d)
    z = _dot(_rms(x, gmix_ref[...]).astype(BF16), win_ref[...])
    gb = z[:, 0:D_CONV]
    p = z[:, D_CONV:2 * D_CONV] * z[:, 2 * D_CONV:3 * D_CONV]
    xb = z[:, 3 * D_CONV:3 * D_CONV + D_LRU]
    gate = z[:, 3 * D_CONV + D_LRU:]

    ya_ref[...] = gb * _conv_bulk(p, wsc_ref)
    xc_ref[...] = _conv_bulk(xb, wlc_ref)
    for b in range(nb):
        head = slice(b * tt, b * tt + SUBLANES)
        tail = slice((b + 1) * tt - SUBLANES, (b + 1) * tt)
        ya_ref[head, :] = gb[head] * _conv_head(cp_ref[b], p[head], wsc_ref)
        xc_ref[head, :] = _conv_head(cx_ref[b], xb[head], wlc_ref)
        cp_ref[b] = p[tail]
        cx_ref[b] = xb[tail]
    tailp_ref[...] = cp_ref[...]
    tailx_ref[...] = cx_ref[...]

    xc = xc_ref[...] + blc_ref[...]
    a, u = _lru_gates(xc, wg_ref, ba_ref[...], bx_ref[...], lam_ref[...])
    n_chunks = a_ref.shape[0]
    for c in range(n_chunks):
        a_ref[c] = a[:, c * LANES:(c + 1) * LANES]
        u_ref[c] = u[:, c * LANES:(c + 1) * LANES]

    def step(t, hs):
        rows = pl.ds(t, nb, stride=tt)
        new = []
        for c in range(n_chunks):
            h = a_ref[c, rows, :] * hs[c] + u_ref[c, rows, :]
            u_ref[c, rows, :] = h
            new.append(h)
        return tuple(new)

    h0 = h_ref[...]
    h_last = lax.fori_loop(0, tt, step, tuple(h0[:, c * LANES:(c + 1) * LANES] for c in range(n_chunks)))
    h_last = jnp.concatenate(h_last, axis=-1)
    h_ref[...] = h_last
    hout_ref[...] = h_last

    yb = jnp.concatenate([u_ref[c] for c in range(n_chunks)], axis=-1) * jax.nn.gelu(gate)
    out = _mix_out(x, ya_ref[...], yb, gco_ref[...], glo_ref[...], wout_ref)
    xo_ref[...] = out.reshape(nb, tt, d)


def _const_spec(shape):
    return pl.BlockSpec(shape, lambda *_: (0,) * len(shape))


def _mixer_prompt(x, p, tt=64):
    nb, t, d = x.shape
    small = [p['gmix'], p['win'], p['wsc'], p['wlc'], p['blc'], p['wg'], p['ba'], p['bx'], p['lam'],
             p['gco'], p['glo'], p['wout']]
    rows = nb * tt
    return pl.pallas_call(
        _mixer_prompt_kernel,
        grid=(t // tt,),
        in_specs=[pl.BlockSpec((nb, tt, d), lambda i: (0, i, 0))] + [_const_spec(a.shape) for a in small],
        out_specs=[pl.BlockSpec((nb, tt, d), lambda i: (0, i, 0)),
                   _const_spec((nb, SUBLANES, D_CONV)), _const_spec((nb, SUBLANES, D_LRU)),
                   _const_spec((nb, D_LRU))],
        out_shape=[jax.ShapeDtypeStruct((nb, t, d), F32),
                   jax.ShapeDtypeStruct((nb, SUBLANES, D_CONV), F32),
                   jax.ShapeDtypeStruct((nb, SUBLANES, D_LRU), F32),
                   jax.ShapeDtypeStruct((nb, D_LRU), F32)],
        scratch_shapes=[pltpu.VMEM((nb, SUBLANES, D_CONV), F32), pltpu.VMEM((nb, SUBLANES, D_LRU), F32),
                        pltpu.VMEM((nb, D_LRU), F32),
                        pltpu.VMEM((rows, D_CONV), F32), pltpu.VMEM((rows, D_LRU), F32),
                        pltpu.VMEM((D_LRU // LANES, rows, LANES), F32),
                        pltpu.VMEM((D_LRU // LANES, rows, LANES), F32)],
        compiler_params=_cparams("arbitrary"),
        name="mixer_prompt",
    )(x, *small)


def _mixer_sample_kernel(x_ref, sc0_ref, sc1_ref, lc0_ref, lc1_ref, lc2_ref, h0_ref,
                         gmix_ref, win_ref, wsc_ref, wlc_ref, blc_ref, wg_ref, ba_ref, bx_ref,
                         lam_ref, gco_ref, glo_ref, wout_ref,
                         xo_ref, pnew_ref, xbnew_ref, hout_ref):
    x = x_ref[...]
    z = _dot(_rms(x, gmix_ref[...]).astype(BF16), win_ref[...])
    gb = z[:, 0:D_CONV]
    p = z[:, D_CONV:2 * D_CONV] * z[:, 2 * D_CONV:3 * D_CONV]
    xb = z[:, 3 * D_CONV:3 * D_CONV + D_LRU]
    gate = z[:, 3 * D_CONV + D_LRU:]

    ya = gb * (sc0_ref[...] * wsc_ref[0:1, :] + sc1_ref[...] * wsc_ref[1:2, :] + p * wsc_ref[2:3, :])
    xc = (lc0_ref[...] * wlc_ref[0:1, :] + lc1_ref[...] * wlc_ref[1:2, :] + lc2_ref[...] * wlc_ref[2:3, :]
          + xb * wlc_ref[3:4, :]) + blc_ref[...]
    a, u = _lru_gates(xc, wg_ref, ba_ref[...], bx_ref[...], lam_ref[...])
    h = a * h0_ref[...] + u
    yb = h * jax.nn.gelu(gate)
    xo_ref[...] = _mix_out(x, ya, yb, gco_ref[...], glo_ref[...], wout_ref)
    pnew_ref[...] = p
    xbnew_ref[...] = xb
    hout_ref[...] = h


def _mixer_sample(x, sc, lc, h0, p):
    nb, d = x.shape
    args = [x, sc[:, 0], sc[:, 1], lc[:, 0], lc[:, 1], lc[:, 2], h0,
            p['gmix'], p['win'], p['wsc'], p['wlc'], p['blc'], p['wg'], p['ba'], p['bx'], p['lam'],
            p['gco'], p['glo'], p['wout']]
    return pl.pallas_call(
        _mixer_sample_kernel,
        grid=(1,),
        in_specs=[_const_spec(a.shape) for a in args],
        out_specs=[_const_spec((nb, d)), _const_spec((nb, D_CONV)), _const_spec((nb, D_LRU)),
                   _const_spec((nb, D_LRU))],
        out_shape=[jax.ShapeDtypeStruct((nb, d), F32), jax.ShapeDtypeStruct((nb, D_CONV), F32),
                   jax.ShapeDtypeStruct((nb, D_LRU), F32), jax.ShapeDtypeStruct((nb, D_LRU), F32)],
        compiler_params=_cparams("arbitrary"),
        name="mixer_sample",
    )(*args)


def _norm_matmul_kernel(x_ref, g_ref, w_ref, o_ref):
    o_ref[...] = _dot(_rms(x_ref[...], g_ref[...]).astype(BF16), w_ref[...])


def _norm_matmul(x, g, w, tm):
    r, d = x.shape
    n = w.shape[1]
    return pl.pallas_call(
        _norm_matmul_kernel,
        grid=(r // tm,),
        in_specs=[pl.BlockSpec((tm, d), lambda i: (i, 0)), _const_spec(g.shape), _const_spec(w.shape)],
        out_specs=pl.BlockSpec((tm, n), lambda i: (i, 0)),
        out_shape=jax.ShapeDtypeStruct((r, n), F32),
        compiler_params=_cparams("parallel"),
        name="norm_matmul",
    )(x, g, w)


def _matmul_residual_kernel(a_ref, w_ref, x_ref, o_ref):
    o_ref[...] = x_ref[...] + _dot(a_ref[...].astype(BF16), w_ref[...])


def _matmul_residual(a, w, x, tm):
    r, d = x.shape
    return pl.pallas_call(
        _matmul_residual_kernel,
        grid=(r // tm,),
        in_specs=[pl.BlockSpec((tm, a.shape[1]), lambda i: (i, 0)), _const_spec(w.shape),
                  pl.BlockSpec((tm, d), lambda i: (i, 0))],
        out_specs=pl.BlockSpec((tm, d), lambda i: (i, 0)),
        out_shape=jax.ShapeDtypeStruct((r, d), F32),
        compiler_params=_cparams("parallel"),
        name="matmul_residual",
    )(a, w, x)


def _xattn_prompt_kernel(x_ref, g_ref, wq_ref, k_ref, v_ref, wo_ref, o_ref):
    x = x_ref[...]
    q = _dot(_rms(x, g_ref[...]).astype(BF16), wq_ref[...])
    heads = []
    for h in range(N_MEM_HEADS):
        sl = slice(h * MEM_HEAD_DIM, (h + 1) * MEM_HEAD_DIM)
        kh = k_ref[:, sl].astype(BF16)
        vh = v_ref[:, sl].astype(BF16)
        s = lax.dot_general(q[:, sl].astype(BF16), kh, (((1,), (1,)), ((), ())),
                            preferred_element_type=F32) * (MEM_HEAD_DIM ** -0.5)
        e = jnp.exp(s - jnp.max(s, axis=-1, keepdims=True))
        pr = e / jnp.sum(e, axis=-1, keepdims=True)
        heads.append(_dot(pr.astype(BF16), vh))
    o = jnp.concatenate(heads, axis=-1).astype(BF16)
    o_ref[...] = x + _dot(o, wo_ref[...])


def _xattn_prompt(x, g, wq, k, v, wo, tq=512):
    nb, t, d = x.shape
    n_mem = k.shape[1]
    return pl.pallas_call(
        _xattn_prompt_kernel,
        grid=(nb, t // tq),
        in_specs=[pl.BlockSpec((None, tq, d), lambda b, i: (b, i, 0)), _const_spec(g.shape),
                  _const_spec(wq.shape),
                  pl.BlockSpec((None, n_mem, d), lambda b, i: (b, 0, 0)),
                  pl.BlockSpec((None, n_mem, d), lambda b, i: (b, 0, 0)),
                  _const_spec(wo.shape)],
        out_specs=pl.BlockSpec((None, tq, d), lambda b, i: (b, i, 0)),
        out_shape=jax.ShapeDtypeStruct((nb, t, d), F32),
        compiler_params=_cparams("parallel", "parallel"),
        name="xattn_prompt",
    )(x, g, wq, k, v, wo)


def _xattn_sample_kernel(q_ref, k_ref, v_ref, o_ref):
    for b in range(q_ref.shape[0]):
        q = q_ref[b]
        prod = k_ref[b] * q
        outs = []
        for h in range(N_MEM_HEADS):
            sl = slice(h * MEM_HEAD_DIM, (h + 1) * MEM_HEAD_DIM)
            s = jnp.sum(prod[:, sl], axis=-1, keepdims=True) * (MEM_HEAD_DIM ** -0.5)
            e = jnp.exp(s - jnp.max(s, axis=0, keepdims=True))
            pr = e / jnp.sum(e, axis=0, keepdims=True)
            outs.append(jnp.sum(pr * v_ref[b, :, sl], axis=0, keepdims=True))
        o_ref[b] = jnp.concatenate(outs, axis=-1)


def _xattn_sample(q, k, v, bb=4):
    nb, d = q.shape
    n_mem = k.shape[1]
    q3 = q.reshape(nb, 1, d)
    out = pl.pallas_call(
        _xattn_sample_kernel,
        grid=(nb // bb,),
        in_specs=[pl.BlockSpec((bb, 1, d), lambda i: (i, 0, 0)),
                  pl.BlockSpec((bb, n_mem, d), lambda i: (i, 0, 0)),
                  pl.BlockSpec((bb, n_mem, d), lambda i: (i, 0, 0))],
        out_specs=pl.BlockSpec((bb, 1, d), lambda i: (i, 0, 0)),
        out_shape=jax.ShapeDtypeStruct((nb, 1, d), F32),
        compiler_params=_cparams("parallel"),
        name="xattn_sample",
    )(q3, k, v)
    return out.reshape(nb, d)


def _split_bf16(a):
    hi = a.astype(BF16)
    return hi, (a - hi.astype(F32)).astype(BF16)


def _top2(logits):
    n = logits.shape[-1]
    lane = lax.broadcasted_iota(jnp.int32, logits.shape, 1)
    m1 = jnp.max(logits, axis=-1, keepdims=True)
    i1 = jnp.min(jnp.where(logits == m1, lane, n), axis=-1, keepdims=True)
    rest = jnp.where(lane == i1, -jnp.inf, logits)
    m2 = jnp.max(rest, axis=-1, keepdims=True)
    i2 = jnp.min(jnp.where(rest == m2, lane, n), axis=-1, keepdims=True)
    e2 = jnp.exp(m2 - m1)
    denom = 1.0 + e2
    return i1, i2, 1.0 / denom, e2 / denom


def _ffn_kernel(x_ref, g_ref, *rest, routed, final):
    if routed:
        wr_ref, rest = rest[0], rest[1:]
    if final:
        gfin_ref, rest = rest[0], rest[1:]
    wg_ref, wu_ref, wd_ref, o_ref, h_ref, acc_ref = rest[:6]
    e = pl.program_id(1)

    @pl.when(e == 0)
    def _():
        x = x_ref[...]
        h = _rms(x, g_ref[...])
        h_ref[...] = h.astype(BF16)
        acc_ref[...] = x
        if routed:
            i1_ref, i2_ref, g1_ref, g2_ref = rest[6:]
            h_hi, h_lo = _split_bf16(h)
            w_hi, w_lo = _split_bf16(wr_ref[...])
            logits = _dot(h_hi, w_hi) + (_dot(h_hi, w_lo) + _dot(h_lo, w_hi))
            lane = lax.broadcasted_iota(jnp.int32, logits.shape, 1)
            logits = jnp.where(lane < N_EXPERTS, logits, -jnp.inf)
            i1_ref[...], i2_ref[...], g1_ref[...], g2_ref[...] = _top2(logits)

    hb = h_ref[...]
    gt = _dot(hb, wg_ref[...])
    up = _dot(hb, wu_ref[...])
    y = _dot((jax.nn.silu(gt) * up).astype(BF16), wd_ref[...])
    if routed:
        i1_ref, i2_ref, g1_ref, g2_ref = rest[6:]
        c = jnp.where(i1_ref[...] == e, g1_ref[...], 0.0) + jnp.where(i2_ref[...] == e, g2_ref[...], 0.0)
        y = c * y
    acc_ref[...] += y

    @pl.when(e == pl.num_programs(1) - 1)
    def _():
        out = acc_ref[...]
        o_ref[...] = _rms(out, gfin_ref[...]) if final else out


def _ffn(x, g, wg, wu, wd, tm, w_router=None, g_final=None):
    r, d = x.shape
    n_e, _, f = wg.shape
    routed, final = w_router is not None, g_final is not None
    args, specs = [x, g], [pl.BlockSpec((tm, d), lambda i, e: (i, 0)), _const_spec(g.shape)]
    if routed:
        args.append(w_router)
        specs.append(_const_spec(w_router.shape))
    if final:
        args.append(g_final)
        specs.append(_const_spec(g_final.shape))
    args += [wg, wu, wd]
    specs += [pl.BlockSpec((None, d, f), lambda i, e: (e, 0, 0)),
              pl.BlockSpec((None, d, f), lambda i, e: (e, 0, 0)),
              pl.BlockSpec((None, f, d), lambda i, e: (e, 0, 0))]
    scratch = [pltpu.VMEM((tm, d), BF16), pltpu.VMEM((tm, d), F32)]
    if routed:
        scratch += [pltpu.VMEM((tm, 1), jnp.int32), pltpu.VMEM((tm, 1), jnp.int32),
                    pltpu.VMEM((tm, 1), F32), pltpu.VMEM((tm, 1), F32)]
    return pl.pallas_call(
        functools.partial(_ffn_kernel, routed=routed, final=final),
        grid=(r // tm, n_e),
        in_specs=specs,
        out_specs=pl.BlockSpec((tm, d), lambda i, e: (i, 0)),
        out_shape=jax.ShapeDtypeStruct((r, d), F32),
        scratch_shapes=scratch,
        compiler_params=_cparams("parallel", "arbitrary"),
        name="ffn_moe" if routed else "ffn_dense",
    )(*args)


def _block_diag_gates(w_a, w_x):
    per_group = LRU_GROUP // LRU_HEAD_DIM
    eye = jnp.eye(per_group, dtype=w_a.dtype)

    def bd(w):
        w = w.reshape(D_LRU // LRU_GROUP, per_group, LRU_HEAD_DIM, LRU_HEAD_DIM)
        return jnp.einsum('ghij,hk->ghikj', w, eye).reshape(D_LRU // LRU_GROUP, LRU_GROUP, LRU_GROUP)

    return jnp.concatenate([bd(w_a), bd(w_x)], axis=-1).astype(BF16)


def _split_ff(w_gate, w_up, w_down, n_split=2):
    d, f = w_gate.shape
    fs = f // n_split
    wg = w_gate.reshape(d, n_split, fs).transpose(1, 0, 2).astype(BF16)
    wu = w_up.reshape(d, n_split, fs).transpose(1, 0, 2).astype(BF16)
    wd = w_down.reshape(n_split, fs, d).astype(BF16)
    return wg, wu, wd


def kernel(x_prompt, x_sample, state_shortconv, state_lru_conv, state_lru_h, cache_mem_k, cache_mem_v,
           mem_prompt, norm_mix, w_in, w_short_conv, w_lru_conv, b_lru_conv, w_lru_a, b_lru_a, w_lru_x,
           b_lru_x, lru_lambda, norm_conv_out, norm_lru_out, w_mix_out, norm_xattn, norm_mem, w_q, w_k,
           w_v, w_o, norm_ffn, w_ff_gate, w_ff_up, w_ff_down, w_router, w_moe_gate, w_moe_up, w_moe_down,
           norm_final):
    depth = w_in.shape[0]
    nb, t, d = x_prompt.shape
    ns = x_sample.shape[0]
    n_mem = mem_prompt.shape[1]
    row = lambda a: a.reshape(1, -1)

    layers = []
    for l in range(depth):
        lp = dict(
            gmix=row(norm_mix[l]), win=w_in[l].astype(BF16), wsc=w_short_conv[l], wlc=w_lru_conv[l],
            blc=row(b_lru_conv[l]), wg=_block_diag_gates(w_lru_a[l], w_lru_x[l]), ba=row(b_lru_a[l]),
            bx=row(b_lru_x[l]), lam=row(lru_lambda[l]), gco=row(norm_conv_out[l]), glo=row(norm_lru_out[l]),
            wout=w_mix_out[l].astype(BF16), gx=row(norm_xattn[l]), gmem=row(norm_mem[l]),
            wq=w_q[l].astype(BF16), wk=w_k[l].astype(BF16), wv=w_v[l].astype(BF16), wo=w_o[l].astype(BF16),
            gffn=row(norm_ffn[l]), gfin=row(norm_final) if l == depth - 1 else None)
        j = l // 2
        if l % 2 == 0:
            lp['ffw'] = _split_ff(w_ff_gate[j], w_ff_up[j], w_ff_down[j])
            lp['wr'] = None
        else:
            lp['ffw'] = (w_moe_gate[j].astype(BF16), w_moe_up[j].astype(BF16), w_moe_down[j].astype(BF16))
            lp['wr'] = jnp.pad(w_router[j], ((0, 0), (0, LANES - N_EXPERTS)))
        layers.append(lp)

    x = x_prompt
    mem = mem_prompt.reshape(nb * n_mem, d)
    pc, plc, ph, pk, pv = [], [], [], [], []
    for lp in layers:
        k = _norm_matmul(mem, lp['gmem'], lp['wk'], 512).reshape(nb, n_mem, d)
        v = _norm_matmul(mem, lp['gmem'], lp['wv'], 512).reshape(nb, n_mem, d)
        x, tail_p, tail_x, h_last = _mixer_prompt(x, lp)
        x = _xattn_prompt(x, lp['gx'], lp['wq'], k, v, lp['wo'])
        x = _ffn(x.reshape(nb * t, d), lp['gffn'], *lp['ffw'], tm=512, w_router=lp['wr'],
                 g_final=lp['gfin']).reshape(nb, t, d)
        pc.append(tail_p[:, SUBLANES - 2:])
        plc.append(tail_x[:, SUBLANES - 3:])
        ph.append(h_last)
        pk.append(k.reshape(nb, n_mem, N_MEM_HEADS, MEM_HEAD_DIM))
        pv.append(v.reshape(nb, n_mem, N_MEM_HEADS, MEM_HEAD_DIM))
    y_prompt = x

    x = x_sample.reshape(ns, d)
    sc, slc, sh = [], [], []
    for l, lp in enumerate(layers):
        x, p_new, xb_new, h_new = _mixer_sample(x, state_shortconv[l], state_lru_conv[l], state_lru_h[l], lp)
        q = _norm_matmul(x, lp['gx'], lp['wq'], ns)
        o = _xattn_sample(q, cache_mem_k[l].reshape(ns, n_mem, d), cache_mem_v[l].reshape(ns, n_mem, d))
        x = _matmul_residual(o, lp['wo'], x, ns)
        x = _ffn(x, lp['gffn'], *lp['ffw'], tm=ns, w_router=lp['wr'], g_final=lp['gfin'])
        sc.append(jnp.stack([state_shortconv[l][:, 1], p_new], axis=1))
        slc.append(jnp.stack([state_lru_conv[l][:, 1], state_lru_conv[l][:, 2], xb_new], axis=1))
        sh.append(h_new)
    y_sample = x.reshape(ns, 1, d)

    return (y_prompt, y_sample, jnp.stack(pc), jnp.stack(plc), jnp.stack(ph), jnp.stack(pk), jnp.stack(pv),
            jnp.stack(sc), jnp.stack(slc), jnp.stack(sh))
```

```python
import functools

import jax
import jax.numpy as jnp
from jax import lax
from jax.experimental import pallas as pl
from jax.experimental.pallas import tpu as pltpu

F32 = jnp.float32
BF16 = jnp.bfloat16

EPS = 1e-6
LRU_C = 8.0
D_CONV = 512
D_LRU = 512
N_LRU_HEADS = 8
LRU_HEAD_DIM = 64
LRU_GROUP = 256
N_MEM_HEADS = 4
MEM_HEAD_DIM = 256
N_EXPERTS = 8
SUBLANES = 8
LANES = 128
VMEM_LIMIT_BYTES = 56 * 1024 * 1024


def _cparams(*sem):
    return pltpu.CompilerParams(dimension_semantics=sem, vmem_limit_bytes=VMEM_LIMIT_BYTES)


def _rms(x, g):
    return x * lax.rsqrt(jnp.mean(x * x, axis=-1, keepdims=True) + EPS) * g


def _dot(a, b):
    return jnp.dot(a, b, preferred_element_type=F32)


def _lru_gates(xc, wg_ref, ba, bx, lam):
    neg_lam = -lam
    softplus = jnp.maximum(neg_lam, 0.0) + jnp.log1p(jnp.exp(-jnp.abs(neg_lam)))
    a_parts, u_parts = [], []
    for g in range(D_LRU // LRU_GROUP):
        sl = slice(g * LRU_GROUP, (g + 1) * LRU_GROUP)
        xg = xc[:, sl]
        gates = _dot(xg.astype(BF16), wg_ref[g])
        r = jax.nn.sigmoid(gates[:, :LRU_GROUP] + ba[:, sl])
        i = jax.nn.sigmoid(gates[:, LRU_GROUP:] + bx[:, sl])
        log_a = (-LRU_C) * r * softplus[:, sl]
        th = jnp.tanh(log_a)
        one_minus_a2 = (-2.0 * th) / (1.0 - th)
        a_parts.append(jnp.exp(log_a))
        u_parts.append(jnp.sqrt(one_minus_a2) * (i * xg))
    return jnp.concatenate(a_parts, axis=-1), jnp.concatenate(u_parts, axis=-1)


def _mix_out(x, ya, yb, gco, glo, wout_ref):
    y = jnp.concatenate([_rms(ya, gco), _rms(yb, glo)], axis=-1).astype(BF16)
    return x + _dot(y, wout_ref[...])


def _conv_bulk(v, w_ref):
    width = w_ref.shape[0]
    acc = v * w_ref[width - 1:width, :]
    for s in range(1, width):
        acc = acc + pltpu.roll(v, s, axis=0) * w_ref[width - 1 - s:width - s, :]
    return acc


def _conv_head(carry, head, w_ref):
    width = w_ref.shape[0]
    ext = jnp.concatenate([carry, head], axis=0)
    acc = head * w_ref[width - 1:width, :]
    for s in range(1, width):
        acc = acc + pltpu.roll(ext, s, axis=0)[SUBLANES:] * w_ref[width - 1 - s:width - s, :]
    return acc


def _mixer_prompt_kernel(x_ref, gmix_ref, win_ref, wsc_ref, wlc_ref, blc_ref, wg_ref, ba_ref, bx_ref,
                         lam_ref, gco_ref, glo_ref, wout_ref,
                         xo_ref, tailp_ref, tailx_ref, hout_ref,
                         cp_ref, cx_ref, h_ref, ya_ref, xc_ref, a_ref, u_ref):
    nb, tt, d = x_ref.shape

    @pl.when(pl.program_id(0) == 0)
    def _():
        cp_ref[...] = jnp.zeros_like(cp_ref)
        cx_ref[...] = jnp.zeros_like(cx_ref)
        h_ref[...] = jnp.zeros_like(h_ref)

    x = x_ref[...].reshape(nb * tt, d)
    z = _dot(_rms(x, gmix_ref[...]).astype(BF16), win_ref[...])
    gb = z[:, 0:D_CONV]
    p = z[:, D_CONV:2 * D_CONV] * z[:, 2 * D_CONV:3 * D_CONV]
    xb = z[:, 3 * D_CONV:3 * D_CONV + D_LRU]
    gate = z[:, 3 * D_CONV + D_LRU:]

    ya_ref[...] = gb * _conv_bulk(p, wsc_ref)
    xc_ref[...] = _conv_bulk(xb, wlc_ref)
    for b in range(nb):
        head = slice(b * tt, b * tt + SUBLANES)
        tail = slice((b + 1) * tt - SUBLANES, (b + 1) * tt)
        ya_ref[head, :] = gb[head] * _conv_head(cp_ref[b], p[head], wsc_ref)
        xc_ref[head, :] = _conv_head(cx_ref[b], xb[head], wlc_ref)
        cp_ref[b] = p[tail]
        cx_ref[b] = xb[tail]
    tailp_ref[...] = cp_ref[...]
    tailx_ref[...] = cx_ref[...]

    xc = xc_ref[...] + blc_ref[...]
    a, u = _lru_gates(xc, wg_ref, ba_ref[...], bx_ref[...], lam_ref[...])
    n_chunks = a_ref.shape[0]
    for c in range(n_chunks):
        a_ref[c] = a[:, c * LANES:(c + 1) * LANES]
        u_ref[c] = u[:, c * LANES:(c + 1) * LANES]

    def step(t, hs):
        rows = pl.ds(t, nb, stride=tt)
        new = []
        for c in range(n_chunks):
            h = a_ref[c, rows, :] * hs[c] + u_ref[c, rows, :]
            u_ref[c, rows, :] = h
            new.append(h)
        return tuple(new)

    h0 = h_ref[...]
    h_last = lax.fori_loop(0, tt, step, tuple(h0[:, c * LANES:(c + 1) * LANES] for c in range(n_chunks)))
    h_last = jnp.concatenate(h_last, axis=-1)
    h_ref[...] = h_last
    hout_ref[...] = h_last

    yb = jnp.concatenate([u_ref[c] for c in range(n_chunks)], axis=-1) * jax.nn.gelu(gate)
    out = _mix_out(x, ya_ref[...], yb, gco_ref[...], glo_ref[...], wout_ref)
    xo_ref[...] = out.reshape(nb, tt, d)


def _const_spec(shape):
    return pl.BlockSpec(shape, lambda *_: (0,) * len(shape))


def _mixer_prompt(x, p, tt=64):
    nb, t, d = x.shape
    small = [p['gmix'], p['win'], p['wsc'], p['wlc'], p['blc'], p['wg'], p['ba'], p['bx'], p['lam'],
             p['gco'], p['glo'], p['wout']]
    rows = nb * tt
    return pl.pallas_call(
        _mixer_prompt_kernel,
        grid=(t // tt,),
        in_specs=[pl.BlockSpec((nb, tt, d), lambda i: (0, i, 0))] + [_const_spec(a.shape) for a in small],
        out_specs=[pl.BlockSpec((nb, tt, d), lambda i: (0, i, 0)),
                   _const_spec((nb, SUBLANES, D_CONV)), _const_spec((nb, SUBLANES, D_LRU)),
                   _const_spec((nb, D_LRU))],
        out_shape=[jax.ShapeDtypeStruct((nb, t, d), F32),
                   jax.ShapeDtypeStruct((nb, SUBLANES, D_CONV), F32),
                   jax.ShapeDtypeStruct((nb, SUBLANES, D_LRU), F32),
                   jax.ShapeDtypeStruct((nb, D_LRU), F32)],
        scratch_shapes=[pltpu.VMEM((nb, SUBLANES, D_CONV), F32), pltpu.VMEM((nb, SUBLANES, D_LRU), F32),
                        pltpu.VMEM((nb, D_LRU), F32),
                        pltpu.VMEM((rows, D_CONV), F32), pltpu.VMEM((rows, D_LRU), F32),
                        pltpu.VMEM((D_LRU // LANES, rows, LANES), F32),
                        pltpu.VMEM((D_LRU // LANES, rows, LANES), F32)],
        compiler_params=_cparams("arbitrary"),
        name="mixer_prompt",
    )(x, *small)


def _mixer_sample_kernel(x_ref, sc0_ref, sc1_ref, lc0_ref, lc1_ref, lc2_ref, h0_ref,
                         gmix_ref, win_ref, wsc_ref, wlc_ref, blc_ref, wg_ref, ba_ref, bx_ref,
                         lam_ref, gco_ref, glo_ref, wout_ref,
                         xo_ref, pnew_ref, xbnew_ref, hout_ref):
    x = x_ref[...]
    z = _dot(_rms(x, gmix_ref[...]).astype(BF16), win_ref[...])
    gb = z[:, 0:D_CONV]
    p = z[:, D_CONV:2 * D_CONV] * z[:, 2 * D_CONV:3 * D_CONV]
    xb = z[:, 3 * D_CONV:3 * D_CONV + D_LRU]
    gate = z[:, 3 * D_CONV + D_LRU:]

    ya = gb * (sc0_ref[...] * wsc_ref[0:1, :] + sc1_ref[...] * wsc_ref[1:2, :] + p * wsc_ref[2:3, :])
    xc = (lc0_ref[...] * wlc_ref[0:1, :] + lc1_ref[...] * wlc_ref[1:2, :] + lc2_ref[...] * wlc_ref[2:3, :]
          + xb * wlc_ref[3:4, :]) + blc_ref[...]
    a, u = _lru_gates(xc, wg_ref, ba_ref[...], bx_ref[...], lam_ref[...])
    h = a * h0_ref[...] + u
    yb = h * jax.nn.gelu(gate)
    xo_ref[...] = _mix_out(x, ya, yb, gco_ref[...], glo_ref[...], wout_ref)
    pnew_ref[...] = p
    xbnew_ref[...] = xb
    hout_ref[...] = h


def _mixer_sample(x, sc, lc, h0, p):
    nb, d = x.shape
    args = [x, sc[:, 0], sc[:, 1], lc[:, 0], lc[:, 1], lc[:, 2], h0,
            p['gmix'], p['win'], p['wsc'], p['wlc'], p['blc'], p['wg'], p['ba'], p['bx'], p['lam'],
            p['gco'], p['glo'], p['wout']]
    return pl.pallas_call(
        _mixer_sample_kernel,
        grid=(1,),
        in_specs=[_const_spec(a.shape) for a in args],
        out_specs=[_const_spec((nb, d)), _const_spec((nb, D_CONV)), _const_spec((nb, D_LRU)),
                   _const_spec((nb, D_LRU))],
        out_shape=[jax.ShapeDtypeStruct((nb, d), F32), jax.ShapeDtypeStruct((nb, D_CONV), F32),
                   jax.ShapeDtypeStruct((nb, D_LRU), F32), jax.ShapeDtypeStruct((nb, D_LRU), F32)],
        compiler_params=_cparams("arbitrary"),
        name="mixer_sample",
    )(*args)


def _norm_matmul_kernel(x_ref, g_ref, w_ref, o_ref):
    o_ref[...] = _dot(_rms(x_ref[...], g_ref[...]).astype(BF16), w_ref[...])


def _norm_matmul(x, g, w, tm):
    r, d = x.shape
    n = w.shape[1]
    return pl.pallas_call(
        _norm_matmul_kernel,
        grid=(r // tm,),
        in_specs=[pl.BlockSpec((tm, d), lambda i: (i, 0)), _const_spec(g.shape), _const_spec(w.shape)],
        out_specs=pl.BlockSpec((tm, n), lambda i: (i, 0)),
        out_shape=jax.ShapeDtypeStruct((r, n), F32),
        compiler_params=_cparams("parallel"),
        name="norm_matmul",
    )(x, g, w)


def _matmul_residual_kernel(a_ref, w_ref, x_ref, o_ref):
    o_ref[...] = x_ref[...] + _dot(a_ref[...].astype(BF16), w_ref[...])


def _matmul_residual(a, w, x, tm):
    r, d = x.shape
    return pl.pallas_call(
        _matmul_residual_kernel,
        grid=(r // tm,),
        in_specs=[pl.BlockSpec((tm, a.shape[1]), lambda i: (i, 0)), _const_spec(w.shape),
                  pl.BlockSpec((tm, d), lambda i: (i, 0))],
        out_specs=pl.BlockSpec((tm, d), lambda i: (i, 0)),
        out_shape=jax.ShapeDtypeStruct((r, d), F32),
        compiler_params=_cparams("parallel"),
        name="matmul_residual",
    )(a, w, x)


def _xattn_prompt_kernel(x_ref, g_ref, wq_ref, k_ref, v_ref, wo_ref, o_ref):
    x = x_ref[...]
    q = _dot(_rms(x, g_ref[...]).astype(BF16), wq_ref[...])
    heads = []
    for h in range(N_MEM_HEADS):
        sl = slice(h * MEM_HEAD_DIM, (h + 1) * MEM_HEAD_DIM)
        kh = k_ref[:, sl]
        vh = v_ref[:, sl]
        s = lax.dot_general(q[:, sl].astype(BF16), kh, (((1,), (1,)), ((), ())),
                            preferred_element_type=F32) * (MEM_HEAD_DIM ** -0.5)
        e = jnp.exp(s - jnp.max(s, axis=-1, keepdims=True))
        pr = e / jnp.sum(e, axis=-1, keepdims=True)
        heads.append(_dot(pr.astype(BF16), vh))
    o = jnp.concatenate(heads, axis=-1).astype(BF16)
    o_ref[...] = x + _dot(o, wo_ref[...])


def _xattn_prompt(x, g, wq, k, v, wo, tq=512):
    nb, t, d = x.shape
    n_mem = k.shape[1]
    return pl.pallas_call(
        _xattn_prompt_kernel,
        grid=(nb, t // tq),
        in_specs=[pl.BlockSpec((None, tq, d), lambda b, i: (b, i, 0)), _const_spec(g.shape),
                  _const_spec(wq.shape),
                  pl.BlockSpec((None, n_mem, d), lambda b, i: (b, 0, 0)),
                  pl.BlockSpec((None, n_mem, d), lambda b, i: (b, 0, 0)),
                  _const_spec(wo.shape)],
        out_specs=pl.BlockSpec((None, tq, d), lambda b, i: (b, i, 0)),
        out_shape=jax.ShapeDtypeStruct((nb, t, d), F32),
        compiler_params=_cparams("parallel", "parallel"),
        name="xattn_prompt",
    )(x, g, wq, k, v, wo)


def _kv_proj_kernel(mem_ref, g_ref, wk_ref, wv_ref, k5_ref, v5_ref, kb_ref, vb_ref):
    bb, n_mem, d = mem_ref.shape
    m = _rms(mem_ref[...].reshape(bb * n_mem, d), g_ref[...]).astype(BF16)
    for w_ref, o5_ref, ob_ref in ((wk_ref, k5_ref, kb_ref), (wv_ref, v5_ref, vb_ref)):
        y = _dot(m, w_ref[...]).reshape(bb, n_mem, d)
        ob_ref[...] = y.astype(BF16)
        for h in range(N_MEM_HEADS):
            o5_ref[:, :, h, :] = y[:, :, h * MEM_HEAD_DIM:(h + 1) * MEM_HEAD_DIM]


def _kv_proj(mem, gmem, wk, wv, bb=2):
    nb, n_mem, d = mem.shape
    depth = wk.shape[0]
    w_spec = pl.BlockSpec((None, d, d), lambda l, i: (l, 0, 0))
    o5_spec = pl.BlockSpec((None, bb, n_mem, N_MEM_HEADS, MEM_HEAD_DIM), lambda l, i: (l, i, 0, 0, 0))
    ob_spec = pl.BlockSpec((None, bb, n_mem, d), lambda l, i: (l, i, 0, 0))
    o5_shape = jax.ShapeDtypeStruct((depth, nb, n_mem, N_MEM_HEADS, MEM_HEAD_DIM), F32)
    ob_shape = jax.ShapeDtypeStruct((depth, nb, n_mem, d), BF16)
    return pl.pallas_call(
        _kv_proj_kernel,
        grid=(depth, nb // bb),
        in_specs=[pl.BlockSpec((bb, n_mem, d), lambda l, i: (i, 0, 0)),
                  pl.BlockSpec((None, 1, d), lambda l, i: (l, 0, 0)), w_spec, w_spec],
        out_specs=[o5_spec, o5_spec, ob_spec, ob_spec],
        out_shape=[o5_shape, o5_shape, ob_shape, ob_shape],
        compiler_params=_cparams("parallel", "parallel"),
        name="kv_proj",
    )(mem, gmem, wk, wv)


LANE_TILES_PER_HEAD = MEM_HEAD_DIM // LANES
HEAD_ROWS = N_MEM_HEADS * LANE_TILES_PER_HEAD


def _to_head_rows(a):
    lead = a.shape[:-1]
    n = len(lead)
    a = a.reshape(*lead, N_MEM_HEADS, LANE_TILES_PER_HEAD, LANES)
    return jnp.swapaxes(a, n, n + 1).reshape(*lead, HEAD_ROWS, LANES)


def _from_head_rows(a):
    lead = a.shape[:-2]
    n = len(lead)
    a = a.reshape(*lead, LANE_TILES_PER_HEAD, N_MEM_HEADS, LANES)
    return jnp.swapaxes(a, n, n + 1).reshape(*lead, N_MEM_HEADS * MEM_HEAD_DIM)


def _xattn_sample_kernel(q_ref, k_ref, v_ref, o_ref):
    for b in range(q_ref.shape[0]):
        part = jnp.sum(k_ref[b] * q_ref[b][None], axis=-1, keepdims=True)
        s = (part + pltpu.roll(part, N_MEM_HEADS, axis=1)) * (MEM_HEAD_DIM ** -0.5)
        e = jnp.exp(s - jnp.max(s, axis=0, keepdims=True))
        pr = e / jnp.sum(e, axis=0, keepdims=True)
        o_ref[b] = jnp.sum(pr * v_ref[b], axis=0)


def _xattn_sample(q, k8, v8, layer, bb=8):
    nb, d = q.shape
    n_mem = k8.shape[2]
    kv_spec = pl.BlockSpec((None, bb, n_mem, HEAD_ROWS, LANES), lambda i: (layer, i, 0, 0, 0))
    out = pl.pallas_call(
        _xattn_sample_kernel,
        grid=(nb // bb,),
        in_specs=[pl.BlockSpec((bb, HEAD_ROWS, LANES), lambda i: (i, 0, 0)), kv_spec, kv_spec],
        out_specs=pl.BlockSpec((bb, HEAD_ROWS, LANES), lambda i: (i, 0, 0)),
        out_shape=jax.ShapeDtypeStruct((nb, HEAD_ROWS, LANES), F32),
        compiler_params=_cparams("parallel"),
        name="xattn_sample",
    )(_to_head_rows(q), k8, v8)
    return _from_head_rows(out)


def _split_bf16(a):
    hi = a.astype(BF16)
    return hi, (a - hi.astype(F32)).astype(BF16)


def _top2(logits):
    n = logits.shape[-1]
    lane = lax.broadcasted_iota(jnp.int32, logits.shape, 1)
    m1 = jnp.max(logits, axis=-1, keepdims=True)
    i1 = jnp.min(jnp.where(logits == m1, lane, n), axis=-1, keepdims=True)
    rest = jnp.where(lane == i1, -jnp.inf, logits)
    m2 = jnp.max(rest, axis=-1, keepdims=True)
    i2 = jnp.min(jnp.where(rest == m2, lane, n), axis=-1, keepdims=True)
    e2 = jnp.exp(m2 - m1)
    denom = 1.0 + e2
    return i1, i2, 1.0 / denom, e2 / denom


def _ffn_kernel(x_ref, g_ref, *rest, routed, final):
    if routed:
        wr_ref, rest = rest[0], rest[1:]
    if final:
        gfin_ref, rest = rest[0], rest[1:]
    wg_ref, wu_ref, wd_ref, o_ref, h_ref, acc_ref = rest[:6]
    e = pl.program_id(1)

    @pl.when(e == 0)
    def _():
        x = x_ref[...]
        h = _rms(x, g_ref[...])
        h_ref[...] = h.astype(BF16)
        acc_ref[...] = x
        if routed:
            i1_ref, i2_ref, g1_ref, g2_ref = rest[6:]
            h_hi, h_lo = _split_bf16(h)
            w_hi, w_lo = _split_bf16(wr_ref[...])
            logits = _dot(h_hi, w_hi) + (_dot(h_hi, w_lo) + _dot(h_lo, w_hi))
            lane = lax.broadcasted_iota(jnp.int32, logits.shape, 1)
            logits = jnp.where(lane < N_EXPERTS, logits, -jnp.inf)
            i1_ref[...], i2_ref[...], g1_ref[...], g2_ref[...] = _top2(logits)

    hb = h_ref[...]
    gt = _dot(hb, wg_ref[...])
    up = _dot(hb, wu_ref[...])
    y = _dot((jax.nn.silu(gt) * up).astype(BF16), wd_ref[...])
    if routed:
        i1_ref, i2_ref, g1_ref, g2_ref = rest[6:]
        c = jnp.where(i1_ref[...] == e, g1_ref[...], 0.0) + jnp.where(i2_ref[...] == e, g2_ref[...], 0.0)
        y = c * y
    acc_ref[...] += y

    @pl.when(e == pl.num_programs(1) - 1)
    def _():
        out = acc_ref[...]
        o_ref[...] = _rms(out, gfin_ref[...]) if final else out


def _ffn(x, g, wg, wu, wd, tm, w_router=None, g_final=None):
    r, d = x.shape
    n_e, _, f = wg.shape
    routed, final = w_router is not None, g_final is not None
    args, specs = [x, g], [pl.BlockSpec((tm, d), lambda i, e: (i, 0)), _const_spec(g.shape)]
    if routed:
        args.append(w_router)
        specs.append(_const_spec(w_router.shape))
    if final:
        args.append(g_final)
        specs.append(_const_spec(g_final.shape))
    args += [wg, wu, wd]
    specs += [pl.BlockSpec((None, d, f), lambda i, e: (e, 0, 0)),
              pl.BlockSpec((None, d, f), lambda i, e: (e, 0, 0)),
              pl.BlockSpec((None, f, d), lambda i, e: (e, 0, 0))]
    scratch = [pltpu.VMEM((tm, d), BF16), pltpu.VMEM((tm, d), F32)]
    if routed:
        scratch += [pltpu.VMEM((tm, 1), jnp.int32), pltpu.VMEM((tm, 1), jnp.int32),
                    pltpu.VMEM((tm, 1), F32), pltpu.VMEM((tm, 1), F32)]
    return pl.pallas_call(
        functools.partial(_ffn_kernel, routed=routed, final=final),
        grid=(r // tm, n_e),
        in_specs=specs,
        out_specs=pl.BlockSpec((tm, d), lambda i, e: (i, 0)),
        out_shape=jax.ShapeDtypeStruct((r, d), F32),
        scratch_shapes=scratch,
        compiler_params=_cparams("parallel", "arbitrary"),
        name="ffn_moe" if routed else "ffn_dense",
    )(*args)


def _block_diag_gates(w_a, w_x):
    per_group = LRU_GROUP // LRU_HEAD_DIM
    eye = jnp.eye(per_group, dtype=w_a.dtype)

    def bd(w):
        w = w.reshape(D_LRU // LRU_GROUP, per_group, LRU_HEAD_DIM, LRU_HEAD_DIM)
        return jnp.einsum('ghij,hk->ghikj', w, eye).reshape(D_LRU // LRU_GROUP, LRU_GROUP, LRU_GROUP)

    return jnp.concatenate([bd(w_a), bd(w_x)], axis=-1).astype(BF16)


def _split_ff(w_gate, w_up, w_down, n_split=2):
    d, f = w_gate.shape
    fs = f // n_split
    wg = w_gate.reshape(d, n_split, fs).transpose(1, 0, 2).astype(BF16)
    wu = w_up.reshape(d, n_split, fs).transpose(1, 0, 2).astype(BF16)
    wd = w_down.reshape(n_split, fs, d).astype(BF16)
    return wg, wu, wd


def kernel(x_prompt, x_sample, state_shortconv, state_lru_conv, state_lru_h, cache_mem_k, cache_mem_v,
           mem_prompt, norm_mix, w_in, w_short_conv, w_lru_conv, b_lru_conv, w_lru_a, b_lru_a, w_lru_x,
           b_lru_x, lru_lambda, norm_conv_out, norm_lru_out, w_mix_out, norm_xattn, norm_mem, w_q, w_k,
           w_v, w_o, norm_ffn, w_ff_gate, w_ff_up, w_ff_down, w_router, w_moe_gate, w_moe_up, w_moe_down,
           norm_final):
    depth = w_in.shape[0]
    nb, t, d = x_prompt.shape
    ns = x_sample.shape[0]
    n_mem = mem_prompt.shape[1]
    row = lambda a: a.reshape(1, -1)

    layers = []
    for l in range(depth):
        lp = dict(
            gmix=row(norm_mix[l]), win=w_in[l].astype(BF16), wsc=w_short_conv[l], wlc=w_lru_conv[l],
            blc=row(b_lru_conv[l]), wg=_block_diag_gates(w_lru_a[l], w_lru_x[l]), ba=row(b_lru_a[l]),
            bx=row(b_lru_x[l]), lam=row(lru_lambda[l]), gco=row(norm_conv_out[l]), glo=row(norm_lru_out[l]),
            wout=w_mix_out[l].astype(BF16), gx=row(norm_xattn[l]),
            wq=w_q[l].astype(BF16), wo=w_o[l].astype(BF16),
            gffn=row(norm_ffn[l]), gfin=row(norm_final) if l == depth - 1 else None)
        j = l // 2
        if l % 2 == 0:
            lp['ffw'] = _split_ff(w_ff_gate[j], w_ff_up[j], w_ff_down[j])
            lp['wr'] = None
        else:
            lp['ffw'] = (w_moe_gate[j].astype(BF16), w_moe_up[j].astype(BF16), w_moe_down[j].astype(BF16))
            lp['wr'] = jnp.pad(w_router[j], ((0, 0), (0, LANES - N_EXPERTS)))
        layers.append(lp)

    x = x_prompt
    p_mem_k, p_mem_v, kb, vb = _kv_proj(mem_prompt, norm_mem.reshape(depth, 1, d), w_k.astype(BF16),
                                        w_v.astype(BF16))
    pc, plc, ph = [], [], []
    for l, lp in enumerate(layers):
        x, tail_p, tail_x, h_last = _mixer_prompt(x, lp)
        x = _xattn_prompt(x, lp['gx'], lp['wq'], kb[l], vb[l], lp['wo'])
        x = _ffn(x.reshape(nb * t, d), lp['gffn'], *lp['ffw'], tm=512, w_router=lp['wr'],
                 g_final=lp['gfin']).reshape(nb, t, d)
        pc.append(tail_p[:, SUBLANES - 2:])
        plc.append(tail_x[:, SUBLANES - 3:])
        ph.append(h_last)
    y_prompt = x

    x = x_sample.reshape(ns, d)
    k8 = _to_head_rows(cache_mem_k.reshape(depth, ns, n_mem, d))
    v8 = _to_head_rows(cache_mem_v.reshape(depth, ns, n_mem, d))
    sc, slc, sh = [], [], []
    for l, lp in enumerate(layers):
        x, p_new, xb_new, h_new = _mixer_sample(x, state_shortconv[l], state_lru_conv[l], state_lru_h[l], lp)
        q = _norm_matmul(x, lp['gx'], lp['wq'], ns)
        o = _xattn_sample(q, k8, v8, l)
        x = _matmul_residual(o, lp['wo'], x, ns)
        x = _ffn(x, lp['gffn'], *lp['ffw'], tm=ns, w_router=lp['wr'], g_final=lp['gfin'])
        sc.append(jnp.stack([state_shortconv[l][:, 1], p_new], axis=1))
        slc.append(jnp.stack([state_lru_conv[l][:, 1], state_lru_conv[l][:, 2], xb_new], axis=1))
        sh.append(h_new)
    y_sample = x.reshape(ns, 1, d)

    return (y_prompt, y_sample, jnp.stack(pc), jnp.stack(plc), jnp.stack(ph), p_mem_k, p_mem_v,
            jnp.stack(sc), jnp.stack(slc), jnp.stack(sh))
```

```python
import functools

import jax
import jax.numpy as jnp
from jax import lax
from jax.experimental import pallas as pl
from jax.experimental.pallas import tpu as pltpu

F32 = jnp.float32
BF16 = jnp.bfloat16

EPS = 1e-6
LRU_C = 8.0
D_CONV = 512
D_LRU = 512
N_LRU_HEADS = 8
LRU_HEAD_DIM = 64
LRU_GROUP = 256
N_MEM_HEADS = 4
MEM_HEAD_DIM = 256
N_EXPERTS = 8
SUBLANES = 8
LANES = 128
VMEM_LIMIT_BYTES = 56 * 1024 * 1024


def _cparams(*sem):
    return pltpu.CompilerParams(dimension_semantics=sem, vmem_limit_bytes=VMEM_LIMIT_BYTES)


def _rms(x, g):
    return x * lax.rsqrt(jnp.mean(x * x, axis=-1, keepdims=True) + EPS) * g


def _dot(a, b):
    return jnp.dot(a, b, preferred_element_type=F32)


def _lru_gates(xc, wg_ref, ba, bx, lam):
    neg_lam = -lam
    softplus = jnp.maximum(neg_lam, 0.0) + jnp.log1p(jnp.exp(-jnp.abs(neg_lam)))
    a_parts, u_parts = [], []
    for g in range(D_LRU // LRU_GROUP):
        sl = slice(g * LRU_GROUP, (g + 1) * LRU_GROUP)
        xg = xc[:, sl]
        gates = _dot(xg.astype(BF16), wg_ref[g])
        r = jax.nn.sigmoid(gates[:, :LRU_GROUP] + ba[:, sl])
        i = jax.nn.sigmoid(gates[:, LRU_GROUP:] + bx[:, sl])
        log_a = (-LRU_C) * r * softplus[:, sl]
        th = jnp.tanh(log_a)
        one_minus_a2 = (-2.0 * th) / (1.0 - th)
        a_parts.append(jnp.exp(log_a))
        u_parts.append(jnp.sqrt(one_minus_a2) * (i * xg))
    return jnp.concatenate(a_parts, axis=-1), jnp.concatenate(u_parts, axis=-1)


def _mix_out(x, ya, yb, gco, glo, wout_ref):
    y = jnp.concatenate([_rms(ya, gco), _rms(yb, glo)], axis=-1).astype(BF16)
    return x + _dot(y, wout_ref[...])


def _conv_bulk(v, w_ref):
    width = w_ref.shape[0]
    acc = v * w_ref[width - 1:width, :]
    for s in range(1, width):
        acc = acc + pltpu.roll(v, s, axis=0) * w_ref[width - 1 - s:width - s, :]
    return acc


def _conv_head(carry, head, w_ref):
    width = w_ref.shape[0]
    ext = jnp.concatenate([carry, head], axis=0)
    acc = head * w_ref[width - 1:width, :]
    for s in range(1, width):
        acc = acc + pltpu.roll(ext, s, axis=0)[SUBLANES:] * w_ref[width - 1 - s:width - s, :]
    return acc


def _mixer_prompt_kernel(x_ref, gmix_ref, win_ref, wsc_ref, wlc_ref, blc_ref, wg_ref, ba_ref, bx_ref,
                         lam_ref, gco_ref, glo_ref, wout_ref,
                         xo_ref, tailp_ref, tailx_ref, hout_ref,
                         cp_ref, cx_ref, h_ref, ya_ref, xc_ref, a_ref, u_ref):
    nb, tt, d = x_ref.shape

    @pl.when(pl.program_id(0) == 0)
    def _():
        cp_ref[...] = jnp.zeros_like(cp_ref)
        cx_ref[...] = jnp.zeros_like(cx_ref)
        h_ref[...] = jnp.zeros_like(h_ref)

    x = x_ref[...].reshape(nb * tt, d)
    z = _dot(_rms(x, gmix_ref[...]).astype(BF16), win_ref[...])
    gb = z[:, 0:D_CONV]
    p = z[:, D_CONV:2 * D_CONV] * z[:, 2 * D_CONV:3 * D_CONV]
    xb = z[:, 3 * D_CONV:3 * D_CONV + D_LRU]
    gate = z[:, 3 * D_CONV + D_LRU:]

    ya_ref[...] = gb * _conv_bulk(p, wsc_ref)
    xc_ref[...] = _conv_bulk(xb, wlc_ref)
    for b in range(nb):
        head = slice(b * tt, b * tt + SUBLANES)
        tail = slice((b + 1) * tt - SUBLANES, (b + 1) * tt)
        ya_ref[head, :] = gb[head] * _conv_head(cp_ref[b], p[head], wsc_ref)
        xc_ref[head, :] = _conv_head(cx_ref[b], xb[head], wlc_ref)
        cp_ref[b] = p[tail]
        cx_ref[b] = xb[tail]
    tailp_ref[...] = cp_ref[...]
    tailx_ref[...] = cx_ref[...]

    xc = xc_ref[...] + blc_ref[...]
    a, u = _lru_gates(xc, wg_ref, ba_ref[...], bx_ref[...], lam_ref[...])
    n_chunks = a_ref.shape[0]
    for c in range(n_chunks):
        a_ref[c] = a[:, c * LANES:(c + 1) * LANES]
        u_ref[c] = u[:, c * LANES:(c + 1) * LANES]

    def step(t, hs):
        rows = pl.ds(t, nb, stride=tt)
        new = []
        for c in range(n_chunks):
            h = a_ref[c, rows, :] * hs[c] + u_ref[c, rows, :]
            u_ref[c, rows, :] = h
            new.append(h)
        return tuple(new)

    h0 = h_ref[...]
    h_last = lax.fori_loop(0, tt, step, tuple(h0[:, c * LANES:(c + 1) * LANES] for c in range(n_chunks)))
    h_last = jnp.concatenate(h_last, axis=-1)
    h_ref[...] = h_last
    hout_ref[...] = h_last

    yb = jnp.concatenate([u_ref[c] for c in range(n_chunks)], axis=-1) * jax.nn.gelu(gate)
    out = _mix_out(x, ya_ref[...], yb, gco_ref[...], glo_ref[...], wout_ref)
    xo_ref[...] = out.reshape(nb, tt, d)


def _const_spec(shape):
    return pl.BlockSpec(shape, lambda *_: (0,) * len(shape))


def _mixer_prompt(x, p, tt=64):
    nb, t, d = x.shape
    small = [p['gmix'], p['win'], p['wsc'], p['wlc'], p['blc'], p['wg'], p['ba'], p['bx'], p['lam'],
             p['gco'], p['glo'], p['wout']]
    rows = nb * tt
    return pl.pallas_call(
        _mixer_prompt_kernel,
        grid=(t // tt,),
        in_specs=[pl.BlockSpec((nb, tt, d), lambda i: (0, i, 0))] + [_const_spec(a.shape) for a in small],
        out_specs=[pl.BlockSpec((nb, tt, d), lambda i: (0, i, 0)),
                   _const_spec((nb, SUBLANES, D_CONV)), _const_spec((nb, SUBLANES, D_LRU)),
                   _const_spec((nb, D_LRU))],
        out_shape=[jax.ShapeDtypeStruct((nb, t, d), F32),
                   jax.ShapeDtypeStruct((nb, SUBLANES, D_CONV), F32),
                   jax.ShapeDtypeStruct((nb, SUBLANES, D_LRU), F32),
                   jax.ShapeDtypeStruct((nb, D_LRU), F32)],
        scratch_shapes=[pltpu.VMEM((nb, SUBLANES, D_CONV), F32), pltpu.VMEM((nb, SUBLANES, D_LRU), F32),
                        pltpu.VMEM((nb, D_LRU), F32),
                        pltpu.VMEM((rows, D_CONV), F32), pltpu.VMEM((rows, D_LRU), F32),
                        pltpu.VMEM((D_LRU // LANES, rows, LANES), F32),
                        pltpu.VMEM((D_LRU // LANES, rows, LANES), F32)],
        compiler_params=_cparams("arbitrary"),
        name="mixer_prompt",
    )(x, *small)


def _mixer_sample_kernel(x_ref, sc0_ref, sc1_ref, lc0_ref, lc1_ref, lc2_ref, h0_ref,
                         gmix_ref, win_ref, wsc_ref, wlc_ref, blc_ref, wg_ref, ba_ref, bx_ref,
                         lam_ref, gco_ref, glo_ref, wout_ref,
                         xo_ref, pnew_ref, xbnew_ref, hout_ref):
    x = x_ref[...]
    z = _dot(_rms(x, gmix_ref[...]).astype(BF16), win_ref[...])
    gb = z[:, 0:D_CONV]
    p = z[:, D_CONV:2 * D_CONV] * z[:, 2 * D_CONV:3 * D_CONV]
    xb = z[:, 3 * D_CONV:3 * D_CONV + D_LRU]
    gate = z[:, 3 * D_CONV + D_LRU:]

    ya = gb * (sc0_ref[...] * wsc_ref[0:1, :] + sc1_ref[...] * wsc_ref[1:2, :] + p * wsc_ref[2:3, :])
    xc = (lc0_ref[...] * wlc_ref[0:1, :] + lc1_ref[...] * wlc_ref[1:2, :] + lc2_ref[...] * wlc_ref[2:3, :]
          + xb * wlc_ref[3:4, :]) + blc_ref[...]
    a, u = _lru_gates(xc, wg_ref, ba_ref[...], bx_ref[...], lam_ref[...])
    h = a * h0_ref[...] + u
    yb = h * jax.nn.gelu(gate)
    xo_ref[...] = _mix_out(x, ya, yb, gco_ref[...], glo_ref[...], wout_ref)
    pnew_ref[...] = p
    xbnew_ref[...] = xb
    hout_ref[...] = h


def _mixer_sample(x, sc, lc, h0, p):
    nb, d = x.shape
    args = [x, sc[:, 0], sc[:, 1], lc[:, 0], lc[:, 1], lc[:, 2], h0,
            p['gmix'], p['win'], p['wsc'], p['wlc'], p['blc'], p['wg'], p['ba'], p['bx'], p['lam'],
            p['gco'], p['glo'], p['wout']]
    return pl.pallas_call(
        _mixer_sample_kernel,
        grid=(1,),
        in_specs=[_const_spec(a.shape) for a in args],
        out_specs=[_const_spec((nb, d)), _const_spec((nb, D_CONV)), _const_spec((nb, D_LRU)),
                   _const_spec((nb, D_LRU))],
        out_shape=[jax.ShapeDtypeStruct((nb, d), F32), jax.ShapeDtypeStruct((nb, D_CONV), F32),
                   jax.ShapeDtypeStruct((nb, D_LRU), F32), jax.ShapeDtypeStruct((nb, D_LRU), F32)],
        compiler_params=_cparams("arbitrary"),
        name="mixer_sample",
    )(*args)


def _norm_matmul_kernel(x_ref, g_ref, w_ref, o_ref):
    o_ref[...] = _dot(_rms(x_ref[...], g_ref[...]).astype(BF16), w_ref[...])


def _norm_matmul(x, g, w, tm):
    r, d = x.shape
    n = w.shape[1]
    return pl.pallas_call(
        _norm_matmul_kernel,
        grid=(r // tm,),
        in_specs=[pl.BlockSpec((tm, d), lambda i: (i, 0)), _const_spec(g.shape), _const_spec(w.shape)],
        out_specs=pl.BlockSpec((tm, n), lambda i: (i, 0)),
        out_shape=jax.ShapeDtypeStruct((r, n), F32),
        compiler_params=_cparams("parallel"),
        name="norm_matmul",
    )(x, g, w)


def _matmul_residual_kernel(a_ref, w_ref, x_ref, o_ref):
    o_ref[...] = x_ref[...] + _dot(a_ref[...].astype(BF16), w_ref[...])


def _matmul_residual(a, w, x, tm):
    r, d = x.shape
    return pl.pallas_call(
        _matmul_residual_kernel,
        grid=(r // tm,),
        in_specs=[pl.BlockSpec((tm, a.shape[1]), lambda i: (i, 0)), _const_spec(w.shape),
                  pl.BlockSpec((tm, d), lambda i: (i, 0))],
        out_specs=pl.BlockSpec((tm, d), lambda i: (i, 0)),
        out_shape=jax.ShapeDtypeStruct((r, d), F32),
        compiler_params=_cparams("parallel"),
        name="matmul_residual",
    )(a, w, x)


def _xattn_prompt_kernel(x_ref, g_ref, wq_ref, k_ref, v_ref, wo_ref, o_ref):
    x = x_ref[...]
    q = _dot(_rms(x, g_ref[...]).astype(BF16), wq_ref[...])
    heads = []
    for h in range(N_MEM_HEADS):
        sl = slice(h * MEM_HEAD_DIM, (h + 1) * MEM_HEAD_DIM)
        kh = k_ref[:, sl]
        vh = v_ref[:, sl]
        s = lax.dot_general(q[:, sl].astype(BF16), kh, (((1,), (1,)), ((), ())),
                            preferred_element_type=F32) * (MEM_HEAD_DIM ** -0.5)
        e = jnp.exp(s - jnp.max(s, axis=-1, keepdims=True))
        pr = e / jnp.sum(e, axis=-1, keepdims=True)
        heads.append(_dot(pr.astype(BF16), vh))
    o = jnp.concatenate(heads, axis=-1).astype(BF16)
    o_ref[...] = x + _dot(o, wo_ref[...])


def _xattn_prompt(x, g, wq, k, v, wo, tq=512):
    nb, t, d = x.shape
    n_mem = k.shape[1]
    return pl.pallas_call(
        _xattn_prompt_kernel,
        grid=(nb, t // tq),
        in_specs=[pl.BlockSpec((None, tq, d), lambda b, i: (b, i, 0)), _const_spec(g.shape),
                  _const_spec(wq.shape),
                  pl.BlockSpec((None, n_mem, d), lambda b, i: (b, 0, 0)),
                  pl.BlockSpec((None, n_mem, d), lambda b, i: (b, 0, 0)),
                  _const_spec(wo.shape)],
        out_specs=pl.BlockSpec((None, tq, d), lambda b, i: (b, i, 0)),
        out_shape=jax.ShapeDtypeStruct((nb, t, d), F32),
        compiler_params=_cparams("parallel", "parallel"),
        name="xattn_prompt",
    )(x, g, wq, k, v, wo)


def _kv_proj_kernel(mem_ref, g_ref, wk_ref, wv_ref, k5_ref, v5_ref, kb_ref, vb_ref):
    bb, n_mem, d = mem_ref.shape
    m = _rms(mem_ref[...].reshape(bb * n_mem, d), g_ref[...]).astype(BF16)
    for w_ref, o5_ref, ob_ref in ((wk_ref, k5_ref, kb_ref), (wv_ref, v5_ref, vb_ref)):
        y = _dot(m, w_ref[...]).reshape(bb, n_mem, d)
        ob_ref[...] = y.astype(BF16)
        for h in range(N_MEM_HEADS):
            o5_ref[:, :, h, :] = y[:, :, h * MEM_HEAD_DIM:(h + 1) * MEM_HEAD_DIM]


def _kv_proj(mem, gmem, wk, wv, bb=2):
    nb, n_mem, d = mem.shape
    depth = wk.shape[0]
    w_spec = pl.BlockSpec((None, d, d), lambda l, i: (l, 0, 0))
    o5_spec = pl.BlockSpec((None, bb, n_mem, N_MEM_HEADS, MEM_HEAD_DIM), lambda l, i: (l, i, 0, 0, 0))
    ob_spec = pl.BlockSpec((None, bb, n_mem, d), lambda l, i: (l, i, 0, 0))
    o5_shape = jax.ShapeDtypeStruct((depth, nb, n_mem, N_MEM_HEADS, MEM_HEAD_DIM), F32)
    ob_shape = jax.ShapeDtypeStruct((depth, nb, n_mem, d), BF16)
    return pl.pallas_call(
        _kv_proj_kernel,
        grid=(depth, nb // bb),
        in_specs=[pl.BlockSpec((bb, n_mem, d), lambda l, i: (i, 0, 0)),
                  pl.BlockSpec((None, 1, d), lambda l, i: (l, 0, 0)), w_spec, w_spec],
        out_specs=[o5_spec, o5_spec, ob_spec, ob_spec],
        out_shape=[o5_shape, o5_shape, ob_shape, ob_shape],
        compiler_params=_cparams("parallel", "parallel"),
        name="kv_proj",
    )(mem, gmem, wk, wv)


LANE_TILES_PER_HEAD = MEM_HEAD_DIM // LANES
HEAD_ROWS = N_MEM_HEADS * LANE_TILES_PER_HEAD


def _to_head_rows(a):
    lead = a.shape[:-1]
    n = len(lead)
    a = a.reshape(*lead, N_MEM_HEADS, LANE_TILES_PER_HEAD, LANES)
    return jnp.swapaxes(a, n, n + 1).reshape(*lead, HEAD_ROWS, LANES)


def _from_head_rows(a):
    lead = a.shape[:-2]
    n = len(lead)
    a = a.reshape(*lead, LANE_TILES_PER_HEAD, N_MEM_HEADS, LANES)
    return jnp.swapaxes(a, n, n + 1).reshape(*lead, N_MEM_HEADS * MEM_HEAD_DIM)


def _xattn_sample_kernel(q_ref, k_ref, v_ref, o_ref):
    for b in range(q_ref.shape[0]):
        part = jnp.sum(k_ref[b] * q_ref[b][None], axis=-1, keepdims=True)
        s = (part + pltpu.roll(part, N_MEM_HEADS, axis=1)) * (MEM_HEAD_DIM ** -0.5)
        e = jnp.exp(s - jnp.max(s, axis=0, keepdims=True))
        pr = e / jnp.sum(e, axis=0, keepdims=True)
        o_ref[b] = jnp.sum(pr * v_ref[b], axis=0)


def _xattn_sample(q, k8, v8, layer, bb=8):
    nb, d = q.shape
    n_mem = k8.shape[2]
    kv_spec = pl.BlockSpec((None, bb, n_mem, HEAD_ROWS, LANES), lambda i: (layer, i, 0, 0, 0))
    out = pl.pallas_call(
        _xattn_sample_kernel,
        grid=(nb // bb,),
        in_specs=[pl.BlockSpec((bb, HEAD_ROWS, LANES), lambda i: (i, 0, 0)), kv_spec, kv_spec],
        out_specs=pl.BlockSpec((bb, HEAD_ROWS, LANES), lambda i: (i, 0, 0)),
        out_shape=jax.ShapeDtypeStruct((nb, HEAD_ROWS, LANES), F32),
        compiler_params=_cparams("parallel"),
        name="xattn_sample",
    )(_to_head_rows(q), k8, v8)
    return _from_head_rows(out)


def _split_bf16(a):
    hi = a.astype(BF16)
    return hi, (a - hi.astype(F32)).astype(BF16)


def _top2(logits):
    n = logits.shape[-1]
    lane = lax.broadcasted_iota(jnp.int32, logits.shape, 1)
    m1 = jnp.max(logits, axis=-1, keepdims=True)
    i1 = jnp.min(jnp.where(logits == m1, lane, n), axis=-1, keepdims=True)
    rest = jnp.where(lane == i1, -jnp.inf, logits)
    m2 = jnp.max(rest, axis=-1, keepdims=True)
    i2 = jnp.min(jnp.where(rest == m2, lane, n), axis=-1, keepdims=True)
    e2 = jnp.exp(m2 - m1)
    denom = 1.0 + e2
    return i1, i2, 1.0 / denom, e2 / denom


def _ffn_kernel(x_ref, g_ref, *rest, routed, final):
    if routed:
        wr_ref, rest = rest[0], rest[1:]
    if final:
        gfin_ref, rest = rest[0], rest[1:]
    wg_ref, wu_ref, wd_ref, o_ref, h_ref, acc_ref = rest[:6]
    e = pl.program_id(1)

    @pl.when(e == 0)
    def _():
        x = x_ref[...]
        h = _rms(x, g_ref[...])
        h_ref[...] = h.astype(BF16)
        acc_ref[...] = x
        if routed:
            i1_ref, i2_ref, g1_ref, g2_ref = rest[6:]
            h_hi, h_lo = _split_bf16(h)
            w_hi, w_lo = _split_bf16(wr_ref[...])
            logits = _dot(h_hi, w_hi) + (_dot(h_hi, w_lo) + _dot(h_lo, w_hi))
            lane = lax.broadcasted_iota(jnp.int32, logits.shape, 1)
            logits = jnp.where(lane < N_EXPERTS, logits, -jnp.inf)
            i1_ref[...], i2_ref[...], g1_ref[...], g2_ref[...] = _top2(logits)

    hb = h_ref[...]
    gt = _dot(hb, wg_ref[...])
    up = _dot(hb, wu_ref[...])
    y = _dot((jax.nn.silu(gt) * up).astype(BF16), wd_ref[...])
    if routed:
        i1_ref, i2_ref, g1_ref, g2_ref = rest[6:]
        c = jnp.where(i1_ref[...] == e, g1_ref[...], 0.0) + jnp.where(i2_ref[...] == e, g2_ref[...], 0.0)
        y = c * y
    acc_ref[...] += y

    @pl.when(e == pl.num_programs(1) - 1)
    def _():
        out = acc_ref[...]
        o_ref[...] = _rms(out, gfin_ref[...]) if final else out


def _ffn(x, g, wg, wu, wd, tm, w_router=None, g_final=None):
    r, d = x.shape
    n_e, _, f = wg.shape
    routed, final = w_router is not None, g_final is not None
    args, specs = [x, g], [pl.BlockSpec((tm, d), lambda i, e: (i, 0)), _const_spec(g.shape)]
    if routed:
        args.append(w_router)
        specs.append(_const_spec(w_router.shape))
    if final:
        args.append(g_final)
        specs.append(_const_spec(g_final.shape))
    args += [wg, wu, wd]
    specs += [pl.BlockSpec((None, d, f), lambda i, e: (e, 0, 0)),
              pl.BlockSpec((None, d, f), lambda i, e: (e, 0, 0)),
              pl.BlockSpec((None, f, d), lambda i, e: (e, 0, 0))]
    scratch = [pltpu.VMEM((tm, d), BF16), pltpu.VMEM((tm, d), F32)]
    if routed:
        scratch += [pltpu.VMEM((tm, 1), jnp.int32), pltpu.VMEM((tm, 1), jnp.int32),
                    pltpu.VMEM((tm, 1), F32), pltpu.VMEM((tm, 1), F32)]
    return pl.pallas_call(
        functools.partial(_ffn_kernel, routed=routed, final=final),
        grid=(r // tm, n_e),
        in_specs=specs,
        out_specs=pl.BlockSpec((tm, d), lambda i, e: (i, 0)),
        out_shape=jax.ShapeDtypeStruct((r, d), F32),
        scratch_shapes=scratch,
        compiler_params=_cparams("parallel", "arbitrary"),
        name="ffn_moe" if routed else "ffn_dense",
    )(*args)


MOE_TILE = 512
TOP_K = 2
SLAB = SUBLANES


def _to_slabs(slab_ref, m):
    n = m.shape[0]
    for c in range(SLAB):
        slab_ref[pl.ds(c, n, stride=SLAB), :] = m[:, c * LANES:(c + 1) * LANES]


def _from_slabs(slab_ref):
    n = slab_ref.shape[0] // SLAB
    return jnp.concatenate([slab_ref[pl.ds(c, n, stride=SLAB), :] for c in range(SLAB)], axis=-1)


def _slab_rows(tok):
    return pl.ds(pl.multiple_of(tok * SLAB, SLAB), SLAB)


def _route_kernel(x_ref, g_ref, wrt_ref, lp_ref, gate_ref, cnt_ref):
    tk = x_ref.shape[0]
    n_e = wrt_ref.shape[0]
    h = _rms(x_ref[...], g_ref[...])
    h_hi, h_lo = _split_bf16(h)
    w_hi, w_lo = _split_bf16(wrt_ref[...])
    nt = (((1,), (1,)), ((), ()))
    dg = lambda a, b: lax.dot_general(a, b, nt, preferred_element_type=F32)
    logits = dg(w_hi, h_hi) + (dg(w_hi, h_lo) + dg(w_lo, h_hi))
    sub = lax.broadcasted_iota(jnp.int32, logits.shape, 0)
    m1 = jnp.max(logits, axis=0, keepdims=True)
    i1 = jnp.min(jnp.where(logits == m1, sub, n_e), axis=0, keepdims=True)
    rest = jnp.where(sub == i1, -jnp.inf, logits)
    m2 = jnp.max(rest, axis=0, keepdims=True)
    i2 = jnp.min(jnp.where(rest == m2, sub, n_e), axis=0, keepdims=True)
    e2 = jnp.exp(m2 - m1)
    denom = 1.0 + e2
    gate_ref[0:1, :] = 1.0 / denom
    gate_ref[1:2, :] = e2 / denom

    pick1 = jnp.where(sub == i1, 1.0, 0.0)
    pick2 = jnp.where(sub == i2, 1.0, 0.0)
    cnt = pick1 + pick2
    earlier = jnp.where(lax.broadcasted_iota(jnp.int32, (tk, tk), 0)
                        < lax.broadcasted_iota(jnp.int32, (tk, tk), 1), 1.0, 0.0).astype(BF16)
    rank = _dot(cnt.astype(BF16), earlier)
    n = jnp.sum(cnt, axis=1, keepdims=True)
    base, acc = [], jnp.zeros((1, 1), F32)
    for e in range(n_e):
        base.append(acc)
        acc = acc + n[e:e + 1, :]
    pos = rank + jnp.concatenate(base, axis=0)
    lp_ref[0:1, :] = jnp.sum(pick1 * pos, axis=0, keepdims=True).astype(jnp.int32)
    lp_ref[1:2, :] = jnp.sum(pick2 * pos, axis=0, keepdims=True).astype(jnp.int32)
    cnt_ref[...] = jnp.broadcast_to(n, cnt_ref.shape)


def _route(x, g, wrt):
    t, d = x.shape
    n_e = wrt.shape[0]
    n_tiles = t // MOE_TILE
    return pl.pallas_call(
        _route_kernel,
        grid=(n_tiles,),
        in_specs=[pl.BlockSpec((MOE_TILE, d), lambda j: (j, 0)), _const_spec(g.shape), _const_spec(wrt.shape)],
        out_specs=[pl.BlockSpec((None, TOP_K, MOE_TILE), lambda j: (j, 0, 0)),
                   pl.BlockSpec((None, TOP_K, MOE_TILE), lambda j: (j, 0, 0)),
                   pl.BlockSpec((None, n_e, LANES), lambda j: (j, 0, 0))],
        out_shape=[jax.ShapeDtypeStruct((n_tiles, TOP_K, MOE_TILE), jnp.int32),
                   jax.ShapeDtypeStruct((n_tiles, TOP_K, MOE_TILE), F32),
                   jax.ShapeDtypeStruct((n_tiles, n_e, LANES), F32)],
        compiler_params=_cparams("parallel"),
        name="moe_route",
    )(x, g, wrt)


def _copy_token_rows(src_ref, src_tok, dst_ref, dst_tok, n, sem, wait):
    for bit in range(MOE_TILE.bit_length()):
        size = 1 << bit
        before = (n >> (bit + 1)) << (bit + 1)

        @pl.when(((n >> bit) & 1) == 1)
        def _():
            cp = pltpu.make_async_copy(
                src_ref.at[pl.ds(pl.multiple_of((src_tok + before) * SLAB, SLAB), size * SLAB)],
                dst_ref.at[pl.ds(pl.multiple_of((dst_tok + before) * SLAB, SLAB), size * SLAB)], sem)
            if wait:
                cp.wait()
            else:
                cp.start()


def _dispatch_kernel(n_ref, off_ref, dst_ref, x_ref, g_ref, lp_ref, xs_ref, slab_ref, comp_ref, sem):
    j = pl.program_id(0)
    tk = x_ref.shape[0]
    _to_slabs(slab_ref, _rms(x_ref[...], g_ref[...]))

    def place(t, carry):
        row = slab_ref[_slab_rows(t), :]
        for k in range(TOP_K):
            comp_ref[_slab_rows(lp_ref[k, t]), :] = row
        return carry

    lax.fori_loop(0, tk, place, 0, unroll=8)
    for wait in (False, True):
        for e in range(n_ref.shape[1]):
            _copy_token_rows(comp_ref, off_ref[j, e], xs_ref, dst_ref[j, e], n_ref[j, e], sem, wait)


def _dispatch(x, g, lp, n, off, dst):
    t, d = x.shape
    n_tiles = t // MOE_TILE
    grid_spec = pltpu.PrefetchScalarGridSpec(
        num_scalar_prefetch=3,
        grid=(n_tiles,),
        in_specs=[pl.BlockSpec((MOE_TILE, d), lambda j, *_: (j, 0)),
                  pl.BlockSpec(g.shape, lambda j, *_: (0, 0)),
                  pl.BlockSpec((None, TOP_K, MOE_TILE), lambda j, *_: (j, 0, 0), memory_space=pltpu.SMEM)],
        out_specs=pl.BlockSpec(memory_space=pl.ANY),
        scratch_shapes=[pltpu.VMEM((MOE_TILE * SLAB, LANES), F32),
                        pltpu.VMEM((TOP_K * MOE_TILE * SLAB, LANES), F32),
                        pltpu.SemaphoreType.DMA(())])
    return pl.pallas_call(
        _dispatch_kernel,
        grid_spec=grid_spec,
        out_shape=jax.ShapeDtypeStruct((TOP_K * t * SLAB, LANES), F32),
        compiler_params=_cparams("arbitrary"),
        name="moe_dispatch",
    )(n, off, dst, x, g, lp)


def _grouped_ffn_kernel(tile_ref, grp_ref, lo_ref, hi_ref, first_ref, valid_ref,
                        xs_ref, wg_ref, wu_ref, wd_ref, ys_ref):
    w = pl.program_id(0)
    tm = xs_ref.shape[0] // SLAB

    @pl.when(valid_ref[w] == 1)
    def _():
        x = _from_slabs(xs_ref).astype(BF16)
        gt = _dot(x, wg_ref[...])
        up = _dot(x, wu_ref[...])
        y = _dot((jax.nn.silu(gt) * up).astype(BF16), wd_ref[...])

        @pl.when(first_ref[w] == 1)
        def _():
            _to_slabs(ys_ref, y)

        @pl.when(first_ref[w] == 0)
        def _():
            r = lax.broadcasted_iota(jnp.int32, (tm, 1), 0)
            mine = jnp.logical_and(r >= lo_ref[w], r < hi_ref[w])
            for c in range(SLAB):
                rows = pl.ds(c, tm, stride=SLAB)
                ys_ref[rows, :] = jnp.where(mine, y[:, c * LANES:(c + 1) * LANES], ys_ref[rows, :])


def _grouped_ffn(xs, wg, wu, wd, tile, grp, lo, hi, first, valid):
    n_e, d, f = wg.shape
    rows = MOE_TILE * SLAB
    grid_spec = pltpu.PrefetchScalarGridSpec(
        num_scalar_prefetch=6,
        grid=(tile.shape[0],),
        in_specs=[pl.BlockSpec((rows, LANES), lambda w, tile, *_: (tile[w], 0)),
                  pl.BlockSpec((None, d, f), lambda w, tile, grp, *_: (grp[w], 0, 0)),
                  pl.BlockSpec((None, d, f), lambda w, tile, grp, *_: (grp[w], 0, 0)),
                  pl.BlockSpec((None, f, d), lambda w, tile, grp, *_: (grp[w], 0, 0))],
        out_specs=pl.BlockSpec((rows, LANES), lambda w, tile, *_: (tile[w], 0)))
    return pl.pallas_call(
        _grouped_ffn_kernel,
        grid_spec=grid_spec,
        out_shape=jax.ShapeDtypeStruct(xs.shape, F32),
        compiler_params=_cparams("arbitrary"),
        name="moe_grouped_ffn",
    )(tile, grp, lo, hi, first, valid, xs, wg, wu, wd)


def _combine_kernel(n_ref, off_ref, dst_ref, x_ref, lp_ref, gate_ref, *rest, final):
    if final:
        gfin_ref, rest = rest[0], rest[1:]
    ys_ref, o_ref, comp_ref, slab_ref, sem = rest
    j = pl.program_id(0)
    tk = x_ref.shape[0]
    for wait in (False, True):
        for e in range(n_ref.shape[1]):
            _copy_token_rows(ys_ref, dst_ref[j, e], comp_ref, off_ref[j, e], n_ref[j, e], sem, wait)

    def mix(t, carry):
        a = comp_ref[_slab_rows(lp_ref[0, t]), :] * gate_ref[0, t]
        b = comp_ref[_slab_rows(lp_ref[1, t]), :] * gate_ref[1, t]
        slab_ref[_slab_rows(t), :] = a + b
        return carry

    lax.fori_loop(0, tk, mix, 0, unroll=8)
    out = x_ref[...] + _from_slabs(slab_ref)
    o_ref[...] = _rms(out, gfin_ref[...]) if final else out


def _combine(x, lp, gates, ys, n, off, dst, g_final):
    t, d = x.shape
    n_tiles = t // MOE_TILE
    final = g_final is not None
    smem_spec = pl.BlockSpec((None, TOP_K, MOE_TILE), lambda j, *_: (j, 0, 0), memory_space=pltpu.SMEM)
    args = [x, lp, gates]
    specs = [pl.BlockSpec((MOE_TILE, d), lambda j, *_: (j, 0)), smem_spec, smem_spec]
    if final:
        args.append(g_final)
        specs.append(pl.BlockSpec(g_final.shape, lambda j, *_: (0, 0)))
    args.append(ys)
    specs.append(pl.BlockSpec(memory_space=pl.ANY))
    grid_spec = pltpu.PrefetchScalarGridSpec(
        num_scalar_prefetch=3,
        grid=(n_tiles,),
        in_specs=specs,
        out_specs=pl.BlockSpec((MOE_TILE, d), lambda j, *_: (j, 0)),
        scratch_shapes=[pltpu.VMEM((TOP_K * MOE_TILE * SLAB, LANES), F32),
                        pltpu.VMEM((MOE_TILE * SLAB, LANES), F32),
                        pltpu.SemaphoreType.DMA(())])
    return pl.pallas_call(
        functools.partial(_combine_kernel, final=final),
        grid_spec=grid_spec,
        out_shape=jax.ShapeDtypeStruct((t, d), F32),
        compiler_params=_cparams("arbitrary"),
        name="moe_combine",
    )(n, off, dst, *args)


def _moe_routed(x, g, w_router, wg, wu, wd, g_final):
    t, _ = x.shape
    n_e = wg.shape[0]
    lp, gates, cnt = _route(x, g, w_router.T)
    n = cnt[:, :, 0].astype(jnp.int32)
    off = jnp.cumsum(n, axis=1) - n
    per_e = jnp.sum(n, axis=0)
    ends = jnp.cumsum(per_e)
    starts = ends - per_e
    dst = starts[None, :] + jnp.cumsum(n, axis=0) - n
    xs = _dispatch(x, g, lp, n, off, dst)

    n_row_tiles = TOP_K * t // MOE_TILE
    n_steps = n_row_tiles + n_e - 1
    first_tile = starts // MOE_TILE
    n_items = jnp.where(per_e > 0, (ends - 1) // MOE_TILE - first_tile + 1, 0)
    item_end = jnp.cumsum(n_items)
    w = jnp.arange(n_steps, dtype=jnp.int32)
    valid = w < item_end[-1]
    wc = jnp.minimum(w, item_end[-1] - 1)
    grp = jnp.sum(wc[:, None] >= item_end[None, :], axis=1).astype(jnp.int32)
    tile = (first_tile[grp] + wc - (item_end - n_items)[grp]).astype(jnp.int32)
    lo = jnp.clip(starts[grp] - tile * MOE_TILE, 0, MOE_TILE).astype(jnp.int32)
    hi = jnp.clip(ends[grp] - tile * MOE_TILE, 0, MOE_TILE).astype(jnp.int32)
    first = jnp.concatenate([jnp.ones((1,), jnp.int32), (tile[1:] != tile[:-1]).astype(jnp.int32)])
    ys = _grouped_ffn(xs, wg, wu, wd, tile, grp, lo, hi, first, valid.astype(jnp.int32))
    return _combine(x, lp, gates, ys, n, off, dst, g_final)


def _block_diag_gates(w_a, w_x):
    per_group = LRU_GROUP // LRU_HEAD_DIM
    eye = jnp.eye(per_group, dtype=w_a.dtype)

    def bd(w):
        w = w.reshape(D_LRU // LRU_GROUP, per_group, LRU_HEAD_DIM, LRU_HEAD_DIM)
        return jnp.einsum('ghij,hk->ghikj', w, eye).reshape(D_LRU // LRU_GROUP, LRU_GROUP, LRU_GROUP)

    return jnp.concatenate([bd(w_a), bd(w_x)], axis=-1).astype(BF16)


def _split_ff(w_gate, w_up, w_down, n_split=2):
    d, f = w_gate.shape
    fs = f // n_split
    wg = w_gate.reshape(d, n_split, fs).transpose(1, 0, 2).astype(BF16)
    wu = w_up.reshape(d, n_split, fs).transpose(1, 0, 2).astype(BF16)
    wd = w_down.reshape(n_split, fs, d).astype(BF16)
    return wg, wu, wd


def kernel(x_prompt, x_sample, state_shortconv, state_lru_conv, state_lru_h, cache_mem_k, cache_mem_v,
           mem_prompt, norm_mix, w_in, w_short_conv, w_lru_conv, b_lru_conv, w_lru_a, b_lru_a, w_lru_x,
           b_lru_x, lru_lambda, norm_conv_out, norm_lru_out, w_mix_out, norm_xattn, norm_mem, w_q, w_k,
           w_v, w_o, norm_ffn, w_ff_gate, w_ff_up, w_ff_down, w_router, w_moe_gate, w_moe_up, w_moe_down,
           norm_final):
    depth = w_in.shape[0]
    nb, t, d = x_prompt.shape
    ns = x_sample.shape[0]
    n_mem = mem_prompt.shape[1]
    row = lambda a: a.reshape(1, -1)

    layers = []
    for l in range(depth):
        lp = dict(
            gmix=row(norm_mix[l]), win=w_in[l].astype(BF16), wsc=w_short_conv[l], wlc=w_lru_conv[l],
            blc=row(b_lru_conv[l]), wg=_block_diag_gates(w_lru_a[l], w_lru_x[l]), ba=row(b_lru_a[l]),
            bx=row(b_lru_x[l]), lam=row(lru_lambda[l]), gco=row(norm_conv_out[l]), glo=row(norm_lru_out[l]),
            wout=w_mix_out[l].astype(BF16), gx=row(norm_xattn[l]),
            wq=w_q[l].astype(BF16), wo=w_o[l].astype(BF16),
            gffn=row(norm_ffn[l]), gfin=row(norm_final) if l == depth - 1 else None)
        j = l // 2
        if l % 2 == 0:
            lp['ffw'] = _split_ff(w_ff_gate[j], w_ff_up[j], w_ff_down[j])
            lp['wr'] = None
        else:
            lp['ffw'] = (w_moe_gate[j].astype(BF16), w_moe_up[j].astype(BF16), w_moe_down[j].astype(BF16))
            lp['wr'] = jnp.pad(w_router[j], ((0, 0), (0, LANES - N_EXPERTS)))
            lp['wr_raw'] = w_router[j]
        layers.append(lp)

    x = x_prompt
    p_mem_k, p_mem_v, kb, vb = _kv_proj(mem_prompt, norm_mem.reshape(depth, 1, d), w_k.astype(BF16),
                                        w_v.astype(BF16))
    pc, plc, ph = [], [], []
    for l, lp in enumerate(layers):
        x, tail_p, tail_x, h_last = _mixer_prompt(x, lp)
        x = _xattn_prompt(x, lp['gx'], lp['wq'], kb[l], vb[l], lp['wo'])
        if lp['wr'] is None:
            x = _ffn(x.reshape(nb * t, d), lp['gffn'], *lp['ffw'], tm=512, g_final=lp['gfin'])
        else:
            x = _moe_routed(x.reshape(nb * t, d), lp['gffn'], lp['wr_raw'], *lp['ffw'], lp['gfin'])
        x = x.reshape(nb, t, d)
        pc.append(tail_p[:, SUBLANES - 2:])
        plc.append(tail_x[:, SUBLANES - 3:])
        ph.append(h_last)
    y_prompt = x

    x = x_sample.reshape(ns, d)
    k8 = _to_head_rows(cache_mem_k.reshape(depth, ns, n_mem, d))
    v8 = _to_head_rows(cache_mem_v.reshape(depth, ns, n_mem, d))
    sc, slc, sh = [], [], []
    for l, lp in enumerate(layers):
        x, p_new, xb_new, h_new = _mixer_sample(x, state_shortconv[l], state_lru_conv[l], state_lru_h[l], lp)
        q = _norm_matmul(x, lp['gx'], lp['wq'], ns)
        o = _xattn_sample(q, k8, v8, l)
        x = _matmul_residual(o, lp['wo'], x, ns)
        x = _ffn(x, lp['gffn'], *lp['ffw'], tm=ns, w_router=lp['wr'], g_final=lp['gfin'])
        sc.append(jnp.stack([state_shortconv[l][:, 1], p_new], axis=1))
        slc.append(jnp.stack([state_lru_conv[l][:, 1], state_lru_conv[l][:, 2], xb_new], axis=1))
        sh.append(h_new)
    y_sample = x.reshape(ns, 1, d)

    return (y_prompt, y_sample, jnp.stack(pc), jnp.stack(plc), jnp.stack(ph), p_mem_k, p_mem_v,
            jnp.stack(sc), jnp.stack(slc), jnp.stack(sh))
```

```python
import functools

import jax
import jax.numpy as jnp
from jax import lax
from jax.experimental import pallas as pl
from jax.experimental.pallas import tpu as pltpu

F32 = jnp.float32
BF16 = jnp.bfloat16

EPS = 1e-6
LRU_C = 8.0
D_CONV = 512
D_LRU = 512
N_LRU_HEADS = 8
LRU_HEAD_DIM = 64
LRU_GROUP = 256
N_MEM_HEADS = 4
MEM_HEAD_DIM = 256
N_EXPERTS = 8
SUBLANES = 8
LANES = 128
VMEM_LIMIT_BYTES = 56 * 1024 * 1024


def _cparams(*sem):
    return pltpu.CompilerParams(dimension_semantics=sem, vmem_limit_bytes=VMEM_LIMIT_BYTES)


def _rms(x, g):
    return x * lax.rsqrt(jnp.mean(x * x, axis=-1, keepdims=True) + EPS) * g


def _dot(a, b):
    return jnp.dot(a, b, preferred_element_type=F32)


def _lru_gates(xc, wg_ref, ba, bx, lam):
    neg_lam = -lam
    softplus = jnp.maximum(neg_lam, 0.0) + jnp.log1p(jnp.exp(-jnp.abs(neg_lam)))
    a_parts, u_parts = [], []
    for g in range(D_LRU // LRU_GROUP):
        sl = slice(g * LRU_GROUP, (g + 1) * LRU_GROUP)
        xg = xc[:, sl]
        gates = _dot(xg.astype(BF16), wg_ref[g])
        r = jax.nn.sigmoid(gates[:, :LRU_GROUP] + ba[:, sl])
        i = jax.nn.sigmoid(gates[:, LRU_GROUP:] + bx[:, sl])
        log_a = (-LRU_C) * r * softplus[:, sl]
        th = jnp.tanh(log_a)
        one_minus_a2 = (-2.0 * th) / (1.0 - th)
        a_parts.append(jnp.exp(log_a))
        u_parts.append(jnp.sqrt(one_minus_a2) * (i * xg))
    return jnp.concatenate(a_parts, axis=-1), jnp.concatenate(u_parts, axis=-1)


def _mix_out(x, ya, yb, gco, glo, wout_ref):
    y = jnp.concatenate([_rms(ya, gco), _rms(yb, glo)], axis=-1).astype(BF16)
    return x + _dot(y, wout_ref[...])


MAX_CONV_HISTORY = 3


def _conv_taps(ext_ref, w_ref, rows, nb):
    width = w_ref.shape[0]
    hist = MAX_CONV_HISTORY * nb
    acc = ext_ref[hist:hist + rows, :] * w_ref[width - 1:width, :]
    for s in range(1, width):
        acc = acc + ext_ref[hist - s * nb:hist - s * nb + rows, :] * w_ref[width - 1 - s:width - s, :]
    return acc


def _mixer_prompt_kernel(x_hbm, gmix_ref, win_ref, wsc_ref, wlc_ref, blc_ref, wg_ref, ba_ref, bx_ref,
                         lam_ref, gco_ref, glo_ref, wout_ref,
                         xo_hbm, tailp_ref, tailx_ref, hout_ref,
                         xin_ref, xout_ref, in_sem, out_sem, pext_ref, xext_ref, h_ref, a_ref, u_ref):
    _, tt, nb, d = xin_ref.shape
    rows = tt * nb
    hist = MAX_CONV_HISTORY * nb
    i = pl.program_id(0)
    n_steps = pl.num_programs(0)
    slot = i % 2

    def in_copies(step, s):
        return [pltpu.make_async_copy(x_hbm.at[b, pl.ds(step * tt, tt), :], xin_ref.at[s, :, b, :],
                                      in_sem.at[s]) for b in range(nb)]

    def out_copies(step, s):
        return [pltpu.make_async_copy(xout_ref.at[s, :, b, :], xo_hbm.at[b, pl.ds(step * tt, tt), :],
                                      out_sem.at[s]) for b in range(nb)]

    @pl.when(i == 0)
    def _():
        pext_ref[0:hist, :] = jnp.zeros((hist, pext_ref.shape[1]), F32)
        xext_ref[0:hist, :] = jnp.zeros((hist, xext_ref.shape[1]), F32)
        h_ref[...] = jnp.zeros_like(h_ref)
        for cp in in_copies(0, 0):
            cp.start()

    @pl.when(i + 1 < n_steps)
    def _():
        for cp in in_copies(i + 1, 1 - slot):
            cp.start()

    for cp in in_copies(i, slot):
        cp.wait()
    x = xin_ref[slot].reshape(rows, d)
    z = _dot(_rms(x, gmix_ref[...]).astype(BF16), win_ref[...])
    gb = z[:, 0:D_CONV]
    gate = z[:, 3 * D_CONV + D_LRU:]
    pext_ref[hist:hist + rows, :] = z[:, D_CONV:2 * D_CONV] * z[:, 2 * D_CONV:3 * D_CONV]
    xext_ref[hist:hist + rows, :] = z[:, 3 * D_CONV:3 * D_CONV + D_LRU]

    ya = gb * _conv_taps(pext_ref, wsc_ref, rows, nb)
    xc = _conv_taps(xext_ref, wlc_ref, rows, nb) + blc_ref[...]
    for ext_ref, tail_ref in ((pext_ref, tailp_ref), (xext_ref, tailx_ref)):
        tail = ext_ref[rows:rows + hist, :]
        ext_ref[0:hist, :] = tail
        tail_ref[...] = tail

    a, u = _lru_gates(xc, wg_ref, ba_ref[...], bx_ref[...], lam_ref[...])
    a_ref[...] = a
    u_ref[...] = u

    def step(t, h):
        grp = pl.ds(pl.multiple_of(t * nb, nb), nb)
        h = a_ref[grp, :] * h + u_ref[grp, :]
        u_ref[grp, :] = h
        return h

    h_last = lax.fori_loop(0, tt, step, h_ref[...], unroll=8)
    h_ref[...] = h_last
    hout_ref[...] = h_last

    yb = u_ref[...] * jax.nn.gelu(gate)
    out = _mix_out(x, ya, yb, gco_ref[...], glo_ref[...], wout_ref)

    @pl.when(i >= 2)
    def _():
        for cp in out_copies(i - 2, slot):
            cp.wait()

    xout_ref[slot] = out.reshape(tt, nb, d)
    for cp in out_copies(i, slot):
        cp.start()

    @pl.when(i == n_steps - 1)
    def _():
        for cp in out_copies(i, slot):
            cp.wait()

        @pl.when(i >= 1)
        def _():
            for cp in out_copies(i - 1, 1 - slot):
                cp.wait()


def _const_spec(shape):
    return pl.BlockSpec(shape, lambda *_: (0,) * len(shape))


def _mixer_prompt(x, p, tt=64):
    nb, t, d = x.shape
    small = [p['gmix'], p['win'], p['wsc'], p['wlc'], p['blc'], p['wg'], p['ba'], p['bx'], p['lam'],
             p['gco'], p['glo'], p['wout']]
    rows = nb * tt
    hist = MAX_CONV_HISTORY * nb
    return pl.pallas_call(
        _mixer_prompt_kernel,
        grid=(t // tt,),
        in_specs=[pl.BlockSpec(memory_space=pl.ANY)] + [_const_spec(a.shape) for a in small],
        out_specs=[pl.BlockSpec(memory_space=pl.ANY),
                   _const_spec((hist, D_CONV)), _const_spec((hist, D_LRU)), _const_spec((nb, D_LRU))],
        out_shape=[jax.ShapeDtypeStruct((nb, t, d), F32),
                   jax.ShapeDtypeStruct((hist, D_CONV), F32),
                   jax.ShapeDtypeStruct((hist, D_LRU), F32),
                   jax.ShapeDtypeStruct((nb, D_LRU), F32)],
        scratch_shapes=[pltpu.VMEM((2, tt, nb, d), F32), pltpu.VMEM((2, tt, nb, d), F32),
                        pltpu.SemaphoreType.DMA((2,)), pltpu.SemaphoreType.DMA((2,)),
                        pltpu.VMEM((hist + rows, D_CONV), F32), pltpu.VMEM((hist + rows, D_LRU), F32),
                        pltpu.VMEM((nb, D_LRU), F32),
                        pltpu.VMEM((rows, D_LRU), F32), pltpu.VMEM((rows, D_LRU), F32)],
        compiler_params=_cparams("arbitrary"),
        name="mixer_prompt",
    )(x, *small)


def _mixer_sample_kernel(x_ref, sc0_ref, sc1_ref, lc0_ref, lc1_ref, lc2_ref, h0_ref,
                         gmix_ref, win_ref, wsc_ref, wlc_ref, blc_ref, wg_ref, ba_ref, bx_ref,
                         lam_ref, gco_ref, glo_ref, wout_ref,
                         xo_ref, pnew_ref, xbnew_ref, hout_ref):
    x = x_ref[...]
    z = _dot(_rms(x, gmix_ref[...]).astype(BF16), win_ref[...])
    gb = z[:, 0:D_CONV]
    p = z[:, D_CONV:2 * D_CONV] * z[:, 2 * D_CONV:3 * D_CONV]
    xb = z[:, 3 * D_CONV:3 * D_CONV + D_LRU]
    gate = z[:, 3 * D_CONV + D_LRU:]

    ya = gb * (sc0_ref[...] * wsc_ref[0:1, :] + sc1_ref[...] * wsc_ref[1:2, :] + p * wsc_ref[2:3, :])
    xc = (lc0_ref[...] * wlc_ref[0:1, :] + lc1_ref[...] * wlc_ref[1:2, :] + lc2_ref[...] * wlc_ref[2:3, :]
          + xb * wlc_ref[3:4, :]) + blc_ref[...]
    a, u = _lru_gates(xc, wg_ref, ba_ref[...], bx_ref[...], lam_ref[...])
    h = a * h0_ref[...] + u
    yb = h * jax.nn.gelu(gate)
    xo_ref[...] = _mix_out(x, ya, yb, gco_ref[...], glo_ref[...], wout_ref)
    pnew_ref[...] = p
    xbnew_ref[...] = xb
    hout_ref[...] = h


def _mixer_sample(x, sc, lc, h0, p):
    nb, d = x.shape
    args = [x, sc[:, 0], sc[:, 1], lc[:, 0], lc[:, 1], lc[:, 2], h0,
            p['gmix'], p['win'], p['wsc'], p['wlc'], p['blc'], p['wg'], p['ba'], p['bx'], p['lam'],
            p['gco'], p['glo'], p['wout']]
    return pl.pallas_call(
        _mixer_sample_kernel,
        grid=(1,),
        in_specs=[_const_spec(a.shape) for a in args],
        out_specs=[_const_spec((nb, d)), _const_spec((nb, D_CONV)), _const_spec((nb, D_LRU)),
                   _const_spec((nb, D_LRU))],
        out_shape=[jax.ShapeDtypeStruct((nb, d), F32), jax.ShapeDtypeStruct((nb, D_CONV), F32),
                   jax.ShapeDtypeStruct((nb, D_LRU), F32), jax.ShapeDtypeStruct((nb, D_LRU), F32)],
        compiler_params=_cparams("arbitrary"),
        name="mixer_sample",
    )(*args)


def _norm_matmul_kernel(x_ref, g_ref, w_ref, o_ref):
    o_ref[...] = _dot(_rms(x_ref[...], g_ref[...]).astype(BF16), w_ref[...])


def _norm_matmul(x, g, w, tm):
    r, d = x.shape
    n = w.shape[1]
    return pl.pallas_call(
        _norm_matmul_kernel,
        grid=(r // tm,),
        in_specs=[pl.BlockSpec((tm, d), lambda i: (i, 0)), _const_spec(g.shape), _const_spec(w.shape)],
        out_specs=pl.BlockSpec((tm, n), lambda i: (i, 0)),
        out_shape=jax.ShapeDtypeStruct((r, n), F32),
        compiler_params=_cparams("parallel"),
        name="norm_matmul",
    )(x, g, w)


def _matmul_residual_kernel(a_ref, w_ref, x_ref, o_ref):
    o_ref[...] = x_ref[...] + _dot(a_ref[...].astype(BF16), w_ref[...])


def _matmul_residual(a, w, x, tm):
    r, d = x.shape
    return pl.pallas_call(
        _matmul_residual_kernel,
        grid=(r // tm,),
        in_specs=[pl.BlockSpec((tm, a.shape[1]), lambda i: (i, 0)), _const_spec(w.shape),
                  pl.BlockSpec((tm, d), lambda i: (i, 0))],
        out_specs=pl.BlockSpec((tm, d), lambda i: (i, 0)),
        out_shape=jax.ShapeDtypeStruct((r, d), F32),
        compiler_params=_cparams("parallel"),
        name="matmul_residual",
    )(a, w, x)


def _xattn_prompt_kernel(x_ref, g_ref, wq_ref, k_ref, v_ref, wo_ref, o_ref):
    x = x_ref[...]
    q = _dot(_rms(x, g_ref[...]).astype(BF16), wq_ref[...])
    heads = []
    for h in range(N_MEM_HEADS):
        sl = slice(h * MEM_HEAD_DIM, (h + 1) * MEM_HEAD_DIM)
        kh = k_ref[:, sl]
        vh = v_ref[:, sl]
        s = lax.dot_general(q[:, sl].astype(BF16), kh, (((1,), (1,)), ((), ())),
                            preferred_element_type=F32) * (MEM_HEAD_DIM ** -0.5)
        e = jnp.exp(s - jnp.max(s, axis=-1, keepdims=True))
        pr = e / jnp.sum(e, axis=-1, keepdims=True)
        heads.append(_dot(pr.astype(BF16), vh))
    o = jnp.concatenate(heads, axis=-1).astype(BF16)
    o_ref[...] = x + _dot(o, wo_ref[...])


def _xattn_prompt(x, g, wq, k, v, wo, tq=512):
    nb, t, d = x.shape
    n_mem = k.shape[1]
    return pl.pallas_call(
        _xattn_prompt_kernel,
        grid=(nb, t // tq),
        in_specs=[pl.BlockSpec((None, tq, d), lambda b, i: (b, i, 0)), _const_spec(g.shape),
                  _const_spec(wq.shape),
                  pl.BlockSpec((None, n_mem, d), lambda b, i: (b, 0, 0)),
                  pl.BlockSpec((None, n_mem, d), lambda b, i: (b, 0, 0)),
                  _const_spec(wo.shape)],
        out_specs=pl.BlockSpec((None, tq, d), lambda b, i: (b, i, 0)),
        out_shape=jax.ShapeDtypeStruct((nb, t, d), F32),
        compiler_params=_cparams("parallel", "parallel"),
        name="xattn_prompt",
    )(x, g, wq, k, v, wo)


def _kv_proj_kernel(mem_ref, g_ref, wk_ref, wv_ref, k5_ref, v5_ref, kb_ref, vb_ref):
    bb, n_mem, d = mem_ref.shape
    m = _rms(mem_ref[...].reshape(bb * n_mem, d), g_ref[...]).astype(BF16)
    for w_ref, o5_ref, ob_ref in ((wk_ref, k5_ref, kb_ref), (wv_ref, v5_ref, vb_ref)):
        y = _dot(m, w_ref[...]).reshape(bb, n_mem, d)
        ob_ref[...] = y.astype(BF16)
        for h in range(N_MEM_HEADS):
            o5_ref[:, :, h, :] = y[:, :, h * MEM_HEAD_DIM:(h + 1) * MEM_HEAD_DIM]


def _kv_proj(mem, gmem, wk, wv, bb=2):
    nb, n_mem, d = mem.shape
    depth = wk.shape[0]
    w_spec = pl.BlockSpec((None, d, d), lambda l, i: (l, 0, 0))
    o5_spec = pl.BlockSpec((None, bb, n_mem, N_MEM_HEADS, MEM_HEAD_DIM), lambda l, i: (l, i, 0, 0, 0))
    ob_spec = pl.BlockSpec((None, bb, n_mem, d), lambda l, i: (l, i, 0, 0))
    o5_shape = jax.ShapeDtypeStruct((depth, nb, n_mem, N_MEM_HEADS, MEM_HEAD_DIM), F32)
    ob_shape = jax.ShapeDtypeStruct((depth, nb, n_mem, d), BF16)
    return pl.pallas_call(
        _kv_proj_kernel,
        grid=(depth, nb // bb),
        in_specs=[pl.BlockSpec((bb, n_mem, d), lambda l, i: (i, 0, 0)),
                  pl.BlockSpec((None, 1, d), lambda l, i: (l, 0, 0)), w_spec, w_spec],
        out_specs=[o5_spec, o5_spec, ob_spec, ob_spec],
        out_shape=[o5_shape, o5_shape, ob_shape, ob_shape],
        compiler_params=_cparams("parallel", "parallel"),
        name="kv_proj",
    )(mem, gmem, wk, wv)


LANE_TILES_PER_HEAD = MEM_HEAD_DIM // LANES
HEAD_ROWS = N_MEM_HEADS * LANE_TILES_PER_HEAD


def _to_head_rows(a):
    lead = a.shape[:-1]
    n = len(lead)
    a = a.reshape(*lead, N_MEM_HEADS, LANE_TILES_PER_HEAD, LANES)
    return jnp.swapaxes(a, n, n + 1).reshape(*lead, HEAD_ROWS, LANES)


def _from_head_rows(a):
    lead = a.shape[:-2]
    n = len(lead)
    a = a.reshape(*lead, LANE_TILES_PER_HEAD, N_MEM_HEADS, LANES)
    return jnp.swapaxes(a, n, n + 1).reshape(*lead, N_MEM_HEADS * MEM_HEAD_DIM)


def _xattn_sample_kernel(q_ref, k_ref, v_ref, o_ref):
    for b in range(q_ref.shape[0]):
        part = jnp.sum(k_ref[b] * q_ref[b][None], axis=-1, keepdims=True)
        s = (part + pltpu.roll(part, N_MEM_HEADS, axis=1)) * (MEM_HEAD_DIM ** -0.5)
        e = jnp.exp(s - jnp.max(s, axis=0, keepdims=True))
        pr = e / jnp.sum(e, axis=0, keepdims=True)
        o_ref[b] = jnp.sum(pr * v_ref[b], axis=0)


def _xattn_sample(q, k8, v8, layer, bb=8):
    nb, d = q.shape
    n_mem = k8.shape[2]
    kv_spec = pl.BlockSpec((None, bb, n_mem, HEAD_ROWS, LANES), lambda i: (layer, i, 0, 0, 0))
    out = pl.pallas_call(
        _xattn_sample_kernel,
        grid=(nb // bb,),
        in_specs=[pl.BlockSpec((bb, HEAD_ROWS, LANES), lambda i: (i, 0, 0)), kv_spec, kv_spec],
        out_specs=pl.BlockSpec((bb, HEAD_ROWS, LANES), lambda i: (i, 0, 0)),
        out_shape=jax.ShapeDtypeStruct((nb, HEAD_ROWS, LANES), F32),
        compiler_params=_cparams("parallel"),
        name="xattn_sample",
    )(_to_head_rows(q), k8, v8)
    return _from_head_rows(out)


def _split_bf16(a):
    hi = a.astype(BF16)
    return hi, (a - hi.astype(F32)).astype(BF16)


def _top2(logits):
    n = logits.shape[-1]
    lane = lax.broadcasted_iota(jnp.int32, logits.shape, 1)
    m1 = jnp.max(logits, axis=-1, keepdims=True)
    i1 = jnp.min(jnp.where(logits == m1, lane, n), axis=-1, keepdims=True)
    rest = jnp.where(lane == i1, -jnp.inf, logits)
    m2 = jnp.max(rest, axis=-1, keepdims=True)
    i2 = jnp.min(jnp.where(rest == m2, lane, n), axis=-1, keepdims=True)
    e2 = jnp.exp(m2 - m1)
    denom = 1.0 + e2
    return i1, i2, 1.0 / denom, e2 / denom


def _ffn_kernel(x_ref, g_ref, *rest, routed, final):
    if routed:
        wr_ref, rest = rest[0], rest[1:]
    if final:
        gfin_ref, rest = rest[0], rest[1:]
    wg_ref, wu_ref, wd_ref, o_ref, h_ref, acc_ref = rest[:6]
    e = pl.program_id(1)

    @pl.when(e == 0)
    def _():
        x = x_ref[...]
        h = _rms(x, g_ref[...])
        h_ref[...] = h.astype(BF16)
        acc_ref[...] = x
        if routed:
            i1_ref, i2_ref, g1_ref, g2_ref = rest[6:]
            h_hi, h_lo = _split_bf16(h)
            w_hi, w_lo = _split_bf16(wr_ref[...])
            logits = _dot(h_hi, w_hi) + (_dot(h_hi, w_lo) + _dot(h_lo, w_hi))
            lane = lax.broadcasted_iota(jnp.int32, logits.shape, 1)
            logits = jnp.where(lane < N_EXPERTS, logits, -jnp.inf)
            i1_ref[...], i2_ref[...], g1_ref[...], g2_ref[...] = _top2(logits)

    hb = h_ref[...]
    gt = _dot(hb, wg_ref[...])
    up = _dot(hb, wu_ref[...])
    y = _dot((jax.nn.silu(gt) * up).astype(BF16), wd_ref[...])
    if routed:
        i1_ref, i2_ref, g1_ref, g2_ref = rest[6:]
        c = jnp.where(i1_ref[...] == e, g1_ref[...], 0.0) + jnp.where(i2_ref[...] == e, g2_ref[...], 0.0)
        y = c * y
    acc_ref[...] += y

    @pl.when(e == pl.num_programs(1) - 1)
    def _():
        out = acc_ref[...]
        o_ref[...] = _rms(out, gfin_ref[...]) if final else out


def _ffn(x, g, wg, wu, wd, tm, w_router=None, g_final=None):
    r, d = x.shape
    n_e, _, f = wg.shape
    routed, final = w_router is not None, g_final is not None
    args, specs = [x, g], [pl.BlockSpec((tm, d), lambda i, e: (i, 0)), _const_spec(g.shape)]
    if routed:
        args.append(w_router)
        specs.append(_const_spec(w_router.shape))
    if final:
        args.append(g_final)
        specs.append(_const_spec(g_final.shape))
    args += [wg, wu, wd]
    specs += [pl.BlockSpec((None, d, f), lambda i, e: (e, 0, 0)),
              pl.BlockSpec((None, d, f), lambda i, e: (e, 0, 0)),
              pl.BlockSpec((None, f, d), lambda i, e: (e, 0, 0))]
    scratch = [pltpu.VMEM((tm, d), BF16), pltpu.VMEM((tm, d), F32)]
    if routed:
        scratch += [pltpu.VMEM((tm, 1), jnp.int32), pltpu.VMEM((tm, 1), jnp.int32),
                    pltpu.VMEM((tm, 1), F32), pltpu.VMEM((tm, 1), F32)]
    return pl.pallas_call(
        functools.partial(_ffn_kernel, routed=routed, final=final),
        grid=(r // tm, n_e),
        in_specs=specs,
        out_specs=pl.BlockSpec((tm, d), lambda i, e: (i, 0)),
        out_shape=jax.ShapeDtypeStruct((r, d), F32),
        scratch_shapes=scratch,
        compiler_params=_cparams("parallel", "arbitrary"),
        name="ffn_moe" if routed else "ffn_dense",
    )(*args)


MOE_TILE = 512
TOP_K = 2
SLAB = SUBLANES


def _to_slabs(slab_ref, m):
    n = m.shape[0]
    for c in range(SLAB):
        slab_ref[pl.ds(c, n, stride=SLAB), :] = m[:, c * LANES:(c + 1) * LANES]


def _from_slabs(slab_ref):
    n = slab_ref.shape[0] // SLAB
    return jnp.concatenate([slab_ref[pl.ds(c, n, stride=SLAB), :] for c in range(SLAB)], axis=-1)


def _slab_rows(tok):
    return pl.ds(pl.multiple_of(tok * SLAB, SLAB), SLAB)


def _route_kernel(x_ref, g_ref, wrt_ref, lp_ref, gate_ref, cnt_ref):
    tk = x_ref.shape[0]
    n_e = wrt_ref.shape[0]
    h = _rms(x_ref[...], g_ref[...])
    h_hi, h_lo = _split_bf16(h)
    w_hi, w_lo = _split_bf16(wrt_ref[...])
    nt = (((1,), (1,)), ((), ()))
    dg = lambda a, b: lax.dot_general(a, b, nt, preferred_element_type=F32)
    logits = dg(w_hi, h_hi) + (dg(w_hi, h_lo) + dg(w_lo, h_hi))
    sub = lax.broadcasted_iota(jnp.int32, logits.shape, 0)
    m1 = jnp.max(logits, axis=0, keepdims=True)
    i1 = jnp.min(jnp.where(logits == m1, sub, n_e), axis=0, keepdims=True)
    rest = jnp.where(sub == i1, -jnp.inf, logits)
    m2 = jnp.max(rest, axis=0, keepdims=True)
    i2 = jnp.min(jnp.where(rest == m2, sub, n_e), axis=0, keepdims=True)
    e2 = jnp.exp(m2 - m1)
    denom = 1.0 + e2
    gate_ref[0:1, :] = 1.0 / denom
    gate_ref[1:2, :] = e2 / denom

    pick1 = jnp.where(sub == i1, 1.0, 0.0)
    pick2 = jnp.where(sub == i2, 1.0, 0.0)
    cnt = pick1 + pick2
    earlier = jnp.where(lax.broadcasted_iota(jnp.int32, (tk, tk), 0)
                        < lax.broadcasted_iota(jnp.int32, (tk, tk), 1), 1.0, 0.0).astype(BF16)
    rank = _dot(cnt.astype(BF16), earlier)
    n = jnp.sum(cnt, axis=1, keepdims=True)
    base, acc = [], jnp.zeros((1, 1), F32)
    for e in range(n_e):
        base.append(acc)
        acc = acc + n[e:e + 1, :]
    pos = rank + jnp.concatenate(base, axis=0)
    lp_ref[0:1, :] = jnp.sum(pick1 * pos, axis=0, keepdims=True).astype(jnp.int32)
    lp_ref[1:2, :] = jnp.sum(pick2 * pos, axis=0, keepdims=True).astype(jnp.int32)
    cnt_ref[...] = jnp.broadcast_to(n, cnt_ref.shape)


def _route(x, g, wrt):
    t, d = x.shape
    n_e = wrt.shape[0]
    n_tiles = t // MOE_TILE
    return pl.pallas_call(
        _route_kernel,
        grid=(n_tiles,),
        in_specs=[pl.BlockSpec((MOE_TILE, d), lambda j: (j, 0)), _const_spec(g.shape), _const_spec(wrt.shape)],
        out_specs=[pl.BlockSpec((None, TOP_K, MOE_TILE), lambda j: (j, 0, 0)),
                   pl.BlockSpec((None, TOP_K, MOE_TILE), lambda j: (j, 0, 0)),
                   pl.BlockSpec((None, n_e, LANES), lambda j: (j, 0, 0))],
        out_shape=[jax.ShapeDtypeStruct((n_tiles, TOP_K, MOE_TILE), jnp.int32),
                   jax.ShapeDtypeStruct((n_tiles, TOP_K, MOE_TILE), F32),
                   jax.ShapeDtypeStruct((n_tiles, n_e, LANES), F32)],
        compiler_params=_cparams("parallel"),
        name="moe_route",
    )(x, g, wrt)


def _copy_token_rows(src_ref, src_tok, dst_ref, dst_tok, n, sem, wait):
    for bit in range(MOE_TILE.bit_length()):
        size = 1 << bit
        before = (n >> (bit + 1)) << (bit + 1)

        @pl.when(((n >> bit) & 1) == 1)
        def _():
            cp = pltpu.make_async_copy(
                src_ref.at[pl.ds(pl.multiple_of((src_tok + before) * SLAB, SLAB), size * SLAB)],
                dst_ref.at[pl.ds(pl.multiple_of((dst_tok + before) * SLAB, SLAB), size * SLAB)], sem)
            if wait:
                cp.wait()
            else:
                cp.start()


def _dispatch_kernel(n_ref, off_ref, dst_ref, x_ref, g_ref, lp_ref, xs_ref, slab_ref, comp_ref, sem):
    j = pl.program_id(0)
    tk = x_ref.shape[0]
    _to_slabs(slab_ref, _rms(x_ref[...], g_ref[...]))

    def place(t, carry):
        row = slab_ref[_slab_rows(t), :]
        for k in range(TOP_K):
            comp_ref[_slab_rows(lp_ref[k, t]), :] = row
        return carry

    lax.fori_loop(0, tk, place, 0, unroll=8)
    for wait in (False, True):
        for e in range(n_ref.shape[1]):
            _copy_token_rows(comp_ref, off_ref[j, e], xs_ref, dst_ref[j, e], n_ref[j, e], sem, wait)


def _dispatch(x, g, lp, n, off, dst):
    t, d = x.shape
    n_tiles = t // MOE_TILE
    grid_spec = pltpu.PrefetchScalarGridSpec(
        num_scalar_prefetch=3,
        grid=(n_tiles,),
        in_specs=[pl.BlockSpec((MOE_TILE, d), lambda j, *_: (j, 0)),
                  pl.BlockSpec(g.shape, lambda j, *_: (0, 0)),
                  pl.BlockSpec((None, TOP_K, MOE_TILE), lambda j, *_: (j, 0, 0), memory_space=pltpu.SMEM)],
        out_specs=pl.BlockSpec(memory_space=pl.ANY),
        scratch_shapes=[pltpu.VMEM((MOE_TILE * SLAB, LANES), F32),
                        pltpu.VMEM((TOP_K * MOE_TILE * SLAB, LANES), F32),
                        pltpu.SemaphoreType.DMA(())])
    return pl.pallas_call(
        _dispatch_kernel,
        grid_spec=grid_spec,
        out_shape=jax.ShapeDtypeStruct((TOP_K * t * SLAB, LANES), F32),
        compiler_params=_cparams("arbitrary"),
        name="moe_dispatch",
    )(n, off, dst, x, g, lp)


def _grouped_ffn_kernel(tile_ref, grp_ref, lo_ref, hi_ref, first_ref, valid_ref,
                        xs_ref, wg_ref, wu_ref, wd_ref, ys_ref):
    w = pl.program_id(0)
    tm = xs_ref.shape[0] // SLAB

    @pl.when(valid_ref[w] == 1)
    def _():
        x = _from_slabs(xs_ref).astype(BF16)
        gt = _dot(x, wg_ref[...])
        up = _dot(x, wu_ref[...])
        y = _dot((jax.nn.silu(gt) * up).astype(BF16), wd_ref[...])

        @pl.when(first_ref[w] == 1)
        def _():
            _to_slabs(ys_ref, y)

        @pl.when(first_ref[w] == 0)
        def _():
            r = lax.broadcasted_iota(jnp.int32, (tm, 1), 0)
            mine = jnp.logical_and(r >= lo_ref[w], r < hi_ref[w])
            for c in range(SLAB):
                rows = pl.ds(c, tm, stride=SLAB)
                ys_ref[rows, :] = jnp.where(mine, y[:, c * LANES:(c + 1) * LANES], ys_ref[rows, :])


def _grouped_ffn(xs, wg, wu, wd, tile, grp, lo, hi, first, valid):
    n_e, d, f = wg.shape
    rows = MOE_TILE * SLAB
    grid_spec = pltpu.PrefetchScalarGridSpec(
        num_scalar_prefetch=6,
        grid=(tile.shape[0],),
        in_specs=[pl.BlockSpec((rows, LANES), lambda w, tile, *_: (tile[w], 0)),
                  pl.BlockSpec((None, d, f), lambda w, tile, grp, *_: (grp[w], 0, 0)),
                  pl.BlockSpec((None, d, f), lambda w, tile, grp, *_: (grp[w], 0, 0)),
                  pl.BlockSpec((None, f, d), lambda w, tile, grp, *_: (grp[w], 0, 0))],
        out_specs=pl.BlockSpec((rows, LANES), lambda w, tile, *_: (tile[w], 0)))
    return pl.pallas_call(
        _grouped_ffn_kernel,
        grid_spec=grid_spec,
        out_shape=jax.ShapeDtypeStruct(xs.shape, F32),
        compiler_params=_cparams("arbitrary"),
        name="moe_grouped_ffn",
    )(tile, grp, lo, hi, first, valid, xs, wg, wu, wd)


def _combine_kernel(n_ref, off_ref, dst_ref, x_ref, lp_ref, gate_ref, *rest, final):
    if final:
        gfin_ref, rest = rest[0], rest[1:]
    ys_ref, o_ref, comp_ref, slab_ref, sem = rest
    j = pl.program_id(0)
    tk = x_ref.shape[0]
    for wait in (False, True):
        for e in range(n_ref.shape[1]):
            _copy_token_rows(ys_ref, dst_ref[j, e], comp_ref, off_ref[j, e], n_ref[j, e], sem, wait)

    def mix(t, carry):
        a = comp_ref[_slab_rows(lp_ref[0, t]), :] * gate_ref[0, t]
        b = comp_ref[_slab_rows(lp_ref[1, t]), :] * gate_ref[1, t]
        slab_ref[_slab_rows(t), :] = a + b
        return carry

    lax.fori_loop(0, tk, mix, 0, unroll=8)
    out = x_ref[...] + _from_slabs(slab_ref)
    o_ref[...] = _rms(out, gfin_ref[...]) if final else out


def _combine(x, lp, gates, ys, n, off, dst, g_final):
    t, d = x.shape
    n_tiles = t // MOE_TILE
    final = g_final is not None
    smem_spec = pl.BlockSpec((None, TOP_K, MOE_TILE), lambda j, *_: (j, 0, 0), memory_space=pltpu.SMEM)
    args = [x, lp, gates]
    specs = [pl.BlockSpec((MOE_TILE, d), lambda j, *_: (j, 0)), smem_spec, smem_spec]
    if final:
        args.append(g_final)
        specs.append(pl.BlockSpec(g_final.shape, lambda j, *_: (0, 0)))
    args.append(ys)
    specs.append(pl.BlockSpec(memory_space=pl.ANY))
    grid_spec = pltpu.PrefetchScalarGridSpec(
        num_scalar_prefetch=3,
        grid=(n_tiles,),
        in_specs=specs,
        out_specs=pl.BlockSpec((MOE_TILE, d), lambda j, *_: (j, 0)),
        scratch_shapes=[pltpu.VMEM((TOP_K * MOE_TILE * SLAB, LANES), F32),
                        pltpu.VMEM((MOE_TILE * SLAB, LANES), F32),
                        pltpu.SemaphoreType.DMA(())])
    return pl.pallas_call(
        functools.partial(_combine_kernel, final=final),
        grid_spec=grid_spec,
        out_shape=jax.ShapeDtypeStruct((t, d), F32),
        compiler_params=_cparams("arbitrary"),
        name="moe_combine",
    )(n, off, dst, *args)


def _moe_routed(x, g, w_router, wg, wu, wd, g_final):
    t, _ = x.shape
    n_e = wg.shape[0]
    lp, gates, cnt = _route(x, g, w_router.T)
    n = cnt[:, :, 0].astype(jnp.int32)
    off = jnp.cumsum(n, axis=1) - n
    per_e = jnp.sum(n, axis=0)
    ends = jnp.cumsum(per_e)
    starts = ends - per_e
    dst = starts[None, :] + jnp.cumsum(n, axis=0) - n
    xs = _dispatch(x, g, lp, n, off, dst)

    n_row_tiles = TOP_K * t // MOE_TILE
    n_steps = n_row_tiles + n_e - 1
    first_tile = starts // MOE_TILE
    n_items = jnp.where(per_e > 0, (ends - 1) // MOE_TILE - first_tile + 1, 0)
    item_end = jnp.cumsum(n_items)
    w = jnp.arange(n_steps, dtype=jnp.int32)
    valid = w < item_end[-1]
    wc = jnp.minimum(w, item_end[-1] - 1)
    grp = jnp.sum(wc[:, None] >= item_end[None, :], axis=1).astype(jnp.int32)
    tile = (first_tile[grp] + wc - (item_end - n_items)[grp]).astype(jnp.int32)
    lo = jnp.clip(starts[grp] - tile * MOE_TILE, 0, MOE_TILE).astype(jnp.int32)
    hi = jnp.clip(ends[grp] - tile * MOE_TILE, 0, MOE_TILE).astype(jnp.int32)
    first = jnp.concatenate([jnp.ones((1,), jnp.int32), (tile[1:] != tile[:-1]).astype(jnp.int32)])
    ys = _grouped_ffn(xs, wg, wu, wd, tile, grp, lo, hi, first, valid.astype(jnp.int32))
    return _combine(x, lp, gates, ys, n, off, dst, g_final)


def _block_diag_gates(w_a, w_x):
    per_group = LRU_GROUP // LRU_HEAD_DIM
    eye = jnp.eye(per_group, dtype=w_a.dtype)

    def bd(w):
        w = w.reshape(D_LRU // LRU_GROUP, per_group, LRU_HEAD_DIM, LRU_HEAD_DIM)
        return jnp.einsum('ghij,hk->ghikj', w, eye).reshape(D_LRU // LRU_GROUP, LRU_GROUP, LRU_GROUP)

    return jnp.concatenate([bd(w_a), bd(w_x)], axis=-1).astype(BF16)


def _split_ff(w_gate, w_up, w_down, n_split=2):
    d, f = w_gate.shape
    fs = f // n_split
    wg = w_gate.reshape(d, n_split, fs).transpose(1, 0, 2).astype(BF16)
    wu = w_up.reshape(d, n_split, fs).transpose(1, 0, 2).astype(BF16)
    wd = w_down.reshape(n_split, fs, d).astype(BF16)
    return wg, wu, wd


def kernel(x_prompt, x_sample, state_shortconv, state_lru_conv, state_lru_h, cache_mem_k, cache_mem_v,
           mem_prompt, norm_mix, w_in, w_short_conv, w_lru_conv, b_lru_conv, w_lru_a, b_lru_a, w_lru_x,
           b_lru_x, lru_lambda, norm_conv_out, norm_lru_out, w_mix_out, norm_xattn, norm_mem, w_q, w_k,
           w_v, w_o, norm_ffn, w_ff_gate, w_ff_up, w_ff_down, w_router, w_moe_gate, w_moe_up, w_moe_down,
           norm_final):
    depth = w_in.shape[0]
    nb, t, d = x_prompt.shape
    ns = x_sample.shape[0]
    n_mem = mem_prompt.shape[1]
    row = lambda a: a.reshape(1, -1)

    layers = []
    for l in range(depth):
        lp = dict(
            gmix=row(norm_mix[l]), win=w_in[l].astype(BF16), wsc=w_short_conv[l], wlc=w_lru_conv[l],
            blc=row(b_lru_conv[l]), wg=_block_diag_gates(w_lru_a[l], w_lru_x[l]), ba=row(b_lru_a[l]),
            bx=row(b_lru_x[l]), lam=row(lru_lambda[l]), gco=row(norm_conv_out[l]), glo=row(norm_lru_out[l]),
            wout=w_mix_out[l].astype(BF16), gx=row(norm_xattn[l]),
            wq=w_q[l].astype(BF16), wo=w_o[l].astype(BF16),
            gffn=row(norm_ffn[l]), gfin=row(norm_final) if l == depth - 1 else None)
        j = l // 2
        if l % 2 == 0:
            lp['ffw'] = _split_ff(w_ff_gate[j], w_ff_up[j], w_ff_down[j])
            lp['wr'] = None
        else:
            lp['ffw'] = (w_moe_gate[j].astype(BF16), w_moe_up[j].astype(BF16), w_moe_down[j].astype(BF16))
            lp['wr'] = jnp.pad(w_router[j], ((0, 0), (0, LANES - N_EXPERTS)))
            lp['wr_raw'] = w_router[j]
        layers.append(lp)

    x = x_prompt
    p_mem_k, p_mem_v, kb, vb = _kv_proj(mem_prompt, norm_mem.reshape(depth, 1, d), w_k.astype(BF16),
                                        w_v.astype(BF16))
    pc, plc, ph = [], [], []
    for l, lp in enumerate(layers):
        x, tail_p, tail_x, h_last = _mixer_prompt(x, lp)
        x = _xattn_prompt(x, lp['gx'], lp['wq'], kb[l], vb[l], lp['wo'])
        if lp['wr'] is None:
            x = _ffn(x.reshape(nb * t, d), lp['gffn'], *lp['ffw'], tm=512, g_final=lp['gfin'])
        else:
            x = _moe_routed(x.reshape(nb * t, d), lp['gffn'], lp['wr_raw'], *lp['ffw'], lp['gfin'])
        x = x.reshape(nb, t, d)
        tail_p = jnp.swapaxes(tail_p.reshape(MAX_CONV_HISTORY, nb, D_CONV), 0, 1)
        tail_x = jnp.swapaxes(tail_x.reshape(MAX_CONV_HISTORY, nb, D_LRU), 0, 1)
        pc.append(tail_p[:, MAX_CONV_HISTORY - (w_short_conv.shape[1] - 1):])
        plc.append(tail_x[:, MAX_CONV_HISTORY - (w_lru_conv.shape[1] - 1):])
        ph.append(h_last)
    y_prompt = x

    x = x_sample.reshape(ns, d)
    k8 = _to_head_rows(cache_mem_k.reshape(depth, ns, n_mem, d))
    v8 = _to_head_rows(cache_mem_v.reshape(depth, ns, n_mem, d))
    sc, slc, sh = [], [], []
    for l, lp in enumerate(layers):
        x, p_new, xb_new, h_new = _mixer_sample(x, state_shortconv[l], state_lru_conv[l], state_lru_h[l], lp)
        q = _norm_matmul(x, lp['gx'], lp['wq'], ns)
        o = _xattn_sample(q, k8, v8, l)
        x = _matmul_residual(o, lp['wo'], x, ns)
        x = _ffn(x, lp['gffn'], *lp['ffw'], tm=ns, w_router=lp['wr'], g_final=lp['gfin'])
        sc.append(jnp.stack([state_shortconv[l][:, 1], p_new], axis=1))
        slc.append(jnp.stack([state_lru_conv[l][:, 1], state_lru_conv[l][:, 2], xb_new], axis=1))
        sh.append(h_new)
    y_sample = x.reshape(ns, 1, d)

    return (y_prompt, y_sample, jnp.stack(pc), jnp.stack(plc), jnp.stack(ph), p_mem_k, p_mem_v,
            jnp.stack(sc), jnp.stack(slc), jnp.stack(sh))
```

```python
import functools

import jax
import jax.numpy as jnp
from jax import lax
from jax.experimental import pallas as pl
from jax.experimental.pallas import tpu as pltpu

F32 = jnp.float32
BF16 = jnp.bfloat16

EPS = 1e-6
LRU_C = 8.0
D_CONV = 512
D_LRU = 512
N_LRU_HEADS = 8
LRU_HEAD_DIM = 64
LRU_GROUP = 256
N_MEM_HEADS = 4
MEM_HEAD_DIM = 256
N_EXPERTS = 8
SUBLANES = 8
LANES = 128
VMEM_LIMIT_BYTES = 56 * 1024 * 1024


def _cparams(*sem):
    return pltpu.CompilerParams(dimension_semantics=sem, vmem_limit_bytes=VMEM_LIMIT_BYTES)


def _rms(x, g):
    return x * lax.rsqrt(jnp.mean(x * x, axis=-1, keepdims=True) + EPS) * g


def _dot(a, b):
    return jnp.dot(a, b, preferred_element_type=F32)


def _lru_gates(xc, wg_ref, ba, bx, lam):
    neg_lam = -lam
    softplus = jnp.maximum(neg_lam, 0.0) + jnp.log1p(jnp.exp(-jnp.abs(neg_lam)))
    a_parts, u_parts = [], []
    for g in range(D_LRU // LRU_GROUP):
        sl = slice(g * LRU_GROUP, (g + 1) * LRU_GROUP)
        xg = xc[:, sl]
        gates = _dot(xg.astype(BF16), wg_ref[g])
        r = jax.nn.sigmoid(gates[:, :LRU_GROUP] + ba[:, sl])
        i = jax.nn.sigmoid(gates[:, LRU_GROUP:] + bx[:, sl])
        log_a = (-LRU_C) * r * softplus[:, sl]
        th = jnp.tanh(log_a)
        one_minus_a2 = (-2.0 * th) / (1.0 - th)
        a_parts.append(jnp.exp(log_a))
        u_parts.append(jnp.sqrt(one_minus_a2) * (i * xg))
    return jnp.concatenate(a_parts, axis=-1), jnp.concatenate(u_parts, axis=-1)


def _mix_out(x, ya, yb, gco, glo, wout_ref):
    y = jnp.concatenate([_rms(ya, gco), _rms(yb, glo)], axis=-1).astype(BF16)
    return x + _dot(y, wout_ref[...])


MAX_CONV_HISTORY = 3


def _conv_taps(ext_ref, w_ref, rows, nb):
    width = w_ref.shape[0]
    hist = MAX_CONV_HISTORY * nb
    acc = ext_ref[hist:hist + rows, :] * w_ref[width - 1:width, :]
    for s in range(1, width):
        acc = acc + ext_ref[hist - s * nb:hist - s * nb + rows, :] * w_ref[width - 1 - s:width - s, :]
    return acc


def _mixer_prompt_kernel(x_hbm, gmix_ref, win_ref, wsc_ref, wlc_ref, blc_ref, wg_ref, ba_ref, bx_ref,
                         lam_ref, gco_ref, glo_ref, wout_ref,
                         xo_hbm, tailp_ref, tailx_ref, hout_ref,
                         xin_ref, xout_ref, in_sem, out_sem, pext_ref, xext_ref, h_ref, a_ref, u_ref):
    _, tt, nb, d = xin_ref.shape
    rows = tt * nb
    hist = MAX_CONV_HISTORY * nb
    i = pl.program_id(0)
    n_steps = pl.num_programs(0)
    slot = i % 2

    def in_copies(step, s):
        return [pltpu.make_async_copy(x_hbm.at[b, pl.ds(step * tt, tt), :], xin_ref.at[s, :, b, :],
                                      in_sem.at[s]) for b in range(nb)]

    def out_copies(step, s):
        return [pltpu.make_async_copy(xout_ref.at[s, :, b, :], xo_hbm.at[b, pl.ds(step * tt, tt), :],
                                      out_sem.at[s]) for b in range(nb)]

    @pl.when(i == 0)
    def _():
        pext_ref[0:hist, :] = jnp.zeros((hist, pext_ref.shape[1]), F32)
        xext_ref[0:hist, :] = jnp.zeros((hist, xext_ref.shape[1]), F32)
        h_ref[...] = jnp.zeros_like(h_ref)
        for cp in in_copies(0, 0):
            cp.start()

    @pl.when(i + 1 < n_steps)
    def _():
        for cp in in_copies(i + 1, 1 - slot):
            cp.start()

    for cp in in_copies(i, slot):
        cp.wait()
    x = xin_ref[slot].reshape(rows, d)
    z = _dot(_rms(x, gmix_ref[...]).astype(BF16), win_ref[...])
    gb = z[:, 0:D_CONV]
    gate = z[:, 3 * D_CONV + D_LRU:]
    pext_ref[hist:hist + rows, :] = z[:, D_CONV:2 * D_CONV] * z[:, 2 * D_CONV:3 * D_CONV]
    xext_ref[hist:hist + rows, :] = z[:, 3 * D_CONV:3 * D_CONV + D_LRU]

    ya = gb * _conv_taps(pext_ref, wsc_ref, rows, nb)
    xc = _conv_taps(xext_ref, wlc_ref, rows, nb) + blc_ref[...]
    for ext_ref, tail_ref in ((pext_ref, tailp_ref), (xext_ref, tailx_ref)):
        tail = ext_ref[rows:rows + hist, :]
        ext_ref[0:hist, :] = tail
        tail_ref[...] = tail

    a, u = _lru_gates(xc, wg_ref, ba_ref[...], bx_ref[...], lam_ref[...])
    a_ref[...] = a
    u_ref[...] = u

    def step(t, h):
        grp = pl.ds(pl.multiple_of(t * nb, nb), nb)
        h = a_ref[grp, :] * h + u_ref[grp, :]
        u_ref[grp, :] = h
        return h

    h_last = lax.fori_loop(0, tt, step, h_ref[...], unroll=8)
    h_ref[...] = h_last
    hout_ref[...] = h_last

    yb = u_ref[...] * jax.nn.gelu(gate)
    out = _mix_out(x, ya, yb, gco_ref[...], glo_ref[...], wout_ref)

    @pl.when(i >= 2)
    def _():
        for cp in out_copies(i - 2, slot):
            cp.wait()

    xout_ref[slot] = out.reshape(tt, nb, d)
    for cp in out_copies(i, slot):
        cp.start()

    @pl.when(i == n_steps - 1)
    def _():
        for cp in out_copies(i, slot):
            cp.wait()

        @pl.when(i >= 1)
        def _():
            for cp in out_copies(i - 1, 1 - slot):
                cp.wait()


def _const_spec(shape):
    return pl.BlockSpec(shape, lambda *_: (0,) * len(shape))


def _mixer_prompt(x, p, tt=64):
    nb, t, d = x.shape
    small = [p['gmix'], p['win'], p['wsc'], p['wlc'], p['blc'], p['wg'], p['ba'], p['bx'], p['lam'],
             p['gco'], p['glo'], p['wout']]
    rows = nb * tt
    hist = MAX_CONV_HISTORY * nb
    return pl.pallas_call(
        _mixer_prompt_kernel,
        grid=(t // tt,),
        in_specs=[pl.BlockSpec(memory_space=pl.ANY)] + [_const_spec(a.shape) for a in small],
        out_specs=[pl.BlockSpec(memory_space=pl.ANY),
                   _const_spec((hist, D_CONV)), _const_spec((hist, D_LRU)), _const_spec((nb, D_LRU))],
        out_shape=[jax.ShapeDtypeStruct((nb, t, d), F32),
                   jax.ShapeDtypeStruct((hist, D_CONV), F32),
                   jax.ShapeDtypeStruct((hist, D_LRU), F32),
                   jax.ShapeDtypeStruct((nb, D_LRU), F32)],
        scratch_shapes=[pltpu.VMEM((2, tt, nb, d), F32), pltpu.VMEM((2, tt, nb, d), F32),
                        pltpu.SemaphoreType.DMA((2,)), pltpu.SemaphoreType.DMA((2,)),
                        pltpu.VMEM((hist + rows, D_CONV), F32), pltpu.VMEM((hist + rows, D_LRU), F32),
                        pltpu.VMEM((nb, D_LRU), F32),
                        pltpu.VMEM((rows, D_LRU), F32), pltpu.VMEM((rows, D_LRU), F32)],
        compiler_params=_cparams("arbitrary"),
        name="mixer_prompt",
    )(x, *small)


def _mixer_sample_kernel(x_ref, sc0_ref, sc1_ref, lc0_ref, lc1_ref, lc2_ref, h0_ref,
                         gmix_ref, win_ref, wsc_ref, wlc_ref, blc_ref, wg_ref, ba_ref, bx_ref,
                         lam_ref, gco_ref, glo_ref, wout_ref,
                         xo_ref, pnew_ref, xbnew_ref, hout_ref):
    x = x_ref[...]
    z = _dot(_rms(x, gmix_ref[...]).astype(BF16), win_ref[...])
    gb = z[:, 0:D_CONV]
    p = z[:, D_CONV:2 * D_CONV] * z[:, 2 * D_CONV:3 * D_CONV]
    xb = z[:, 3 * D_CONV:3 * D_CONV + D_LRU]
    gate = z[:, 3 * D_CONV + D_LRU:]

    ya = gb * (sc0_ref[...] * wsc_ref[0:1, :] + sc1_ref[...] * wsc_ref[1:2, :] + p * wsc_ref[2:3, :])
    xc = (lc0_ref[...] * wlc_ref[0:1, :] + lc1_ref[...] * wlc_ref[1:2, :] + lc2_ref[...] * wlc_ref[2:3, :]
          + xb * wlc_ref[3:4, :]) + blc_ref[...]
    a, u = _lru_gates(xc, wg_ref, ba_ref[...], bx_ref[...], lam_ref[...])
    h = a * h0_ref[...] + u
    yb = h * jax.nn.gelu(gate)
    xo_ref[...] = _mix_out(x, ya, yb, gco_ref[...], glo_ref[...], wout_ref)
    pnew_ref[...] = p
    xbnew_ref[...] = xb
    hout_ref[...] = h


def _mixer_sample(x, sc, lc, h0, p):
    nb, d = x.shape
    args = [x, sc[:, 0], sc[:, 1], lc[:, 0], lc[:, 1], lc[:, 2], h0,
            p['gmix'], p['win'], p['wsc'], p['wlc'], p['blc'], p['wg'], p['ba'], p['bx'], p['lam'],
            p['gco'], p['glo'], p['wout']]
    return pl.pallas_call(
        _mixer_sample_kernel,
        grid=(1,),
        in_specs=[_const_spec(a.shape) for a in args],
        out_specs=[_const_spec((nb, d)), _const_spec((nb, D_CONV)), _const_spec((nb, D_LRU)),
                   _const_spec((nb, D_LRU))],
        out_shape=[jax.ShapeDtypeStruct((nb, d), F32), jax.ShapeDtypeStruct((nb, D_CONV), F32),
                   jax.ShapeDtypeStruct((nb, D_LRU), F32), jax.ShapeDtypeStruct((nb, D_LRU), F32)],
        compiler_params=_cparams("arbitrary"),
        name="mixer_sample",
    )(*args)


def _norm_matmul_kernel(x_ref, g_ref, w_ref, o_ref):
    o_ref[...] = _dot(_rms(x_ref[...], g_ref[...]).astype(BF16), w_ref[...])


def _norm_matmul(x, g, w, tm):
    r, d = x.shape
    n = w.shape[1]
    return pl.pallas_call(
        _norm_matmul_kernel,
        grid=(r // tm,),
        in_specs=[pl.BlockSpec((tm, d), lambda i: (i, 0)), _const_spec(g.shape), _const_spec(w.shape)],
        out_specs=pl.BlockSpec((tm, n), lambda i: (i, 0)),
        out_shape=jax.ShapeDtypeStruct((r, n), F32),
        compiler_params=_cparams("parallel"),
        name="norm_matmul",
    )(x, g, w)


def _matmul_residual_kernel(a_ref, w_ref, x_ref, o_ref):
    o_ref[...] = x_ref[...] + _dot(a_ref[...].astype(BF16), w_ref[...])


def _matmul_residual(a, w, x, tm):
    r, d = x.shape
    return pl.pallas_call(
        _matmul_residual_kernel,
        grid=(r // tm,),
        in_specs=[pl.BlockSpec((tm, a.shape[1]), lambda i: (i, 0)), _const_spec(w.shape),
                  pl.BlockSpec((tm, d), lambda i: (i, 0))],
        out_specs=pl.BlockSpec((tm, d), lambda i: (i, 0)),
        out_shape=jax.ShapeDtypeStruct((r, d), F32),
        compiler_params=_cparams("parallel"),
        name="matmul_residual",
    )(a, w, x)


def _xattn_prompt_kernel(x_ref, g_ref, wq_ref, k_ref, v_ref, wo_ref, o_ref):
    x = x_ref[...]
    q = _dot(_rms(x, g_ref[...]).astype(BF16), wq_ref[...])
    heads = []
    for h in range(N_MEM_HEADS):
        sl = slice(h * MEM_HEAD_DIM, (h + 1) * MEM_HEAD_DIM)
        kh = k_ref[:, sl]
        vh = v_ref[:, sl]
        s = lax.dot_general(q[:, sl].astype(BF16), kh, (((1,), (1,)), ((), ())),
                            preferred_element_type=F32) * (MEM_HEAD_DIM ** -0.5)
        e = jnp.exp(s - jnp.max(s, axis=-1, keepdims=True))
        pr = e / jnp.sum(e, axis=-1, keepdims=True)
        heads.append(_dot(pr.astype(BF16), vh))
    o = jnp.concatenate(heads, axis=-1).astype(BF16)
    o_ref[...] = x + _dot(o, wo_ref[...])


def _xattn_prompt(x, g, wq, k, v, wo, tq=512):
    nb, t, d = x.shape
    n_mem = k.shape[1]
    return pl.pallas_call(
        _xattn_prompt_kernel,
        grid=(nb, t // tq),
        in_specs=[pl.BlockSpec((None, tq, d), lambda b, i: (b, i, 0)), _const_spec(g.shape),
                  _const_spec(wq.shape),
                  pl.BlockSpec((None, n_mem, d), lambda b, i: (b, 0, 0)),
                  pl.BlockSpec((None, n_mem, d), lambda b, i: (b, 0, 0)),
                  _const_spec(wo.shape)],
        out_specs=pl.BlockSpec((None, tq, d), lambda b, i: (b, i, 0)),
        out_shape=jax.ShapeDtypeStruct((nb, t, d), F32),
        compiler_params=_cparams("parallel", "parallel"),
        name="xattn_prompt",
    )(x, g, wq, k, v, wo)


def _kv_proj_kernel(mem_ref, g_ref, wk_ref, wv_ref, k5_ref, v5_ref, kb_ref, vb_ref):
    bb, n_mem, d = mem_ref.shape
    m = _rms(mem_ref[...].reshape(bb * n_mem, d), g_ref[...]).astype(BF16)
    for w_ref, o5_ref, ob_ref in ((wk_ref, k5_ref, kb_ref), (wv_ref, v5_ref, vb_ref)):
        y = _dot(m, w_ref[...]).reshape(bb, n_mem, d)
        ob_ref[...] = y.astype(BF16)
        for h in range(N_MEM_HEADS):
            o5_ref[:, :, h, :] = y[:, :, h * MEM_HEAD_DIM:(h + 1) * MEM_HEAD_DIM]


def _kv_proj(mem, gmem, wk, wv, bb=2):
    nb, n_mem, d = mem.shape
    depth = wk.shape[0]
    w_spec = pl.BlockSpec((None, d, d), lambda l, i: (l, 0, 0))
    o5_spec = pl.BlockSpec((None, bb, n_mem, N_MEM_HEADS, MEM_HEAD_DIM), lambda l, i: (l, i, 0, 0, 0))
    ob_spec = pl.BlockSpec((None, bb, n_mem, d), lambda l, i: (l, i, 0, 0))
    o5_shape = jax.ShapeDtypeStruct((depth, nb, n_mem, N_MEM_HEADS, MEM_HEAD_DIM), F32)
    ob_shape = jax.ShapeDtypeStruct((depth, nb, n_mem, d), BF16)
    return pl.pallas_call(
        _kv_proj_kernel,
        grid=(depth, nb // bb),
        in_specs=[pl.BlockSpec((bb, n_mem, d), lambda l, i: (i, 0, 0)),
                  pl.BlockSpec((None, 1, d), lambda l, i: (l, 0, 0)), w_spec, w_spec],
        out_specs=[o5_spec, o5_spec, ob_spec, ob_spec],
        out_shape=[o5_shape, o5_shape, ob_shape, ob_shape],
        compiler_params=_cparams("parallel", "parallel"),
        name="kv_proj",
    )(mem, gmem, wk, wv)


LANE_TILES_PER_HEAD = MEM_HEAD_DIM // LANES
HEAD_ROWS = N_MEM_HEADS * LANE_TILES_PER_HEAD


def _to_head_rows(a):
    lead = a.shape[:-1]
    n = len(lead)
    a = a.reshape(*lead, N_MEM_HEADS, LANE_TILES_PER_HEAD, LANES)
    return jnp.swapaxes(a, n, n + 1).reshape(*lead, HEAD_ROWS, LANES)


def _from_head_rows(a):
    lead = a.shape[:-2]
    n = len(lead)
    a = a.reshape(*lead, LANE_TILES_PER_HEAD, N_MEM_HEADS, LANES)
    return jnp.swapaxes(a, n, n + 1).reshape(*lead, N_MEM_HEADS * MEM_HEAD_DIM)


def _xattn_sample_kernel(q_ref, k_ref, v_ref, o_ref):
    for b in range(q_ref.shape[0]):
        part = jnp.sum(k_ref[b] * q_ref[b][None], axis=-1, keepdims=True)
        s = (part + pltpu.roll(part, N_MEM_HEADS, axis=1)) * (MEM_HEAD_DIM ** -0.5)
        e = jnp.exp(s - jnp.max(s, axis=0, keepdims=True))
        pr = e / jnp.sum(e, axis=0, keepdims=True)
        o_ref[b] = jnp.sum(pr * v_ref[b], axis=0)


def _xattn_sample(q, k8, v8, layer, bb=8):
    nb, d = q.shape
    n_mem = k8.shape[2]
    kv_spec = pl.BlockSpec((None, bb, n_mem, HEAD_ROWS, LANES), lambda i: (layer, i, 0, 0, 0))
    out = pl.pallas_call(
        _xattn_sample_kernel,
        grid=(nb // bb,),
        in_specs=[pl.BlockSpec((bb, HEAD_ROWS, LANES), lambda i: (i, 0, 0)), kv_spec, kv_spec],
        out_specs=pl.BlockSpec((bb, HEAD_ROWS, LANES), lambda i: (i, 0, 0)),
        out_shape=jax.ShapeDtypeStruct((nb, HEAD_ROWS, LANES), F32),
        compiler_params=_cparams("parallel"),
        name="xattn_sample",
    )(_to_head_rows(q), k8, v8)
    return _from_head_rows(out)


def _split_bf16(a):
    hi = a.astype(BF16)
    return hi, (a - hi.astype(F32)).astype(BF16)


def _top2(logits):
    n = logits.shape[-1]
    lane = lax.broadcasted_iota(jnp.int32, logits.shape, 1)
    m1 = jnp.max(logits, axis=-1, keepdims=True)
    i1 = jnp.min(jnp.where(logits == m1, lane, n), axis=-1, keepdims=True)
    rest = jnp.where(lane == i1, -jnp.inf, logits)
    m2 = jnp.max(rest, axis=-1, keepdims=True)
    i2 = jnp.min(jnp.where(rest == m2, lane, n), axis=-1, keepdims=True)
    e2 = jnp.exp(m2 - m1)
    denom = 1.0 + e2
    return i1, i2, 1.0 / denom, e2 / denom


def _ffn_kernel(x_ref, g_ref, *rest, routed, final):
    if routed:
        wr_ref, rest = rest[0], rest[1:]
    if final:
        gfin_ref, rest = rest[0], rest[1:]
    wg_ref, wu_ref, wd_ref, o_ref, h_ref, acc_ref = rest[:6]
    e = pl.program_id(1)

    @pl.when(e == 0)
    def _():
        x = x_ref[...]
        h = _rms(x, g_ref[...])
        h_ref[...] = h.astype(BF16)
        acc_ref[...] = x
        if routed:
            i1_ref, i2_ref, g1_ref, g2_ref = rest[6:]
            h_hi, h_lo = _split_bf16(h)
            w_hi, w_lo = _split_bf16(wr_ref[...])
            logits = _dot(h_hi, w_hi) + (_dot(h_hi, w_lo) + _dot(h_lo, w_hi))
            lane = lax.broadcasted_iota(jnp.int32, logits.shape, 1)
            logits = jnp.where(lane < N_EXPERTS, logits, -jnp.inf)
            i1_ref[...], i2_ref[...], g1_ref[...], g2_ref[...] = _top2(logits)

    hb = h_ref[...]
    gt = _dot(hb, wg_ref[...])
    up = _dot(hb, wu_ref[...])
    y = _dot((jax.nn.silu(gt) * up).astype(BF16), wd_ref[...])
    if routed:
        i1_ref, i2_ref, g1_ref, g2_ref = rest[6:]
        c = jnp.where(i1_ref[...] == e, g1_ref[...], 0.0) + jnp.where(i2_ref[...] == e, g2_ref[...], 0.0)
        y = c * y
    acc_ref[...] += y

    @pl.when(e == pl.num_programs(1) - 1)
    def _():
        out = acc_ref[...]
        o_ref[...] = _rms(out, gfin_ref[...]) if final else out


def _ffn(x, g, wg, wu, wd, tm, w_router=None, g_final=None):
    r, d = x.shape
    n_e, _, f = wg.shape
    routed, final = w_router is not None, g_final is not None
    args, specs = [x, g], [pl.BlockSpec((tm, d), lambda i, e: (i, 0)), _const_spec(g.shape)]
    if routed:
        args.append(w_router)
        specs.append(_const_spec(w_router.shape))
    if final:
        args.append(g_final)
        specs.append(_const_spec(g_final.shape))
    args += [wg, wu, wd]
    specs += [pl.BlockSpec((None, d, f), lambda i, e: (e, 0, 0)),
              pl.BlockSpec((None, d, f), lambda i, e: (e, 0, 0)),
              pl.BlockSpec((None, f, d), lambda i, e: (e, 0, 0))]
    scratch = [pltpu.VMEM((tm, d), BF16), pltpu.VMEM((tm, d), F32)]
    if routed:
        scratch += [pltpu.VMEM((tm, 1), jnp.int32), pltpu.VMEM((tm, 1), jnp.int32),
                    pltpu.VMEM((tm, 1), F32), pltpu.VMEM((tm, 1), F32)]
    return pl.pallas_call(
        functools.partial(_ffn_kernel, routed=routed, final=final),
        grid=(r // tm, n_e),
        in_specs=specs,
        out_specs=pl.BlockSpec((tm, d), lambda i, e: (i, 0)),
        out_shape=jax.ShapeDtypeStruct((r, d), F32),
        scratch_shapes=scratch,
        compiler_params=_cparams("parallel", "arbitrary"),
        name="ffn_moe" if routed else "ffn_dense",
    )(*args)


MOE_TILE = 512
TOP_K = 2
ROW_ALIGN = SUBLANES
COMPACT_ROWS = -(-(TOP_K * MOE_TILE + N_EXPERTS * (ROW_ALIGN - 1)) // (2 * SUBLANES)) * (2 * SUBLANES)
SEGMENT_BITS = (MOE_TILE // ROW_ALIGN).bit_length()


def _route_kernel(x_ref, g_ref, wrt_ref, earlier_ref, lp_ref, col_ref, cnt_ref):
    tk = x_ref.shape[0]
    n_e = wrt_ref.shape[0]
    h = _rms(x_ref[...], g_ref[...])
    h_hi, h_lo = _split_bf16(h)
    w_hi, w_lo = _split_bf16(wrt_ref[...])
    nt = (((1,), (1,)), ((), ()))
    dg = lambda a, b: lax.dot_general(a, b, nt, preferred_element_type=F32)
    logits = dg(w_hi, h_hi) + (dg(w_hi, h_lo) + dg(w_lo, h_hi))
    sub = lax.broadcasted_iota(jnp.int32, logits.shape, 0)
    m1 = jnp.max(logits, axis=0, keepdims=True)
    i1 = jnp.min(jnp.where(logits == m1, sub, n_e), axis=0, keepdims=True)
    rest = jnp.where(sub == i1, -jnp.inf, logits)
    m2 = jnp.max(rest, axis=0, keepdims=True)
    i2 = jnp.min(jnp.where(rest == m2, sub, n_e), axis=0, keepdims=True)
    e2 = jnp.exp(m2 - m1)
    denom = 1.0 + e2

    pick1 = jnp.where(sub == i1, 1.0, 0.0)
    pick2 = jnp.where(sub == i2, 1.0, 0.0)
    cnt = pick1 + pick2
    rank = _dot(cnt.astype(BF16), earlier_ref[...])
    n = jnp.sum(cnt, axis=1, keepdims=True)
    n_pad = jnp.floor((n + (ROW_ALIGN - 1)) * (1.0 / ROW_ALIGN)) * ROW_ALIGN
    base, acc = [], jnp.zeros((1, 1), F32)
    for e in range(n_e):
        base.append(acc)
        acc = acc + n_pad[e:e + 1, :]
    pos = rank + jnp.concatenate(base, axis=0)
    lp1 = jnp.sum(pick1 * pos, axis=0, keepdims=True)
    lp2 = jnp.sum(pick2 * pos, axis=0, keepdims=True)
    lp_ref[0:1, :] = lp1.astype(jnp.int32)
    lp_ref[1:2, :] = lp2.astype(jnp.int32)
    info = jnp.concatenate([lp1, lp2, 1.0 / denom, e2 / denom, jnp.zeros((SUBLANES - 4, tk), F32)], axis=0)
    col_ref[...] = info.T
    cnt_ref[...] = jnp.broadcast_to(n, cnt_ref.shape)


def _route(x, g, wrt):
    t, d = x.shape
    n_e = wrt.shape[0]
    n_tiles = t // MOE_TILE
    idx = jnp.arange(MOE_TILE)
    earlier = (idx[:, None] < idx[None, :]).astype(BF16)
    return pl.pallas_call(
        _route_kernel,
        grid=(n_tiles,),
        in_specs=[pl.BlockSpec((MOE_TILE, d), lambda j: (j, 0)), _const_spec(g.shape), _const_spec(wrt.shape),
                  _const_spec(earlier.shape)],
        out_specs=[pl.BlockSpec((None, TOP_K, MOE_TILE), lambda j: (j, 0, 0)),
                   pl.BlockSpec((MOE_TILE, SUBLANES), lambda j: (j, 0)),
                   pl.BlockSpec((None, n_e, LANES), lambda j: (j, 0, 0))],
        out_shape=[jax.ShapeDtypeStruct((n_tiles, TOP_K, MOE_TILE), jnp.int32),
                   jax.ShapeDtypeStruct((t, SUBLANES), F32),
                   jax.ShapeDtypeStruct((n_tiles, n_e, LANES), F32)],
        compiler_params=_cparams("parallel"),
        name="moe_route",
    )(x, g, wrt, earlier)


def _segment_copies(src_ref, src_row, dst_ref, dst_row, groups, sem):
    out = []
    for bit in range(SEGMENT_BITS):
        rows = ROW_ALIGN << bit
        before = ((groups >> (bit + 1)) << (bit + 1)) * ROW_ALIGN
        cp = pltpu.make_async_copy(
            src_ref.at[pl.ds(pl.multiple_of(src_row + before, ROW_ALIGN), rows), :],
            dst_ref.at[pl.ds(pl.multiple_of(dst_row + before, ROW_ALIGN), rows), :], sem)
        out.append((((groups >> bit) & 1) == 1, cp))
    return out


def _start_all(copies):
    for pred, cp in copies:
        pl.when(pred)(cp.start)


def _wait_all(copies):
    for pred, cp in copies:
        pl.when(pred)(cp.wait)


def _dispatch_kernel(grp_ref, off_ref, dst_ref, total_ref, x_ref, g_ref, lp_ref, xs_ref, comp_ref, sem):
    j = pl.program_id(0)
    n_steps = pl.num_programs(0)
    n_e = grp_ref.shape[1]
    slot = j % 2

    def copies(step, s):
        out = []
        for e in range(n_e):
            out += _segment_copies(comp_ref.at[s], off_ref[step, e], xs_ref, dst_ref[step, e],
                                   grp_ref[step, e], sem.at[s])
        return out

    h = _rms(x_ref[...], g_ref[...]).astype(BF16)
    row = lax.broadcasted_iota(jnp.int32, (COMPACT_ROWS, h.shape[0]), 0)
    onehot = jnp.where(row == lp_ref[0:1, :], 1.0, jnp.where(row == lp_ref[1:2, :], 1.0, 0.0)).astype(BF16)
    compact = _dot(onehot, h)

    @pl.when(j >= 2)
    def _():
        _wait_all(copies(j - 2, slot))

    comp_ref[slot] = compact
    _start_all(copies(j, slot))

    @pl.when(j == n_steps - 1)
    def _():
        _wait_all(copies(j, slot))

        @pl.when(j >= 1)
        def _():
            _wait_all(copies(j - 1, 1 - slot))

        comp_ref[slot, 0:MOE_TILE, :] = jnp.zeros((MOE_TILE, comp_ref.shape[2]), F32)
        groups_left = (xs_ref.shape[0] - total_ref[0]) // ROW_ALIGN
        tail = []
        min_total = TOP_K * x_ref.shape[0] * grp_ref.shape[0]
        for c in range(-(-(xs_ref.shape[0] - min_total) // MOE_TILE)):
            tail += _segment_copies(comp_ref.at[slot], 0, xs_ref, total_ref[0] + c * MOE_TILE,
                                    jnp.clip(groups_left - c * (MOE_TILE // ROW_ALIGN), 0, MOE_TILE // ROW_ALIGN),
                                    sem.at[slot])
        _start_all(tail)
        _wait_all(tail)


def _dispatch(x, g, lp, groups, off, dst, total, n_rows):
    t, d = x.shape
    n_tiles = t // MOE_TILE
    grid_spec = pltpu.PrefetchScalarGridSpec(
        num_scalar_prefetch=4,
        grid=(n_tiles,),
        in_specs=[pl.BlockSpec((MOE_TILE, d), lambda j, *_: (j, 0)),
                  pl.BlockSpec(g.shape, lambda j, *_: (0, 0)),
                  pl.BlockSpec((None, TOP_K, MOE_TILE), lambda j, *_: (j, 0, 0))],
        out_specs=pl.BlockSpec(memory_space=pl.ANY),
        scratch_shapes=[pltpu.VMEM((2, COMPACT_ROWS, d), F32), pltpu.SemaphoreType.DMA((2,))])
    return pl.pallas_call(
        _dispatch_kernel,
        grid_spec=grid_spec,
        out_shape=jax.ShapeDtypeStruct((n_rows, d), F32),
        compiler_params=_cparams("arbitrary"),
        name="moe_dispatch",
    )(groups, off, dst, total, x, g, lp)


STEP_SKIP, STEP_FFN, STEP_ZERO = 0, 1, 2


def _grouped_ffn_kernel(tile_ref, grp_ref, lo_ref, hi_ref, first_ref, mode_ref,
                        xs_ref, wg_ref, wu_ref, wd_ref, ys_ref):
    w = pl.program_id(0)
    tm = xs_ref.shape[0]

    @pl.when(mode_ref[w] == STEP_ZERO)
    def _():
        ys_ref[...] = jnp.zeros_like(ys_ref)

    @pl.when(mode_ref[w] == STEP_FFN)
    def _():
        x = xs_ref[...].astype(BF16)
        gt = _dot(x, wg_ref[...])
        up = _dot(x, wu_ref[...])
        y = _dot((jax.nn.silu(gt) * up).astype(BF16), wd_ref[...])

        @pl.when(first_ref[w] == 1)
        def _():
            ys_ref[...] = y

        @pl.when(first_ref[w] == 0)
        def _():
            r = lax.broadcasted_iota(jnp.int32, (tm, 1), 0)
            mine = jnp.logical_and(r >= lo_ref[w], r < hi_ref[w])
            ys_ref[...] = jnp.where(mine, y, ys_ref[...])


def _grouped_ffn(xs, wg, wu, wd, tile, grp, lo, hi, first, mode):
    n_e, d, f = wg.shape
    grid_spec = pltpu.PrefetchScalarGridSpec(
        num_scalar_prefetch=6,
        grid=(tile.shape[0],),
        in_specs=[pl.BlockSpec((MOE_TILE, d), lambda w, tile, *_: (tile[w], 0)),
                  pl.BlockSpec((None, d, f), lambda w, tile, grp, *_: (grp[w], 0, 0)),
                  pl.BlockSpec((None, d, f), lambda w, tile, grp, *_: (grp[w], 0, 0)),
                  pl.BlockSpec((None, f, d), lambda w, tile, grp, *_: (grp[w], 0, 0))],
        out_specs=pl.BlockSpec((MOE_TILE, d), lambda w, tile, *_: (tile[w], 0)))
    return pl.pallas_call(
        _grouped_ffn_kernel,
        grid_spec=grid_spec,
        out_shape=jax.ShapeDtypeStruct(xs.shape, F32),
        compiler_params=_cparams("arbitrary"),
        name="moe_grouped_ffn",
    )(tile, grp, lo, hi, first, mode, xs, wg, wu, wd)


def _combine_kernel(grp_ref, off_ref, dst_ref, x_ref, col_ref, *rest, final):
    if final:
        gfin_ref, rest = rest[0], rest[1:]
    ys_ref, o_ref, comp_ref, sem = rest
    j = pl.program_id(0)
    n_steps = pl.num_programs(0)
    n_e = grp_ref.shape[1]
    slot = j % 2

    def copies(step, s):
        out = []
        for e in range(n_e):
            out += _segment_copies(ys_ref, dst_ref[step, e], comp_ref.at[s], off_ref[step, e],
                                   grp_ref[step, e], sem.at[s])
        return out

    @pl.when(j == 0)
    def _():
        comp_ref[...] = jnp.zeros_like(comp_ref)
        _start_all(copies(0, 0))

    @pl.when(j + 1 < n_steps)
    def _():
        _start_all(copies(j + 1, 1 - slot))

    _wait_all(copies(j, slot))
    y = comp_ref[slot].astype(BF16)
    col = col_ref[...]
    row = lax.broadcasted_iota(jnp.int32, (col.shape[0], COMPACT_ROWS), 1)
    out = x_ref[...]
    for k in range(TOP_K):
        pick = jnp.where(row == col[:, k:k + 1].astype(jnp.int32), 1.0, 0.0).astype(BF16)
        out = out + col[:, TOP_K + k:TOP_K + k + 1] * _dot(pick, y)
    o_ref[...] = _rms(out, gfin_ref[...]) if final else out


def _combine(x, col, ys, groups, off, dst, g_final):
    t, d = x.shape
    n_tiles = t // MOE_TILE
    final = g_final is not None
    args = [x, col]
    specs = [pl.BlockSpec((MOE_TILE, d), lambda j, *_: (j, 0)),
             pl.BlockSpec((MOE_TILE, SUBLANES), lambda j, *_: (j, 0))]
    if final:
        args.append(g_final)
        specs.append(pl.BlockSpec(g_final.shape, lambda j, *_: (0, 0)))
    args.append(ys)
    specs.append(pl.BlockSpec(memory_space=pl.ANY))
    grid_spec = pltpu.PrefetchScalarGridSpec(
        num_scalar_prefetch=3,
        grid=(n_tiles,),
        in_specs=specs,
        out_specs=pl.BlockSpec((MOE_TILE, d), lambda j, *_: (j, 0)),
        scratch_shapes=[pltpu.VMEM((2, COMPACT_ROWS, d), F32), pltpu.SemaphoreType.DMA((2,))])
    return pl.pallas_call(
        functools.partial(_combine_kernel, final=final),
        grid_spec=grid_spec,
        out_shape=jax.ShapeDtypeStruct((t, d), F32),
        compiler_params=_cparams("arbitrary"),
        name="moe_combine",
    )(groups, off, dst, *args)


def _moe_routed(x, g, w_router, wg, wu, wd, g_final):
    t, _ = x.shape
    n_e = wg.shape[0]
    n_tiles = t // MOE_TILE
    lp, col, cnt = _route(x, g, w_router.T)
    groups = (cnt[:, :, 0].astype(jnp.int32) + (ROW_ALIGN - 1)) // ROW_ALIGN
    n = groups * ROW_ALIGN
    off = jnp.cumsum(n, axis=1) - n
    per_e = jnp.sum(n, axis=0)
    ends = jnp.cumsum(per_e)
    starts = ends - per_e
    dst = starts[None, :] + jnp.cumsum(n, axis=0) - n
    max_rows = TOP_K * t + n_tiles * n_e * (ROW_ALIGN - 1)
    n_row_tiles = -(-max_rows // MOE_TILE)
    xs = _dispatch(x, g, lp, groups, off, dst, ends[-1:], (n_row_tiles + 1) * MOE_TILE)

    n_steps = n_row_tiles + n_e
    first_tile = starts // MOE_TILE
    n_items = jnp.where(per_e > 0, (ends - 1) // MOE_TILE - first_tile + 1, 0)
    item_end = jnp.cumsum(n_items)
    n_ffn = item_end[-1]
    w = jnp.arange(n_steps, dtype=jnp.int32)
    wc = jnp.minimum(w, n_ffn - 1)
    grp = jnp.sum(wc[:, None] >= item_end[None, :], axis=1).astype(jnp.int32)
    ffn_tile = first_tile[grp] + wc - (item_end - n_items)[grp]
    zero_tile = (ends[-1] - 1) // MOE_TILE + 1 + (w - n_ffn)
    tile = jnp.where(w < n_ffn, ffn_tile, jnp.minimum(zero_tile, n_row_tiles)).astype(jnp.int32)
    mode = jnp.where(w < n_ffn, STEP_FFN, jnp.where(zero_tile <= n_row_tiles, STEP_ZERO, STEP_SKIP))
    lo = jnp.clip(starts[grp] - tile * MOE_TILE, 0, MOE_TILE).astype(jnp.int32)
    hi = jnp.clip(ends[grp] - tile * MOE_TILE, 0, MOE_TILE).astype(jnp.int32)
    first = jnp.concatenate([jnp.ones((1,), jnp.int32), (tile[1:] != tile[:-1]).astype(jnp.int32)])
    ys = _grouped_ffn(xs, wg, wu, wd, tile, grp, lo, hi, first, mode.astype(jnp.int32))
    return _combine(x, col, ys, groups, off, dst, g_final)


def _block_diag_gates(w_a, w_x):
    per_group = LRU_GROUP // LRU_HEAD_DIM
    eye = jnp.eye(per_group, dtype=w_a.dtype)

    def bd(w):
        w = w.reshape(D_LRU // LRU_GROUP, per_group, LRU_HEAD_DIM, LRU_HEAD_DIM)
        return jnp.einsum('ghij,hk->ghikj', w, eye).reshape(D_LRU // LRU_GROUP, LRU_GROUP, LRU_GROUP)

    return jnp.concatenate([bd(w_a), bd(w_x)], axis=-1).astype(BF16)


def _split_ff(w_gate, w_up, w_down, n_split=2):
    d, f = w_gate.shape
    fs = f // n_split
    wg = w_gate.reshape(d, n_split, fs).transpose(1, 0, 2).astype(BF16)
    wu = w_up.reshape(d, n_split, fs).transpose(1, 0, 2).astype(BF16)
    wd = w_down.reshape(n_split, fs, d).astype(BF16)
    return wg, wu, wd


def kernel(x_prompt, x_sample, state_shortconv, state_lru_conv, state_lru_h, cache_mem_k, cache_mem_v,
           mem_prompt, norm_mix, w_in, w_short_conv, w_lru_conv, b_lru_conv, w_lru_a, b_lru_a, w_lru_x,
           b_lru_x, lru_lambda, norm_conv_out, norm_lru_out, w_mix_out, norm_xattn, norm_mem, w_q, w_k,
           w_v, w_o, norm_ffn, w_ff_gate, w_ff_up, w_ff_down, w_router, w_moe_gate, w_moe_up, w_moe_down,
           norm_final):
    depth = w_in.shape[0]
    nb, t, d = x_prompt.shape
    ns = x_sample.shape[0]
    n_mem = mem_prompt.shape[1]
    row = lambda a: a.reshape(1, -1)

    layers = []
    for l in range(depth):
        lp = dict(
            gmix=row(norm_mix[l]), win=w_in[l].astype(BF16), wsc=w_short_conv[l], wlc=w_lru_conv[l],
            blc=row(b_lru_conv[l]), wg=_block_diag_gates(w_lru_a[l], w_lru_x[l]), ba=row(b_lru_a[l]),
            bx=row(b_lru_x[l]), lam=row(lru_lambda[l]), gco=row(norm_conv_out[l]), glo=row(norm_lru_out[l]),
            wout=w_mix_out[l].astype(BF16), gx=row(norm_xattn[l]),
            wq=w_q[l].astype(BF16), wo=w_o[l].astype(BF16),
            gffn=row(norm_ffn[l]), gfin=row(norm_final) if l == depth - 1 else None)
        j = l // 2
        if l % 2 == 0:
            lp['ffw'] = _split_ff(w_ff_gate[j], w_ff_up[j], w_ff_down[j])
            lp['wr'] = None
        else:
            lp['ffw'] = (w_moe_gate[j].astype(BF16), w_moe_up[j].astype(BF16), w_moe_down[j].astype(BF16))
            lp['wr'] = jnp.pad(w_router[j], ((0, 0), (0, LANES - N_EXPERTS)))
            lp['wr_raw'] = w_router[j]
        layers.append(lp)

    x = x_prompt
    p_mem_k, p_mem_v, kb, vb = _kv_proj(mem_prompt, norm_mem.reshape(depth, 1, d), w_k.astype(BF16),
                                        w_v.astype(BF16))
    pc, plc, ph = [], [], []
    for l, lp in enumerate(layers):
        x, tail_p, tail_x, h_last = _mixer_prompt(x, lp)
        x = _xattn_prompt(x, lp['gx'], lp['wq'], kb[l], vb[l], lp['wo'])
        if lp['wr'] is None:
            x = _ffn(x.reshape(nb * t, d), lp['gffn'], *lp['ffw'], tm=512, g_final=lp['gfin'])
        else:
            x = _moe_routed(x.reshape(nb * t, d), lp['gffn'], lp['wr_raw'], *lp['ffw'], lp['gfin'])
        x = x.reshape(nb, t, d)
        tail_p = jnp.swapaxes(tail_p.reshape(MAX_CONV_HISTORY, nb, D_CONV), 0, 1)
        tail_x = jnp.swapaxes(tail_x.reshape(MAX_CONV_HISTORY, nb, D_LRU), 0, 1)
        pc.append(tail_p[:, MAX_CONV_HISTORY - (w_short_conv.shape[1] - 1):])
        plc.append(tail_x[:, MAX_CONV_HISTORY - (w_lru_conv.shape[1] - 1):])
        ph.append(h_last)
    y_prompt = x

    x = x_sample.reshape(ns, d)
    k8 = _to_head_rows(cache_mem_k.reshape(depth, ns, n_mem, d))
    v8 = _to_head_rows(cache_mem_v.reshape(depth, ns, n_mem, d))
    sc, slc, sh = [], [], []
    for l, lp in enumerate(layers):
        x, p_new, xb_new, h_new = _mixer_sample(x, state_shortconv[l], state_lru_conv[l], state_lru_h[l], lp)
        q = _norm_matmul(x, lp['gx'], lp['wq'], ns)
        o = _xattn_sample(q, k8, v8, l)
        x = _matmul_residual(o, lp['wo'], x, ns)
        x = _ffn(x, lp['gffn'], *lp['ffw'], tm=ns, w_router=lp['wr'], g_final=lp['gfin'])
        sc.append(jnp.stack([state_shortconv[l][:, 1], p_new], axis=1))
        slc.append(jnp.stack([state_lru_conv[l][:, 1], state_lru_conv[l][:, 2], xb_new], axis=1))
        sh.append(h_new)
    y_sample = x.reshape(ns, 1, d)

    return (y_prompt, y_sample, jnp.stack(pc), jnp.stack(plc), jnp.stack(ph), p_mem_k, p_mem_v,
            jnp.stack(sc), jnp.stack(slc), jnp.stack(sh))
```

```python
import functools

import jax
import jax.numpy as jnp
from jax import lax
from jax.experimental import pallas as pl
from jax.experimental.pallas import tpu as pltpu

F32 = jnp.float32
BF16 = jnp.bfloat16

EPS = 1e-6
LRU_C = 8.0
D_CONV = 512
D_LRU = 512
N_LRU_HEADS = 8
LRU_HEAD_DIM = 64
LRU_GROUP = 256
N_MEM_HEADS = 4
MEM_HEAD_DIM = 256
N_EXPERTS = 8
SUBLANES = 8
LANES = 128
VMEM_LIMIT_BYTES = 56 * 1024 * 1024


def _cparams(*sem):
    return pltpu.CompilerParams(dimension_semantics=sem, vmem_limit_bytes=VMEM_LIMIT_BYTES)


def _rms(x, g):
    return x * lax.rsqrt(jnp.mean(x * x, axis=-1, keepdims=True) + EPS) * g


def _dot(a, b):
    return jnp.dot(a, b, preferred_element_type=F32)


def _lru_gates(xc, wg_ref, ba, bx, lam):
    neg_lam = -lam
    softplus = jnp.maximum(neg_lam, 0.0) + jnp.log1p(jnp.exp(-jnp.abs(neg_lam)))
    a_parts, u_parts = [], []
    for g in range(D_LRU // LRU_GROUP):
        sl = slice(g * LRU_GROUP, (g + 1) * LRU_GROUP)
        xg = xc[:, sl]
        gates = _dot(xg.astype(BF16), wg_ref[g])
        r = jax.nn.sigmoid(gates[:, :LRU_GROUP] + ba[:, sl])
        i = jax.nn.sigmoid(gates[:, LRU_GROUP:] + bx[:, sl])
        log_a = (-LRU_C) * r * softplus[:, sl]
        th = jnp.tanh(log_a)
        one_minus_a2 = (-2.0 * th) / (1.0 - th)
        a_parts.append(jnp.exp(log_a))
        u_parts.append(jnp.sqrt(one_minus_a2) * (i * xg))
    return jnp.concatenate(a_parts, axis=-1), jnp.concatenate(u_parts, axis=-1)


def _mix_out(x, ya, yb, gco, glo, wout_ref):
    y = jnp.concatenate([_rms(ya, gco), _rms(yb, glo)], axis=-1).astype(BF16)
    return x + _dot(y, wout_ref[...])


MAX_CONV_HISTORY = 3


def _conv_taps(ext_ref, w_ref, rows, nb):
    width = w_ref.shape[0]
    hist = MAX_CONV_HISTORY * nb
    acc = ext_ref[hist:hist + rows, :] * w_ref[width - 1:width, :]
    for s in range(1, width):
        acc = acc + ext_ref[hist - s * nb:hist - s * nb + rows, :] * w_ref[width - 1 - s:width - s, :]
    return acc


def _mixer_prompt_kernel(x_hbm, gmix_ref, win_ref, wsc_ref, wlc_ref, blc_ref, wg_ref, ba_ref, bx_ref,
                         lam_ref, gco_ref, glo_ref, wout_ref,
                         xo_hbm, tailp_ref, tailx_ref, hout_ref,
                         xin_ref, xout_ref, in_sem, out_sem, pext_ref, xext_ref, h_ref, a_ref, u_ref):
    _, tt, nb, d = xin_ref.shape
    rows = tt * nb
    hist = MAX_CONV_HISTORY * nb
    i = pl.program_id(0)
    n_steps = pl.num_programs(0)
    slot = i % 2

    def in_copies(step, s):
        return [pltpu.make_async_copy(x_hbm.at[b, pl.ds(step * tt, tt), :], xin_ref.at[s, :, b, :],
                                      in_sem.at[s]) for b in range(nb)]

    def out_copies(step, s):
        return [pltpu.make_async_copy(xout_ref.at[s, :, b, :], xo_hbm.at[b, pl.ds(step * tt, tt), :],
                                      out_sem.at[s]) for b in range(nb)]

    @pl.when(i == 0)
    def _():
        pext_ref[0:hist, :] = jnp.zeros((hist, pext_ref.shape[1]), F32)
        xext_ref[0:hist, :] = jnp.zeros((hist, xext_ref.shape[1]), F32)
        h_ref[...] = jnp.zeros_like(h_ref)
        for cp in in_copies(0, 0):
            cp.start()

    @pl.when(i + 1 < n_steps)
    def _():
        for cp in in_copies(i + 1, 1 - slot):
            cp.start()

    for cp in in_copies(i, slot):
        cp.wait()
    x = xin_ref[slot].reshape(rows, d)
    z = _dot(_rms(x, gmix_ref[...]).astype(BF16), win_ref[...])
    gb = z[:, 0:D_CONV]
    gate = z[:, 3 * D_CONV + D_LRU:]
    pext_ref[hist:hist + rows, :] = z[:, D_CONV:2 * D_CONV] * z[:, 2 * D_CONV:3 * D_CONV]
    xext_ref[hist:hist + rows, :] = z[:, 3 * D_CONV:3 * D_CONV + D_LRU]

    ya = gb * _conv_taps(pext_ref, wsc_ref, rows, nb)
    xc = _conv_taps(xext_ref, wlc_ref, rows, nb) + blc_ref[...]
    for ext_ref, tail_ref in ((pext_ref, tailp_ref), (xext_ref, tailx_ref)):
        tail = ext_ref[rows:rows + hist, :]
        ext_ref[0:hist, :] = tail
        tail_ref[...] = tail

    a, u = _lru_gates(xc, wg_ref, ba_ref[...], bx_ref[...], lam_ref[...])
    a_ref[...] = a
    u_ref[...] = u

    def step(t, h):
        grp = pl.ds(pl.multiple_of(t * nb, nb), nb)
        h = a_ref[grp, :] * h + u_ref[grp, :]
        u_ref[grp, :] = h
        return h

    h_last = lax.fori_loop(0, tt, step, h_ref[...], unroll=8)
    h_ref[...] = h_last
    hout_ref[...] = h_last

    yb = u_ref[...] * jax.nn.gelu(gate)
    out = _mix_out(x, ya, yb, gco_ref[...], glo_ref[...], wout_ref)

    @pl.when(i >= 2)
    def _():
        for cp in out_copies(i - 2, slot):
            cp.wait()

    xout_ref[slot] = out.reshape(tt, nb, d)
    for cp in out_copies(i, slot):
        cp.start()

    @pl.when(i == n_steps - 1)
    def _():
        for cp in out_copies(i, slot):
            cp.wait()

        @pl.when(i >= 1)
        def _():
            for cp in out_copies(i - 1, 1 - slot):
                cp.wait()


def _const_spec(shape):
    return pl.BlockSpec(shape, lambda *_: (0,) * len(shape))


def _mixer_prompt(x, p, tt=64):
    nb, t, d = x.shape
    small = [p['gmix'], p['win'], p['wsc'], p['wlc'], p['blc'], p['wg'], p['ba'], p['bx'], p['lam'],
             p['gco'], p['glo'], p['wout']]
    rows = nb * tt
    hist = MAX_CONV_HISTORY * nb
    return pl.pallas_call(
        _mixer_prompt_kernel,
        grid=(t // tt,),
        in_specs=[pl.BlockSpec(memory_space=pl.ANY)] + [_const_spec(a.shape) for a in small],
        out_specs=[pl.BlockSpec(memory_space=pl.ANY),
                   _const_spec((hist, D_CONV)), _const_spec((hist, D_LRU)), _const_spec((nb, D_LRU))],
        out_shape=[jax.ShapeDtypeStruct((nb, t, d), F32),
                   jax.ShapeDtypeStruct((hist, D_CONV), F32),
                   jax.ShapeDtypeStruct((hist, D_LRU), F32),
                   jax.ShapeDtypeStruct((nb, D_LRU), F32)],
        scratch_shapes=[pltpu.VMEM((2, tt, nb, d), F32), pltpu.VMEM((2, tt, nb, d), F32),
                        pltpu.SemaphoreType.DMA((2,)), pltpu.SemaphoreType.DMA((2,)),
                        pltpu.VMEM((hist + rows, D_CONV), F32), pltpu.VMEM((hist + rows, D_LRU), F32),
                        pltpu.VMEM((nb, D_LRU), F32),
                        pltpu.VMEM((rows, D_LRU), F32), pltpu.VMEM((rows, D_LRU), F32)],
        compiler_params=_cparams("arbitrary"),
        name="mixer_prompt",
    )(x, *small)


def _mixer_sample_kernel(x_ref, sc0_ref, sc1_ref, lc0_ref, lc1_ref, lc2_ref, h0_ref,
                         gmix_ref, win_ref, wsc_ref, wlc_ref, blc_ref, wg_ref, ba_ref, bx_ref,
                         lam_ref, gco_ref, glo_ref, wout_ref,
                         xo_ref, pnew_ref, xbnew_ref, hout_ref):
    x = x_ref[...]
    z = _dot(_rms(x, gmix_ref[...]).astype(BF16), win_ref[...])
    gb = z[:, 0:D_CONV]
    p = z[:, D_CONV:2 * D_CONV] * z[:, 2 * D_CONV:3 * D_CONV]
    xb = z[:, 3 * D_CONV:3 * D_CONV + D_LRU]
    gate = z[:, 3 * D_CONV + D_LRU:]

    ya = gb * (sc0_ref[...] * wsc_ref[0:1, :] + sc1_ref[...] * wsc_ref[1:2, :] + p * wsc_ref[2:3, :])
    xc = (lc0_ref[...] * wlc_ref[0:1, :] + lc1_ref[...] * wlc_ref[1:2, :] + lc2_ref[...] * wlc_ref[2:3, :]
          + xb * wlc_ref[3:4, :]) + blc_ref[...]
    a, u = _lru_gates(xc, wg_ref, ba_ref[...], bx_ref[...], lam_ref[...])
    h = a * h0_ref[...] + u
    yb = h * jax.nn.gelu(gate)
    xo_ref[...] = _mix_out(x, ya, yb, gco_ref[...], glo_ref[...], wout_ref)
    pnew_ref[...] = p
    xbnew_ref[...] = xb
    hout_ref[...] = h


def _mixer_sample(x, sc, lc, h0, p):
    nb, d = x.shape
    args = [x, sc[:, 0], sc[:, 1], lc[:, 0], lc[:, 1], lc[:, 2], h0,
            p['gmix'], p['win'], p['wsc'], p['wlc'], p['blc'], p['wg'], p['ba'], p['bx'], p['lam'],
            p['gco'], p['glo'], p['wout']]
    return pl.pallas_call(
        _mixer_sample_kernel,
        grid=(1,),
        in_specs=[_const_spec(a.shape) for a in args],
        out_specs=[_const_spec((nb, d)), _const_spec((nb, D_CONV)), _const_spec((nb, D_LRU)),
                   _const_spec((nb, D_LRU))],
        out_shape=[jax.ShapeDtypeStruct((nb, d), F32), jax.ShapeDtypeStruct((nb, D_CONV), F32),
                   jax.ShapeDtypeStruct((nb, D_LRU), F32), jax.ShapeDtypeStruct((nb, D_LRU), F32)],
        compiler_params=_cparams("arbitrary"),
        name="mixer_sample",
    )(*args)


def _norm_matmul_kernel(x_ref, g_ref, w_ref, o_ref):
    o_ref[...] = _dot(_rms(x_ref[...], g_ref[...]).astype(BF16), w_ref[...])


def _norm_matmul(x, g, w, tm):
    r, d = x.shape
    n = w.shape[1]
    return pl.pallas_call(
        _norm_matmul_kernel,
        grid=(r // tm,),
        in_specs=[pl.BlockSpec((tm, d), lambda i: (i, 0)), _const_spec(g.shape), _const_spec(w.shape)],
        out_specs=pl.BlockSpec((tm, n), lambda i: (i, 0)),
        out_shape=jax.ShapeDtypeStruct((r, n), F32),
        compiler_params=_cparams("parallel"),
        name="norm_matmul",
    )(x, g, w)


def _matmul_residual_kernel(a_ref, w_ref, x_ref, o_ref):
    o_ref[...] = x_ref[...] + _dot(a_ref[...].astype(BF16), w_ref[...])


def _matmul_residual(a, w, x, tm):
    r, d = x.shape
    return pl.pallas_call(
        _matmul_residual_kernel,
        grid=(r // tm,),
        in_specs=[pl.BlockSpec((tm, a.shape[1]), lambda i: (i, 0)), _const_spec(w.shape),
                  pl.BlockSpec((tm, d), lambda i: (i, 0))],
        out_specs=pl.BlockSpec((tm, d), lambda i: (i, 0)),
        out_shape=jax.ShapeDtypeStruct((r, d), F32),
        compiler_params=_cparams("parallel"),
        name="matmul_residual",
    )(a, w, x)


def _xattn_prompt_kernel(x_ref, g_ref, wq_ref, k_ref, v_ref, wo_ref, o_ref):
    x = x_ref[...]
    q = _dot(_rms(x, g_ref[...]).astype(BF16), wq_ref[...])
    heads = []
    for h in range(N_MEM_HEADS):
        sl = slice(h * MEM_HEAD_DIM, (h + 1) * MEM_HEAD_DIM)
        kh = k_ref[:, sl]
        vh = v_ref[:, sl]
        s = lax.dot_general(q[:, sl].astype(BF16), kh, (((1,), (1,)), ((), ())),
                            preferred_element_type=F32) * (MEM_HEAD_DIM ** -0.5)
        e = jnp.exp(s - jnp.max(s, axis=-1, keepdims=True))
        pr = e / jnp.sum(e, axis=-1, keepdims=True)
        heads.append(_dot(pr.astype(BF16), vh))
    o = jnp.concatenate(heads, axis=-1).astype(BF16)
    o_ref[...] = x + _dot(o, wo_ref[...])


def _xattn_prompt(x, g, wq, k, v, wo, tq=512):
    nb, t, d = x.shape
    n_mem = k.shape[1]
    return pl.pallas_call(
        _xattn_prompt_kernel,
        grid=(nb, t // tq),
        in_specs=[pl.BlockSpec((None, tq, d), lambda b, i: (b, i, 0)), _const_spec(g.shape),
                  _const_spec(wq.shape),
                  pl.BlockSpec((None, n_mem, d), lambda b, i: (b, 0, 0)),
                  pl.BlockSpec((None, n_mem, d), lambda b, i: (b, 0, 0)),
                  _const_spec(wo.shape)],
        out_specs=pl.BlockSpec((None, tq, d), lambda b, i: (b, i, 0)),
        out_shape=jax.ShapeDtypeStruct((nb, t, d), F32),
        compiler_params=_cparams("parallel", "parallel"),
        name="xattn_prompt",
    )(x, g, wq, k, v, wo)


def _kv_proj_kernel(mem_ref, g_ref, wk_ref, wv_ref, k5_ref, v5_ref, kb_ref, vb_ref):
    bb, n_mem, d = mem_ref.shape
    m = _rms(mem_ref[...].reshape(bb * n_mem, d), g_ref[...]).astype(BF16)
    for w_ref, o5_ref, ob_ref in ((wk_ref, k5_ref, kb_ref), (wv_ref, v5_ref, vb_ref)):
        y = _dot(m, w_ref[...]).reshape(bb, n_mem, d)
        ob_ref[...] = y.astype(BF16)
        for h in range(N_MEM_HEADS):
            o5_ref[:, :, h, :] = y[:, :, h * MEM_HEAD_DIM:(h + 1) * MEM_HEAD_DIM]


def _kv_proj(mem, gmem, wk, wv, bb=2):
    nb, n_mem, d = mem.shape
    depth = wk.shape[0]
    w_spec = pl.BlockSpec((None, d, d), lambda l, i: (l, 0, 0))
    o5_spec = pl.BlockSpec((None, bb, n_mem, N_MEM_HEADS, MEM_HEAD_DIM), lambda l, i: (l, i, 0, 0, 0))
    ob_spec = pl.BlockSpec((None, bb, n_mem, d), lambda l, i: (l, i, 0, 0))
    o5_shape = jax.ShapeDtypeStruct((depth, nb, n_mem, N_MEM_HEADS, MEM_HEAD_DIM), F32)
    ob_shape = jax.ShapeDtypeStruct((depth, nb, n_mem, d), BF16)
    return pl.pallas_call(
        _kv_proj_kernel,
        grid=(depth, nb // bb),
        in_specs=[pl.BlockSpec((bb, n_mem, d), lambda l, i: (i, 0, 0)),
                  pl.BlockSpec((None, 1, d), lambda l, i: (l, 0, 0)), w_spec, w_spec],
        out_specs=[o5_spec, o5_spec, ob_spec, ob_spec],
        out_shape=[o5_shape, o5_shape, ob_shape, ob_shape],
        compiler_params=_cparams("parallel", "parallel"),
        name="kv_proj",
    )(mem, gmem, wk, wv)


LANE_TILES_PER_HEAD = MEM_HEAD_DIM // LANES
HEAD_ROWS = N_MEM_HEADS * LANE_TILES_PER_HEAD


def _to_head_rows(a):
    lead = a.shape[:-1]
    n = len(lead)
    a = a.reshape(*lead, N_MEM_HEADS, LANE_TILES_PER_HEAD, LANES)
    return jnp.swapaxes(a, n, n + 1).reshape(*lead, HEAD_ROWS, LANES)


def _from_head_rows(a):
    lead = a.shape[:-2]
    n = len(lead)
    a = a.reshape(*lead, LANE_TILES_PER_HEAD, N_MEM_HEADS, LANES)
    return jnp.swapaxes(a, n, n + 1).reshape(*lead, N_MEM_HEADS * MEM_HEAD_DIM)


def _xattn_sample_kernel(q_ref, k_ref, v_ref, o_ref):
    n_mem = k_ref.shape[1]
    ones = jnp.ones((LANES, LANES), BF16)
    for b in range(q_ref.shape[0]):
        prod = (k_ref[b] * q_ref[b][None]).reshape(n_mem * HEAD_ROWS, LANES).astype(BF16)
        part = _dot(prod, ones).reshape(n_mem, HEAD_ROWS, LANES)
        s = (part + pltpu.roll(part, N_MEM_HEADS, axis=1)) * (MEM_HEAD_DIM ** -0.5)
        e = jnp.exp(s - jnp.max(s, axis=0, keepdims=True))
        pr = e / jnp.sum(e, axis=0, keepdims=True)
        o_ref[b] = jnp.sum(pr * v_ref[b], axis=0)


def _xattn_sample(q, k8, v8, layer, bb=8):
    nb, d = q.shape
    n_mem = k8.shape[2]
    kv_spec = pl.BlockSpec((None, bb, n_mem, HEAD_ROWS, LANES), lambda i: (layer, i, 0, 0, 0))
    out = pl.pallas_call(
        _xattn_sample_kernel,
        grid=(nb // bb,),
        in_specs=[pl.BlockSpec((bb, HEAD_ROWS, LANES), lambda i: (i, 0, 0)), kv_spec, kv_spec],
        out_specs=pl.BlockSpec((bb, HEAD_ROWS, LANES), lambda i: (i, 0, 0)),
        out_shape=jax.ShapeDtypeStruct((nb, HEAD_ROWS, LANES), F32),
        compiler_params=_cparams("parallel"),
        name="xattn_sample",
    )(_to_head_rows(q), k8, v8)
    return _from_head_rows(out)


def _split_bf16(a):
    hi = a.astype(BF16)
    return hi, (a - hi.astype(F32)).astype(BF16)


def _top2(logits):
    n = logits.shape[-1]
    lane = lax.broadcasted_iota(jnp.int32, logits.shape, 1)
    m1 = jnp.max(logits, axis=-1, keepdims=True)
    i1 = jnp.min(jnp.where(logits == m1, lane, n), axis=-1, keepdims=True)
    rest = jnp.where(lane == i1, -jnp.inf, logits)
    m2 = jnp.max(rest, axis=-1, keepdims=True)
    i2 = jnp.min(jnp.where(rest == m2, lane, n), axis=-1, keepdims=True)
    e2 = jnp.exp(m2 - m1)
    denom = 1.0 + e2
    return i1, i2, 1.0 / denom, e2 / denom


def _resident_spec(shape):
    return pl.BlockSpec(shape, lambda *_: (0,) * len(shape), pipeline_mode=pl.Buffered(1))


def _ffn_dense_kernel(x_ref, g_ref, *rest, final):
    if final:
        gfin_ref, rest = rest[0], rest[1:]
    wg_ref, wu_ref, wd_ref, o_ref = rest
    x = x_ref[...]
    h = _rms(x, g_ref[...]).astype(BF16)
    act = (jax.nn.silu(_dot(h, wg_ref[...])) * _dot(h, wu_ref[...])).astype(BF16)
    out = x + _dot(act, wd_ref[...])
    o_ref[...] = _rms(out, gfin_ref[...]) if final else out


def _ffn_dense(x, g, wg, wu, wd, tm, g_final=None):
    r, d = x.shape
    final = g_final is not None
    args, specs = [x, g], [pl.BlockSpec((tm, d), lambda i: (i, 0)), _const_spec(g.shape)]
    if final:
        args.append(g_final)
        specs.append(_const_spec(g_final.shape))
    args += [wg, wu, wd]
    specs += [_resident_spec(wg.shape), _resident_spec(wu.shape), _resident_spec(wd.shape)]
    return pl.pallas_call(
        functools.partial(_ffn_dense_kernel, final=final),
        grid=(r // tm,),
        in_specs=specs,
        out_specs=pl.BlockSpec((tm, d), lambda i: (i, 0)),
        out_shape=jax.ShapeDtypeStruct((r, d), F32),
        compiler_params=_cparams("parallel"),
        name="ffn_dense",
    )(*args)


def _moe_all_experts_kernel(x_ref, g_ref, wr_ref, *rest, final):
    if final:
        gfin_ref, rest = rest[0], rest[1:]
    wgu_ref, wd_ref, o_ref, h_ref, acc_ref, i1_ref, i2_ref, g1_ref, g2_ref = rest
    e = pl.program_id(1)
    f = wd_ref.shape[0]

    @pl.when(e == 0)
    def _():
        x = x_ref[...]
        h = _rms(x, g_ref[...])
        h_ref[...] = h.astype(BF16)
        acc_ref[...] = x
        h_hi, h_lo = _split_bf16(h)
        w_hi, w_lo = _split_bf16(wr_ref[...])
        logits = _dot(h_hi, w_hi) + (_dot(h_hi, w_lo) + _dot(h_lo, w_hi))
        lane = lax.broadcasted_iota(jnp.int32, logits.shape, 1)
        logits = jnp.where(lane < N_EXPERTS, logits, -jnp.inf)
        i1_ref[...], i2_ref[...], g1_ref[...], g2_ref[...] = _top2(logits)

    gu = _dot(h_ref[...], wgu_ref[...])
    y = _dot((jax.nn.silu(gu[:, :f]) * gu[:, f:]).astype(BF16), wd_ref[...])
    c = jnp.where(i1_ref[...] == e, g1_ref[...], 0.0) + jnp.where(i2_ref[...] == e, g2_ref[...], 0.0)
    acc_ref[...] += c * y

    @pl.when(e == pl.num_programs(1) - 1)
    def _():
        out = acc_ref[...]
        o_ref[...] = _rms(out, gfin_ref[...]) if final else out


def _moe_all_experts(x, g, w_router, wgu, wd, tm, g_final=None):
    r, d = x.shape
    n_e, f, _ = wd.shape
    final = g_final is not None
    args = [x, g, w_router]
    specs = [pl.BlockSpec((tm, d), lambda i, e: (i, 0)), _const_spec(g.shape), _const_spec(w_router.shape)]
    if final:
        args.append(g_final)
        specs.append(_const_spec(g_final.shape))
    args += [wgu, wd]
    specs += [pl.BlockSpec((None, d, 2 * f), lambda i, e: (e, 0, 0)),
              pl.BlockSpec((None, f, d), lambda i, e: (e, 0, 0))]
    return pl.pallas_call(
        functools.partial(_moe_all_experts_kernel, final=final),
        grid=(r // tm, n_e),
        in_specs=specs,
        out_specs=pl.BlockSpec((tm, d), lambda i, e: (i, 0)),
        out_shape=jax.ShapeDtypeStruct((r, d), F32),
        scratch_shapes=[pltpu.VMEM((tm, d), BF16), pltpu.VMEM((tm, d), F32),
                        pltpu.VMEM((tm, 1), jnp.int32), pltpu.VMEM((tm, 1), jnp.int32),
                        pltpu.VMEM((tm, 1), F32), pltpu.VMEM((tm, 1), F32)],
        compiler_params=_cparams("parallel", "arbitrary"),
        name="moe_all_experts",
    )(*args)


MOE_TILE = 512
TOP_K = 2
ROW_ALIGN = SUBLANES
COMPACT_ROWS = -(-(TOP_K * MOE_TILE + N_EXPERTS * (ROW_ALIGN - 1)) // (2 * SUBLANES)) * (2 * SUBLANES)
SEGMENT_BITS = (MOE_TILE // ROW_ALIGN).bit_length()


def _route_kernel(x_ref, g_ref, wrt_ref, earlier_ref, lp_ref, col_ref, cnt_ref):
    tk = x_ref.shape[0]
    n_e = wrt_ref.shape[0]
    h = _rms(x_ref[...], g_ref[...])
    h_hi, h_lo = _split_bf16(h)
    w_hi, w_lo = _split_bf16(wrt_ref[...])
    nt = (((1,), (1,)), ((), ()))
    dg = lambda a, b: lax.dot_general(a, b, nt, preferred_element_type=F32)
    logits = dg(w_hi, h_hi) + (dg(w_hi, h_lo) + dg(w_lo, h_hi))
    sub = lax.broadcasted_iota(jnp.int32, logits.shape, 0)
    m1 = jnp.max(logits, axis=0, keepdims=True)
    i1 = jnp.min(jnp.where(logits == m1, sub, n_e), axis=0, keepdims=True)
    rest = jnp.where(sub == i1, -jnp.inf, logits)
    m2 = jnp.max(rest, axis=0, keepdims=True)
    i2 = jnp.min(jnp.where(rest == m2, sub, n_e), axis=0, keepdims=True)
    e2 = jnp.exp(m2 - m1)
    denom = 1.0 + e2

    pick1 = jnp.where(sub == i1, 1.0, 0.0)
    pick2 = jnp.where(sub == i2, 1.0, 0.0)
    cnt = pick1 + pick2
    rank = _dot(cnt.astype(BF16), earlier_ref[...])
    n = jnp.sum(cnt, axis=1, keepdims=True)
    n_pad = jnp.floor((n + (ROW_ALIGN - 1)) * (1.0 / ROW_ALIGN)) * ROW_ALIGN
    base, acc = [], jnp.zeros((1, 1), F32)
    for e in range(n_e):
        base.append(acc)
        acc = acc + n_pad[e:e + 1, :]
    pos = rank + jnp.concatenate(base, axis=0)
    lp1 = jnp.sum(pick1 * pos, axis=0, keepdims=True)
    lp2 = jnp.sum(pick2 * pos, axis=0, keepdims=True)
    lp_ref[0:1, :] = lp1.astype(jnp.int32)
    lp_ref[1:2, :] = lp2.astype(jnp.int32)
    info = jnp.concatenate([lp1, lp2, 1.0 / denom, e2 / denom, jnp.zeros((SUBLANES - 4, tk), F32)], axis=0)
    col_ref[...] = info.T
    cnt_ref[...] = jnp.broadcast_to(n, cnt_ref.shape)


def _route(x, g, wrt):
    t, d = x.shape
    n_e = wrt.shape[0]
    n_tiles = t // MOE_TILE
    idx = jnp.arange(MOE_TILE)
    earlier = (idx[:, None] < idx[None, :]).astype(BF16)
    return pl.pallas_call(
        _route_kernel,
        grid=(n_tiles,),
        in_specs=[pl.BlockSpec((MOE_TILE, d), lambda j: (j, 0)), _const_spec(g.shape), _const_spec(wrt.shape),
                  _const_spec(earlier.shape)],
        out_specs=[pl.BlockSpec((None, TOP_K, MOE_TILE), lambda j: (j, 0, 0)),
                   pl.BlockSpec((MOE_TILE, SUBLANES), lambda j: (j, 0)),
                   pl.BlockSpec((None, n_e, LANES), lambda j: (j, 0, 0))],
        out_shape=[jax.ShapeDtypeStruct((n_tiles, TOP_K, MOE_TILE), jnp.int32),
                   jax.ShapeDtypeStruct((t, SUBLANES), F32),
                   jax.ShapeDtypeStruct((n_tiles, n_e, LANES), F32)],
        compiler_params=_cparams("parallel"),
        name="moe_route",
    )(x, g, wrt, earlier)


def _segment_copies(src_ref, src_row, dst_ref, dst_row, groups, sem):
    out = []
    for bit in range(SEGMENT_BITS):
        rows = ROW_ALIGN << bit
        before = ((groups >> (bit + 1)) << (bit + 1)) * ROW_ALIGN
        cp = pltpu.make_async_copy(
            src_ref.at[pl.ds(pl.multiple_of(src_row + before, ROW_ALIGN), rows), :],
            dst_ref.at[pl.ds(pl.multiple_of(dst_row + before, ROW_ALIGN), rows), :], sem)
        out.append((((groups >> bit) & 1) == 1, cp))
    return out


def _start_all(copies):
    for pred, cp in copies:
        pl.when(pred)(cp.start)


def _wait_all(copies):
    for pred, cp in copies:
        pl.when(pred)(cp.wait)


def _dispatch_kernel(grp_ref, off_ref, dst_ref, total_ref, x_ref, g_ref, lp_ref, xs_ref, comp_ref, sem):
    j = pl.program_id(0)
    n_steps = pl.num_programs(0)
    n_e = grp_ref.shape[1]
    slot = j % 2

    def copies(step, s):
        out = []
        for e in range(n_e):
            out += _segment_copies(comp_ref.at[s], off_ref[step, e], xs_ref, dst_ref[step, e],
                                   grp_ref[step, e], sem.at[s])
        return out

    h = _rms(x_ref[...], g_ref[...]).astype(BF16)
    row = lax.broadcasted_iota(jnp.int32, (COMPACT_ROWS, h.shape[0]), 0)
    onehot = jnp.where(row == lp_ref[0:1, :], 1.0, jnp.where(row == lp_ref[1:2, :], 1.0, 0.0)).astype(BF16)
    compact = _dot(onehot, h)

    @pl.when(j >= 2)
    def _():
        _wait_all(copies(j - 2, slot))

    comp_ref[slot] = compact
    _start_all(copies(j, slot))

    @pl.when(j == n_steps - 1)
    def _():
        _wait_all(copies(j, slot))

        @pl.when(j >= 1)
        def _():
            _wait_all(copies(j - 1, 1 - slot))

        comp_ref[slot, 0:MOE_TILE, :] = jnp.zeros((MOE_TILE, comp_ref.shape[2]), F32)
        groups_left = (xs_ref.shape[0] - total_ref[0]) // ROW_ALIGN
        tail = []
        min_total = TOP_K * x_ref.shape[0] * grp_ref.shape[0]
        for c in range(-(-(xs_ref.shape[0] - min_total) // MOE_TILE)):
            tail += _segment_copies(comp_ref.at[slot], 0, xs_ref, total_ref[0] + c * MOE_TILE,
                                    jnp.clip(groups_left - c * (MOE_TILE // ROW_ALIGN), 0, MOE_TILE // ROW_ALIGN),
                                    sem.at[slot])
        _start_all(tail)
        _wait_all(tail)


def _dispatch(x, g, lp, groups, off, dst, total, n_rows):
    t, d = x.shape
    n_tiles = t // MOE_TILE
    grid_spec = pltpu.PrefetchScalarGridSpec(
        num_scalar_prefetch=4,
        grid=(n_tiles,),
        in_specs=[pl.BlockSpec((MOE_TILE, d), lambda j, *_: (j, 0)),
                  pl.BlockSpec(g.shape, lambda j, *_: (0, 0)),
                  pl.BlockSpec((None, TOP_K, MOE_TILE), lambda j, *_: (j, 0, 0))],
        out_specs=pl.BlockSpec(memory_space=pl.ANY),
        scratch_shapes=[pltpu.VMEM((2, COMPACT_ROWS, d), F32), pltpu.SemaphoreType.DMA((2,))])
    return pl.pallas_call(
        _dispatch_kernel,
        grid_spec=grid_spec,
        out_shape=jax.ShapeDtypeStruct((n_rows, d), F32),
        compiler_params=_cparams("arbitrary"),
        name="moe_dispatch",
    )(groups, off, dst, total, x, g, lp)


STEP_SKIP, STEP_FFN, STEP_ZERO = 0, 1, 2


def _grouped_ffn_kernel(tile_ref, grp_ref, lo_ref, hi_ref, first_ref, mode_ref,
                        xs_ref, wgu_ref, wd_ref, ys_ref):
    w = pl.program_id(0)
    tm = xs_ref.shape[0]

    @pl.when(mode_ref[w] == STEP_ZERO)
    def _():
        ys_ref[...] = jnp.zeros_like(ys_ref)

    @pl.when(mode_ref[w] == STEP_FFN)
    def _():
        x = xs_ref[...].astype(BF16)
        f = wd_ref.shape[0]
        gu = _dot(x, wgu_ref[...])
        y = _dot((jax.nn.silu(gu[:, :f]) * gu[:, f:]).astype(BF16), wd_ref[...])

        @pl.when(first_ref[w] == 1)
        def _():
            ys_ref[...] = y

        @pl.when(first_ref[w] == 0)
        def _():
            r = lax.broadcasted_iota(jnp.int32, (tm, 1), 0)
            mine = jnp.logical_and(r >= lo_ref[w], r < hi_ref[w])
            ys_ref[...] = jnp.where(mine, y, ys_ref[...])


def _grouped_ffn(xs, wgu, wd, tile, grp, lo, hi, first, mode):
    n_e, f, d = wd.shape
    grid_spec = pltpu.PrefetchScalarGridSpec(
        num_scalar_prefetch=6,
        grid=(tile.shape[0],),
        in_specs=[pl.BlockSpec((MOE_TILE, d), lambda w, tile, *_: (tile[w], 0)),
                  pl.BlockSpec((None, d, 2 * f), lambda w, tile, grp, *_: (grp[w], 0, 0)),
                  pl.BlockSpec((None, f, d), lambda w, tile, grp, *_: (grp[w], 0, 0))],
        out_specs=pl.BlockSpec((MOE_TILE, d), lambda w, tile, *_: (tile[w], 0)))
    return pl.pallas_call(
        _grouped_ffn_kernel,
        grid_spec=grid_spec,
        out_shape=jax.ShapeDtypeStruct(xs.shape, F32),
        compiler_params=_cparams("arbitrary"),
        name="moe_grouped_ffn",
    )(tile, grp, lo, hi, first, mode, xs, wgu, wd)


def _combine_kernel(grp_ref, off_ref, dst_ref, x_ref, col_ref, *rest, final):
    if final:
        gfin_ref, rest = rest[0], rest[1:]
    ys_ref, o_ref, comp_ref, sem = rest
    j = pl.program_id(0)
    n_steps = pl.num_programs(0)
    n_e = grp_ref.shape[1]
    slot = j % 2

    def copies(step, s):
        out = []
        for e in range(n_e):
            out += _segment_copies(ys_ref, dst_ref[step, e], comp_ref.at[s], off_ref[step, e],
                                   grp_ref[step, e], sem.at[s])
        return out

    @pl.when(j == 0)
    def _():
        comp_ref[...] = jnp.zeros_like(comp_ref)
        _start_all(copies(0, 0))

    @pl.when(j + 1 < n_steps)
    def _():
        _start_all(copies(j + 1, 1 - slot))

    _wait_all(copies(j, slot))
    y = comp_ref[slot].astype(BF16)
    col = col_ref[...]
    row = lax.broadcasted_iota(jnp.int32, (col.shape[0], COMPACT_ROWS), 1)
    out = x_ref[...]
    for k in range(TOP_K):
        pick = jnp.where(row == col[:, k:k + 1].astype(jnp.int32), 1.0, 0.0).astype(BF16)
        out = out + col[:, TOP_K + k:TOP_K + k + 1] * _dot(pick, y)
    o_ref[...] = _rms(out, gfin_ref[...]) if final else out


def _combine(x, col, ys, groups, off, dst, g_final):
    t, d = x.shape
    n_tiles = t // MOE_TILE
    final = g_final is not None
    args = [x, col]
    specs = [pl.BlockSpec((MOE_TILE, d), lambda j, *_: (j, 0)),
             pl.BlockSpec((MOE_TILE, SUBLANES), lambda j, *_: (j, 0))]
    if final:
        args.append(g_final)
        specs.append(pl.BlockSpec(g_final.shape, lambda j, *_: (0, 0)))
    args.append(ys)
    specs.append(pl.BlockSpec(memory_space=pl.ANY))
    grid_spec = pltpu.PrefetchScalarGridSpec(
        num_scalar_prefetch=3,
        grid=(n_tiles,),
        in_specs=specs,
        out_specs=pl.BlockSpec((MOE_TILE, d), lambda j, *_: (j, 0)),
        scratch_shapes=[pltpu.VMEM((2, COMPACT_ROWS, d), F32), pltpu.SemaphoreType.DMA((2,))])
    return pl.pallas_call(
        functools.partial(_combine_kernel, final=final),
        grid_spec=grid_spec,
        out_shape=jax.ShapeDtypeStruct((t, d), F32),
        compiler_params=_cparams("arbitrary"),
        name="moe_combine",
    )(groups, off, dst, *args)


def _moe_routed(x, g, w_router, wgu, wd, g_final):
    t, _ = x.shape
    n_e = wd.shape[0]
    n_tiles = t // MOE_TILE
    lp, col, cnt = _route(x, g, w_router.T)
    groups = (cnt[:, :, 0].astype(jnp.int32) + (ROW_ALIGN - 1)) // ROW_ALIGN
    n = groups * ROW_ALIGN
    off = jnp.cumsum(n, axis=1) - n
    per_e = jnp.sum(n, axis=0)
    ends = jnp.cumsum(per_e)
    starts = ends - per_e
    dst = starts[None, :] + jnp.cumsum(n, axis=0) - n
    max_rows = TOP_K * t + n_tiles * n_e * (ROW_ALIGN - 1)
    n_row_tiles = -(-max_rows // MOE_TILE)
    xs = _dispatch(x, g, lp, groups, off, dst, ends[-1:], (n_row_tiles + 1) * MOE_TILE)

    n_steps = n_row_tiles + n_e
    first_tile = starts // MOE_TILE
    n_items = jnp.where(per_e > 0, (ends - 1) // MOE_TILE - first_tile + 1, 0)
    item_end = jnp.cumsum(n_items)
    n_ffn = item_end[-1]
    w = jnp.arange(n_steps, dtype=jnp.int32)
    wc = jnp.minimum(w, n_ffn - 1)
    grp = jnp.sum(wc[:, None] >= item_end[None, :], axis=1).astype(jnp.int32)
    ffn_tile = first_tile[grp] + wc - (item_end - n_items)[grp]
    zero_tile = (ends[-1] - 1) // MOE_TILE + 1 + (w - n_ffn)
    tile = jnp.where(w < n_ffn, ffn_tile, jnp.minimum(zero_tile, n_row_tiles)).astype(jnp.int32)
    mode = jnp.where(w < n_ffn, STEP_FFN, jnp.where(zero_tile <= n_row_tiles, STEP_ZERO, STEP_SKIP))
    lo = jnp.clip(starts[grp] - tile * MOE_TILE, 0, MOE_TILE).astype(jnp.int32)
    hi = jnp.clip(ends[grp] - tile * MOE_TILE, 0, MOE_TILE).astype(jnp.int32)
    first = jnp.concatenate([jnp.ones((1,), jnp.int32), (tile[1:] != tile[:-1]).astype(jnp.int32)])
    ys = _grouped_ffn(xs, wgu, wd, tile, grp, lo, hi, first, mode.astype(jnp.int32))
    return _combine(x, col, ys, groups, off, dst, g_final)


def _block_diag_gates(w_a, w_x):
    per_group = LRU_GROUP // LRU_HEAD_DIM
    eye = jnp.eye(per_group, dtype=w_a.dtype)

    def bd(w):
        w = w.reshape(D_LRU // LRU_GROUP, per_group, LRU_HEAD_DIM, LRU_HEAD_DIM)
        return jnp.einsum('ghij,hk->ghikj', w, eye).reshape(D_LRU // LRU_GROUP, LRU_GROUP, LRU_GROUP)

    return jnp.concatenate([bd(w_a), bd(w_x)], axis=-1).astype(BF16)


def kernel(x_prompt, x_sample, state_shortconv, state_lru_conv, state_lru_h, cache_mem_k, cache_mem_v,
           mem_prompt, norm_mix, w_in, w_short_conv, w_lru_conv, b_lru_conv, w_lru_a, b_lru_a, w_lru_x,
           b_lru_x, lru_lambda, norm_conv_out, norm_lru_out, w_mix_out, norm_xattn, norm_mem, w_q, w_k,
           w_v, w_o, norm_ffn, w_ff_gate, w_ff_up, w_ff_down, w_router, w_moe_gate, w_moe_up, w_moe_down,
           norm_final):
    depth = w_in.shape[0]
    nb, t, d = x_prompt.shape
    ns = x_sample.shape[0]
    n_mem = mem_prompt.shape[1]
    row = lambda a: a.reshape(1, -1)

    layers = []
    for l in range(depth):
        lp = dict(
            gmix=row(norm_mix[l]), win=w_in[l].astype(BF16), wsc=w_short_conv[l], wlc=w_lru_conv[l],
            blc=row(b_lru_conv[l]), wg=_block_diag_gates(w_lru_a[l], w_lru_x[l]), ba=row(b_lru_a[l]),
            bx=row(b_lru_x[l]), lam=row(lru_lambda[l]), gco=row(norm_conv_out[l]), glo=row(norm_lru_out[l]),
            wout=w_mix_out[l].astype(BF16), gx=row(norm_xattn[l]),
            wq=w_q[l].astype(BF16), wo=w_o[l].astype(BF16),
            gffn=row(norm_ffn[l]), gfin=row(norm_final) if l == depth - 1 else None)
        j = l // 2
        if l % 2 == 0:
            lp['ffw'] = (w_ff_gate[j].astype(BF16), w_ff_up[j].astype(BF16), w_ff_down[j].astype(BF16))
            lp['wr'] = None
        else:
            lp['ffw'] = (jnp.concatenate([w_moe_gate[j], w_moe_up[j]], axis=-1).astype(BF16),
                         w_moe_down[j].astype(BF16))
            lp['wr'] = jnp.pad(w_router[j], ((0, 0), (0, LANES - N_EXPERTS)))
            lp['wr_raw'] = w_router[j]
        layers.append(lp)

    x = x_prompt
    p_mem_k, p_mem_v, kb, vb = _kv_proj(mem_prompt, norm_mem.reshape(depth, 1, d), w_k.astype(BF16),
                                        w_v.astype(BF16))
    pc, plc, ph = [], [], []
    for l, lp in enumerate(layers):
        x, tail_p, tail_x, h_last = _mixer_prompt(x, lp)
        x = _xattn_prompt(x, lp['gx'], lp['wq'], kb[l], vb[l], lp['wo'])
        if lp['wr'] is None:
            x = _ffn_dense(x.reshape(nb * t, d), lp['gffn'], *lp['ffw'], tm=512, g_final=lp['gfin'])
        else:
            x = _moe_routed(x.reshape(nb * t, d), lp['gffn'], lp['wr_raw'], *lp['ffw'], lp['gfin'])
        x = x.reshape(nb, t, d)
        tail_p = jnp.swapaxes(tail_p.reshape(MAX_CONV_HISTORY, nb, D_CONV), 0, 1)
        tail_x = jnp.swapaxes(tail_x.reshape(MAX_CONV_HISTORY, nb, D_LRU), 0, 1)
        pc.append(tail_p[:, MAX_CONV_HISTORY - (w_short_conv.shape[1] - 1):])
        plc.append(tail_x[:, MAX_CONV_HISTORY - (w_lru_conv.shape[1] - 1):])
        ph.append(h_last)
    y_prompt = x

    x = x_sample.reshape(ns, d)
    k8 = _to_head_rows(cache_mem_k.reshape(depth, ns, n_mem, d))
    v8 = _to_head_rows(cache_mem_v.reshape(depth, ns, n_mem, d))
    sc, slc, sh = [], [], []
    for l, lp in enumerate(layers):
        x, p_new, xb_new, h_new = _mixer_sample(x, state_shortconv[l], state_lru_conv[l], state_lru_h[l], lp)
        q = _norm_matmul(x, lp['gx'], lp['wq'], ns)
        o = _xattn_sample(q, k8, v8, l)
        x = _matmul_residual(o, lp['wo'], x, ns)
        if lp['wr'] is None:
            x = _ffn_dense(x, lp['gffn'], *lp['ffw'], tm=ns, g_final=lp['gfin'])
        else:
            x = _moe_all_experts(x, lp['gffn'], lp['wr'], *lp['ffw'], tm=ns, g_final=lp['gfin'])
        sc.append(jnp.stack([state_shortconv[l][:, 1], p_new], axis=1))
        slc.append(jnp.stack([state_lru_conv[l][:, 1], state_lru_conv[l][:, 2], xb_new], axis=1))
        sh.append(h_new)
    y_sample = x.reshape(ns, 1, d)

    return (y_prompt, y_sample, jnp.stack(pc), jnp.stack(plc), jnp.stack(ph), p_mem_k, p_mem_v,
            jnp.stack(sc), jnp.stack(slc), jnp.stack(sh))
```

```python
import functools

import jax
import jax.numpy as jnp
from jax import lax
from jax.experimental import pallas as pl
from jax.experimental.pallas import tpu as pltpu

F32 = jnp.float32
BF16 = jnp.bfloat16

EPS = 1e-6
LRU_C = 8.0
D_CONV = 512
D_LRU = 512
N_LRU_HEADS = 8
LRU_HEAD_DIM = 64
LRU_GROUP = 256
N_MEM_HEADS = 4
MEM_HEAD_DIM = 256
N_EXPERTS = 8
SUBLANES = 8
LANES = 128
VMEM_LIMIT_BYTES = 56 * 1024 * 1024


def _cparams(*sem):
    return pltpu.CompilerParams(dimension_semantics=sem, vmem_limit_bytes=VMEM_LIMIT_BYTES)


def _rms(x, g):
    return x * lax.rsqrt(jnp.mean(x * x, axis=-1, keepdims=True) + EPS) * g


def _dot(a, b):
    return jnp.dot(a, b, preferred_element_type=F32)


def _lru_gates(xc, wg_ref, ba, bx, lam):
    neg_lam = -lam
    softplus = jnp.maximum(neg_lam, 0.0) + jnp.log1p(jnp.exp(-jnp.abs(neg_lam)))
    a_parts, u_parts = [], []
    for g in range(D_LRU // LRU_GROUP):
        sl = slice(g * LRU_GROUP, (g + 1) * LRU_GROUP)
        xg = xc[:, sl]
        gates = _dot(xg.astype(BF16), wg_ref[g])
        r = jax.nn.sigmoid(gates[:, :LRU_GROUP] + ba[:, sl])
        i = jax.nn.sigmoid(gates[:, LRU_GROUP:] + bx[:, sl])
        log_a = (-LRU_C) * r * softplus[:, sl]
        th = jnp.tanh(log_a)
        one_minus_a2 = (-2.0 * th) / (1.0 - th)
        a_parts.append(jnp.exp(log_a))
        u_parts.append(jnp.sqrt(one_minus_a2) * (i * xg))
    return jnp.concatenate(a_parts, axis=-1), jnp.concatenate(u_parts, axis=-1)


def _mix_out(x, ya, yb, gco, glo, wout_ref):
    y = jnp.concatenate([_rms(ya, gco), _rms(yb, glo)], axis=-1).astype(BF16)
    return x + _dot(y, wout_ref[...])


MAX_CONV_HISTORY = 3


def _conv_taps(ext_ref, w_ref, rows, nb):
    width = w_ref.shape[0]
    hist = MAX_CONV_HISTORY * nb
    acc = ext_ref[hist:hist + rows, :] * w_ref[width - 1:width, :]
    for s in range(1, width):
        acc = acc + ext_ref[hist - s * nb:hist - s * nb + rows, :] * w_ref[width - 1 - s:width - s, :]
    return acc


def _mixer_prompt_kernel(x_hbm, gmix_ref, win_ref, wsc_ref, wlc_ref, blc_ref, wg_ref, ba_ref, bx_ref,
                         lam_ref, gco_ref, glo_ref, wout_ref,
                         xo_hbm, tailp_ref, tailx_ref, hout_ref,
                         xin_ref, xout_ref, in_sem, out_sem, pext_ref, xext_ref, h_ref, a_ref, u_ref):
    _, tt, nb, d = xin_ref.shape
    rows = tt * nb
    hist = MAX_CONV_HISTORY * nb
    i = pl.program_id(0)
    n_steps = pl.num_programs(0)
    slot = i % 2

    def in_copies(step, s):
        return [pltpu.make_async_copy(x_hbm.at[b, pl.ds(step * tt, tt), :], xin_ref.at[s, :, b, :],
                                      in_sem.at[s]) for b in range(nb)]

    def out_copies(step, s):
        return [pltpu.make_async_copy(xout_ref.at[s, :, b, :], xo_hbm.at[b, pl.ds(step * tt, tt), :],
                                      out_sem.at[s]) for b in range(nb)]

    @pl.when(i == 0)
    def _():
        pext_ref[0:hist, :] = jnp.zeros((hist, pext_ref.shape[1]), F32)
        xext_ref[0:hist, :] = jnp.zeros((hist, xext_ref.shape[1]), F32)
        h_ref[...] = jnp.zeros_like(h_ref)
        for cp in in_copies(0, 0):
            cp.start()

    @pl.when(i + 1 < n_steps)
    def _():
        for cp in in_copies(i + 1, 1 - slot):
            cp.start()

    for cp in in_copies(i, slot):
        cp.wait()
    x = xin_ref[slot].reshape(rows, d)
    z = _dot(_rms(x, gmix_ref[...]).astype(BF16), win_ref[...])
    gb = z[:, 0:D_CONV]
    gate = z[:, 3 * D_CONV + D_LRU:]
    pext_ref[hist:hist + rows, :] = z[:, D_CONV:2 * D_CONV] * z[:, 2 * D_CONV:3 * D_CONV]
    xext_ref[hist:hist + rows, :] = z[:, 3 * D_CONV:3 * D_CONV + D_LRU]

    ya = gb * _conv_taps(pext_ref, wsc_ref, rows, nb)
    xc = _conv_taps(xext_ref, wlc_ref, rows, nb) + blc_ref[...]
    for ext_ref, tail_ref in ((pext_ref, tailp_ref), (xext_ref, tailx_ref)):
        tail = ext_ref[rows:rows + hist, :]
        ext_ref[0:hist, :] = tail
        tail_ref[...] = tail

    a, u = _lru_gates(xc, wg_ref, ba_ref[...], bx_ref[...], lam_ref[...])
    a_ref[...] = a
    u_ref[...] = u

    def step(t, h):
        grp = pl.ds(pl.multiple_of(t * nb, nb), nb)
        h = a_ref[grp, :] * h + u_ref[grp, :]
        u_ref[grp, :] = h
        return h

    h_last = lax.fori_loop(0, tt, step, h_ref[...], unroll=8)
    h_ref[...] = h_last
    hout_ref[...] = h_last

    yb = u_ref[...] * jax.nn.gelu(gate)
    out = _mix_out(x, ya, yb, gco_ref[...], glo_ref[...], wout_ref)

    @pl.when(i >= 2)
    def _():
        for cp in out_copies(i - 2, slot):
            cp.wait()

    xout_ref[slot] = out.reshape(tt, nb, d)
    for cp in out_copies(i, slot):
        cp.start()

    @pl.when(i == n_steps - 1)
    def _():
        for cp in out_copies(i, slot):
            cp.wait()

        @pl.when(i >= 1)
        def _():
            for cp in out_copies(i - 1, 1 - slot):
                cp.wait()


def _const_spec(shape):
    return pl.BlockSpec(shape, lambda *_: (0,) * len(shape))


def _mixer_prompt(x, p, tt=128):
    nb, t, d = x.shape
    small = [p['gmix'], p['win'], p['wsc'], p['wlc'], p['blc'], p['wg'], p['ba'], p['bx'], p['lam'],
             p['gco'], p['glo'], p['wout']]
    rows = nb * tt
    hist = MAX_CONV_HISTORY * nb
    return pl.pallas_call(
        _mixer_prompt_kernel,
        grid=(t // tt,),
        in_specs=[pl.BlockSpec(memory_space=pl.ANY)] + [_resident_spec(a.shape) for a in small],
        out_specs=[pl.BlockSpec(memory_space=pl.ANY),
                   _const_spec((hist, D_CONV)), _const_spec((hist, D_LRU)), _const_spec((nb, D_LRU))],
        out_shape=[jax.ShapeDtypeStruct((nb, t, d), F32),
                   jax.ShapeDtypeStruct((hist, D_CONV), F32),
                   jax.ShapeDtypeStruct((hist, D_LRU), F32),
                   jax.ShapeDtypeStruct((nb, D_LRU), F32)],
        scratch_shapes=[pltpu.VMEM((2, tt, nb, d), F32), pltpu.VMEM((2, tt, nb, d), F32),
                        pltpu.SemaphoreType.DMA((2,)), pltpu.SemaphoreType.DMA((2,)),
                        pltpu.VMEM((hist + rows, D_CONV), F32), pltpu.VMEM((hist + rows, D_LRU), F32),
                        pltpu.VMEM((nb, D_LRU), F32),
                        pltpu.VMEM((rows, D_LRU), F32), pltpu.VMEM((rows, D_LRU), F32)],
        compiler_params=_cparams("arbitrary"),
        name="mixer_prompt",
    )(x, *small)


def _mixer_sample_kernel(x_ref, sc0_ref, sc1_ref, lc0_ref, lc1_ref, lc2_ref, h0_ref,
                         gmix_ref, win_ref, wsc_ref, wlc_ref, blc_ref, wg_ref, ba_ref, bx_ref,
                         lam_ref, gco_ref, glo_ref, wout_ref,
                         xo_ref, pnew_ref, xbnew_ref, hout_ref):
    x = x_ref[...]
    z = _dot(_rms(x, gmix_ref[...]).astype(BF16), win_ref[...])
    gb = z[:, 0:D_CONV]
    p = z[:, D_CONV:2 * D_CONV] * z[:, 2 * D_CONV:3 * D_CONV]
    xb = z[:, 3 * D_CONV:3 * D_CONV + D_LRU]
    gate = z[:, 3 * D_CONV + D_LRU:]

    ya = gb * (sc0_ref[...] * wsc_ref[0:1, :] + sc1_ref[...] * wsc_ref[1:2, :] + p * wsc_ref[2:3, :])
    xc = (lc0_ref[...] * wlc_ref[0:1, :] + lc1_ref[...] * wlc_ref[1:2, :] + lc2_ref[...] * wlc_ref[2:3, :]
          + xb * wlc_ref[3:4, :]) + blc_ref[...]
    a, u = _lru_gates(xc, wg_ref, ba_ref[...], bx_ref[...], lam_ref[...])
    h = a * h0_ref[...] + u
    yb = h * jax.nn.gelu(gate)
    xo_ref[...] = _mix_out(x, ya, yb, gco_ref[...], glo_ref[...], wout_ref)
    pnew_ref[...] = p
    xbnew_ref[...] = xb
    hout_ref[...] = h


def _mixer_sample(x, sc, lc, h0, p):
    nb, d = x.shape
    args = [x, sc[:, 0], sc[:, 1], lc[:, 0], lc[:, 1], lc[:, 2], h0,
            p['gmix'], p['win'], p['wsc'], p['wlc'], p['blc'], p['wg'], p['ba'], p['bx'], p['lam'],
            p['gco'], p['glo'], p['wout']]
    return pl.pallas_call(
        _mixer_sample_kernel,
        grid=(1,),
        in_specs=[_const_spec(a.shape) for a in args],
        out_specs=[_const_spec((nb, d)), _const_spec((nb, D_CONV)), _const_spec((nb, D_LRU)),
                   _const_spec((nb, D_LRU))],
        out_shape=[jax.ShapeDtypeStruct((nb, d), F32), jax.ShapeDtypeStruct((nb, D_CONV), F32),
                   jax.ShapeDtypeStruct((nb, D_LRU), F32), jax.ShapeDtypeStruct((nb, D_LRU), F32)],
        compiler_params=_cparams("arbitrary"),
        name="mixer_sample",
    )(*args)


def _norm_matmul_kernel(x_ref, g_ref, w_ref, o_ref):
    o_ref[...] = _dot(_rms(x_ref[...], g_ref[...]).astype(BF16), w_ref[...])


def _norm_matmul(x, g, w, tm):
    r, d = x.shape
    n = w.shape[1]
    return pl.pallas_call(
        _norm_matmul_kernel,
        grid=(r // tm,),
        in_specs=[pl.BlockSpec((tm, d), lambda i: (i, 0)), _const_spec(g.shape), _const_spec(w.shape)],
        out_specs=pl.BlockSpec((tm, n), lambda i: (i, 0)),
        out_shape=jax.ShapeDtypeStruct((r, n), F32),
        compiler_params=_cparams("parallel"),
        name="norm_matmul",
    )(x, g, w)


def _matmul_residual_kernel(a_ref, w_ref, x_ref, o_ref):
    o_ref[...] = x_ref[...] + _dot(a_ref[...].astype(BF16), w_ref[...])


def _matmul_residual(a, w, x, tm):
    r, d = x.shape
    return pl.pallas_call(
        _matmul_residual_kernel,
        grid=(r // tm,),
        in_specs=[pl.BlockSpec((tm, a.shape[1]), lambda i: (i, 0)), _const_spec(w.shape),
                  pl.BlockSpec((tm, d), lambda i: (i, 0))],
        out_specs=pl.BlockSpec((tm, d), lambda i: (i, 0)),
        out_shape=jax.ShapeDtypeStruct((r, d), F32),
        compiler_params=_cparams("parallel"),
        name="matmul_residual",
    )(a, w, x)


def _xattn_prompt_kernel(x_ref, g_ref, wq_ref, k_ref, v_ref, wo_ref, o_ref):
    x = x_ref[...]
    q = _dot(_rms(x, g_ref[...]).astype(BF16), wq_ref[...])
    heads = []
    for h in range(N_MEM_HEADS):
        sl = slice(h * MEM_HEAD_DIM, (h + 1) * MEM_HEAD_DIM)
        kh = k_ref[:, sl]
        vh = v_ref[:, sl]
        s = lax.dot_general(q[:, sl].astype(BF16), kh, (((1,), (1,)), ((), ())),
                            preferred_element_type=F32) * (MEM_HEAD_DIM ** -0.5)
        e = jnp.exp(s - jnp.max(s, axis=-1, keepdims=True))
        pr = e / jnp.sum(e, axis=-1, keepdims=True)
        heads.append(_dot(pr.astype(BF16), vh))
    o = jnp.concatenate(heads, axis=-1).astype(BF16)
    o_ref[...] = x + _dot(o, wo_ref[...])


def _xattn_prompt(x, g, wq, k, v, wo, tq=1024):
    nb, t, d = x.shape
    n_mem = k.shape[1]
    return pl.pallas_call(
        _xattn_prompt_kernel,
        grid=(nb, t // tq),
        in_specs=[pl.BlockSpec((None, tq, d), lambda b, i: (b, i, 0)), _const_spec(g.shape),
                  _const_spec(wq.shape),
                  pl.BlockSpec((None, n_mem, d), lambda b, i: (b, 0, 0)),
                  pl.BlockSpec((None, n_mem, d), lambda b, i: (b, 0, 0)),
                  _const_spec(wo.shape)],
        out_specs=pl.BlockSpec((None, tq, d), lambda b, i: (b, i, 0)),
        out_shape=jax.ShapeDtypeStruct((nb, t, d), F32),
        compiler_params=_cparams("parallel", "parallel"),
        name="xattn_prompt",
    )(x, g, wq, k, v, wo)


def _kv_proj_kernel(mem_ref, g_ref, wk_ref, wv_ref, k5_ref, v5_ref, kb_ref, vb_ref):
    bb, n_mem, d = mem_ref.shape
    m = _rms(mem_ref[...].reshape(bb * n_mem, d), g_ref[...]).astype(BF16)
    for w_ref, o5_ref, ob_ref in ((wk_ref, k5_ref, kb_ref), (wv_ref, v5_ref, vb_ref)):
        y = _dot(m, w_ref[...]).reshape(bb, n_mem, d)
        ob_ref[...] = y.astype(BF16)
        for h in range(N_MEM_HEADS):
            o5_ref[:, :, h, :] = y[:, :, h * MEM_HEAD_DIM:(h + 1) * MEM_HEAD_DIM]


def _kv_proj(mem, gmem, wk, wv, bb=2):
    nb, n_mem, d = mem.shape
    depth = wk.shape[0]
    w_spec = pl.BlockSpec((None, d, d), lambda l, i: (l, 0, 0))
    o5_spec = pl.BlockSpec((None, bb, n_mem, N_MEM_HEADS, MEM_HEAD_DIM), lambda l, i: (l, i, 0, 0, 0))
    ob_spec = pl.BlockSpec((None, bb, n_mem, d), lambda l, i: (l, i, 0, 0))
    o5_shape = jax.ShapeDtypeStruct((depth, nb, n_mem, N_MEM_HEADS, MEM_HEAD_DIM), F32)
    ob_shape = jax.ShapeDtypeStruct((depth, nb, n_mem, d), BF16)
    return pl.pallas_call(
        _kv_proj_kernel,
        grid=(depth, nb // bb),
        in_specs=[pl.BlockSpec((bb, n_mem, d), lambda l, i: (i, 0, 0)),
                  pl.BlockSpec((None, 1, d), lambda l, i: (l, 0, 0)), w_spec, w_spec],
        out_specs=[o5_spec, o5_spec, ob_spec, ob_spec],
        out_shape=[o5_shape, o5_shape, ob_shape, ob_shape],
        compiler_params=_cparams("parallel", "parallel"),
        name="kv_proj",
    )(mem, gmem, wk, wv)


LANE_TILES_PER_HEAD = MEM_HEAD_DIM // LANES
HEAD_ROWS = N_MEM_HEADS * LANE_TILES_PER_HEAD


def _to_head_rows(a):
    lead = a.shape[:-1]
    n = len(lead)
    a = a.reshape(*lead, N_MEM_HEADS, LANE_TILES_PER_HEAD, LANES)
    return jnp.swapaxes(a, n, n + 1).reshape(*lead, HEAD_ROWS, LANES)


def _from_head_rows(a):
    lead = a.shape[:-2]
    n = len(lead)
    a = a.reshape(*lead, LANE_TILES_PER_HEAD, N_MEM_HEADS, LANES)
    return jnp.swapaxes(a, n, n + 1).reshape(*lead, N_MEM_HEADS * MEM_HEAD_DIM)


def _xattn_sample_kernel(q_ref, k_ref, v_ref, o_ref):
    n_mem = k_ref.shape[1]
    ones = jnp.ones((LANES, LANES), BF16)
    for b in range(q_ref.shape[0]):
        prod = (k_ref[b] * q_ref[b][None]).reshape(n_mem * HEAD_ROWS, LANES).astype(BF16)
        part = _dot(prod, ones).reshape(n_mem, HEAD_ROWS, LANES)
        s = (part + pltpu.roll(part, N_MEM_HEADS, axis=1)) * (MEM_HEAD_DIM ** -0.5)
        e = jnp.exp(s - jnp.max(s, axis=0, keepdims=True))
        pr = e / jnp.sum(e, axis=0, keepdims=True)
        o_ref[b] = jnp.sum(pr * v_ref[b], axis=0)


def _xattn_sample(q, k8, v8, layer, bb=8):
    nb, d = q.shape
    n_mem = k8.shape[2]
    kv_spec = pl.BlockSpec((None, bb, n_mem, HEAD_ROWS, LANES), lambda i: (layer, i, 0, 0, 0))
    out = pl.pallas_call(
        _xattn_sample_kernel,
        grid=(nb // bb,),
        in_specs=[pl.BlockSpec((bb, HEAD_ROWS, LANES), lambda i: (i, 0, 0)), kv_spec, kv_spec],
        out_specs=pl.BlockSpec((bb, HEAD_ROWS, LANES), lambda i: (i, 0, 0)),
        out_shape=jax.ShapeDtypeStruct((nb, HEAD_ROWS, LANES), F32),
        compiler_params=_cparams("parallel"),
        name="xattn_sample",
    )(_to_head_rows(q), k8, v8)
    return _from_head_rows(out)


def _split_bf16(a):
    hi = a.astype(BF16)
    return hi, (a - hi.astype(F32)).astype(BF16)


def _top2(logits):
    n = logits.shape[-1]
    lane = lax.broadcasted_iota(jnp.int32, logits.shape, 1)
    m1 = jnp.max(logits, axis=-1, keepdims=True)
    i1 = jnp.min(jnp.where(logits == m1, lane, n), axis=-1, keepdims=True)
    rest = jnp.where(lane == i1, -jnp.inf, logits)
    m2 = jnp.max(rest, axis=-1, keepdims=True)
    i2 = jnp.min(jnp.where(rest == m2, lane, n), axis=-1, keepdims=True)
    e2 = jnp.exp(m2 - m1)
    denom = 1.0 + e2
    return i1, i2, 1.0 / denom, e2 / denom


def _resident_spec(shape):
    return pl.BlockSpec(shape, lambda *_: (0,) * len(shape), pipeline_mode=pl.Buffered(1))


def _ffn_dense_kernel(x_ref, g_ref, *rest, final):
    if final:
        gfin_ref, rest = rest[0], rest[1:]
    wg_ref, wu_ref, wd_ref, o_ref = rest
    x = x_ref[...]
    h = _rms(x, g_ref[...]).astype(BF16)
    act = (jax.nn.silu(_dot(h, wg_ref[...])) * _dot(h, wu_ref[...])).astype(BF16)
    out = x + _dot(act, wd_ref[...])
    o_ref[...] = _rms(out, gfin_ref[...]) if final else out


def _ffn_dense(x, g, wg, wu, wd, tm, g_final=None):
    r, d = x.shape
    final = g_final is not None
    args, specs = [x, g], [pl.BlockSpec((tm, d), lambda i: (i, 0)), _const_spec(g.shape)]
    if final:
        args.append(g_final)
        specs.append(_const_spec(g_final.shape))
    args += [wg, wu, wd]
    specs += [_resident_spec(wg.shape), _resident_spec(wu.shape), _resident_spec(wd.shape)]
    return pl.pallas_call(
        functools.partial(_ffn_dense_kernel, final=final),
        grid=(r // tm,),
        in_specs=specs,
        out_specs=pl.BlockSpec((tm, d), lambda i: (i, 0)),
        out_shape=jax.ShapeDtypeStruct((r, d), F32),
        compiler_params=_cparams("parallel"),
        name="ffn_dense",
    )(*args)


def _moe_all_experts_kernel(x_ref, g_ref, wr_ref, *rest, final):
    if final:
        gfin_ref, rest = rest[0], rest[1:]
    wg_ref, wu_ref, wd_ref, o_ref, h_ref, acc_ref, i1_ref, i2_ref, g1_ref, g2_ref = rest
    e = pl.program_id(1)

    @pl.when(e == 0)
    def _():
        x = x_ref[...]
        h = _rms(x, g_ref[...])
        h_ref[...] = h.astype(BF16)
        acc_ref[...] = x
        h_hi, h_lo = _split_bf16(h)
        w_hi, w_lo = _split_bf16(wr_ref[...])
        logits = _dot(h_hi, w_hi) + (_dot(h_hi, w_lo) + _dot(h_lo, w_hi))
        lane = lax.broadcasted_iota(jnp.int32, logits.shape, 1)
        logits = jnp.where(lane < N_EXPERTS, logits, -jnp.inf)
        i1_ref[...], i2_ref[...], g1_ref[...], g2_ref[...] = _top2(logits)

    hb = h_ref[...]
    y = _dot((jax.nn.silu(_dot(hb, wg_ref[...])) * _dot(hb, wu_ref[...])).astype(BF16), wd_ref[...])
    c = jnp.where(i1_ref[...] == e, g1_ref[...], 0.0) + jnp.where(i2_ref[...] == e, g2_ref[...], 0.0)
    acc_ref[...] += c * y

    @pl.when(e == pl.num_programs(1) - 1)
    def _():
        out = acc_ref[...]
        o_ref[...] = _rms(out, gfin_ref[...]) if final else out


def _moe_all_experts(x, g, w_router, wg, wu, wd, tm, g_final=None):
    r, d = x.shape
    n_e, f, _ = wd.shape
    final = g_final is not None
    args = [x, g, w_router]
    specs = [pl.BlockSpec((tm, d), lambda i, e: (i, 0)), _const_spec(g.shape), _const_spec(w_router.shape)]
    if final:
        args.append(g_final)
        specs.append(_const_spec(g_final.shape))
    args += [wg, wu, wd]
    specs += [pl.BlockSpec((None, d, f), lambda i, e: (e, 0, 0)),
              pl.BlockSpec((None, d, f), lambda i, e: (e, 0, 0)),
              pl.BlockSpec((None, f, d), lambda i, e: (e, 0, 0))]
    return pl.pallas_call(
        functools.partial(_moe_all_experts_kernel, final=final),
        grid=(r // tm, n_e),
        in_specs=specs,
        out_specs=pl.BlockSpec((tm, d), lambda i, e: (i, 0)),
        out_shape=jax.ShapeDtypeStruct((r, d), F32),
        scratch_shapes=[pltpu.VMEM((tm, d), BF16), pltpu.VMEM((tm, d), F32),
                        pltpu.VMEM((tm, 1), jnp.int32), pltpu.VMEM((tm, 1), jnp.int32),
                        pltpu.VMEM((tm, 1), F32), pltpu.VMEM((tm, 1), F32)],
        compiler_params=_cparams("parallel", "arbitrary"),
        name="moe_all_experts",
    )(*args)


MOE_TILE = 512
TOP_K = 2
ROW_ALIGN = SUBLANES
COMPACT_ROWS = -(-(TOP_K * MOE_TILE + N_EXPERTS * (ROW_ALIGN - 1)) // (2 * SUBLANES)) * (2 * SUBLANES)
SEGMENT_BITS = (MOE_TILE // ROW_ALIGN).bit_length()


def _route_kernel(x_ref, g_ref, wrt_ref, earlier_ref, lp_ref, col_ref, cnt_ref):
    tk = x_ref.shape[0]
    n_e = wrt_ref.shape[0]
    h = _rms(x_ref[...], g_ref[...])
    h_hi, h_lo = _split_bf16(h)
    w_hi, w_lo = _split_bf16(wrt_ref[...])
    nt = (((1,), (1,)), ((), ()))
    dg = lambda a, b: lax.dot_general(a, b, nt, preferred_element_type=F32)
    logits = dg(w_hi, h_hi) + (dg(w_hi, h_lo) + dg(w_lo, h_hi))
    sub = lax.broadcasted_iota(jnp.int32, logits.shape, 0)
    m1 = jnp.max(logits, axis=0, keepdims=True)
    i1 = jnp.min(jnp.where(logits == m1, sub, n_e), axis=0, keepdims=True)
    rest = jnp.where(sub == i1, -jnp.inf, logits)
    m2 = jnp.max(rest, axis=0, keepdims=True)
    i2 = jnp.min(jnp.where(rest == m2, sub, n_e), axis=0, keepdims=True)
    e2 = jnp.exp(m2 - m1)
    denom = 1.0 + e2

    pick1 = jnp.where(sub == i1, 1.0, 0.0)
    pick2 = jnp.where(sub == i2, 1.0, 0.0)
    cnt = pick1 + pick2
    rank = _dot(cnt.astype(BF16), earlier_ref[...])
    n = jnp.sum(cnt, axis=1, keepdims=True)
    n_pad = jnp.floor((n + (ROW_ALIGN - 1)) * (1.0 / ROW_ALIGN)) * ROW_ALIGN
    base, acc = [], jnp.zeros((1, 1), F32)
    for e in range(n_e):
        base.append(acc)
        acc = acc + n_pad[e:e + 1, :]
    pos = rank + jnp.concatenate(base, axis=0)
    lp1 = jnp.sum(pick1 * pos, axis=0, keepdims=True)
    lp2 = jnp.sum(pick2 * pos, axis=0, keepdims=True)
    lp_ref[0:1, :] = lp1.astype(jnp.int32)
    lp_ref[1:2, :] = lp2.astype(jnp.int32)
    info = jnp.concatenate([lp1, lp2, 1.0 / denom, e2 / denom, jnp.zeros((SUBLANES - 4, tk), F32)], axis=0)
    col_ref[...] = info.T
    cnt_ref[...] = jnp.broadcast_to(n, cnt_ref.shape)


def _route(x, g, wrt):
    t, d = x.shape
    n_e = wrt.shape[0]
    n_tiles = t // MOE_TILE
    idx = jnp.arange(MOE_TILE)
    earlier = (idx[:, None] < idx[None, :]).astype(BF16)
    return pl.pallas_call(
        _route_kernel,
        grid=(n_tiles,),
        in_specs=[pl.BlockSpec((MOE_TILE, d), lambda j: (j, 0)), _const_spec(g.shape), _const_spec(wrt.shape),
                  _const_spec(earlier.shape)],
        out_specs=[pl.BlockSpec((None, TOP_K, MOE_TILE), lambda j: (j, 0, 0)),
                   pl.BlockSpec((MOE_TILE, SUBLANES), lambda j: (j, 0)),
                   pl.BlockSpec((None, n_e, LANES), lambda j: (j, 0, 0))],
        out_shape=[jax.ShapeDtypeStruct((n_tiles, TOP_K, MOE_TILE), jnp.int32),
                   jax.ShapeDtypeStruct((t, SUBLANES), F32),
                   jax.ShapeDtypeStruct((n_tiles, n_e, LANES), F32)],
        compiler_params=_cparams("parallel"),
        name="moe_route",
    )(x, g, wrt, earlier)


def _segment_copies(src_ref, src_row, dst_ref, dst_row, groups, sem):
    out = []
    for bit in range(SEGMENT_BITS):
        rows = ROW_ALIGN << bit
        before = ((groups >> (bit + 1)) << (bit + 1)) * ROW_ALIGN
        cp = pltpu.make_async_copy(
            src_ref.at[pl.ds(pl.multiple_of(src_row + before, ROW_ALIGN), rows), :],
            dst_ref.at[pl.ds(pl.multiple_of(dst_row + before, ROW_ALIGN), rows), :], sem)
        out.append((((groups >> bit) & 1) == 1, cp))
    return out


def _start_all(copies):
    for pred, cp in copies:
        pl.when(pred)(cp.start)


def _wait_all(copies):
    for pred, cp in copies:
        pl.when(pred)(cp.wait)


def _dispatch_kernel(grp_ref, off_ref, dst_ref, total_ref, x_ref, g_ref, lp_ref, xs_ref, comp_ref, sem):
    j = pl.program_id(0)
    n_steps = pl.num_programs(0)
    n_e = grp_ref.shape[1]
    slot = j % 2

    def copies(step, s):
        out = []
        for e in range(n_e):
            out += _segment_copies(comp_ref.at[s], off_ref[step, e], xs_ref, dst_ref[step, e],
                                   grp_ref[step, e], sem.at[s])
        return out

    h = _rms(x_ref[...], g_ref[...]).astype(BF16)
    row = lax.broadcasted_iota(jnp.int32, (COMPACT_ROWS, h.shape[0]), 0)
    onehot = jnp.where(row == lp_ref[0:1, :], 1.0, jnp.where(row == lp_ref[1:2, :], 1.0, 0.0)).astype(BF16)
    compact = _dot(onehot, h)

    @pl.when(j >= 2)
    def _():
        _wait_all(copies(j - 2, slot))

    comp_ref[slot] = compact
    _start_all(copies(j, slot))

    @pl.when(j == n_steps - 1)
    def _():
        _wait_all(copies(j, slot))

        @pl.when(j >= 1)
        def _():
            _wait_all(copies(j - 1, 1 - slot))

        comp_ref[slot, 0:MOE_TILE, :] = jnp.zeros((MOE_TILE, comp_ref.shape[2]), F32)
        groups_left = (xs_ref.shape[0] - total_ref[0]) // ROW_ALIGN
        tail = []
        min_total = TOP_K * x_ref.shape[0] * grp_ref.shape[0]
        for c in range(-(-(xs_ref.shape[0] - min_total) // MOE_TILE)):
            tail += _segment_copies(comp_ref.at[slot], 0, xs_ref, total_ref[0] + c * MOE_TILE,
                                    jnp.clip(groups_left - c * (MOE_TILE // ROW_ALIGN), 0, MOE_TILE // ROW_ALIGN),
                                    sem.at[slot])
        _start_all(tail)
        _wait_all(tail)


def _dispatch(x, g, lp, groups, off, dst, total, n_rows):
    t, d = x.shape
    n_tiles = t // MOE_TILE
    grid_spec = pltpu.PrefetchScalarGridSpec(
        num_scalar_prefetch=4,
        grid=(n_tiles,),
        in_specs=[pl.BlockSpec((MOE_TILE, d), lambda j, *_: (j, 0)),
                  pl.BlockSpec(g.shape, lambda j, *_: (0, 0)),
                  pl.BlockSpec((None, TOP_K, MOE_TILE), lambda j, *_: (j, 0, 0))],
        out_specs=pl.BlockSpec(memory_space=pl.ANY),
        scratch_shapes=[pltpu.VMEM((2, COMPACT_ROWS, d), F32), pltpu.SemaphoreType.DMA((2,))])
    return pl.pallas_call(
        _dispatch_kernel,
        grid_spec=grid_spec,
        out_shape=jax.ShapeDtypeStruct((n_rows, d), F32),
        compiler_params=_cparams("arbitrary"),
        name="moe_dispatch",
    )(groups, off, dst, total, x, g, lp)


STEP_SKIP, STEP_FFN, STEP_ZERO = 0, 1, 2


def _grouped_ffn_kernel(tile_ref, grp_ref, lo_ref, hi_ref, first_ref, mode_ref,
                        xs_ref, wg_ref, wu_ref, wd_ref, ys_ref):
    w = pl.program_id(0)
    tm = xs_ref.shape[0]

    @pl.when(mode_ref[w] == STEP_ZERO)
    def _():
        ys_ref[...] = jnp.zeros_like(ys_ref)

    @pl.when(mode_ref[w] == STEP_FFN)
    def _():
        x = xs_ref[...].astype(BF16)
        y = _dot((jax.nn.silu(_dot(x, wg_ref[...])) * _dot(x, wu_ref[...])).astype(BF16), wd_ref[...])

        @pl.when(first_ref[w] == 1)
        def _():
            ys_ref[...] = y

        @pl.when(first_ref[w] == 0)
        def _():
            r = lax.broadcasted_iota(jnp.int32, (tm, 1), 0)
            mine = jnp.logical_and(r >= lo_ref[w], r < hi_ref[w])
            ys_ref[...] = jnp.where(mine, y, ys_ref[...])


def _grouped_ffn(xs, wg, wu, wd, tile, grp, lo, hi, first, mode):
    n_e, f, d = wd.shape
    grid_spec = pltpu.PrefetchScalarGridSpec(
        num_scalar_prefetch=6,
        grid=(tile.shape[0],),
        in_specs=[pl.BlockSpec((MOE_TILE, d), lambda w, tile, *_: (tile[w], 0)),
                  pl.BlockSpec((None, d, f), lambda w, tile, grp, *_: (grp[w], 0, 0)),
                  pl.BlockSpec((None, d, f), lambda w, tile, grp, *_: (grp[w], 0, 0)),
                  pl.BlockSpec((None, f, d), lambda w, tile, grp, *_: (grp[w], 0, 0))],
        out_specs=pl.BlockSpec((MOE_TILE, d), lambda w, tile, *_: (tile[w], 0)))
    return pl.pallas_call(
        _grouped_ffn_kernel,
        grid_spec=grid_spec,
        out_shape=jax.ShapeDtypeStruct(xs.shape, F32),
        compiler_params=_cparams("arbitrary"),
        name="moe_grouped_ffn",
    )(tile, grp, lo, hi, first, mode, xs, wg, wu, wd)


def _combine_kernel(grp_ref, off_ref, dst_ref, x_ref, col_ref, *rest, final):
    if final:
        gfin_ref, rest = rest[0], rest[1:]
    ys_ref, o_ref, comp_ref, sem = rest
    j = pl.program_id(0)
    n_steps = pl.num_programs(0)
    n_e = grp_ref.shape[1]
    slot = j % 2

    def copies(step, s):
        out = []
        for e in range(n_e):
            out += _segment_copies(ys_ref, dst_ref[step, e], comp_ref.at[s], off_ref[step, e],
                                   grp_ref[step, e], sem.at[s])
        return out

    @pl.when(j == 0)
    def _():
        comp_ref[...] = jnp.zeros_like(comp_ref)
        _start_all(copies(0, 0))

    @pl.when(j + 1 < n_steps)
    def _():
        _start_all(copies(j + 1, 1 - slot))

    _wait_all(copies(j, slot))
    y = comp_ref[slot].astype(BF16)
    col = col_ref[...]
    row = lax.broadcasted_iota(jnp.int32, (col.shape[0], COMPACT_ROWS), 1)
    out = x_ref[...]
    for k in range(TOP_K):
        pick = jnp.where(row == col[:, k:k + 1].astype(jnp.int32), 1.0, 0.0).astype(BF16)
        out = out + col[:, TOP_K + k:TOP_K + k + 1] * _dot(pick, y)
    o_ref[...] = _rms(out, gfin_ref[...]) if final else out


def _combine(x, col, ys, groups, off, dst, g_final):
    t, d = x.shape
    n_tiles = t // MOE_TILE
    final = g_final is not None
    args = [x, col]
    specs = [pl.BlockSpec((MOE_TILE, d), lambda j, *_: (j, 0)),
             pl.BlockSpec((MOE_TILE, SUBLANES), lambda j, *_: (j, 0))]
    if final:
        args.append(g_final)
        specs.append(pl.BlockSpec(g_final.shape, lambda j, *_: (0, 0)))
    args.append(ys)
    specs.append(pl.BlockSpec(memory_space=pl.ANY))
    grid_spec = pltpu.PrefetchScalarGridSpec(
        num_scalar_prefetch=3,
        grid=(n_tiles,),
        in_specs=specs,
        out_specs=pl.BlockSpec((MOE_TILE, d), lambda j, *_: (j, 0)),
        scratch_shapes=[pltpu.VMEM((2, COMPACT_ROWS, d), F32), pltpu.SemaphoreType.DMA((2,))])
    return pl.pallas_call(
        functools.partial(_combine_kernel, final=final),
        grid_spec=grid_spec,
        out_shape=jax.ShapeDtypeStruct((t, d), F32),
        compiler_params=_cparams("arbitrary"),
        name="moe_combine",
    )(groups, off, dst, *args)


def _moe_routed(x, g, w_router, wg, wu, wd, g_final):
    t, _ = x.shape
    n_e = wd.shape[0]
    n_tiles = t // MOE_TILE
    lp, col, cnt = _route(x, g, w_router.T)
    groups = (cnt[:, :, 0].astype(jnp.int32) + (ROW_ALIGN - 1)) // ROW_ALIGN
    n = groups * ROW_ALIGN
    off = jnp.cumsum(n, axis=1) - n
    per_e = jnp.sum(n, axis=0)
    ends = jnp.cumsum(per_e)
    starts = ends - per_e
    dst = starts[None, :] + jnp.cumsum(n, axis=0) - n
    max_rows = TOP_K * t + n_tiles * n_e * (ROW_ALIGN - 1)
    n_row_tiles = -(-max_rows // MOE_TILE)
    xs = _dispatch(x, g, lp, groups, off, dst, ends[-1:], (n_row_tiles + 1) * MOE_TILE)

    n_steps = n_row_tiles + n_e
    first_tile = starts // MOE_TILE
    n_items = jnp.where(per_e > 0, (ends - 1) // MOE_TILE - first_tile + 1, 0)
    item_end = jnp.cumsum(n_items)
    n_ffn = item_end[-1]
    w = jnp.arange(n_steps, dtype=jnp.int32)
    wc = jnp.minimum(w, n_ffn - 1)
    grp = jnp.sum(wc[:, None] >= item_end[None, :], axis=1).astype(jnp.int32)
    ffn_tile = first_tile[grp] + wc - (item_end - n_items)[grp]
    zero_tile = (ends[-1] - 1) // MOE_TILE + 1 + (w - n_ffn)
    tile = jnp.where(w < n_ffn, ffn_tile, jnp.minimum(zero_tile, n_row_tiles)).astype(jnp.int32)
    mode = jnp.where(w < n_ffn, STEP_FFN, jnp.where(zero_tile <= n_row_tiles, STEP_ZERO, STEP_SKIP))
    lo = jnp.clip(starts[grp] - tile * MOE_TILE, 0, MOE_TILE).astype(jnp.int32)
    hi = jnp.clip(ends[grp] - tile * MOE_TILE, 0, MOE_TILE).astype(jnp.int32)
    first = jnp.concatenate([jnp.ones((1,), jnp.int32), (tile[1:] != tile[:-1]).astype(jnp.int32)])
    ys = _grouped_ffn(xs, wg, wu, wd, tile, grp, lo, hi, first, mode.astype(jnp.int32))
    return _combine(x, col, ys, groups, off, dst, g_final)


def _block_diag_gates(w_a, w_x):
    per_group = LRU_GROUP // LRU_HEAD_DIM
    eye = jnp.eye(per_group, dtype=w_a.dtype)

    def bd(w):
        w = w.reshape(D_LRU // LRU_GROUP, per_group, LRU_HEAD_DIM, LRU_HEAD_DIM)
        return jnp.einsum('ghij,hk->ghikj', w, eye).reshape(D_LRU // LRU_GROUP, LRU_GROUP, LRU_GROUP)

    return jnp.concatenate([bd(w_a), bd(w_x)], axis=-1).astype(BF16)


def kernel(x_prompt, x_sample, state_shortconv, state_lru_conv, state_lru_h, cache_mem_k, cache_mem_v,
           mem_prompt, norm_mix, w_in, w_short_conv, w_lru_conv, b_lru_conv, w_lru_a, b_lru_a, w_lru_x,
           b_lru_x, lru_lambda, norm_conv_out, norm_lru_out, w_mix_out, norm_xattn, norm_mem, w_q, w_k,
           w_v, w_o, norm_ffn, w_ff_gate, w_ff_up, w_ff_down, w_router, w_moe_gate, w_moe_up, w_moe_down,
           norm_final):
    depth = w_in.shape[0]
    nb, t, d = x_prompt.shape
    ns = x_sample.shape[0]
    n_mem = mem_prompt.shape[1]
    row = lambda a: a.reshape(1, -1)

    layers = []
    for l in range(depth):
        lp = dict(
            gmix=row(norm_mix[l]), win=w_in[l].astype(BF16), wsc=w_short_conv[l], wlc=w_lru_conv[l],
            blc=row(b_lru_conv[l]), wg=_block_diag_gates(w_lru_a[l], w_lru_x[l]), ba=row(b_lru_a[l]),
            bx=row(b_lru_x[l]), lam=row(lru_lambda[l]), gco=row(norm_conv_out[l]), glo=row(norm_lru_out[l]),
            wout=w_mix_out[l].astype(BF16), gx=row(norm_xattn[l]),
            wq=w_q[l].astype(BF16), wo=w_o[l].astype(BF16),
            gffn=row(norm_ffn[l]), gfin=row(norm_final) if l == depth - 1 else None)
        j = l // 2
        if l % 2 == 0:
            lp['ffw'] = (w_ff_gate[j].astype(BF16), w_ff_up[j].astype(BF16), w_ff_down[j].astype(BF16))
            lp['wr'] = None
        else:
            lp['ffw'] = (w_moe_gate[j].astype(BF16), w_moe_up[j].astype(BF16), w_moe_down[j].astype(BF16))
            lp['wr'] = jnp.pad(w_router[j], ((0, 0), (0, LANES - N_EXPERTS)))
            lp['wr_raw'] = w_router[j]
        layers.append(lp)

    x = x_prompt
    p_mem_k, p_mem_v, kb, vb = _kv_proj(mem_prompt, norm_mem.reshape(depth, 1, d), w_k.astype(BF16),
                                        w_v.astype(BF16))
    pc, plc, ph = [], [], []
    for l, lp in enumerate(layers):
        x, tail_p, tail_x, h_last = _mixer_prompt(x, lp)
        x = _xattn_prompt(x, lp['gx'], lp['wq'], kb[l], vb[l], lp['wo'])
        if lp['wr'] is None:
            x = _ffn_dense(x.reshape(nb * t, d), lp['gffn'], *lp['ffw'], tm=512, g_final=lp['gfin'])
        else:
            x = _moe_routed(x.reshape(nb * t, d), lp['gffn'], lp['wr_raw'], *lp['ffw'], lp['gfin'])
        x = x.reshape(nb, t, d)
        tail_p = jnp.swapaxes(tail_p.reshape(MAX_CONV_HISTORY, nb, D_CONV), 0, 1)
        tail_x = jnp.swapaxes(tail_x.reshape(MAX_CONV_HISTORY, nb, D_LRU), 0, 1)
        pc.append(tail_p[:, MAX_CONV_HISTORY - (w_short_conv.shape[1] - 1):])
        plc.append(tail_x[:, MAX_CONV_HISTORY - (w_lru_conv.shape[1] - 1):])
        ph.append(h_last)
    y_prompt = x

    x = x_sample.reshape(ns, d)
    k8 = _to_head_rows(cache_mem_k.reshape(depth, ns, n_mem, d))
    v8 = _to_head_rows(cache_mem_v.reshape(depth, ns, n_mem, d))
    sc, slc, sh = [], [], []
    for l, lp in enumerate(layers):
        x, p_new, xb_new, h_new = _mixer_sample(x, state_shortconv[l], state_lru_conv[l], state_lru_h[l], lp)
        q = _norm_matmul(x, lp['gx'], lp['wq'], ns)
        o = _xattn_sample(q, k8, v8, l)
        x = _matmul_residual(o, lp['wo'], x, ns)
        if lp['wr'] is None:
            x = _ffn_dense(x, lp['gffn'], *lp['ffw'], tm=ns, g_final=lp['gfin'])
        else:
            x = _moe_all_experts(x, lp['gffn'], lp['wr'], *lp['ffw'], tm=ns, g_final=lp['gfin'])
        sc.append(jnp.stack([state_shortconv[l][:, 1], p_new], axis=1))
        slc.append(jnp.stack([state_lru_conv[l][:, 1], state_lru_conv[l][:, 2], xb_new], axis=1))
        sh.append(h_new)
    y_sample = x.reshape(ns, 1, d)

    return (y_prompt, y_sample, jnp.stack(pc), jnp.stack(plc), jnp.stack(ph), p_mem_k, p_mem_v,
            jnp.stack(sc), jnp.stack(slc), jnp.stack(sh))
```

```python
import functools

import jax
import jax.numpy as jnp
from jax import lax
from jax.experimental import pallas as pl
from jax.experimental.pallas import tpu as pltpu

F32 = jnp.float32
BF16 = jnp.bfloat16

EPS = 1e-6
LRU_C = 8.0
D_CONV = 512
D_LRU = 512
N_LRU_HEADS = 8
LRU_HEAD_DIM = 64
LRU_GROUP = 256
N_MEM_HEADS = 4
MEM_HEAD_DIM = 256
N_EXPERTS = 8
SUBLANES = 8
LANES = 128
VMEM_LIMIT_BYTES = 56 * 1024 * 1024


def _cparams(*sem):
    return pltpu.CompilerParams(dimension_semantics=sem, vmem_limit_bytes=VMEM_LIMIT_BYTES)


def _rms(x, g):
    return x * lax.rsqrt(jnp.mean(x * x, axis=-1, keepdims=True) + EPS) * g


def _dot(a, b):
    return jnp.dot(a, b, preferred_element_type=F32)


def _lru_gates(xc, wg_ref, ba, bx, lam):
    neg_lam = -lam
    softplus = jnp.maximum(neg_lam, 0.0) + jnp.log1p(jnp.exp(-jnp.abs(neg_lam)))
    a_parts, u_parts = [], []
    for g in range(D_LRU // LRU_GROUP):
        sl = slice(g * LRU_GROUP, (g + 1) * LRU_GROUP)
        xg = xc[:, sl]
        gates = _dot(xg.astype(BF16), wg_ref[g])
        r = jax.nn.sigmoid(gates[:, :LRU_GROUP] + ba[:, sl])
        i = jax.nn.sigmoid(gates[:, LRU_GROUP:] + bx[:, sl])
        log_a = (-LRU_C) * r * softplus[:, sl]
        th = jnp.tanh(log_a)
        one_minus_a2 = (-2.0 * th) / (1.0 - th)
        a_parts.append(jnp.exp(log_a))
        u_parts.append(jnp.sqrt(one_minus_a2) * (i * xg))
    return jnp.concatenate(a_parts, axis=-1), jnp.concatenate(u_parts, axis=-1)


def _mix_out(x, ya, yb, gco, glo, wout_ref):
    y = jnp.concatenate([_rms(ya, gco), _rms(yb, glo)], axis=-1).astype(BF16)
    return x + _dot(y, wout_ref[...])


MAX_CONV_HISTORY = 3


def _conv_taps(ext_ref, w_ref, rows, nb):
    width = w_ref.shape[0]
    hist = MAX_CONV_HISTORY * nb
    acc = ext_ref[hist:hist + rows, :] * w_ref[width - 1:width, :]
    for s in range(1, width):
        acc = acc + ext_ref[hist - s * nb:hist - s * nb + rows, :] * w_ref[width - 1 - s:width - s, :]
    return acc


def _mixer_prompt_kernel(x_hbm, gmix_ref, win_ref, wsc_ref, wlc_ref, blc_ref, wg_ref, ba_ref, bx_ref,
                         lam_ref, gco_ref, glo_ref, wout_ref,
                         xo_hbm, tailp_ref, tailx_ref, hout_ref,
                         xin_ref, xout_ref, in_sem, out_sem, pext_ref, xext_ref, h_ref, a_ref, u_ref):
    _, tt, nb, d = xin_ref.shape
    rows = tt * nb
    hist = MAX_CONV_HISTORY * nb
    i = pl.program_id(0)
    n_steps = pl.num_programs(0)
    slot = i % 2

    def in_copies(step, s):
        return [pltpu.make_async_copy(x_hbm.at[b, pl.ds(step * tt, tt), :], xin_ref.at[s, :, b, :],
                                      in_sem.at[s]) for b in range(nb)]

    def out_copies(step, s):
        return [pltpu.make_async_copy(xout_ref.at[s, :, b, :], xo_hbm.at[b, pl.ds(step * tt, tt), :],
                                      out_sem.at[s]) for b in range(nb)]

    @pl.when(i == 0)
    def _():
        pext_ref[0:hist, :] = jnp.zeros((hist, pext_ref.shape[1]), F32)
        xext_ref[0:hist, :] = jnp.zeros((hist, xext_ref.shape[1]), F32)
        h_ref[...] = jnp.zeros_like(h_ref)
        for cp in in_copies(0, 0):
            cp.start()

    @pl.when(i + 1 < n_steps)
    def _():
        for cp in in_copies(i + 1, 1 - slot):
            cp.start()

    for cp in in_copies(i, slot):
        cp.wait()
    x = xin_ref[slot].reshape(rows, d)
    z = _dot(_rms(x, gmix_ref[...]).astype(BF16), win_ref[...])
    gb = z[:, 0:D_CONV]
    gate = z[:, 3 * D_CONV + D_LRU:]
    pext_ref[hist:hist + rows, :] = z[:, D_CONV:2 * D_CONV] * z[:, 2 * D_CONV:3 * D_CONV]
    xext_ref[hist:hist + rows, :] = z[:, 3 * D_CONV:3 * D_CONV + D_LRU]

    ya = gb * _conv_taps(pext_ref, wsc_ref, rows, nb)
    xc = _conv_taps(xext_ref, wlc_ref, rows, nb) + blc_ref[...]
    for ext_ref, tail_ref in ((pext_ref, tailp_ref), (xext_ref, tailx_ref)):
        tail = ext_ref[rows:rows + hist, :]
        ext_ref[0:hist, :] = tail
        tail_ref[...] = tail

    a, u = _lru_gates(xc, wg_ref, ba_ref[...], bx_ref[...], lam_ref[...])
    a_ref[...] = a
    u_ref[...] = u

    def step(t, h):
        grp = pl.ds(pl.multiple_of(t * nb, nb), nb)
        h = a_ref[grp, :] * h + u_ref[grp, :]
        u_ref[grp, :] = h
        return h

    h_last = lax.fori_loop(0, tt, step, h_ref[...], unroll=8)
    h_ref[...] = h_last
    hout_ref[...] = h_last

    yb = u_ref[...] * jax.nn.gelu(gate)
    out = _mix_out(x, ya, yb, gco_ref[...], glo_ref[...], wout_ref)

    @pl.when(i >= 2)
    def _():
        for cp in out_copies(i - 2, slot):
            cp.wait()

    xout_ref[slot] = out.reshape(tt, nb, d)
    for cp in out_copies(i, slot):
        cp.start()

    @pl.when(i == n_steps - 1)
    def _():
        for cp in out_copies(i, slot):
            cp.wait()

        @pl.when(i >= 1)
        def _():
            for cp in out_copies(i - 1, 1 - slot):
                cp.wait()


def _const_spec(shape):
    return pl.BlockSpec(shape, lambda *_: (0,) * len(shape))


def _mixer_prompt(x, p, tt=128):
    nb, t, d = x.shape
    small = [p['gmix'], p['win'], p['wsc'], p['wlc'], p['blc'], p['wg'], p['ba'], p['bx'], p['lam'],
             p['gco'], p['glo'], p['wout']]
    rows = nb * tt
    hist = MAX_CONV_HISTORY * nb
    return pl.pallas_call(
        _mixer_prompt_kernel,
        grid=(t // tt,),
        in_specs=[pl.BlockSpec(memory_space=pl.ANY)] + [_resident_spec(a.shape) for a in small],
        out_specs=[pl.BlockSpec(memory_space=pl.ANY),
                   _const_spec((hist, D_CONV)), _const_spec((hist, D_LRU)), _const_spec((nb, D_LRU))],
        out_shape=[jax.ShapeDtypeStruct((nb, t, d), F32),
                   jax.ShapeDtypeStruct((hist, D_CONV), F32),
                   jax.ShapeDtypeStruct((hist, D_LRU), F32),
                   jax.ShapeDtypeStruct((nb, D_LRU), F32)],
        scratch_shapes=[pltpu.VMEM((2, tt, nb, d), F32), pltpu.VMEM((2, tt, nb, d), F32),
                        pltpu.SemaphoreType.DMA((2,)), pltpu.SemaphoreType.DMA((2,)),
                        pltpu.VMEM((hist + rows, D_CONV), F32), pltpu.VMEM((hist + rows, D_LRU), F32),
                        pltpu.VMEM((nb, D_LRU), F32),
                        pltpu.VMEM((rows, D_LRU), F32), pltpu.VMEM((rows, D_LRU), F32)],
        compiler_params=_cparams("arbitrary"),
        name="mixer_prompt",
    )(x, *small)


def _mixer_sample_kernel(x_ref, sc0_ref, sc1_ref, lc0_ref, lc1_ref, lc2_ref, h0_ref,
                         gmix_ref, win_ref, wsc_ref, wlc_ref, blc_ref, wg_ref, ba_ref, bx_ref,
                         lam_ref, gco_ref, glo_ref, wout_ref, gx_ref, wq_ref,
                         xo_ref, q_ref, pnew_ref, xbnew_ref, hout_ref):
    x = x_ref[...]
    z = _dot(_rms(x, gmix_ref[...]).astype(BF16), win_ref[...])
    gb = z[:, 0:D_CONV]
    p = z[:, D_CONV:2 * D_CONV] * z[:, 2 * D_CONV:3 * D_CONV]
    xb = z[:, 3 * D_CONV:3 * D_CONV + D_LRU]
    gate = z[:, 3 * D_CONV + D_LRU:]

    ya = gb * (sc0_ref[...] * wsc_ref[0:1, :] + sc1_ref[...] * wsc_ref[1:2, :] + p * wsc_ref[2:3, :])
    xc = (lc0_ref[...] * wlc_ref[0:1, :] + lc1_ref[...] * wlc_ref[1:2, :] + lc2_ref[...] * wlc_ref[2:3, :]
          + xb * wlc_ref[3:4, :]) + blc_ref[...]
    a, u = _lru_gates(xc, wg_ref, ba_ref[...], bx_ref[...], lam_ref[...])
    h = a * h0_ref[...] + u
    yb = h * jax.nn.gelu(gate)
    x1 = _mix_out(x, ya, yb, gco_ref[...], glo_ref[...], wout_ref)
    xo_ref[...] = x1
    q_ref[...] = _dot(_rms(x1, gx_ref[...]).astype(BF16), wq_ref[...])
    pnew_ref[...] = p
    xbnew_ref[...] = xb
    hout_ref[...] = h


def _mixer_sample(x, sc, lc, h0, p):
    nb, d = x.shape
    args = [x, sc[:, 0], sc[:, 1], lc[:, 0], lc[:, 1], lc[:, 2], h0,
            p['gmix'], p['win'], p['wsc'], p['wlc'], p['blc'], p['wg'], p['ba'], p['bx'], p['lam'],
            p['gco'], p['glo'], p['wout'], p['gx'], p['wq']]
    return pl.pallas_call(
        _mixer_sample_kernel,
        grid=(1,),
        in_specs=[_const_spec(a.shape) for a in args],
        out_specs=[_const_spec((nb, d)), _const_spec((nb, d)), _const_spec((nb, D_CONV)),
                   _const_spec((nb, D_LRU)), _const_spec((nb, D_LRU))],
        out_shape=[jax.ShapeDtypeStruct((nb, d), F32), jax.ShapeDtypeStruct((nb, d), F32),
                   jax.ShapeDtypeStruct((nb, D_CONV), F32), jax.ShapeDtypeStruct((nb, D_LRU), F32),
                   jax.ShapeDtypeStruct((nb, D_LRU), F32)],
        compiler_params=_cparams("arbitrary"),
        name="mixer_sample",
    )(*args)


def _xattn_prompt_kernel(x_ref, g_ref, wq_ref, k_ref, v_ref, wo_ref, o_ref):
    x = x_ref[...]
    q = _dot(_rms(x, g_ref[...]).astype(BF16), wq_ref[...])
    heads = []
    for h in range(N_MEM_HEADS):
        sl = slice(h * MEM_HEAD_DIM, (h + 1) * MEM_HEAD_DIM)
        kh = k_ref[:, sl]
        vh = v_ref[:, sl]
        s = lax.dot_general(q[:, sl].astype(BF16), kh, (((1,), (1,)), ((), ())),
                            preferred_element_type=F32) * (MEM_HEAD_DIM ** -0.5)
        e = jnp.exp(s - jnp.max(s, axis=-1, keepdims=True))
        pr = e / jnp.sum(e, axis=-1, keepdims=True)
        heads.append(_dot(pr.astype(BF16), vh))
    o = jnp.concatenate(heads, axis=-1).astype(BF16)
    o_ref[...] = x + _dot(o, wo_ref[...])


def _xattn_prompt(x, g, wq, k, v, wo, tq=1024):
    nb, t, d = x.shape
    n_mem = k.shape[1]
    return pl.pallas_call(
        _xattn_prompt_kernel,
        grid=(nb, t // tq),
        in_specs=[pl.BlockSpec((None, tq, d), lambda b, i: (b, i, 0)), _const_spec(g.shape),
                  _const_spec(wq.shape),
                  pl.BlockSpec((None, n_mem, d), lambda b, i: (b, 0, 0)),
                  pl.BlockSpec((None, n_mem, d), lambda b, i: (b, 0, 0)),
                  _const_spec(wo.shape)],
        out_specs=pl.BlockSpec((None, tq, d), lambda b, i: (b, i, 0)),
        out_shape=jax.ShapeDtypeStruct((nb, t, d), F32),
        compiler_params=_cparams("parallel", "parallel"),
        name="xattn_prompt",
    )(x, g, wq, k, v, wo)


def _kv_proj_kernel(mem_ref, g_ref, wk_ref, wv_ref, k5_ref, v5_ref, kb_ref, vb_ref):
    bb, n_mem, d = mem_ref.shape
    m = _rms(mem_ref[...].reshape(bb * n_mem, d), g_ref[...]).astype(BF16)
    for w_ref, o5_ref, ob_ref in ((wk_ref, k5_ref, kb_ref), (wv_ref, v5_ref, vb_ref)):
        y = _dot(m, w_ref[...]).reshape(bb, n_mem, d)
        ob_ref[...] = y.astype(BF16)
        for h in range(N_MEM_HEADS):
            o5_ref[:, :, h, :] = y[:, :, h * MEM_HEAD_DIM:(h + 1) * MEM_HEAD_DIM]


def _kv_proj(mem, gmem, wk, wv, bb=2):
    nb, n_mem, d = mem.shape
    depth = wk.shape[0]
    w_spec = pl.BlockSpec((None, d, d), lambda l, i: (l, 0, 0))
    o5_spec = pl.BlockSpec((None, bb, n_mem, N_MEM_HEADS, MEM_HEAD_DIM), lambda l, i: (l, i, 0, 0, 0))
    ob_spec = pl.BlockSpec((None, bb, n_mem, d), lambda l, i: (l, i, 0, 0))
    o5_shape = jax.ShapeDtypeStruct((depth, nb, n_mem, N_MEM_HEADS, MEM_HEAD_DIM), F32)
    ob_shape = jax.ShapeDtypeStruct((depth, nb, n_mem, d), BF16)
    return pl.pallas_call(
        _kv_proj_kernel,
        grid=(depth, nb // bb),
        in_specs=[pl.BlockSpec((bb, n_mem, d), lambda l, i: (i, 0, 0)),
                  pl.BlockSpec((None, 1, d), lambda l, i: (l, 0, 0)), w_spec, w_spec],
        out_specs=[o5_spec, o5_spec, ob_spec, ob_spec],
        out_shape=[o5_shape, o5_shape, ob_shape, ob_shape],
        compiler_params=_cparams("parallel", "parallel"),
        name="kv_proj",
    )(mem, gmem, wk, wv)


LANE_TILES_PER_HEAD = MEM_HEAD_DIM // LANES
HEAD_ROWS = N_MEM_HEADS * LANE_TILES_PER_HEAD


def _to_head_rows(a):
    lead = a.shape[:-1]
    n = len(lead)
    a = a.reshape(*lead, N_MEM_HEADS, LANE_TILES_PER_HEAD, LANES)
    return jnp.swapaxes(a, n, n + 1).reshape(*lead, HEAD_ROWS, LANES)


def _from_head_rows(a):
    lead = a.shape[:-2]
    n = len(lead)
    a = a.reshape(*lead, LANE_TILES_PER_HEAD, N_MEM_HEADS, LANES)
    return jnp.swapaxes(a, n, n + 1).reshape(*lead, N_MEM_HEADS * MEM_HEAD_DIM)


def _xattn_sample_kernel(q_ref, k_ref, v_ref, o_ref):
    n_mem = k_ref.shape[1]
    ones = jnp.ones((LANES, LANES), BF16)
    for b in range(q_ref.shape[0]):
        prod = (k_ref[b] * q_ref[b][None]).reshape(n_mem * HEAD_ROWS, LANES).astype(BF16)
        part = _dot(prod, ones).reshape(n_mem, HEAD_ROWS, LANES)
        s = (part + pltpu.roll(part, N_MEM_HEADS, axis=1)) * (MEM_HEAD_DIM ** -0.5)
        e = jnp.exp(s - jnp.max(s, axis=0, keepdims=True))
        pr = e / jnp.sum(e, axis=0, keepdims=True)
        o_ref[b] = jnp.sum(pr * v_ref[b], axis=0)


def _xattn_sample(q, k8, v8, layer, bb=8):
    nb, d = q.shape
    n_mem = k8.shape[2]
    kv_spec = pl.BlockSpec((None, bb, n_mem, HEAD_ROWS, LANES), lambda i: (layer, i, 0, 0, 0))
    out = pl.pallas_call(
        _xattn_sample_kernel,
        grid=(nb // bb,),
        in_specs=[pl.BlockSpec((bb, HEAD_ROWS, LANES), lambda i: (i, 0, 0)), kv_spec, kv_spec],
        out_specs=pl.BlockSpec((bb, HEAD_ROWS, LANES), lambda i: (i, 0, 0)),
        out_shape=jax.ShapeDtypeStruct((nb, HEAD_ROWS, LANES), F32),
        compiler_params=_cparams("parallel"),
        name="xattn_sample",
    )(_to_head_rows(q), k8, v8)
    return _from_head_rows(out)


def _split_bf16(a):
    hi = a.astype(BF16)
    return hi, (a - hi.astype(F32)).astype(BF16)


def _top2(logits):
    n = logits.shape[-1]
    lane = lax.broadcasted_iota(jnp.int32, logits.shape, 1)
    m1 = jnp.max(logits, axis=-1, keepdims=True)
    i1 = jnp.min(jnp.where(logits == m1, lane, n), axis=-1, keepdims=True)
    rest = jnp.where(lane == i1, -jnp.inf, logits)
    m2 = jnp.max(rest, axis=-1, keepdims=True)
    i2 = jnp.min(jnp.where(rest == m2, lane, n), axis=-1, keepdims=True)
    e2 = jnp.exp(m2 - m1)
    denom = 1.0 + e2
    return i1, i2, 1.0 / denom, e2 / denom


def _resident_spec(shape):
    return pl.BlockSpec(shape, lambda *_: (0,) * len(shape), pipeline_mode=pl.Buffered(1))


def _add_attn_proj(x, rest, proj):
    if not proj:
        return x, rest
    a_ref, wo_ref = rest[:2]
    return x + _dot(a_ref[...].astype(BF16), wo_ref[...]), rest[2:]


def _ffn_dense_kernel(x_ref, g_ref, *rest, final, proj):
    x, rest = _add_attn_proj(x_ref[...], rest, proj)
    if final:
        gfin_ref, rest = rest[0], rest[1:]
    wg_ref, wu_ref, wd_ref, o_ref = rest
    h = _rms(x, g_ref[...]).astype(BF16)
    act = (jax.nn.silu(_dot(h, wg_ref[...])) * _dot(h, wu_ref[...])).astype(BF16)
    out = x + _dot(act, wd_ref[...])
    o_ref[...] = _rms(out, gfin_ref[...]) if final else out


def _ffn_dense(x, g, wg, wu, wd, tm, g_final=None, attn=None):
    r, d = x.shape
    final, proj = g_final is not None, attn is not None
    args, specs = [x, g], [pl.BlockSpec((tm, d), lambda i: (i, 0)), _const_spec(g.shape)]
    if proj:
        args += list(attn)
        specs += [pl.BlockSpec((tm, attn[0].shape[1]), lambda i: (i, 0)), _resident_spec(attn[1].shape)]
    if final:
        args.append(g_final)
        specs.append(_const_spec(g_final.shape))
    args += [wg, wu, wd]
    specs += [_resident_spec(wg.shape), _resident_spec(wu.shape), _resident_spec(wd.shape)]
    return pl.pallas_call(
        functools.partial(_ffn_dense_kernel, final=final, proj=proj),
        grid=(r // tm,),
        in_specs=specs,
        out_specs=pl.BlockSpec((tm, d), lambda i: (i, 0)),
        out_shape=jax.ShapeDtypeStruct((r, d), F32),
        compiler_params=_cparams("parallel"),
        name="ffn_dense",
    )(*args)


def _moe_all_experts_kernel(x_ref, g_ref, wr_ref, *rest, final, proj):
    if proj:
        attn_refs, rest = rest[:2], rest[2:]
    if final:
        gfin_ref, rest = rest[0], rest[1:]
    wg_ref, wu_ref, wd_ref, o_ref, h_ref, acc_ref, i1_ref, i2_ref, g1_ref, g2_ref = rest
    e = pl.program_id(1)

    @pl.when(e == 0)
    def _():
        x = x_ref[...]
        if proj:
            x, _ = _add_attn_proj(x, attn_refs, proj)
        h = _rms(x, g_ref[...])
        h_ref[...] = h.astype(BF16)
        acc_ref[...] = x
        h_hi, h_lo = _split_bf16(h)
        w_hi, w_lo = _split_bf16(wr_ref[...])
        logits = _dot(h_hi, w_hi) + (_dot(h_hi, w_lo) + _dot(h_lo, w_hi))
        lane = lax.broadcasted_iota(jnp.int32, logits.shape, 1)
        logits = jnp.where(lane < N_EXPERTS, logits, -jnp.inf)
        i1_ref[...], i2_ref[...], g1_ref[...], g2_ref[...] = _top2(logits)

    hb = h_ref[...]
    y = _dot((jax.nn.silu(_dot(hb, wg_ref[...])) * _dot(hb, wu_ref[...])).astype(BF16), wd_ref[...])
    c = jnp.where(i1_ref[...] == e, g1_ref[...], 0.0) + jnp.where(i2_ref[...] == e, g2_ref[...], 0.0)
    acc_ref[...] += c * y

    @pl.when(e == pl.num_programs(1) - 1)
    def _():
        out = acc_ref[...]
        o_ref[...] = _rms(out, gfin_ref[...]) if final else out


def _moe_all_experts(x, g, w_router, wg, wu, wd, tm, g_final=None, attn=None):
    r, d = x.shape
    n_e, f, _ = wd.shape
    final, proj = g_final is not None, attn is not None
    args = [x, g, w_router]
    specs = [pl.BlockSpec((tm, d), lambda i, e: (i, 0)), _const_spec(g.shape), _const_spec(w_router.shape)]
    if proj:
        args += list(attn)
        specs += [pl.BlockSpec((tm, attn[0].shape[1]), lambda i, e: (i, 0)), _const_spec(attn[1].shape)]
    if final:
        args.append(g_final)
        specs.append(_const_spec(g_final.shape))
    args += [wg, wu, wd]
    specs += [pl.BlockSpec((None, d, f), lambda i, e: (e, 0, 0)),
              pl.BlockSpec((None, d, f), lambda i, e: (e, 0, 0)),
              pl.BlockSpec((None, f, d), lambda i, e: (e, 0, 0))]
    return pl.pallas_call(
        functools.partial(_moe_all_experts_kernel, final=final, proj=proj),
        grid=(r // tm, n_e),
        in_specs=specs,
        out_specs=pl.BlockSpec((tm, d), lambda i, e: (i, 0)),
        out_shape=jax.ShapeDtypeStruct((r, d), F32),
        scratch_shapes=[pltpu.VMEM((tm, d), BF16), pltpu.VMEM((tm, d), F32),
                        pltpu.VMEM((tm, 1), jnp.int32), pltpu.VMEM((tm, 1), jnp.int32),
                        pltpu.VMEM((tm, 1), F32), pltpu.VMEM((tm, 1), F32)],
        compiler_params=_cparams("parallel", "arbitrary"),
        name="moe_all_experts",
    )(*args)


MOE_TILE = 512
TOP_K = 2
ROW_ALIGN = SUBLANES
COMPACT_ROWS = -(-(TOP_K * MOE_TILE + N_EXPERTS * (ROW_ALIGN - 1)) // (2 * SUBLANES)) * (2 * SUBLANES)
SEGMENT_BITS = (MOE_TILE // ROW_ALIGN).bit_length()


def _route_kernel(x_ref, g_ref, wrt_ref, earlier_ref, lp_ref, col_ref, cnt_ref):
    tk = x_ref.shape[0]
    n_e = wrt_ref.shape[0]
    h = _rms(x_ref[...], g_ref[...])
    h_hi, h_lo = _split_bf16(h)
    w_hi, w_lo = _split_bf16(wrt_ref[...])
    nt = (((1,), (1,)), ((), ()))
    dg = lambda a, b: lax.dot_general(a, b, nt, preferred_element_type=F32)
    logits = dg(w_hi, h_hi) + (dg(w_hi, h_lo) + dg(w_lo, h_hi))
    sub = lax.broadcasted_iota(jnp.int32, logits.shape, 0)
    m1 = jnp.max(logits, axis=0, keepdims=True)
    i1 = jnp.min(jnp.where(logits == m1, sub, n_e), axis=0, keepdims=True)
    rest = jnp.where(sub == i1, -jnp.inf, logits)
    m2 = jnp.max(rest, axis=0, keepdims=True)
    i2 = jnp.min(jnp.where(rest == m2, sub, n_e), axis=0, keepdims=True)
    e2 = jnp.exp(m2 - m1)
    denom = 1.0 + e2

    pick1 = jnp.where(sub == i1, 1.0, 0.0)
    pick2 = jnp.where(sub == i2, 1.0, 0.0)
    cnt = pick1 + pick2
    rank = _dot(cnt.astype(BF16), earlier_ref[...])
    n = jnp.sum(cnt, axis=1, keepdims=True)
    n_pad = jnp.floor((n + (ROW_ALIGN - 1)) * (1.0 / ROW_ALIGN)) * ROW_ALIGN
    base, acc = [], jnp.zeros((1, 1), F32)
    for e in range(n_e):
        base.append(acc)
        acc = acc + n_pad[e:e + 1, :]
    pos = rank + jnp.concatenate(base, axis=0)
    lp1 = jnp.sum(pick1 * pos, axis=0, keepdims=True)
    lp2 = jnp.sum(pick2 * pos, axis=0, keepdims=True)
    lp_ref[0:1, :] = lp1.astype(jnp.int32)
    lp_ref[1:2, :] = lp2.astype(jnp.int32)
    info = jnp.concatenate([lp1, lp2, 1.0 / denom, e2 / denom, jnp.zeros((SUBLANES - 4, tk), F32)], axis=0)
    col_ref[...] = info.T
    cnt_ref[...] = jnp.broadcast_to(n, cnt_ref.shape)


def _route(x, g, wrt):
    t, d = x.shape
    n_e = wrt.shape[0]
    n_tiles = t // MOE_TILE
    idx = jnp.arange(MOE_TILE)
    earlier = (idx[:, None] < idx[None, :]).astype(BF16)
    return pl.pallas_call(
        _route_kernel,
        grid=(n_tiles,),
        in_specs=[pl.BlockSpec((MOE_TILE, d), lambda j: (j, 0)), _const_spec(g.shape), _const_spec(wrt.shape),
                  _const_spec(earlier.shape)],
        out_specs=[pl.BlockSpec((None, TOP_K, MOE_TILE), lambda j: (j, 0, 0)),
                   pl.BlockSpec((MOE_TILE, SUBLANES), lambda j: (j, 0)),
                   pl.BlockSpec((None, n_e, LANES), lambda j: (j, 0, 0))],
        out_shape=[jax.ShapeDtypeStruct((n_tiles, TOP_K, MOE_TILE), jnp.int32),
                   jax.ShapeDtypeStruct((t, SUBLANES), F32),
                   jax.ShapeDtypeStruct((n_tiles, n_e, LANES), F32)],
        compiler_params=_cparams("parallel"),
        name="moe_route",
    )(x, g, wrt, earlier)


def _segment_copies(src_ref, src_row, dst_ref, dst_row, groups, sem):
    out = []
    for bit in range(SEGMENT_BITS):
        rows = ROW_ALIGN << bit
        before = ((groups >> (bit + 1)) << (bit + 1)) * ROW_ALIGN
        cp = pltpu.make_async_copy(
            src_ref.at[pl.ds(pl.multiple_of(src_row + before, ROW_ALIGN), rows), :],
            dst_ref.at[pl.ds(pl.multiple_of(dst_row + before, ROW_ALIGN), rows), :], sem)
        out.append((((groups >> bit) & 1) == 1, cp))
    return out


def _start_all(copies):
    for pred, cp in copies:
        pl.when(pred)(cp.start)


def _wait_all(copies):
    for pred, cp in copies:
        pl.when(pred)(cp.wait)


def _dispatch_kernel(grp_ref, off_ref, dst_ref, pad_ref, x_ref, g_ref, lp_ref, xs_ref, comp_ref, sem):
    j = pl.program_id(0)
    n_steps = pl.num_programs(0)
    n_e = grp_ref.shape[1]
    slot = j % 2

    def copies(step, s):
        out = []
        for e in range(n_e):
            out += _segment_copies(comp_ref.at[s], off_ref[step, e], xs_ref, dst_ref[step, e],
                                   grp_ref[step, e], sem.at[s])
        return out

    h = _rms(x_ref[...], g_ref[...]).astype(BF16)
    row = lax.broadcasted_iota(jnp.int32, (COMPACT_ROWS, h.shape[0]), 0)
    onehot = jnp.where(row == lp_ref[0:1, :], 1.0, jnp.where(row == lp_ref[1:2, :], 1.0, 0.0)).astype(BF16)
    compact = _dot(onehot, h)

    @pl.when(j >= 2)
    def _():
        _wait_all(copies(j - 2, slot))

    comp_ref[slot] = compact
    _start_all(copies(j, slot))

    @pl.when(j == n_steps - 1)
    def _():
        _wait_all(copies(j, slot))

        @pl.when(j >= 1)
        def _():
            _wait_all(copies(j - 1, 1 - slot))

        comp_ref[slot, 0:MOE_TILE, :] = jnp.zeros((MOE_TILE, comp_ref.shape[2]), F32)
        tail = []
        for e in range(n_e):
            tail += _segment_copies(comp_ref.at[slot], 0, xs_ref, pad_ref[0, e], pad_ref[1, e], sem.at[slot])
        tile_groups = MOE_TILE // ROW_ALIGN
        groups_left = (xs_ref.shape[0] - pad_ref[2, 0]) // ROW_ALIGN
        min_total = TOP_K * x_ref.shape[0] * grp_ref.shape[0]
        for c in range((xs_ref.shape[0] - min_total) // MOE_TILE):
            tail += _segment_copies(comp_ref.at[slot], 0, xs_ref, pad_ref[2, 0] + c * MOE_TILE,
                                    jnp.clip(groups_left - c * tile_groups, 0, tile_groups), sem.at[slot])
        _start_all(tail)
        _wait_all(tail)


def _dispatch(x, g, lp, groups, off, dst, pad, n_rows):
    t, d = x.shape
    n_tiles = t // MOE_TILE
    grid_spec = pltpu.PrefetchScalarGridSpec(
        num_scalar_prefetch=4,
        grid=(n_tiles,),
        in_specs=[pl.BlockSpec((MOE_TILE, d), lambda j, *_: (j, 0)),
                  pl.BlockSpec(g.shape, lambda j, *_: (0, 0)),
                  pl.BlockSpec((None, TOP_K, MOE_TILE), lambda j, *_: (j, 0, 0))],
        out_specs=pl.BlockSpec(memory_space=pl.ANY),
        scratch_shapes=[pltpu.VMEM((2, COMPACT_ROWS, d), F32), pltpu.SemaphoreType.DMA((2,))])
    return pl.pallas_call(
        _dispatch_kernel,
        grid_spec=grid_spec,
        out_shape=jax.ShapeDtypeStruct((n_rows, d), F32),
        compiler_params=_cparams("arbitrary"),
        name="moe_dispatch",
    )(groups, off, dst, pad, x, g, lp)


STEP_FFN, STEP_ZERO = 1, 2


def _grouped_ffn_kernel(grp_ref, mode_ref, xtile_ref, xs_ref, wg_ref, wu_ref, wd_ref, ys_ref):
    w = pl.program_id(0)

    @pl.when(mode_ref[w] == STEP_ZERO)
    def _():
        ys_ref[...] = jnp.zeros_like(ys_ref)

    @pl.when(mode_ref[w] == STEP_FFN)
    def _():
        x = xs_ref[...].astype(BF16)
        act = (jax.nn.silu(_dot(x, wg_ref[...])) * _dot(x, wu_ref[...])).astype(BF16)
        ys_ref[...] = _dot(act, wd_ref[...])


def _grouped_ffn(xs, wg, wu, wd, grp, mode, xtile):
    n_e, f, d = wd.shape
    grid_spec = pltpu.PrefetchScalarGridSpec(
        num_scalar_prefetch=3,
        grid=(xs.shape[0] // MOE_TILE,),
        in_specs=[pl.BlockSpec((MOE_TILE, d), lambda w, grp, mode, xtile: (xtile[w], 0)),
                  pl.BlockSpec((None, d, f), lambda w, grp, *_: (grp[w], 0, 0)),
                  pl.BlockSpec((None, d, f), lambda w, grp, *_: (grp[w], 0, 0)),
                  pl.BlockSpec((None, f, d), lambda w, grp, *_: (grp[w], 0, 0))],
        out_specs=pl.BlockSpec((MOE_TILE, d), lambda w, *_: (w, 0)))
    return pl.pallas_call(
        _grouped_ffn_kernel,
        grid_spec=grid_spec,
        out_shape=jax.ShapeDtypeStruct(xs.shape, F32),
        compiler_params=_cparams("arbitrary"),
        name="moe_grouped_ffn",
    )(grp, mode, xtile, xs, wg, wu, wd)


def _combine_kernel(grp_ref, off_ref, dst_ref, x_ref, col_ref, *rest, final):
    if final:
        gfin_ref, rest = rest[0], rest[1:]
    ys_ref, o_ref, comp_ref, sem = rest
    j = pl.program_id(0)
    n_steps = pl.num_programs(0)
    n_e = grp_ref.shape[1]
    slot = j % 2

    def copies(step, s):
        out = []
        for e in range(n_e):
            out += _segment_copies(ys_ref, dst_ref[step, e], comp_ref.at[s], off_ref[step, e],
                                   grp_ref[step, e], sem.at[s])
        return out

    @pl.when(j == 0)
    def _():
        comp_ref[...] = jnp.zeros_like(comp_ref)
        _start_all(copies(0, 0))

    @pl.when(j + 1 < n_steps)
    def _():
        _start_all(copies(j + 1, 1 - slot))

    _wait_all(copies(j, slot))
    y = comp_ref[slot].astype(BF16)
    col = col_ref[...]
    row = lax.broadcasted_iota(jnp.int32, (col.shape[0], COMPACT_ROWS), 1)
    out = x_ref[...]
    for k in range(TOP_K):
        pick = jnp.where(row == col[:, k:k + 1].astype(jnp.int32), 1.0, 0.0).astype(BF16)
        out = out + col[:, TOP_K + k:TOP_K + k + 1] * _dot(pick, y)
    o_ref[...] = _rms(out, gfin_ref[...]) if final else out


def _combine(x, col, ys, groups, off, dst, g_final):
    t, d = x.shape
    n_tiles = t // MOE_TILE
    final = g_final is not None
    args = [x, col]
    specs = [pl.BlockSpec((MOE_TILE, d), lambda j, *_: (j, 0)),
             pl.BlockSpec((MOE_TILE, SUBLANES), lambda j, *_: (j, 0))]
    if final:
        args.append(g_final)
        specs.append(pl.BlockSpec(g_final.shape, lambda j, *_: (0, 0)))
    args.append(ys)
    specs.append(pl.BlockSpec(memory_space=pl.ANY))
    grid_spec = pltpu.PrefetchScalarGridSpec(
        num_scalar_prefetch=3,
        grid=(n_tiles,),
        in_specs=specs,
        out_specs=pl.BlockSpec((MOE_TILE, d), lambda j, *_: (j, 0)),
        scratch_shapes=[pltpu.VMEM((2, COMPACT_ROWS, d), F32), pltpu.SemaphoreType.DMA((2,))])
    return pl.pallas_call(
        functools.partial(_combine_kernel, final=final),
        grid_spec=grid_spec,
        out_shape=jax.ShapeDtypeStruct((t, d), F32),
        compiler_params=_cparams("arbitrary"),
        name="moe_combine",
    )(groups, off, dst, *args)


def _moe_routed(x, g, w_router, wg, wu, wd, g_final):
    t, _ = x.shape
    n_e = wd.shape[0]
    n_tiles = t // MOE_TILE
    lp, col, cnt = _route(x, g, w_router.T)
    groups = (cnt[:, :, 0].astype(jnp.int32) + (ROW_ALIGN - 1)) // ROW_ALIGN
    n = groups * ROW_ALIGN
    off = jnp.cumsum(n, axis=1) - n
    per_e = jnp.sum(n, axis=0)
    tiles_e = (per_e + (MOE_TILE - 1)) // MOE_TILE
    tile_end = jnp.cumsum(tiles_e)
    starts = (tile_end - tiles_e) * MOE_TILE
    dst = starts[None, :] + jnp.cumsum(n, axis=0) - n
    pad = jnp.stack([starts + per_e, (tiles_e * MOE_TILE - per_e) // ROW_ALIGN,
                     jnp.broadcast_to(tile_end[-1] * MOE_TILE, per_e.shape)])
    max_rows = TOP_K * t + n_tiles * n_e * (ROW_ALIGN - 1)
    n_row_tiles = -(-max_rows // MOE_TILE) + n_e
    xs = _dispatch(x, g, lp, groups, off, dst, pad, n_row_tiles * MOE_TILE)

    w = jnp.arange(n_row_tiles, dtype=jnp.int32)
    used = w < tile_end[-1]
    grp = jnp.minimum(jnp.sum(w[:, None] >= tile_end[None, :], axis=1), n_e - 1).astype(jnp.int32)
    mode = jnp.where(used, STEP_FFN, STEP_ZERO).astype(jnp.int32)
    xtile = jnp.minimum(w, tile_end[-1] - 1)
    ys = _grouped_ffn(xs, wg, wu, wd, grp, mode, xtile)
    return _combine(x, col, ys, groups, off, dst, g_final)


def _block_diag_gates(w_a, w_x):
    per_group = LRU_GROUP // LRU_HEAD_DIM
    eye = jnp.eye(per_group, dtype=w_a.dtype)

    def bd(w):
        w = w.reshape(D_LRU // LRU_GROUP, per_group, LRU_HEAD_DIM, LRU_HEAD_DIM)
        return jnp.einsum('ghij,hk->ghikj', w, eye).reshape(D_LRU // LRU_GROUP, LRU_GROUP, LRU_GROUP)

    return jnp.concatenate([bd(w_a), bd(w_x)], axis=-1).astype(BF16)


def kernel(x_prompt, x_sample, state_shortconv, state_lru_conv, state_lru_h, cache_mem_k, cache_mem_v,
           mem_prompt, norm_mix, w_in, w_short_conv, w_lru_conv, b_lru_conv, w_lru_a, b_lru_a, w_lru_x,
           b_lru_x, lru_lambda, norm_conv_out, norm_lru_out, w_mix_out, norm_xattn, norm_mem, w_q, w_k,
           w_v, w_o, norm_ffn, w_ff_gate, w_ff_up, w_ff_down, w_router, w_moe_gate, w_moe_up, w_moe_down,
           norm_final):
    depth = w_in.shape[0]
    nb, t, d = x_prompt.shape
    ns = x_sample.shape[0]
    n_mem = mem_prompt.shape[1]
    row = lambda a: a.reshape(1, -1)

    layers = []
    for l in range(depth):
        lp = dict(
            gmix=row(norm_mix[l]), win=w_in[l].astype(BF16), wsc=w_short_conv[l], wlc=w_lru_conv[l],
            blc=row(b_lru_conv[l]), wg=_block_diag_gates(w_lru_a[l], w_lru_x[l]), ba=row(b_lru_a[l]),
            bx=row(b_lru_x[l]), lam=row(lru_lambda[l]), gco=row(norm_conv_out[l]), glo=row(norm_lru_out[l]),
            wout=w_mix_out[l].astype(BF16), gx=row(norm_xattn[l]),
            wq=w_q[l].astype(BF16), wo=w_o[l].astype(BF16),
            gffn=row(norm_ffn[l]), gfin=row(norm_final) if l == depth - 1 else None)
        j = l // 2
        if l % 2 == 0:
            lp['ffw'] = (w_ff_gate[j].astype(BF16), w_ff_up[j].astype(BF16), w_ff_down[j].astype(BF16))
            lp['wr'] = None
        else:
            lp['ffw'] = (w_moe_gate[j].astype(BF16), w_moe_up[j].astype(BF16), w_moe_down[j].astype(BF16))
            lp['wr'] = jnp.pad(w_router[j], ((0, 0), (0, LANES - N_EXPERTS)))
            lp['wr_raw'] = w_router[j]
        layers.append(lp)

    x = x_prompt
    p_mem_k, p_mem_v, kb, vb = _kv_proj(mem_prompt, norm_mem.reshape(depth, 1, d), w_k.astype(BF16),
                                        w_v.astype(BF16))
    pc, plc, ph = [], [], []
    for l, lp in enumerate(layers):
        x, tail_p, tail_x, h_last = _mixer_prompt(x, lp)
        x = _xattn_prompt(x, lp['gx'], lp['wq'], kb[l], vb[l], lp['wo'])
        if lp['wr'] is None:
            x = _ffn_dense(x.reshape(nb * t, d), lp['gffn'], *lp['ffw'], tm=512, g_final=lp['gfin'])
        else:
            x = _moe_routed(x.reshape(nb * t, d), lp['gffn'], lp['wr_raw'], *lp['ffw'], lp['gfin'])
        x = x.reshape(nb, t, d)
        tail_p = jnp.swapaxes(tail_p.reshape(MAX_CONV_HISTORY, nb, D_CONV), 0, 1)
        tail_x = jnp.swapaxes(tail_x.reshape(MAX_CONV_HISTORY, nb, D_LRU), 0, 1)
        pc.append(tail_p[:, MAX_CONV_HISTORY - (w_short_conv.shape[1] - 1):])
        plc.append(tail_x[:, MAX_CONV_HISTORY - (w_lru_conv.shape[1] - 1):])
        ph.append(h_last)
    y_prompt = x

    x = x_sample.reshape(ns, d)
    k8 = _to_head_rows(cache_mem_k.reshape(depth, ns, n_mem, d))
    v8 = _to_head_rows(cache_mem_v.reshape(depth, ns, n_mem, d))
    sc, slc, sh = [], [], []
    for l, lp in enumerate(layers):
        x, q, p_new, xb_new, h_new = _mixer_sample(x, state_shortconv[l], state_lru_conv[l], state_lru_h[l], lp)
        attn = (_xattn_sample(q, k8, v8, l), lp['wo'])
        if lp['wr'] is None:
            x = _ffn_dense(x, lp['gffn'], *lp['ffw'], tm=ns, g_final=lp['gfin'], attn=attn)
        else:
            x = _moe_all_experts(x, lp['gffn'], lp['wr'], *lp['ffw'], tm=ns, g_final=lp['gfin'], attn=attn)
        sc.append(jnp.stack([state_shortconv[l][:, 1], p_new], axis=1))
        slc.append(jnp.stack([state_lru_conv[l][:, 1], state_lru_conv[l][:, 2], xb_new], axis=1))
        sh.append(h_new)
    y_sample = x.reshape(ns, 1, d)

    return (y_prompt, y_sample, jnp.stack(pc), jnp.stack(plc), jnp.stack(ph), p_mem_k, p_mem_v,
            jnp.stack(sc), jnp.stack(slc), jnp.stack(sh))
```

```python
import functools

import jax
import jax.numpy as jnp
from jax import lax
from jax.experimental import pallas as pl
from jax.experimental.pallas import tpu as pltpu

F32 = jnp.float32
BF16 = jnp.bfloat16

EPS = 1e-6
LRU_C = 8.0
D_CONV = 512
D_LRU = 512
N_LRU_HEADS = 8
LRU_HEAD_DIM = 64
LRU_GROUP = 256
N_MEM_HEADS = 4
MEM_HEAD_DIM = 256
N_EXPERTS = 8
SUBLANES = 8
LANES = 128
VMEM_LIMIT_BYTES = 56 * 1024 * 1024

MIXER_TIME_CHUNK = 128
XATTN_ROWS = 1024
FFN_ROWS = 512
KV_PROJ_SEQS = 2
SAMPLE_ATTN_SEQS = 8


def _cparams(*sem):
    return pltpu.CompilerParams(dimension_semantics=sem, vmem_limit_bytes=VMEM_LIMIT_BYTES)


def _rms(x, g):
    return x * lax.rsqrt(jnp.mean(x * x, axis=-1, keepdims=True) + EPS) * g


def _dot(a, b):
    return jnp.dot(a, b, preferred_element_type=F32)


def _lru_gates(xc, wg_ref, ba, bx, lam):
    neg_lam = -lam
    softplus = jnp.maximum(neg_lam, 0.0) + jnp.log1p(jnp.exp(-jnp.abs(neg_lam)))
    a_parts, u_parts = [], []
    for g in range(D_LRU // LRU_GROUP):
        sl = slice(g * LRU_GROUP, (g + 1) * LRU_GROUP)
        xg = xc[:, sl]
        gates = _dot(xg.astype(BF16), wg_ref[g])
        r = jax.nn.sigmoid(gates[:, :LRU_GROUP] + ba[:, sl])
        i = jax.nn.sigmoid(gates[:, LRU_GROUP:] + bx[:, sl])
        log_a = (-LRU_C) * r * softplus[:, sl]
        th = jnp.tanh(log_a)
        one_minus_a2 = (-2.0 * th) / (1.0 - th)
        a_parts.append(jnp.exp(log_a))
        u_parts.append(jnp.sqrt(one_minus_a2) * (i * xg))
    return jnp.concatenate(a_parts, axis=-1), jnp.concatenate(u_parts, axis=-1)


def _mix_out(x, ya, yb, gco, glo, wout_ref):
    y = jnp.concatenate([_rms(ya, gco), _rms(yb, glo)], axis=-1).astype(BF16)
    return x + _dot(y, wout_ref[...])


MAX_CONV_HISTORY = 3


def _conv_taps(ext_ref, w_ref, rows, nb):
    width = w_ref.shape[0]
    hist = MAX_CONV_HISTORY * nb
    acc = ext_ref[hist:hist + rows, :] * w_ref[width - 1:width, :]
    for s in range(1, width):
        acc = acc + ext_ref[hist - s * nb:hist - s * nb + rows, :] * w_ref[width - 1 - s:width - s, :]
    return acc


def _mixer_prompt_kernel(x_hbm, gmix_ref, win_ref, wsc_ref, wlc_ref, blc_ref, wg_ref, ba_ref, bx_ref,
                         lam_ref, gco_ref, glo_ref, wout_ref,
                         xo_hbm, tailp_ref, tailx_ref, hout_ref,
                         xin_ref, xout_ref, in_sem, out_sem, pext_ref, xext_ref, h_ref, a_ref, u_ref):
    _, tt, nb, d = xin_ref.shape
    rows = tt * nb
    hist = MAX_CONV_HISTORY * nb
    i = pl.program_id(0)
    n_steps = pl.num_programs(0)
    slot = i % 2

    def in_copies(step, s):
        return [pltpu.make_async_copy(x_hbm.at[b, pl.ds(step * tt, tt), :], xin_ref.at[s, :, b, :],
                                      in_sem.at[s]) for b in range(nb)]

    def out_copies(step, s):
        return [pltpu.make_async_copy(xout_ref.at[s, :, b, :], xo_hbm.at[b, pl.ds(step * tt, tt), :],
                                      out_sem.at[s]) for b in range(nb)]

    @pl.when(i == 0)
    def _():
        pext_ref[0:hist, :] = jnp.zeros((hist, pext_ref.shape[1]), F32)
        xext_ref[0:hist, :] = jnp.zeros((hist, xext_ref.shape[1]), F32)
        h_ref[...] = jnp.zeros_like(h_ref)
        for cp in in_copies(0, 0):
            cp.start()

    @pl.when(i + 1 < n_steps)
    def _():
        for cp in in_copies(i + 1, 1 - slot):
            cp.start()

    for cp in in_copies(i, slot):
        cp.wait()
    x = xin_ref[slot].reshape(rows, d)
    z = _dot(_rms(x, gmix_ref[...]).astype(BF16), win_ref[...])
    gb = z[:, 0:D_CONV]
    gate = z[:, 3 * D_CONV + D_LRU:]
    pext_ref[hist:hist + rows, :] = z[:, D_CONV:2 * D_CONV] * z[:, 2 * D_CONV:3 * D_CONV]
    xext_ref[hist:hist + rows, :] = z[:, 3 * D_CONV:3 * D_CONV + D_LRU]

    ya = gb * _conv_taps(pext_ref, wsc_ref, rows, nb)
    xc = _conv_taps(xext_ref, wlc_ref, rows, nb) + blc_ref[...]
    for ext_ref, tail_ref in ((pext_ref, tailp_ref), (xext_ref, tailx_ref)):
        tail = ext_ref[rows:rows + hist, :]
        ext_ref[0:hist, :] = tail
        tail_ref[...] = tail

    a, u = _lru_gates(xc, wg_ref, ba_ref[...], bx_ref[...], lam_ref[...])
    a_ref[...] = a
    u_ref[...] = u

    def step(t, h):
        grp = pl.ds(pl.multiple_of(t * nb, nb), nb)
        h = a_ref[grp, :] * h + u_ref[grp, :]
        u_ref[grp, :] = h
        return h

    h_last = lax.fori_loop(0, tt, step, h_ref[...], unroll=8)
    h_ref[...] = h_last
    hout_ref[...] = h_last

    yb = u_ref[...] * jax.nn.gelu(gate)
    out = _mix_out(x, ya, yb, gco_ref[...], glo_ref[...], wout_ref)

    @pl.when(i >= 2)
    def _():
        for cp in out_copies(i - 2, slot):
            cp.wait()

    xout_ref[slot] = out.reshape(tt, nb, d)
    for cp in out_copies(i, slot):
        cp.start()

    @pl.when(i == n_steps - 1)
    def _():
        for cp in out_copies(i, slot):
            cp.wait()

        @pl.when(i >= 1)
        def _():
            for cp in out_copies(i - 1, 1 - slot):
                cp.wait()


def _const_spec(shape):
    return pl.BlockSpec(shape, lambda *_: (0,) * len(shape))


def _mixer_prompt(x, p, tt=MIXER_TIME_CHUNK):
    nb, t, d = x.shape
    small = [p['gmix'], p['win'], p['wsc'], p['wlc'], p['blc'], p['wg'], p['ba'], p['bx'], p['lam'],
             p['gco'], p['glo'], p['wout']]
    rows = nb * tt
    hist = MAX_CONV_HISTORY * nb
    return pl.pallas_call(
        _mixer_prompt_kernel,
        grid=(t // tt,),
        in_specs=[pl.BlockSpec(memory_space=pl.ANY)] + [_resident_spec(a.shape) for a in small],
        out_specs=[pl.BlockSpec(memory_space=pl.ANY),
                   _const_spec((hist, D_CONV)), _const_spec((hist, D_LRU)), _const_spec((nb, D_LRU))],
        out_shape=[jax.ShapeDtypeStruct((nb, t, d), F32),
                   jax.ShapeDtypeStruct((hist, D_CONV), F32),
                   jax.ShapeDtypeStruct((hist, D_LRU), F32),
                   jax.ShapeDtypeStruct((nb, D_LRU), F32)],
        scratch_shapes=[pltpu.VMEM((2, tt, nb, d), F32), pltpu.VMEM((2, tt, nb, d), F32),
                        pltpu.SemaphoreType.DMA((2,)), pltpu.SemaphoreType.DMA((2,)),
                        pltpu.VMEM((hist + rows, D_CONV), F32), pltpu.VMEM((hist + rows, D_LRU), F32),
                        pltpu.VMEM((nb, D_LRU), F32),
                        pltpu.VMEM((rows, D_LRU), F32), pltpu.VMEM((rows, D_LRU), F32)],
        compiler_params=_cparams("arbitrary"),
        name="mixer_prompt",
    )(x, *small)


def _mixer_sample_kernel(x_ref, sc0_ref, sc1_ref, lc0_ref, lc1_ref, lc2_ref, h0_ref,
                         gmix_ref, win_ref, wsc_ref, wlc_ref, blc_ref, wg_ref, ba_ref, bx_ref,
                         lam_ref, gco_ref, glo_ref, wout_ref, gx_ref, wq_ref,
                         xo_ref, q_ref, pnew_ref, xbnew_ref, hout_ref):
    x = x_ref[...]
    z = _dot(_rms(x, gmix_ref[...]).astype(BF16), win_ref[...])
    gb = z[:, 0:D_CONV]
    p = z[:, D_CONV:2 * D_CONV] * z[:, 2 * D_CONV:3 * D_CONV]
    xb = z[:, 3 * D_CONV:3 * D_CONV + D_LRU]
    gate = z[:, 3 * D_CONV + D_LRU:]

    ya = gb * (sc0_ref[...] * wsc_ref[0:1, :] + sc1_ref[...] * wsc_ref[1:2, :] + p * wsc_ref[2:3, :])
    xc = (lc0_ref[...] * wlc_ref[0:1, :] + lc1_ref[...] * wlc_ref[1:2, :] + lc2_ref[...] * wlc_ref[2:3, :]
          + xb * wlc_ref[3:4, :]) + blc_ref[...]
    a, u = _lru_gates(xc, wg_ref, ba_ref[...], bx_ref[...], lam_ref[...])
    h = a * h0_ref[...] + u
    yb = h * jax.nn.gelu(gate)
    x1 = _mix_out(x, ya, yb, gco_ref[...], glo_ref[...], wout_ref)
    xo_ref[...] = x1
    q_ref[...] = _dot(_rms(x1, gx_ref[...]).astype(BF16), wq_ref[...])
    pnew_ref[...] = p
    xbnew_ref[...] = xb
    hout_ref[...] = h


def _mixer_sample(x, sc, lc, h0, p):
    nb, d = x.shape
    args = [x, sc[:, 0], sc[:, 1], lc[:, 0], lc[:, 1], lc[:, 2], h0,
            p['gmix'], p['win'], p['wsc'], p['wlc'], p['blc'], p['wg'], p['ba'], p['bx'], p['lam'],
            p['gco'], p['glo'], p['wout'], p['gx'], p['wq']]
    return pl.pallas_call(
        _mixer_sample_kernel,
        grid=(1,),
        in_specs=[_const_spec(a.shape) for a in args],
        out_specs=[_const_spec((nb, d)), _const_spec((nb, d)), _const_spec((nb, D_CONV)),
                   _const_spec((nb, D_LRU)), _const_spec((nb, D_LRU))],
        out_shape=[jax.ShapeDtypeStruct((nb, d), F32), jax.ShapeDtypeStruct((nb, d), F32),
                   jax.ShapeDtypeStruct((nb, D_CONV), F32), jax.ShapeDtypeStruct((nb, D_LRU), F32),
                   jax.ShapeDtypeStruct((nb, D_LRU), F32)],
        compiler_params=_cparams("arbitrary"),
        name="mixer_sample",
    )(*args)


def _xattn_prompt_kernel(x_ref, g_ref, wq_ref, k_ref, v_ref, wo_ref, o_ref):
    x = x_ref[...]
    q = _dot(_rms(x, g_ref[...]).astype(BF16), wq_ref[...])
    heads = []
    for h in range(N_MEM_HEADS):
        sl = slice(h * MEM_HEAD_DIM, (h + 1) * MEM_HEAD_DIM)
        kh = k_ref[:, sl]
        vh = v_ref[:, sl]
        s = lax.dot_general(q[:, sl].astype(BF16), kh, (((1,), (1,)), ((), ())),
                            preferred_element_type=F32) * (MEM_HEAD_DIM ** -0.5)
        e = jnp.exp(s - jnp.max(s, axis=-1, keepdims=True))
        pr = e / jnp.sum(e, axis=-1, keepdims=True)
        heads.append(_dot(pr.astype(BF16), vh))
    o = jnp.concatenate(heads, axis=-1).astype(BF16)
    o_ref[...] = x + _dot(o, wo_ref[...])


def _xattn_prompt(x, g, wq, k, v, wo, tq=XATTN_ROWS):
    nb, t, d = x.shape
    n_mem = k.shape[1]
    return pl.pallas_call(
        _xattn_prompt_kernel,
        grid=(nb, t // tq),
        in_specs=[pl.BlockSpec((None, tq, d), lambda b, i: (b, i, 0)), _const_spec(g.shape),
                  _const_spec(wq.shape),
                  pl.BlockSpec((None, n_mem, d), lambda b, i: (b, 0, 0)),
                  pl.BlockSpec((None, n_mem, d), lambda b, i: (b, 0, 0)),
                  _const_spec(wo.shape)],
        out_specs=pl.BlockSpec((None, tq, d), lambda b, i: (b, i, 0)),
        out_shape=jax.ShapeDtypeStruct((nb, t, d), F32),
        compiler_params=_cparams("parallel", "parallel"),
        name="xattn_prompt",
    )(x, g, wq, k, v, wo)


def _kv_proj_kernel(mem_ref, g_ref, wk_ref, wv_ref, k5_ref, v5_ref, kb_ref, vb_ref):
    bb, n_mem, d = mem_ref.shape
    m = _rms(mem_ref[...].reshape(bb * n_mem, d), g_ref[...]).astype(BF16)
    for w_ref, o5_ref, ob_ref in ((wk_ref, k5_ref, kb_ref), (wv_ref, v5_ref, vb_ref)):
        y = _dot(m, w_ref[...]).reshape(bb, n_mem, d)
        ob_ref[...] = y.astype(BF16)
        for h in range(N_MEM_HEADS):
            o5_ref[:, :, h, :] = y[:, :, h * MEM_HEAD_DIM:(h + 1) * MEM_HEAD_DIM]


def _kv_proj(mem, gmem, wk, wv, bb=KV_PROJ_SEQS):
    nb, n_mem, d = mem.shape
    depth = wk.shape[0]
    w_spec = pl.BlockSpec((None, d, d), lambda l, i: (l, 0, 0))
    o5_spec = pl.BlockSpec((None, bb, n_mem, N_MEM_HEADS, MEM_HEAD_DIM), lambda l, i: (l, i, 0, 0, 0))
    ob_spec = pl.BlockSpec((None, bb, n_mem, d), lambda l, i: (l, i, 0, 0))
    o5_shape = jax.ShapeDtypeStruct((depth, nb, n_mem, N_MEM_HEADS, MEM_HEAD_DIM), F32)
    ob_shape = jax.ShapeDtypeStruct((depth, nb, n_mem, d), BF16)
    return pl.pallas_call(
        _kv_proj_kernel,
        grid=(depth, nb // bb),
        in_specs=[pl.BlockSpec((bb, n_mem, d), lambda l, i: (i, 0, 0)),
                  pl.BlockSpec((None, 1, d), lambda l, i: (l, 0, 0)), w_spec, w_spec],
        out_specs=[o5_spec, o5_spec, ob_spec, ob_spec],
        out_shape=[o5_shape, o5_shape, ob_shape, ob_shape],
        compiler_params=_cparams("parallel", "parallel"),
        name="kv_proj",
    )(mem, gmem, wk, wv)


LANE_TILES_PER_HEAD = MEM_HEAD_DIM // LANES
HEAD_ROWS = N_MEM_HEADS * LANE_TILES_PER_HEAD


def _to_head_rows(a):
    lead = a.shape[:-1]
    n = len(lead)
    a = a.reshape(*lead, N_MEM_HEADS, LANE_TILES_PER_HEAD, LANES)
    return jnp.swapaxes(a, n, n + 1).reshape(*lead, HEAD_ROWS, LANES)


def _from_head_rows(a):
    lead = a.shape[:-2]
    n = len(lead)
    a = a.reshape(*lead, LANE_TILES_PER_HEAD, N_MEM_HEADS, LANES)
    return jnp.swapaxes(a, n, n + 1).reshape(*lead, N_MEM_HEADS * MEM_HEAD_DIM)


def _xattn_sample_kernel(q_ref, k_ref, v_ref, o_ref):
    n_mem = k_ref.shape[1]
    ones = jnp.ones((LANES, LANES), BF16)
    for b in range(q_ref.shape[0]):
        prod = (k_ref[b] * q_ref[b][None]).reshape(n_mem * HEAD_ROWS, LANES).astype(BF16)
        part = _dot(prod, ones).reshape(n_mem, HEAD_ROWS, LANES)
        s = (part + pltpu.roll(part, N_MEM_HEADS, axis=1)) * (MEM_HEAD_DIM ** -0.5)
        e = jnp.exp(s - jnp.max(s, axis=0, keepdims=True))
        pr = e / jnp.sum(e, axis=0, keepdims=True)
        o_ref[b] = jnp.sum(pr * v_ref[b], axis=0)


def _xattn_sample(q, k8, v8, layer, bb=SAMPLE_ATTN_SEQS):
    nb, d = q.shape
    n_mem = k8.shape[2]
    kv_spec = pl.BlockSpec((None, bb, n_mem, HEAD_ROWS, LANES), lambda i: (layer, i, 0, 0, 0))
    out = pl.pallas_call(
        _xattn_sample_kernel,
        grid=(nb // bb,),
        in_specs=[pl.BlockSpec((bb, HEAD_ROWS, LANES), lambda i: (i, 0, 0)), kv_spec, kv_spec],
        out_specs=pl.BlockSpec((bb, HEAD_ROWS, LANES), lambda i: (i, 0, 0)),
        out_shape=jax.ShapeDtypeStruct((nb, HEAD_ROWS, LANES), F32),
        compiler_params=_cparams("parallel"),
        name="xattn_sample",
    )(_to_head_rows(q), k8, v8)
    return _from_head_rows(out)


def _split_bf16(a):
    hi = a.astype(BF16)
    return hi, (a - hi.astype(F32)).astype(BF16)


def _top2(logits):
    n = logits.shape[-1]
    lane = lax.broadcasted_iota(jnp.int32, logits.shape, 1)
    m1 = jnp.max(logits, axis=-1, keepdims=True)
    i1 = jnp.min(jnp.where(logits == m1, lane, n), axis=-1, keepdims=True)
    rest = jnp.where(lane == i1, -jnp.inf, logits)
    m2 = jnp.max(rest, axis=-1, keepdims=True)
    i2 = jnp.min(jnp.where(rest == m2, lane, n), axis=-1, keepdims=True)
    e2 = jnp.exp(m2 - m1)
    denom = 1.0 + e2
    return i1, i2, 1.0 / denom, e2 / denom


def _resident_spec(shape):
    return pl.BlockSpec(shape, lambda *_: (0,) * len(shape), pipeline_mode=pl.Buffered(1))


def _add_attn_proj(x, rest, proj):
    if not proj:
        return x, rest
    a_ref, wo_ref = rest[:2]
    return x + _dot(a_ref[...].astype(BF16), wo_ref[...]), rest[2:]


def _ffn_dense_kernel(x_ref, g_ref, *rest, final, proj):
    x, rest = _add_attn_proj(x_ref[...], rest, proj)
    if final:
        gfin_ref, rest = rest[0], rest[1:]
    wg_ref, wu_ref, wd_ref, o_ref = rest
    h = _rms(x, g_ref[...]).astype(BF16)
    act = (jax.nn.silu(_dot(h, wg_ref[...])) * _dot(h, wu_ref[...])).astype(BF16)
    out = x + _dot(act, wd_ref[...])
    o_ref[...] = _rms(out, gfin_ref[...]) if final else out


def _ffn_dense(x, g, wg, wu, wd, tm, g_final=None, attn=None):
    r, d = x.shape
    final, proj = g_final is not None, attn is not None
    args, specs = [x, g], [pl.BlockSpec((tm, d), lambda i: (i, 0)), _const_spec(g.shape)]
    if proj:
        args += list(attn)
        specs += [pl.BlockSpec((tm, attn[0].shape[1]), lambda i: (i, 0)), _resident_spec(attn[1].shape)]
    if final:
        args.append(g_final)
        specs.append(_const_spec(g_final.shape))
    args += [wg, wu, wd]
    specs += [_resident_spec(wg.shape), _resident_spec(wu.shape), _resident_spec(wd.shape)]
    return pl.pallas_call(
        functools.partial(_ffn_dense_kernel, final=final, proj=proj),
        grid=(r // tm,),
        in_specs=specs,
        out_specs=pl.BlockSpec((tm, d), lambda i: (i, 0)),
        out_shape=jax.ShapeDtypeStruct((r, d), F32),
        compiler_params=_cparams("parallel"),
        name="ffn_dense",
    )(*args)


def _moe_all_experts_kernel(x_ref, g_ref, wr_ref, *rest, final, proj):
    if proj:
        attn_refs, rest = rest[:2], rest[2:]
    if final:
        gfin_ref, rest = rest[0], rest[1:]
    wg_ref, wu_ref, wd_ref, o_ref, h_ref, acc_ref, i1_ref, i2_ref, g1_ref, g2_ref = rest
    e = pl.program_id(1)

    @pl.when(e == 0)
    def _():
        x = x_ref[...]
        if proj:
            x, _ = _add_attn_proj(x, attn_refs, proj)
        h = _rms(x, g_ref[...])
        h_ref[...] = h.astype(BF16)
        acc_ref[...] = x
        h_hi, h_lo = _split_bf16(h)
        w_hi, w_lo = _split_bf16(wr_ref[...])
        logits = _dot(h_hi, w_hi) + (_dot(h_hi, w_lo) + _dot(h_lo, w_hi))
        lane = lax.broadcasted_iota(jnp.int32, logits.shape, 1)
        logits = jnp.where(lane < N_EXPERTS, logits, -jnp.inf)
        i1_ref[...], i2_ref[...], g1_ref[...], g2_ref[...] = _top2(logits)

    hb = h_ref[...]
    y = _dot((jax.nn.silu(_dot(hb, wg_ref[...])) * _dot(hb, wu_ref[...])).astype(BF16), wd_ref[...])
    c = jnp.where(i1_ref[...] == e, g1_ref[...], 0.0) + jnp.where(i2_ref[...] == e, g2_ref[...], 0.0)
    acc_ref[...] += c * y

    @pl.when(e == pl.num_programs(1) - 1)
    def _():
        out = acc_ref[...]
        o_ref[...] = _rms(out, gfin_ref[...]) if final else out


def _moe_all_experts(x, g, w_router, wg, wu, wd, tm, g_final=None, attn=None):
    r, d = x.shape
    n_e, f, _ = wd.shape
    final, proj = g_final is not None, attn is not None
    args = [x, g, w_router]
    specs = [pl.BlockSpec((tm, d), lambda i, e: (i, 0)), _const_spec(g.shape), _const_spec(w_router.shape)]
    if proj:
        args += list(attn)
        specs += [pl.BlockSpec((tm, attn[0].shape[1]), lambda i, e: (i, 0)), _const_spec(attn[1].shape)]
    if final:
        args.append(g_final)
        specs.append(_const_spec(g_final.shape))
    args += [wg, wu, wd]
    specs += [pl.BlockSpec((None, d, f), lambda i, e: (e, 0, 0)),
              pl.BlockSpec((None, d, f), lambda i, e: (e, 0, 0)),
              pl.BlockSpec((None, f, d), lambda i, e: (e, 0, 0))]
    return pl.pallas_call(
        functools.partial(_moe_all_experts_kernel, final=final, proj=proj),
        grid=(r // tm, n_e),
        in_specs=specs,
        out_specs=pl.BlockSpec((tm, d), lambda i, e: (i, 0)),
        out_shape=jax.ShapeDtypeStruct((r, d), F32),
        scratch_shapes=[pltpu.VMEM((tm, d), BF16), pltpu.VMEM((tm, d), F32),
                        pltpu.VMEM((tm, 1), jnp.int32), pltpu.VMEM((tm, 1), jnp.int32),
                        pltpu.VMEM((tm, 1), F32), pltpu.VMEM((tm, 1), F32)],
        compiler_params=_cparams("parallel", "arbitrary"),
        name="moe_all_experts",
    )(*args)


MOE_TILE = 512
TOP_K = 2
ROW_ALIGN = SUBLANES
COMPACT_ROWS = -(-(TOP_K * MOE_TILE + N_EXPERTS * (ROW_ALIGN - 1)) // (2 * SUBLANES)) * (2 * SUBLANES)
SEGMENT_BITS = (MOE_TILE // ROW_ALIGN).bit_length()


def _route_kernel(x_ref, g_ref, wrt_ref, earlier_ref, lp_ref, col_ref, cnt_ref):
    tk = x_ref.shape[0]
    n_e = wrt_ref.shape[0]
    h = _rms(x_ref[...], g_ref[...])
    h_hi, h_lo = _split_bf16(h)
    w_hi, w_lo = _split_bf16(wrt_ref[...])
    nt = (((1,), (1,)), ((), ()))
    dg = lambda a, b: lax.dot_general(a, b, nt, preferred_element_type=F32)
    logits = dg(w_hi, h_hi) + (dg(w_hi, h_lo) + dg(w_lo, h_hi))
    sub = lax.broadcasted_iota(jnp.int32, logits.shape, 0)
    m1 = jnp.max(logits, axis=0, keepdims=True)
    i1 = jnp.min(jnp.where(logits == m1, sub, n_e), axis=0, keepdims=True)
    rest = jnp.where(sub == i1, -jnp.inf, logits)
    m2 = jnp.max(rest, axis=0, keepdims=True)
    i2 = jnp.min(jnp.where(rest == m2, sub, n_e), axis=0, keepdims=True)
    e2 = jnp.exp(m2 - m1)
    denom = 1.0 + e2

    pick1 = jnp.where(sub == i1, 1.0, 0.0)
    pick2 = jnp.where(sub == i2, 1.0, 0.0)
    cnt = pick1 + pick2
    rank = _dot(cnt.astype(BF16), earlier_ref[...])
    n = jnp.sum(cnt, axis=1, keepdims=True)
    n_pad = jnp.floor((n + (ROW_ALIGN - 1)) * (1.0 / ROW_ALIGN)) * ROW_ALIGN
    base, acc = [], jnp.zeros((1, 1), F32)
    for e in range(n_e):
        base.append(acc)
        acc = acc + n_pad[e:e + 1, :]
    pos = rank + jnp.concatenate(base, axis=0)
    lp1 = jnp.sum(pick1 * pos, axis=0, keepdims=True)
    lp2 = jnp.sum(pick2 * pos, axis=0, keepdims=True)
    lp_ref[0:1, :] = lp1.astype(jnp.int32)
    lp_ref[1:2, :] = lp2.astype(jnp.int32)
    info = jnp.concatenate([lp1, lp2, 1.0 / denom, e2 / denom, jnp.zeros((SUBLANES - 4, tk), F32)], axis=0)
    col_ref[...] = info.T
    cnt_ref[...] = jnp.broadcast_to(n, cnt_ref.shape)


def _route(x, g, wrt):
    t, d = x.shape
    n_e = wrt.shape[0]
    n_tiles = t // MOE_TILE
    idx = jnp.arange(MOE_TILE)
    earlier = (idx[:, None] < idx[None, :]).astype(BF16)
    return pl.pallas_call(
        _route_kernel,
        grid=(n_tiles,),
        in_specs=[pl.BlockSpec((MOE_TILE, d), lambda j: (j, 0)), _const_spec(g.shape), _const_spec(wrt.shape),
                  _const_spec(earlier.shape)],
        out_specs=[pl.BlockSpec((None, TOP_K, MOE_TILE), lambda j: (j, 0, 0)),
                   pl.BlockSpec((MOE_TILE, SUBLANES), lambda j: (j, 0)),
                   pl.BlockSpec((None, n_e, LANES), lambda j: (j, 0, 0))],
        out_shape=[jax.ShapeDtypeStruct((n_tiles, TOP_K, MOE_TILE), jnp.int32),
                   jax.ShapeDtypeStruct((t, SUBLANES), F32),
                   jax.ShapeDtypeStruct((n_tiles, n_e, LANES), F32)],
        compiler_params=_cparams("parallel"),
        name="moe_route",
    )(x, g, wrt, earlier)


def _segment_copies(src_ref, src_row, dst_ref, dst_row, groups, sem):
    out = []
    for bit in range(SEGMENT_BITS):
        rows = ROW_ALIGN << bit
        before = ((groups >> (bit + 1)) << (bit + 1)) * ROW_ALIGN
        cp = pltpu.make_async_copy(
            src_ref.at[pl.ds(pl.multiple_of(src_row + before, ROW_ALIGN), rows), :],
            dst_ref.at[pl.ds(pl.multiple_of(dst_row + before, ROW_ALIGN), rows), :], sem)
        out.append((((groups >> bit) & 1) == 1, cp))
    return out


def _start_all(copies):
    for pred, cp in copies:
        pl.when(pred)(cp.start)


def _wait_all(copies):
    for pred, cp in copies:
        pl.when(pred)(cp.wait)


def _dispatch_kernel(grp_ref, off_ref, dst_ref, pad_ref, x_ref, g_ref, lp_ref, xs_ref, comp_ref, sem):
    j = pl.program_id(0)
    n_steps = pl.num_programs(0)
    n_e = grp_ref.shape[1]
    slot = j % 2

    def copies(step, s):
        out = []
        for e in range(n_e):
            out += _segment_copies(comp_ref.at[s], off_ref[step, e], xs_ref, dst_ref[step, e],
                                   grp_ref[step, e], sem.at[s])
        return out

    h = _rms(x_ref[...], g_ref[...]).astype(BF16)
    row = lax.broadcasted_iota(jnp.int32, (COMPACT_ROWS, h.shape[0]), 0)
    onehot = jnp.where(row == lp_ref[0:1, :], 1.0, jnp.where(row == lp_ref[1:2, :], 1.0, 0.0)).astype(BF16)
    compact = _dot(onehot, h)

    @pl.when(j >= 2)
    def _():
        _wait_all(copies(j - 2, slot))

    comp_ref[slot] = compact
    _start_all(copies(j, slot))

    @pl.when(j == n_steps - 1)
    def _():
        _wait_all(copies(j, slot))

        @pl.when(j >= 1)
        def _():
            _wait_all(copies(j - 1, 1 - slot))

        comp_ref[slot, 0:MOE_TILE, :] = jnp.zeros((MOE_TILE, comp_ref.shape[2]), F32)
        tail = []
        for e in range(n_e):
            tail += _segment_copies(comp_ref.at[slot], 0, xs_ref, pad_ref[0, e], pad_ref[1, e], sem.at[slot])
        tile_groups = MOE_TILE // ROW_ALIGN
        groups_left = (xs_ref.shape[0] - pad_ref[2, 0]) // ROW_ALIGN
        min_total = TOP_K * x_ref.shape[0] * grp_ref.shape[0]
        for c in range((xs_ref.shape[0] - min_total) // MOE_TILE):
            tail += _segment_copies(comp_ref.at[slot], 0, xs_ref, pad_ref[2, 0] + c * MOE_TILE,
                                    jnp.clip(groups_left - c * tile_groups, 0, tile_groups), sem.at[slot])
        _start_all(tail)
        _wait_all(tail)


def _dispatch(x, g, lp, groups, off, dst, pad, n_rows):
    t, d = x.shape
    n_tiles = t // MOE_TILE
    grid_spec = pltpu.PrefetchScalarGridSpec(
        num_scalar_prefetch=4,
        grid=(n_tiles,),
        in_specs=[pl.BlockSpec((MOE_TILE, d), lambda j, *_: (j, 0)),
                  pl.BlockSpec(g.shape, lambda j, *_: (0, 0)),
                  pl.BlockSpec((None, TOP_K, MOE_TILE), lambda j, *_: (j, 0, 0))],
        out_specs=pl.BlockSpec(memory_space=pl.ANY),
        scratch_shapes=[pltpu.VMEM((2, COMPACT_ROWS, d), F32), pltpu.SemaphoreType.DMA((2,))])
    return pl.pallas_call(
        _dispatch_kernel,
        grid_spec=grid_spec,
        out_shape=jax.ShapeDtypeStruct((n_rows, d), F32),
        compiler_params=_cparams("arbitrary"),
        name="moe_dispatch",
    )(groups, off, dst, pad, x, g, lp)


STEP_FFN, STEP_ZERO = 1, 2


def _grouped_ffn_kernel(grp_ref, mode_ref, xtile_ref, xs_ref, wg_ref, wu_ref, wd_ref, ys_ref):
    w = pl.program_id(0)

    @pl.when(mode_ref[w] == STEP_ZERO)
    def _():
        ys_ref[...] = jnp.zeros_like(ys_ref)

    @pl.when(mode_ref[w] == STEP_FFN)
    def _():
        x = xs_ref[...].astype(BF16)
        act = (jax.nn.silu(_dot(x, wg_ref[...])) * _dot(x, wu_ref[...])).astype(BF16)
        ys_ref[...] = _dot(act, wd_ref[...])


def _grouped_ffn(xs, wg, wu, wd, grp, mode, xtile):
    n_e, f, d = wd.shape
    grid_spec = pltpu.PrefetchScalarGridSpec(
        num_scalar_prefetch=3,
        grid=(xs.shape[0] // MOE_TILE,),
        in_specs=[pl.BlockSpec((MOE_TILE, d), lambda w, grp, mode, xtile: (xtile[w], 0)),
                  pl.BlockSpec((None, d, f), lambda w, grp, *_: (grp[w], 0, 0)),
                  pl.BlockSpec((None, d, f), lambda w, grp, *_: (grp[w], 0, 0)),
                  pl.BlockSpec((None, f, d), lambda w, grp, *_: (grp[w], 0, 0))],
        out_specs=pl.BlockSpec((MOE_TILE, d), lambda w, *_: (w, 0)))
    return pl.pallas_call(
        _grouped_ffn_kernel,
        grid_spec=grid_spec,
        out_shape=jax.ShapeDtypeStruct(xs.shape, F32),
        compiler_params=_cparams("arbitrary"),
        name="moe_grouped_ffn",
    )(grp, mode, xtile, xs, wg, wu, wd)


def _combine_kernel(grp_ref, off_ref, dst_ref, x_ref, col_ref, *rest, final):
    if final:
        gfin_ref, rest = rest[0], rest[1:]
    ys_ref, o_ref, comp_ref, sem = rest
    j = pl.program_id(0)
    n_steps = pl.num_programs(0)
    n_e = grp_ref.shape[1]
    slot = j % 2

    def copies(step, s):
        out = []
        for e in range(n_e):
            out += _segment_copies(ys_ref, dst_ref[step, e], comp_ref.at[s], off_ref[step, e],
                                   grp_ref[step, e], sem.at[s])
        return out

    @pl.when(j == 0)
    def _():
        comp_ref[...] = jnp.zeros_like(comp_ref)
        _start_all(copies(0, 0))

    @pl.when(j + 1 < n_steps)
    def _():
        _start_all(copies(j + 1, 1 - slot))

    _wait_all(copies(j, slot))
    y = comp_ref[slot].astype(BF16)
    col = col_ref[...]
    row = lax.broadcasted_iota(jnp.int32, (col.shape[0], COMPACT_ROWS), 1)
    weights = jnp.zeros(row.shape, F32)
    for k in range(TOP_K):
        weights = jnp.where(row == col[:, k:k + 1].astype(jnp.int32), col[:, TOP_K + k:TOP_K + k + 1], weights)
    out = x_ref[...] + _dot(weights.astype(BF16), y)
    o_ref[...] = _rms(out, gfin_ref[...]) if final else out


def _combine(x, col, ys, groups, off, dst, g_final):
    t, d = x.shape
    n_tiles = t // MOE_TILE
    final = g_final is not None
    args = [x, col]
    specs = [pl.BlockSpec((MOE_TILE, d), lambda j, *_: (j, 0)),
             pl.BlockSpec((MOE_TILE, SUBLANES), lambda j, *_: (j, 0))]
    if final:
        args.append(g_final)
        specs.append(pl.BlockSpec(g_final.shape, lambda j, *_: (0, 0)))
    args.append(ys)
    specs.append(pl.BlockSpec(memory_space=pl.ANY))
    grid_spec = pltpu.PrefetchScalarGridSpec(
        num_scalar_prefetch=3,
        grid=(n_tiles,),
        in_specs=specs,
        out_specs=pl.BlockSpec((MOE_TILE, d), lambda j, *_: (j, 0)),
        scratch_shapes=[pltpu.VMEM((2, COMPACT_ROWS, d), F32), pltpu.SemaphoreType.DMA((2,))])
    return pl.pallas_call(
        functools.partial(_combine_kernel, final=final),
        grid_spec=grid_spec,
        out_shape=jax.ShapeDtypeStruct((t, d), F32),
        compiler_params=_cparams("arbitrary"),
        name="moe_combine",
    )(groups, off, dst, *args)


def _moe_routed(x, g, w_router, wg, wu, wd, g_final):
    t, _ = x.shape
    n_e = wd.shape[0]
    n_tiles = t // MOE_TILE
    lp, col, cnt = _route(x, g, w_router.T)
    groups = (cnt[:, :, 0].astype(jnp.int32) + (ROW_ALIGN - 1)) // ROW_ALIGN
    n = groups * ROW_ALIGN
    off = jnp.cumsum(n, axis=1) - n
    per_e = jnp.sum(n, axis=0)
    tiles_e = (per_e + (MOE_TILE - 1)) // MOE_TILE
    tile_end = jnp.cumsum(tiles_e)
    starts = (tile_end - tiles_e) * MOE_TILE
    dst = starts[None, :] + jnp.cumsum(n, axis=0) - n
    pad = jnp.stack([starts + per_e, (tiles_e * MOE_TILE - per_e) // ROW_ALIGN,
                     jnp.broadcast_to(tile_end[-1] * MOE_TILE, per_e.shape)])
    max_rows = TOP_K * t + n_tiles * n_e * (ROW_ALIGN - 1)
    n_row_tiles = -(-max_rows // MOE_TILE) + n_e
    xs = _dispatch(x, g, lp, groups, off, dst, pad, n_row_tiles * MOE_TILE)

    w = jnp.arange(n_row_tiles, dtype=jnp.int32)
    used = w < tile_end[-1]
    grp = jnp.minimum(jnp.sum(w[:, None] >= tile_end[None, :], axis=1), n_e - 1).astype(jnp.int32)
    mode = jnp.where(used, STEP_FFN, STEP_ZERO).astype(jnp.int32)
    xtile = jnp.minimum(w, tile_end[-1] - 1)
    ys = _grouped_ffn(xs, wg, wu, wd, grp, mode, xtile)
    return _combine(x, col, ys, groups, off, dst, g_final)


def _block_diag_gates(w_a, w_x):
    per_group = LRU_GROUP // LRU_HEAD_DIM
    eye = jnp.eye(per_group, dtype=w_a.dtype)

    def bd(w):
        w = w.reshape(D_LRU // LRU_GROUP, per_group, LRU_HEAD_DIM, LRU_HEAD_DIM)
        return jnp.einsum('ghij,hk->ghikj', w, eye).reshape(D_LRU // LRU_GROUP, LRU_GROUP, LRU_GROUP)

    return jnp.concatenate([bd(w_a), bd(w_x)], axis=-1).astype(BF16)


def kernel(x_prompt, x_sample, state_shortconv, state_lru_conv, state_lru_h, cache_mem_k, cache_mem_v,
           mem_prompt, norm_mix, w_in, w_short_conv, w_lru_conv, b_lru_conv, w_lru_a, b_lru_a, w_lru_x,
           b_lru_x, lru_lambda, norm_conv_out, norm_lru_out, w_mix_out, norm_xattn, norm_mem, w_q, w_k,
           w_v, w_o, norm_ffn, w_ff_gate, w_ff_up, w_ff_down, w_router, w_moe_gate, w_moe_up, w_moe_down,
           norm_final):
    depth = w_in.shape[0]
    nb, t, d = x_prompt.shape
    ns = x_sample.shape[0]
    n_mem = mem_prompt.shape[1]
    row = lambda a: a.reshape(1, -1)

    layers = []
    for l in range(depth):
        lp = dict(
            gmix=row(norm_mix[l]), win=w_in[l].astype(BF16), wsc=w_short_conv[l], wlc=w_lru_conv[l],
            blc=row(b_lru_conv[l]), wg=_block_diag_gates(w_lru_a[l], w_lru_x[l]), ba=row(b_lru_a[l]),
            bx=row(b_lru_x[l]), lam=row(lru_lambda[l]), gco=row(norm_conv_out[l]), glo=row(norm_lru_out[l]),
            wout=w_mix_out[l].astype(BF16), gx=row(norm_xattn[l]),
            wq=w_q[l].astype(BF16), wo=w_o[l].astype(BF16),
            gffn=row(norm_ffn[l]), gfin=row(norm_final) if l == depth - 1 else None)
        j = l // 2
        if l % 2 == 0:
            lp['ffw'] = (w_ff_gate[j].astype(BF16), w_ff_up[j].astype(BF16), w_ff_down[j].astype(BF16))
            lp['wr'] = None
        else:
            lp['ffw'] = (w_moe_gate[j].astype(BF16), w_moe_up[j].astype(BF16), w_moe_down[j].astype(BF16))
            lp['wr'] = jnp.pad(w_router[j], ((0, 0), (0, LANES - N_EXPERTS)))
            lp['wr_raw'] = w_router[j]
        layers.append(lp)

    x = x_prompt
    p_mem_k, p_mem_v, kb, vb = _kv_proj(mem_prompt, norm_mem.reshape(depth, 1, d), w_k.astype(BF16),
                                        w_v.astype(BF16))
    pc, plc, ph = [], [], []
    for l, lp in enumerate(layers):
        x, tail_p, tail_x, h_last = _mixer_prompt(x, lp)
        x = _xattn_prompt(x, lp['gx'], lp['wq'], kb[l], vb[l], lp['wo'])
        if lp['wr'] is None:
            x = _ffn_dense(x.reshape(nb * t, d), lp['gffn'], *lp['ffw'], tm=FFN_ROWS, g_final=lp['gfin'])
        else:
            x = _moe_routed(x.reshape(nb * t, d), lp['gffn'], lp['wr_raw'], *lp['ffw'], lp['gfin'])
        x = x.reshape(nb, t, d)
        tail_p = jnp.swapaxes(tail_p.reshape(MAX_CONV_HISTORY, nb, D_CONV), 0, 1)
        tail_x = jnp.swapaxes(tail_x.reshape(MAX_CONV_HISTORY, nb, D_LRU), 0, 1)
        pc.append(tail_p[:, MAX_CONV_HISTORY - (w_short_conv.shape[1] - 1):])
        plc.append(tail_x[:, MAX_CONV_HISTORY - (w_lru_conv.shape[1] - 1):])
        ph.append(h_last)
    y_prompt = x

    x = x_sample.reshape(ns, d)
    k8 = _to_head_rows(cache_mem_k.reshape(depth, ns, n_mem, d))
    v8 = _to_head_rows(cache_mem_v.reshape(depth, ns, n_mem, d))
    sc, slc, sh = [], [], []
    for l, lp in enumerate(layers):
        x, q, p_new, xb_new, h_new = _mixer_sample(x, state_shortconv[l], state_lru_conv[l], state_lru_h[l], lp)
        attn = (_xattn_sample(q, k8, v8, l), lp['wo'])
        if lp['wr'] is None:
            x = _ffn_dense(x, lp['gffn'], *lp['ffw'], tm=ns, g_final=lp['gfin'], attn=attn)
        else:
            x = _moe_all_experts(x, lp['gffn'], lp['wr'], *lp['ffw'], tm=ns, g_final=lp['gfin'], attn=attn)
        sc.append(jnp.stack([state_shortconv[l][:, 1], p_new], axis=1))
        slc.append(jnp.stack([state_lru_conv[l][:, 1], state_lru_conv[l][:, 2], xb_new], axis=1))
        sh.append(h_new)
    y_sample = x.reshape(ns, 1, d)

    return (y_prompt, y_sample, jnp.stack(pc), jnp.stack(plc), jnp.stack(ph), p_mem_k, p_mem_v,
            jnp.stack(sc), jnp.stack(slc), jnp.stack(sh))
```

```python
import functools

import jax
import jax.numpy as jnp
from jax import lax
from jax.experimental import pallas as pl
from jax.experimental.pallas import tpu as pltpu

F32 = jnp.float32
BF16 = jnp.bfloat16

EPS = 1e-6
LRU_C = 8.0
D_CONV = 512
D_LRU = 512
N_LRU_HEADS = 8
LRU_HEAD_DIM = 64
LRU_GROUP = 256
N_MEM_HEADS = 4
MEM_HEAD_DIM = 256
N_EXPERTS = 8
SUBLANES = 8
LANES = 128
VMEM_LIMIT_BYTES = 56 * 1024 * 1024

MIXER_TIME_CHUNK = 128
XATTN_ROWS = 1024
FFN_ROWS = 512
KV_PROJ_SEQS = 2
SAMPLE_ATTN_SEQS = 8


def _cparams(*sem):
    return pltpu.CompilerParams(dimension_semantics=sem, vmem_limit_bytes=VMEM_LIMIT_BYTES)


def _rms(x, g):
    return x * lax.rsqrt(jnp.mean(x * x, axis=-1, keepdims=True) + EPS) * g


def _dot(a, b):
    return jnp.dot(a, b, preferred_element_type=F32)


def _lru_gates(xc, wg_ref, ba, bx, lam):
    neg_lam = -lam
    softplus = jnp.maximum(neg_lam, 0.0) + jnp.log1p(jnp.exp(-jnp.abs(neg_lam)))
    a_parts, u_parts = [], []
    for g in range(D_LRU // LRU_GROUP):
        sl = slice(g * LRU_GROUP, (g + 1) * LRU_GROUP)
        xg = xc[:, sl]
        gates = _dot(xg.astype(BF16), wg_ref[g])
        r = jax.nn.sigmoid(gates[:, :LRU_GROUP] + ba[:, sl])
        i = jax.nn.sigmoid(gates[:, LRU_GROUP:] + bx[:, sl])
        log_a = (-LRU_C) * r * softplus[:, sl]
        th = jnp.tanh(log_a)
        one_minus_a2 = (-2.0 * th) / (1.0 - th)
        a_parts.append(jnp.exp(log_a))
        u_parts.append(jnp.sqrt(one_minus_a2) * (i * xg))
    return jnp.concatenate(a_parts, axis=-1), jnp.concatenate(u_parts, axis=-1)


def _mix_out(x, ya, yb, gco, glo, wout_ref):
    y = jnp.concatenate([_rms(ya, gco), _rms(yb, glo)], axis=-1).astype(BF16)
    return x + _dot(y, wout_ref[...])


MAX_CONV_HISTORY = 3


def _conv_taps(ext_ref, w_ref, rows, nb):
    width = w_ref.shape[0]
    hist = MAX_CONV_HISTORY * nb
    acc = ext_ref[hist:hist + rows, :] * w_ref[width - 1:width, :]
    for s in range(1, width):
        acc = acc + ext_ref[hist - s * nb:hist - s * nb + rows, :] * w_ref[width - 1 - s:width - s, :]
    return acc


def _mixer_prompt_kernel(x_hbm, gmix_ref, win_ref, wsc_ref, wlc_ref, blc_ref, wg_ref, ba_ref, bx_ref,
                         lam_ref, gco_ref, glo_ref, wout_ref,
                         xo_hbm, tailp_ref, tailx_ref, hout_ref,
                         xin_ref, xout_ref, in_sem, out_sem, pext_ref, xext_ref, h_ref, a_ref, u_ref):
    _, tt, nb, d = xin_ref.shape
    rows = tt * nb
    hist = MAX_CONV_HISTORY * nb
    i = pl.program_id(0)
    n_steps = pl.num_programs(0)
    slot = i % 2

    def in_copies(step, s):
        return [pltpu.make_async_copy(x_hbm.at[b, pl.ds(step * tt, tt), :], xin_ref.at[s, :, b, :],
                                      in_sem.at[s]) for b in range(nb)]

    def out_copies(step, s):
        return [pltpu.make_async_copy(xout_ref.at[s, :, b, :], xo_hbm.at[b, pl.ds(step * tt, tt), :],
                                      out_sem.at[s]) for b in range(nb)]

    @pl.when(i == 0)
    def _():
        pext_ref[0:hist, :] = jnp.zeros((hist, pext_ref.shape[1]), F32)
        xext_ref[0:hist, :] = jnp.zeros((hist, xext_ref.shape[1]), F32)
        h_ref[...] = jnp.zeros_like(h_ref)
        for cp in in_copies(0, 0):
            cp.start()

    @pl.when(i + 1 < n_steps)
    def _():
        for cp in in_copies(i + 1, 1 - slot):
            cp.start()

    for cp in in_copies(i, slot):
        cp.wait()
    x = xin_ref[slot].reshape(rows, d)
    z = _dot(_rms(x, gmix_ref[...]).astype(BF16), win_ref[...])
    gb = z[:, 0:D_CONV]
    gate = z[:, 3 * D_CONV + D_LRU:]
    pext_ref[hist:hist + rows, :] = z[:, D_CONV:2 * D_CONV] * z[:, 2 * D_CONV:3 * D_CONV]
    xext_ref[hist:hist + rows, :] = z[:, 3 * D_CONV:3 * D_CONV + D_LRU]

    ya = gb * _conv_taps(pext_ref, wsc_ref, rows, nb)
    xc = _conv_taps(xext_ref, wlc_ref, rows, nb) + blc_ref[...]
    for ext_ref, tail_ref in ((pext_ref, tailp_ref), (xext_ref, tailx_ref)):
        tail = ext_ref[rows:rows + hist, :]
        ext_ref[0:hist, :] = tail
        tail_ref[...] = tail

    a, u = _lru_gates(xc, wg_ref, ba_ref[...], bx_ref[...], lam_ref[...])
    a_ref[...] = a
    u_ref[...] = u

    def step(t, h):
        grp = pl.ds(pl.multiple_of(t * nb, nb), nb)
        h = a_ref[grp, :] * h + u_ref[grp, :]
        u_ref[grp, :] = h
        return h

    h_last = lax.fori_loop(0, tt, step, h_ref[...], unroll=8)
    h_ref[...] = h_last
    hout_ref[...] = h_last

    yb = u_ref[...] * jax.nn.gelu(gate)
    out = _mix_out(x, ya, yb, gco_ref[...], glo_ref[...], wout_ref)

    @pl.when(i >= 2)
    def _():
        for cp in out_copies(i - 2, slot):
            cp.wait()

    xout_ref[slot] = out.reshape(tt, nb, d)
    for cp in out_copies(i, slot):
        cp.start()

    @pl.when(i == n_steps - 1)
    def _():
        for cp in out_copies(i, slot):
            cp.wait()

        @pl.when(i >= 1)
        def _():
            for cp in out_copies(i - 1, 1 - slot):
                cp.wait()


def _const_spec(shape):
    return pl.BlockSpec(shape, lambda *_: (0,) * len(shape))


def _param_array(p):
    return p[0] if isinstance(p, tuple) else p


def _param_spec(p, resident=False):
    a, layer = p if isinstance(p, tuple) else (p, None)
    mode = dict(pipeline_mode=pl.Buffered(1)) if resident else {}
    if layer is None:
        return pl.BlockSpec(a.shape, lambda *_: (0,) * a.ndim, **mode)
    return pl.BlockSpec((None,) + a.shape[1:], lambda *_: (layer,) + (0,) * (a.ndim - 1), **mode)


def _mixer_prompt(x, p, tt=MIXER_TIME_CHUNK):
    nb, t, d = x.shape
    small = [p['gmix'], p['win'], p['wsc'], p['wlc'], p['blc'], p['wg'], p['ba'], p['bx'], p['lam'],
             p['gco'], p['glo'], p['wout']]
    rows = nb * tt
    hist = MAX_CONV_HISTORY * nb
    return pl.pallas_call(
        _mixer_prompt_kernel,
        grid=(t // tt,),
        in_specs=[pl.BlockSpec(memory_space=pl.ANY)] + [_param_spec(a, resident=True) for a in small],
        out_specs=[pl.BlockSpec(memory_space=pl.ANY),
                   _const_spec((hist, D_CONV)), _const_spec((hist, D_LRU)), _const_spec((nb, D_LRU))],
        out_shape=[jax.ShapeDtypeStruct((nb, t, d), F32),
                   jax.ShapeDtypeStruct((hist, D_CONV), F32),
                   jax.ShapeDtypeStruct((hist, D_LRU), F32),
                   jax.ShapeDtypeStruct((nb, D_LRU), F32)],
        scratch_shapes=[pltpu.VMEM((2, tt, nb, d), F32), pltpu.VMEM((2, tt, nb, d), F32),
                        pltpu.SemaphoreType.DMA((2,)), pltpu.SemaphoreType.DMA((2,)),
                        pltpu.VMEM((hist + rows, D_CONV), F32), pltpu.VMEM((hist + rows, D_LRU), F32),
                        pltpu.VMEM((nb, D_LRU), F32),
                        pltpu.VMEM((rows, D_LRU), F32), pltpu.VMEM((rows, D_LRU), F32)],
        compiler_params=_cparams("arbitrary"),
        name="mixer_prompt",
    )(x, *map(_param_array, small))


def _mixer_sample_kernel(x_ref, sc0_ref, sc1_ref, lc0_ref, lc1_ref, lc2_ref, h0_ref,
                         gmix_ref, win_ref, wsc_ref, wlc_ref, blc_ref, wg_ref, ba_ref, bx_ref,
                         lam_ref, gco_ref, glo_ref, wout_ref, gx_ref, wq_ref,
                         xo_ref, q_ref, pnew_ref, xbnew_ref, hout_ref):
    x = x_ref[...]
    z = _dot(_rms(x, gmix_ref[...]).astype(BF16), win_ref[...])
    gb = z[:, 0:D_CONV]
    p = z[:, D_CONV:2 * D_CONV] * z[:, 2 * D_CONV:3 * D_CONV]
    xb = z[:, 3 * D_CONV:3 * D_CONV + D_LRU]
    gate = z[:, 3 * D_CONV + D_LRU:]

    ya = gb * (sc0_ref[...] * wsc_ref[0:1, :] + sc1_ref[...] * wsc_ref[1:2, :] + p * wsc_ref[2:3, :])
    xc = (lc0_ref[...] * wlc_ref[0:1, :] + lc1_ref[...] * wlc_ref[1:2, :] + lc2_ref[...] * wlc_ref[2:3, :]
          + xb * wlc_ref[3:4, :]) + blc_ref[...]
    a, u = _lru_gates(xc, wg_ref, ba_ref[...], bx_ref[...], lam_ref[...])
    h = a * h0_ref[...] + u
    yb = h * jax.nn.gelu(gate)
    x1 = _mix_out(x, ya, yb, gco_ref[...], glo_ref[...], wout_ref)
    xo_ref[...] = x1
    q_ref[...] = _dot(_rms(x1, gx_ref[...]).astype(BF16), wq_ref[...])
    pnew_ref[...] = p
    xbnew_ref[...] = xb
    hout_ref[...] = h


def _mixer_sample(x, sc, lc, h0, p):
    nb, d = x.shape
    args = [x, sc[:, 0], sc[:, 1], lc[:, 0], lc[:, 1], lc[:, 2], h0,
            p['gmix'], p['win'], p['wsc'], p['wlc'], p['blc'], p['wg'], p['ba'], p['bx'], p['lam'],
            p['gco'], p['glo'], p['wout'], p['gx'], p['wq']]
    return pl.pallas_call(
        _mixer_sample_kernel,
        grid=(1,),
        in_specs=[_param_spec(a) for a in args],
        out_specs=[_const_spec((nb, d)), _const_spec((nb, d)), _const_spec((nb, D_CONV)),
                   _const_spec((nb, D_LRU)), _const_spec((nb, D_LRU))],
        out_shape=[jax.ShapeDtypeStruct((nb, d), F32), jax.ShapeDtypeStruct((nb, d), F32),
                   jax.ShapeDtypeStruct((nb, D_CONV), F32), jax.ShapeDtypeStruct((nb, D_LRU), F32),
                   jax.ShapeDtypeStruct((nb, D_LRU), F32)],
        compiler_params=_cparams("arbitrary"),
        name="mixer_sample",
    )(*map(_param_array, args))


def _xattn_prompt_kernel(x_ref, g_ref, wq_ref, k_ref, v_ref, wo_ref, o_ref):
    x = x_ref[...]
    q = _dot(_rms(x, g_ref[...]).astype(BF16), wq_ref[...])
    heads = []
    for h in range(N_MEM_HEADS):
        sl = slice(h * MEM_HEAD_DIM, (h + 1) * MEM_HEAD_DIM)
        kh = k_ref[:, sl]
        vh = v_ref[:, sl]
        s = lax.dot_general(q[:, sl].astype(BF16), kh, (((1,), (1,)), ((), ())),
                            preferred_element_type=F32) * (MEM_HEAD_DIM ** -0.5)
        e = jnp.exp(s - jnp.max(s, axis=-1, keepdims=True))
        pr = e / jnp.sum(e, axis=-1, keepdims=True)
        heads.append(_dot(pr.astype(BF16), vh))
    o = jnp.concatenate(heads, axis=-1).astype(BF16)
    o_ref[...] = x + _dot(o, wo_ref[...])


def _xattn_prompt(x, g, wq, k, v, wo, layer, tq=XATTN_ROWS):
    nb, t, d = x.shape
    n_mem = k.shape[2]
    kv_spec = pl.BlockSpec((None, None, n_mem, d), lambda b, i: (layer, b, 0, 0))
    return pl.pallas_call(
        _xattn_prompt_kernel,
        grid=(nb, t // tq),
        in_specs=[pl.BlockSpec((None, tq, d), lambda b, i: (b, i, 0)), _param_spec(g),
                  _param_spec(wq, resident=True), kv_spec, kv_spec, _param_spec(wo, resident=True)],
        out_specs=pl.BlockSpec((None, tq, d), lambda b, i: (b, i, 0)),
        out_shape=jax.ShapeDtypeStruct((nb, t, d), F32),
        compiler_params=_cparams("parallel", "parallel"),
        name="xattn_prompt",
    )(x, _param_array(g), _param_array(wq), k, v, _param_array(wo))


def _kv_proj_kernel(mem_ref, g_ref, wk_ref, wv_ref, k5_ref, v5_ref, kb_ref, vb_ref):
    bb, n_mem, d = mem_ref.shape
    m = _rms(mem_ref[...].reshape(bb * n_mem, d), g_ref[...]).astype(BF16)
    for w_ref, o5_ref, ob_ref in ((wk_ref, k5_ref, kb_ref), (wv_ref, v5_ref, vb_ref)):
        y = _dot(m, w_ref[...]).reshape(bb, n_mem, d)
        ob_ref[...] = y.astype(BF16)
        for h in range(N_MEM_HEADS):
            o5_ref[:, :, h, :] = y[:, :, h * MEM_HEAD_DIM:(h + 1) * MEM_HEAD_DIM]


def _kv_proj(mem, gmem, wk, wv, bb=KV_PROJ_SEQS):
    nb, n_mem, d = mem.shape
    depth = wk.shape[0]
    w_spec = pl.BlockSpec((None, d, d), lambda l, i: (l, 0, 0))
    o5_spec = pl.BlockSpec((None, bb, n_mem, N_MEM_HEADS, MEM_HEAD_DIM), lambda l, i: (l, i, 0, 0, 0))
    ob_spec = pl.BlockSpec((None, bb, n_mem, d), lambda l, i: (l, i, 0, 0))
    o5_shape = jax.ShapeDtypeStruct((depth, nb, n_mem, N_MEM_HEADS, MEM_HEAD_DIM), F32)
    ob_shape = jax.ShapeDtypeStruct((depth, nb, n_mem, d), BF16)
    return pl.pallas_call(
        _kv_proj_kernel,
        grid=(depth, nb // bb),
        in_specs=[pl.BlockSpec((bb, n_mem, d), lambda l, i: (i, 0, 0)),
                  pl.BlockSpec((None, 1, d), lambda l, i: (l, 0, 0)), w_spec, w_spec],
        out_specs=[o5_spec, o5_spec, ob_spec, ob_spec],
        out_shape=[o5_shape, o5_shape, ob_shape, ob_shape],
        compiler_params=_cparams("parallel", "parallel"),
        name="kv_proj",
    )(mem, gmem, wk, wv)


LANE_TILES_PER_HEAD = MEM_HEAD_DIM // LANES
HEAD_ROWS = N_MEM_HEADS * LANE_TILES_PER_HEAD


def _to_head_rows(a):
    lead = a.shape[:-1]
    n = len(lead)
    a = a.reshape(*lead, N_MEM_HEADS, LANE_TILES_PER_HEAD, LANES)
    return jnp.swapaxes(a, n, n + 1).reshape(*lead, HEAD_ROWS, LANES)


def _from_head_rows(a):
    lead = a.shape[:-2]
    n = len(lead)
    a = a.reshape(*lead, LANE_TILES_PER_HEAD, N_MEM_HEADS, LANES)
    return jnp.swapaxes(a, n, n + 1).reshape(*lead, N_MEM_HEADS * MEM_HEAD_DIM)


def _xattn_sample_kernel(q_ref, k_ref, v_ref, o_ref):
    n_mem = k_ref.shape[1]
    ones = jnp.ones((LANES, LANES), BF16)
    for b in range(q_ref.shape[0]):
        prod = (k_ref[b] * q_ref[b][None]).reshape(n_mem * HEAD_ROWS, LANES).astype(BF16)
        part = _dot(prod, ones).reshape(n_mem, HEAD_ROWS, LANES)
        s = (part + pltpu.roll(part, N_MEM_HEADS, axis=1)) * (MEM_HEAD_DIM ** -0.5)
        e = jnp.exp(s - jnp.max(s, axis=0, keepdims=True))
        pr = e / jnp.sum(e, axis=0, keepdims=True)
        o_ref[b] = jnp.sum(pr * v_ref[b], axis=0)


def _xattn_sample(q, k8, v8, layer, bb=SAMPLE_ATTN_SEQS):
    nb, d = q.shape
    n_mem = k8.shape[2]
    kv_spec = pl.BlockSpec((None, bb, n_mem, HEAD_ROWS, LANES), lambda i: (layer, i, 0, 0, 0))
    out = pl.pallas_call(
        _xattn_sample_kernel,
        grid=(nb // bb,),
        in_specs=[pl.BlockSpec((bb, HEAD_ROWS, LANES), lambda i: (i, 0, 0)), kv_spec, kv_spec],
        out_specs=pl.BlockSpec((bb, HEAD_ROWS, LANES), lambda i: (i, 0, 0)),
        out_shape=jax.ShapeDtypeStruct((nb, HEAD_ROWS, LANES), F32),
        compiler_params=_cparams("parallel"),
        name="xattn_sample",
    )(_to_head_rows(q), k8, v8)
    return _from_head_rows(out)


def _split_bf16(a):
    hi = a.astype(BF16)
    return hi, (a - hi.astype(F32)).astype(BF16)


def _top2(logits):
    n = logits.shape[-1]
    lane = lax.broadcasted_iota(jnp.int32, logits.shape, 1)
    m1 = jnp.max(logits, axis=-1, keepdims=True)
    i1 = jnp.min(jnp.where(logits == m1, lane, n), axis=-1, keepdims=True)
    rest = jnp.where(lane == i1, -jnp.inf, logits)
    m2 = jnp.max(rest, axis=-1, keepdims=True)
    i2 = jnp.min(jnp.where(rest == m2, lane, n), axis=-1, keepdims=True)
    e2 = jnp.exp(m2 - m1)
    denom = 1.0 + e2
    return i1, i2, 1.0 / denom, e2 / denom


def _add_attn_proj(x, rest, proj):
    if not proj:
        return x, rest
    a_ref, wo_ref = rest[:2]
    return x + _dot(a_ref[...].astype(BF16), wo_ref[...]), rest[2:]


def _ffn_dense_kernel(x_ref, g_ref, *rest, final, proj):
    x, rest = _add_attn_proj(x_ref[...], rest, proj)
    if final:
        gfin_ref, rest = rest[0], rest[1:]
    wg_ref, wu_ref, wd_ref, o_ref = rest
    h = _rms(x, g_ref[...]).astype(BF16)
    act = (jax.nn.silu(_dot(h, wg_ref[...])) * _dot(h, wu_ref[...])).astype(BF16)
    out = x + _dot(act, wd_ref[...])
    o_ref[...] = _rms(out, gfin_ref[...]) if final else out


def _ffn_dense(x, g, wg, wu, wd, tm, g_final=None, attn=None):
    r, d = x.shape
    final, proj = g_final is not None, attn is not None
    args, specs = [x, g], [pl.BlockSpec((tm, d), lambda i: (i, 0)), _param_spec(g)]
    if proj:
        args += list(attn)
        specs += [pl.BlockSpec((tm, attn[0].shape[1]), lambda i: (i, 0)), _param_spec(attn[1], resident=True)]
    if final:
        args.append(g_final)
        specs.append(_param_spec(g_final))
    args += [wg, wu, wd]
    specs += [_param_spec(w, resident=True) for w in (wg, wu, wd)]
    return pl.pallas_call(
        functools.partial(_ffn_dense_kernel, final=final, proj=proj),
        grid=(r // tm,),
        in_specs=specs,
        out_specs=pl.BlockSpec((tm, d), lambda i: (i, 0)),
        out_shape=jax.ShapeDtypeStruct((r, d), F32),
        compiler_params=_cparams("parallel"),
        name="ffn_dense",
    )(*map(_param_array, args))


def _moe_all_experts_kernel(x_ref, g_ref, wr_ref, *rest, final, proj):
    if proj:
        attn_refs, rest = rest[:2], rest[2:]
    if final:
        gfin_ref, rest = rest[0], rest[1:]
    wg_ref, wu_ref, wd_ref, o_ref, h_ref, acc_ref, i1_ref, i2_ref, g1_ref, g2_ref = rest
    e = pl.program_id(1)

    @pl.when(e == 0)
    def _():
        x = x_ref[...]
        if proj:
            x, _ = _add_attn_proj(x, attn_refs, proj)
        h = _rms(x, g_ref[...])
        h_ref[...] = h.astype(BF16)
        acc_ref[...] = x
        h_hi, h_lo = _split_bf16(h)
        w_hi, w_lo = _split_bf16(wr_ref[...])
        logits = _dot(h_hi, w_hi) + (_dot(h_hi, w_lo) + _dot(h_lo, w_hi))
        lane = lax.broadcasted_iota(jnp.int32, logits.shape, 1)
        logits = jnp.where(lane < N_EXPERTS, logits, -jnp.inf)
        i1_ref[...], i2_ref[...], g1_ref[...], g2_ref[...] = _top2(logits)

    hb = h_ref[...]
    y = _dot((jax.nn.silu(_dot(hb, wg_ref[...])) * _dot(hb, wu_ref[...])).astype(BF16), wd_ref[...])
    c = jnp.where(i1_ref[...] == e, g1_ref[...], 0.0) + jnp.where(i2_ref[...] == e, g2_ref[...], 0.0)
    acc_ref[...] += c * y

    @pl.when(e == pl.num_programs(1) - 1)
    def _():
        out = acc_ref[...]
        o_ref[...] = _rms(out, gfin_ref[...]) if final else out


def _moe_all_experts(x, g, w_router, wg, wu, wd, tm, g_final=None, attn=None):
    r, d = x.shape
    n_e, f, _ = wd.shape
    final, proj = g_final is not None, attn is not None
    args = [x, g, w_router]
    specs = [pl.BlockSpec((tm, d), lambda i, e: (i, 0)), _param_spec(g), _param_spec(w_router)]
    if proj:
        args += list(attn)
        specs += [pl.BlockSpec((tm, attn[0].shape[1]), lambda i, e: (i, 0)), _param_spec(attn[1])]
    if final:
        args.append(g_final)
        specs.append(_param_spec(g_final))
    args += [wg, wu, wd]
    specs += [pl.BlockSpec((None, d, f), lambda i, e: (e, 0, 0)),
              pl.BlockSpec((None, d, f), lambda i, e: (e, 0, 0)),
              pl.BlockSpec((None, f, d), lambda i, e: (e, 0, 0))]
    return pl.pallas_call(
        functools.partial(_moe_all_experts_kernel, final=final, proj=proj),
        grid=(r // tm, n_e),
        in_specs=specs,
        out_specs=pl.BlockSpec((tm, d), lambda i, e: (i, 0)),
        out_shape=jax.ShapeDtypeStruct((r, d), F32),
        scratch_shapes=[pltpu.VMEM((tm, d), BF16), pltpu.VMEM((tm, d), F32),
                        pltpu.VMEM((tm, 1), jnp.int32), pltpu.VMEM((tm, 1), jnp.int32),
                        pltpu.VMEM((tm, 1), F32), pltpu.VMEM((tm, 1), F32)],
        compiler_params=_cparams("parallel", "arbitrary"),
        name="moe_all_experts",
    )(*map(_param_array, args))


MOE_TILE = 512
TOP_K = 2
ROW_ALIGN = SUBLANES
COMPACT_ROWS = -(-(TOP_K * MOE_TILE + N_EXPERTS * (ROW_ALIGN - 1)) // (2 * SUBLANES)) * (2 * SUBLANES)
SEGMENT_BITS = (MOE_TILE // ROW_ALIGN).bit_length()


def _route_kernel(x_ref, g_ref, wrt_ref, earlier_ref, lp_ref, col_ref, cnt_ref):
    tk = x_ref.shape[0]
    n_e = wrt_ref.shape[0]
    h = _rms(x_ref[...], g_ref[...])
    h_hi, h_lo = _split_bf16(h)
    w_hi, w_lo = _split_bf16(wrt_ref[...])
    nt = (((1,), (1,)), ((), ()))
    dg = lambda a, b: lax.dot_general(a, b, nt, preferred_element_type=F32)
    logits = dg(w_hi, h_hi) + (dg(w_hi, h_lo) + dg(w_lo, h_hi))
    sub = lax.broadcasted_iota(jnp.int32, logits.shape, 0)
    m1 = jnp.max(logits, axis=0, keepdims=True)
    i1 = jnp.min(jnp.where(logits == m1, sub, n_e), axis=0, keepdims=True)
    rest = jnp.where(sub == i1, -jnp.inf, logits)
    m2 = jnp.max(rest, axis=0, keepdims=True)
    i2 = jnp.min(jnp.where(rest == m2, sub, n_e), axis=0, keepdims=True)
    e2 = jnp.exp(m2 - m1)
    denom = 1.0 + e2

    pick1 = jnp.where(sub == i1, 1.0, 0.0)
    pick2 = jnp.where(sub == i2, 1.0, 0.0)
    cnt = pick1 + pick2
    rank = _dot(cnt.astype(BF16), earlier_ref[...])
    n = jnp.sum(cnt, axis=1, keepdims=True)
    n_pad = jnp.floor((n + (ROW_ALIGN - 1)) * (1.0 / ROW_ALIGN)) * ROW_ALIGN
    base, acc = [], jnp.zeros((1, 1), F32)
    for e in range(n_e):
        base.append(acc)
        acc = acc + n_pad[e:e + 1, :]
    pos = rank + jnp.concatenate(base, axis=0)
    lp1 = jnp.sum(pick1 * pos, axis=0, keepdims=True)
    lp2 = jnp.sum(pick2 * pos, axis=0, keepdims=True)
    lp_ref[0:1, :] = lp1.astype(jnp.int32)
    lp_ref[1:2, :] = lp2.astype(jnp.int32)
    info = jnp.concatenate([lp1, lp2, 1.0 / denom, e2 / denom, jnp.zeros((SUBLANES - 4, tk), F32)], axis=0)
    col_ref[...] = info.T
    cnt_ref[...] = jnp.broadcast_to(n, cnt_ref.shape)


def _route(x, g, wrt):
    t, d = x.shape
    n_e = wrt.shape[0]
    n_tiles = t // MOE_TILE
    idx = jnp.arange(MOE_TILE)
    earlier = (idx[:, None] < idx[None, :]).astype(BF16)
    return pl.pallas_call(
        _route_kernel,
        grid=(n_tiles,),
        in_specs=[pl.BlockSpec((MOE_TILE, d), lambda j: (j, 0)), _param_spec(g), _const_spec(wrt.shape),
                  _const_spec(earlier.shape)],
        out_specs=[pl.BlockSpec((None, TOP_K, MOE_TILE), lambda j: (j, 0, 0)),
                   pl.BlockSpec((MOE_TILE, SUBLANES), lambda j: (j, 0)),
                   pl.BlockSpec((None, n_e, LANES), lambda j: (j, 0, 0))],
        out_shape=[jax.ShapeDtypeStruct((n_tiles, TOP_K, MOE_TILE), jnp.int32),
                   jax.ShapeDtypeStruct((t, SUBLANES), F32),
                   jax.ShapeDtypeStruct((n_tiles, n_e, LANES), F32)],
        compiler_params=_cparams("parallel"),
        name="moe_route",
    )(x, _param_array(g), wrt, earlier)


def _segment_copies(src_ref, src_row, dst_ref, dst_row, groups, sem):
    out = []
    for bit in range(SEGMENT_BITS):
        rows = ROW_ALIGN << bit
        before = ((groups >> (bit + 1)) << (bit + 1)) * ROW_ALIGN
        cp = pltpu.make_async_copy(
            src_ref.at[pl.ds(pl.multiple_of(src_row + before, ROW_ALIGN), rows), :],
            dst_ref.at[pl.ds(pl.multiple_of(dst_row + before, ROW_ALIGN), rows), :], sem)
        out.append((((groups >> bit) & 1) == 1, cp))
    return out


def _start_all(copies):
    for pred, cp in copies:
        pl.when(pred)(cp.start)


def _wait_all(copies):
    for pred, cp in copies:
        pl.when(pred)(cp.wait)


def _dispatch_kernel(grp_ref, off_ref, dst_ref, pad_ref, x_ref, g_ref, lp_ref, xs_ref, comp_ref, sem):
    j = pl.program_id(0)
    n_steps = pl.num_programs(0)
    n_e = grp_ref.shape[1]
    slot = j % 2

    def copies(step, s):
        out = []
        for e in range(n_e):
            out += _segment_copies(comp_ref.at[s], off_ref[step, e], xs_ref, dst_ref[step, e],
                                   grp_ref[step, e], sem.at[s])
        return out

    h = _rms(x_ref[...], g_ref[...]).astype(BF16)
    row = lax.broadcasted_iota(jnp.int32, (COMPACT_ROWS, h.shape[0]), 0)
    onehot = jnp.where(row == lp_ref[0:1, :], 1.0, jnp.where(row == lp_ref[1:2, :], 1.0, 0.0)).astype(BF16)
    compact = _dot(onehot, h)

    @pl.when(j >= 2)
    def _():
        _wait_all(copies(j - 2, slot))

    comp_ref[slot] = compact
    _start_all(copies(j, slot))

    @pl.when(j == n_steps - 1)
    def _():
        _wait_all(copies(j, slot))

        @pl.when(j >= 1)
        def _():
            _wait_all(copies(j - 1, 1 - slot))

        comp_ref[slot, 0:MOE_TILE, :] = jnp.zeros((MOE_TILE, comp_ref.shape[2]), F32)
        tail = []
        for e in range(n_e):
            tail += _segment_copies(comp_ref.at[slot], 0, xs_ref, pad_ref[0, e], pad_ref[1, e], sem.at[slot])
        tile_groups = MOE_TILE // ROW_ALIGN
        groups_left = (xs_ref.shape[0] - pad_ref[2, 0]) // ROW_ALIGN
        min_total = TOP_K * x_ref.shape[0] * grp_ref.shape[0]
        for c in range((xs_ref.shape[0] - min_total) // MOE_TILE):
            tail += _segment_copies(comp_ref.at[slot], 0, xs_ref, pad_ref[2, 0] + c * MOE_TILE,
                                    jnp.clip(groups_left - c * tile_groups, 0, tile_groups), sem.at[slot])
        _start_all(tail)
        _wait_all(tail)


def _dispatch(x, g, lp, groups, off, dst, pad, n_rows):
    t, d = x.shape
    n_tiles = t // MOE_TILE
    grid_spec = pltpu.PrefetchScalarGridSpec(
        num_scalar_prefetch=4,
        grid=(n_tiles,),
        in_specs=[pl.BlockSpec((MOE_TILE, d), lambda j, *_: (j, 0)),
                  _param_spec(g),
                  pl.BlockSpec((None, TOP_K, MOE_TILE), lambda j, *_: (j, 0, 0))],
        out_specs=pl.BlockSpec(memory_space=pl.ANY),
        scratch_shapes=[pltpu.VMEM((2, COMPACT_ROWS, d), F32), pltpu.SemaphoreType.DMA((2,))])
    return pl.pallas_call(
        _dispatch_kernel,
        grid_spec=grid_spec,
        out_shape=jax.ShapeDtypeStruct((n_rows, d), F32),
        compiler_params=_cparams("arbitrary"),
        name="moe_dispatch",
    )(groups, off, dst, pad, x, _param_array(g), lp)


STEP_FFN, STEP_ZERO = 1, 2


def _grouped_ffn_kernel(grp_ref, mode_ref, xtile_ref, xs_ref, wg_ref, wu_ref, wd_ref, ys_ref):
    w = pl.program_id(0)

    @pl.when(mode_ref[w] == STEP_ZERO)
    def _():
        ys_ref[...] = jnp.zeros_like(ys_ref)

    @pl.when(mode_ref[w] == STEP_FFN)
    def _():
        x = xs_ref[...].astype(BF16)
        act = (jax.nn.silu(_dot(x, wg_ref[...])) * _dot(x, wu_ref[...])).astype(BF16)
        ys_ref[...] = _dot(act, wd_ref[...])


def _grouped_ffn(xs, wg, wu, wd, grp, mode, xtile):
    n_e, f, d = wd.shape
    grid_spec = pltpu.PrefetchScalarGridSpec(
        num_scalar_prefetch=3,
        grid=(xs.shape[0] // MOE_TILE,),
        in_specs=[pl.BlockSpec((MOE_TILE, d), lambda w, grp, mode, xtile: (xtile[w], 0)),
                  pl.BlockSpec((None, d, f), lambda w, grp, *_: (grp[w], 0, 0)),
                  pl.BlockSpec((None, d, f), lambda w, grp, *_: (grp[w], 0, 0)),
                  pl.BlockSpec((None, f, d), lambda w, grp, *_: (grp[w], 0, 0))],
        out_specs=pl.BlockSpec((MOE_TILE, d), lambda w, *_: (w, 0)))
    return pl.pallas_call(
        _grouped_ffn_kernel,
        grid_spec=grid_spec,
        out_shape=jax.ShapeDtypeStruct(xs.shape, F32),
        compiler_params=_cparams("arbitrary"),
        name="moe_grouped_ffn",
    )(grp, mode, xtile, xs, wg, wu, wd)


def _combine_kernel(grp_ref, off_ref, dst_ref, x_ref, col_ref, *rest, final):
    if final:
        gfin_ref, rest = rest[0], rest[1:]
    ys_ref, o_ref, comp_ref, sem = rest
    j = pl.program_id(0)
    n_steps = pl.num_programs(0)
    n_e = grp_ref.shape[1]
    slot = j % 2

    def copies(step, s):
        out = []
        for e in range(n_e):
            out += _segment_copies(ys_ref, dst_ref[step, e], comp_ref.at[s], off_ref[step, e],
                                   grp_ref[step, e], sem.at[s])
        return out

    @pl.when(j == 0)
    def _():
        comp_ref[...] = jnp.zeros_like(comp_ref)
        _start_all(copies(0, 0))

    @pl.when(j + 1 < n_steps)
    def _():
        _start_all(copies(j + 1, 1 - slot))

    _wait_all(copies(j, slot))
    y = comp_ref[slot].astype(BF16)
    col = col_ref[...]
    row = lax.broadcasted_iota(jnp.int32, (col.shape[0], COMPACT_ROWS), 1)
    weights = jnp.zeros(row.shape, F32)
    for k in range(TOP_K):
        weights = jnp.where(row == col[:, k:k + 1].astype(jnp.int32), col[:, TOP_K + k:TOP_K + k + 1], weights)
    out = x_ref[...] + _dot(weights.astype(BF16), y)
    o_ref[...] = _rms(out, gfin_ref[...]) if final else out


def _combine(x, col, ys, groups, off, dst, g_final):
    t, d = x.shape
    n_tiles = t // MOE_TILE
    final = g_final is not None
    args = [x, col]
    specs = [pl.BlockSpec((MOE_TILE, d), lambda j, *_: (j, 0)),
             pl.BlockSpec((MOE_TILE, SUBLANES), lambda j, *_: (j, 0))]
    if final:
        args.append(g_final)
        specs.append(_param_spec(g_final))
    args.append(ys)
    specs.append(pl.BlockSpec(memory_space=pl.ANY))
    grid_spec = pltpu.PrefetchScalarGridSpec(
        num_scalar_prefetch=3,
        grid=(n_tiles,),
        in_specs=specs,
        out_specs=pl.BlockSpec((MOE_TILE, d), lambda j, *_: (j, 0)),
        scratch_shapes=[pltpu.VMEM((2, COMPACT_ROWS, d), F32), pltpu.SemaphoreType.DMA((2,))])
    return pl.pallas_call(
        functools.partial(_combine_kernel, final=final),
        grid_spec=grid_spec,
        out_shape=jax.ShapeDtypeStruct((t, d), F32),
        compiler_params=_cparams("arbitrary"),
        name="moe_combine",
    )(groups, off, dst, *map(_param_array, args))


def _moe_routed(x, g, w_router, wg, wu, wd, g_final):
    t, _ = x.shape
    n_e = wd.shape[0]
    n_tiles = t // MOE_TILE
    lp, col, cnt = _route(x, g, w_router.T)
    groups = (cnt[:, :, 0].astype(jnp.int32) + (ROW_ALIGN - 1)) // ROW_ALIGN
    n = groups * ROW_ALIGN
    off = jnp.cumsum(n, axis=1) - n
    per_e = jnp.sum(n, axis=0)
    tiles_e = (per_e + (MOE_TILE - 1)) // MOE_TILE
    tile_end = jnp.cumsum(tiles_e)
    starts = (tile_end - tiles_e) * MOE_TILE
    dst = starts[None, :] + jnp.cumsum(n, axis=0) - n
    pad = jnp.stack([starts + per_e, (tiles_e * MOE_TILE - per_e) // ROW_ALIGN,
                     jnp.broadcast_to(tile_end[-1] * MOE_TILE, per_e.shape)])
    max_rows = TOP_K * t + n_tiles * n_e * (ROW_ALIGN - 1)
    n_row_tiles = -(-max_rows // MOE_TILE) + n_e
    xs = _dispatch(x, g, lp, groups, off, dst, pad, n_row_tiles * MOE_TILE)

    w = jnp.arange(n_row_tiles, dtype=jnp.int32)
    used = w < tile_end[-1]
    grp = jnp.minimum(jnp.sum(w[:, None] >= tile_end[None, :], axis=1), n_e - 1).astype(jnp.int32)
    mode = jnp.where(used, STEP_FFN, STEP_ZERO).astype(jnp.int32)
    xtile = jnp.minimum(w, tile_end[-1] - 1)
    ys = _grouped_ffn(xs, wg, wu, wd, grp, mode, xtile)
    return _combine(x, col, ys, groups, off, dst, g_final)


def _block_diag_gates(w_a, w_x):
    per_group = LRU_GROUP // LRU_HEAD_DIM
    eye = jnp.eye(per_group, dtype=w_a.dtype)

    def bd(w):
        w = w.reshape(D_LRU // LRU_GROUP, per_group, LRU_HEAD_DIM, LRU_HEAD_DIM)
        return jnp.einsum('ghij,hk->ghikj', w, eye).reshape(D_LRU // LRU_GROUP, LRU_GROUP, LRU_GROUP)

    return jnp.concatenate([bd(w_a), bd(w_x)], axis=-1).astype(BF16)


def kernel(x_prompt, x_sample, state_shortconv, state_lru_conv, state_lru_h, cache_mem_k, cache_mem_v,
           mem_prompt, norm_mix, w_in, w_short_conv, w_lru_conv, b_lru_conv, w_lru_a, b_lru_a, w_lru_x,
           b_lru_x, lru_lambda, norm_conv_out, norm_lru_out, w_mix_out, norm_xattn, norm_mem, w_q, w_k,
           w_v, w_o, norm_ffn, w_ff_gate, w_ff_up, w_ff_down, w_router, w_moe_gate, w_moe_up, w_moe_down,
           norm_final):
    depth = w_in.shape[0]
    nb, t, d = x_prompt.shape
    ns = x_sample.shape[0]
    n_mem = mem_prompt.shape[1]
    rows = lambda a: a.reshape(a.shape[0], 1, -1)
    stacks = dict(
        gmix=rows(norm_mix), win=w_in.astype(BF16), wsc=w_short_conv, wlc=w_lru_conv, blc=rows(b_lru_conv),
        wg=jax.vmap(_block_diag_gates)(w_lru_a, w_lru_x), ba=rows(b_lru_a), bx=rows(b_lru_x),
        lam=rows(lru_lambda), gco=rows(norm_conv_out), glo=rows(norm_lru_out), wout=w_mix_out.astype(BF16),
        gx=rows(norm_xattn), wq=w_q.astype(BF16), wo=w_o.astype(BF16), gffn=rows(norm_ffn))
    g_final = norm_final.reshape(1, -1)
    ff_dense = (w_ff_gate.astype(BF16), w_ff_up.astype(BF16), w_ff_down.astype(BF16))
    ff_moe = (w_moe_gate.astype(BF16), w_moe_up.astype(BF16), w_moe_down.astype(BF16))

    layers = []
    for l in range(depth):
        lp = {name: (stack, l) for name, stack in stacks.items()}
        lp['gfin'] = g_final if l == depth - 1 else None
        j = l // 2
        if l % 2 == 0:
            lp['ffw'] = tuple((w, j) for w in ff_dense)
            lp['wr'] = None
        else:
            lp['ffw'] = tuple(w[j] for w in ff_moe)
            lp['wr'] = jnp.pad(w_router[j], ((0, 0), (0, LANES - N_EXPERTS)))
            lp['wr_raw'] = w_router[j]
        layers.append(lp)

    x = x_prompt
    p_mem_k, p_mem_v, kb, vb = _kv_proj(mem_prompt, norm_mem.reshape(depth, 1, d), w_k.astype(BF16),
                                        w_v.astype(BF16))
    pc, plc, ph = [], [], []
    for l, lp in enumerate(layers):
        x, tail_p, tail_x, h_last = _mixer_prompt(x, lp)
        x = _xattn_prompt(x, lp['gx'], lp['wq'], kb, vb, lp['wo'], l)
        if lp['wr'] is None:
            x = _ffn_dense(x.reshape(nb * t, d), lp['gffn'], *lp['ffw'], tm=FFN_ROWS, g_final=lp['gfin'])
        else:
            x = _moe_routed(x.reshape(nb * t, d), lp['gffn'], lp['wr_raw'], *lp['ffw'], lp['gfin'])
        x = x.reshape(nb, t, d)
        tail_p = jnp.swapaxes(tail_p.reshape(MAX_CONV_HISTORY, nb, D_CONV), 0, 1)
        tail_x = jnp.swapaxes(tail_x.reshape(MAX_CONV_HISTORY, nb, D_LRU), 0, 1)
        pc.append(tail_p[:, MAX_CONV_HISTORY - (w_short_conv.shape[1] - 1):])
        plc.append(tail_x[:, MAX_CONV_HISTORY - (w_lru_conv.shape[1] - 1):])
        ph.append(h_last)
    y_prompt = x

    x = x_sample.reshape(ns, d)
    k8 = _to_head_rows(cache_mem_k.reshape(depth, ns, n_mem, d))
    v8 = _to_head_rows(cache_mem_v.reshape(depth, ns, n_mem, d))
    sc, slc, sh = [], [], []
    for l, lp in enumerate(layers):
        x, q, p_new, xb_new, h_new = _mixer_sample(x, state_shortconv[l], state_lru_conv[l], state_lru_h[l], lp)
        attn = (_xattn_sample(q, k8, v8, l), lp['wo'])
        if lp['wr'] is None:
            x = _ffn_dense(x, lp['gffn'], *lp['ffw'], tm=ns, g_final=lp['gfin'], attn=attn)
        else:
            x = _moe_all_experts(x, lp['gffn'], lp['wr'], *lp['ffw'], tm=ns, g_final=lp['gfin'], attn=attn)
        sc.append(jnp.stack([state_shortconv[l][:, 1], p_new], axis=1))
        slc.append(jnp.stack([state_lru_conv[l][:, 1], state_lru_conv[l][:, 2], xb_new], axis=1))
        sh.append(h_new)
    y_sample = x.reshape(ns, 1, d)

    return (y_prompt, y_sample, jnp.stack(pc), jnp.stack(plc), jnp.stack(ph), p_mem_k, p_mem_v,
            jnp.stack(sc), jnp.stack(slc), jnp.stack(sh))
```

```python
import functools

import jax
import jax.numpy as jnp
from jax import lax
from jax.experimental import pallas as pl
from jax.experimental.pallas import tpu as pltpu

F32 = jnp.float32
BF16 = jnp.bfloat16

EPS = 1e-6
LRU_C = 8.0
D_CONV = 512
D_LRU = 512
N_LRU_HEADS = 8
LRU_HEAD_DIM = 64
LRU_GROUP = 256
N_MEM_HEADS = 4
MEM_HEAD_DIM = 256
N_EXPERTS = 8
SUBLANES = 8
LANES = 128
MXU_COLS = 256
VMEM_LIMIT_BYTES = 56 * 1024 * 1024

MIXER_TIME_CHUNK = 128
XATTN_ROWS = 1024
FFN_ROWS = 512
KV_PROJ_SEQS = 2
SAMPLE_ATTN_SEQS = 8


def _cparams(*sem):
    return pltpu.CompilerParams(dimension_semantics=sem, vmem_limit_bytes=VMEM_LIMIT_BYTES)


def _rms(x, g):
    return x * lax.rsqrt(jnp.mean(x * x, axis=-1, keepdims=True) + EPS) * g


def _dot(a, b):
    return jnp.dot(a, b, preferred_element_type=F32)


def _gate_up(x, wg_ref, wu_ref):
    f = wg_ref.shape[1]
    main = f // MXU_COLS * MXU_COLS
    if main == f or 2 * (f - main) != MXU_COLS:
        return _dot(x, wg_ref[...]), _dot(x, wu_ref[...])
    rest = _dot(x, jnp.concatenate([wg_ref[:, main:], wu_ref[:, main:]], axis=1))
    gt = jnp.concatenate([_dot(x, wg_ref[:, :main]), rest[:, :f - main]], axis=1)
    up = jnp.concatenate([_dot(x, wu_ref[:, :main]), rest[:, f - main:]], axis=1)
    return gt, up


def _lru_gates(xc, wg_ref, ba, bx, lam):
    neg_lam = -lam
    softplus = jnp.maximum(neg_lam, 0.0) + jnp.log1p(jnp.exp(-jnp.abs(neg_lam)))
    a_parts, u_parts = [], []
    for g in range(D_LRU // LRU_GROUP):
        sl = slice(g * LRU_GROUP, (g + 1) * LRU_GROUP)
        xg = xc[:, sl]
        gates = _dot(xg.astype(BF16), wg_ref[g])
        r = jax.nn.sigmoid(gates[:, :LRU_GROUP] + ba[:, sl])
        i = jax.nn.sigmoid(gates[:, LRU_GROUP:] + bx[:, sl])
        log_a = (-LRU_C) * r * softplus[:, sl]
        th = jnp.tanh(log_a)
        one_minus_a2 = (-2.0 * th) / (1.0 - th)
        a_parts.append(jnp.exp(log_a))
        u_parts.append(jnp.sqrt(one_minus_a2) * (i * xg))
    return jnp.concatenate(a_parts, axis=-1), jnp.concatenate(u_parts, axis=-1)


def _mix_out(x, ya, yb, gco, glo, wout_ref):
    y = jnp.concatenate([_rms(ya, gco), _rms(yb, glo)], axis=-1).astype(BF16)
    return x + _dot(y, wout_ref[...])


MAX_CONV_HISTORY = 3


def _conv_taps(ext_ref, w_ref, rows, nb):
    width = w_ref.shape[0]
    hist = MAX_CONV_HISTORY * nb
    acc = ext_ref[hist:hist + rows, :] * w_ref[width - 1:width, :]
    for s in range(1, width):
        acc = acc + ext_ref[hist - s * nb:hist - s * nb + rows, :] * w_ref[width - 1 - s:width - s, :]
    return acc


def _mixer_prompt_kernel(x_hbm, gmix_ref, win_ref, wsc_ref, wlc_ref, blc_ref, wg_ref, ba_ref, bx_ref,
                         lam_ref, gco_ref, glo_ref, wout_ref,
                         xo_hbm, tailp_ref, tailx_ref, hout_ref,
                         xin_ref, xout_ref, in_sem, out_sem, pext_ref, xext_ref, h_ref, a_ref, u_ref):
    _, tt, nb, d = xin_ref.shape
    rows = tt * nb
    hist = MAX_CONV_HISTORY * nb
    i = pl.program_id(0)
    n_steps = pl.num_programs(0)
    slot = i % 2

    def in_copies(step, s):
        return [pltpu.make_async_copy(x_hbm.at[b, pl.ds(step * tt, tt), :], xin_ref.at[s, :, b, :],
                                      in_sem.at[s]) for b in range(nb)]

    def out_copies(step, s):
        return [pltpu.make_async_copy(xout_ref.at[s, :, b, :], xo_hbm.at[b, pl.ds(step * tt, tt), :],
                                      out_sem.at[s]) for b in range(nb)]

    @pl.when(i == 0)
    def _():
        pext_ref[0:hist, :] = jnp.zeros((hist, pext_ref.shape[1]), F32)
        xext_ref[0:hist, :] = jnp.zeros((hist, xext_ref.shape[1]), F32)
        h_ref[...] = jnp.zeros_like(h_ref)
        for cp in in_copies(0, 0):
            cp.start()

    @pl.when(i + 1 < n_steps)
    def _():
        for cp in in_copies(i + 1, 1 - slot):
            cp.start()

    for cp in in_copies(i, slot):
        cp.wait()
    x = xin_ref[slot].reshape(rows, d)
    z = _dot(_rms(x, gmix_ref[...]).astype(BF16), win_ref[...])
    gb = z[:, 0:D_CONV]
    gate = z[:, 3 * D_CONV + D_LRU:]
    pext_ref[hist:hist + rows, :] = z[:, D_CONV:2 * D_CONV] * z[:, 2 * D_CONV:3 * D_CONV]
    xext_ref[hist:hist + rows, :] = z[:, 3 * D_CONV:3 * D_CONV + D_LRU]

    ya = gb * _conv_taps(pext_ref, wsc_ref, rows, nb)
    xc = _conv_taps(xext_ref, wlc_ref, rows, nb) + blc_ref[...]
    for ext_ref, tail_ref in ((pext_ref, tailp_ref), (xext_ref, tailx_ref)):
        tail = ext_ref[rows:rows + hist, :]
        ext_ref[0:hist, :] = tail
        tail_ref[...] = tail

    a, u = _lru_gates(xc, wg_ref, ba_ref[...], bx_ref[...], lam_ref[...])
    a_ref[...] = a
    u_ref[...] = u

    def step(t, h):
        grp = pl.ds(pl.multiple_of(t * nb, nb), nb)
        h = a_ref[grp, :] * h + u_ref[grp, :]
        u_ref[grp, :] = h
        return h

    h_last = lax.fori_loop(0, tt, step, h_ref[...], unroll=8)
    h_ref[...] = h_last
    hout_ref[...] = h_last

    yb = u_ref[...] * jax.nn.gelu(gate)
    out = _mix_out(x, ya, yb, gco_ref[...], glo_ref[...], wout_ref)

    @pl.when(i >= 2)
    def _():
        for cp in out_copies(i - 2, slot):
            cp.wait()

    xout_ref[slot] = out.reshape(tt, nb, d)
    for cp in out_copies(i, slot):
        cp.start()

    @pl.when(i == n_steps - 1)
    def _():
        for cp in out_copies(i, slot):
            cp.wait()

        @pl.when(i >= 1)
        def _():
            for cp in out_copies(i - 1, 1 - slot):
                cp.wait()


def _const_spec(shape):
    return pl.BlockSpec(shape, lambda *_: (0,) * len(shape))


def _param_array(p):
    return p[0] if isinstance(p, tuple) else p


def _param_spec(p, resident=False):
    a, layer = p if isinstance(p, tuple) else (p, None)
    mode = dict(pipeline_mode=pl.Buffered(1)) if resident else {}
    if layer is None:
        return pl.BlockSpec(a.shape, lambda *_: (0,) * a.ndim, **mode)
    return pl.BlockSpec((None,) + a.shape[1:], lambda *_: (layer,) + (0,) * (a.ndim - 1), **mode)


def _mixer_prompt(x, p, tt=MIXER_TIME_CHUNK):
    nb, t, d = x.shape
    small = [p['gmix'], p['win'], p['wsc'], p['wlc'], p['blc'], p['wg'], p['ba'], p['bx'], p['lam'],
             p['gco'], p['glo'], p['wout']]
    rows = nb * tt
    hist = MAX_CONV_HISTORY * nb
    return pl.pallas_call(
        _mixer_prompt_kernel,
        grid=(t // tt,),
        in_specs=[pl.BlockSpec(memory_space=pl.ANY)] + [_param_spec(a, resident=True) for a in small],
        out_specs=[pl.BlockSpec(memory_space=pl.ANY),
                   _const_spec((hist, D_CONV)), _const_spec((hist, D_LRU)), _const_spec((nb, D_LRU))],
        out_shape=[jax.ShapeDtypeStruct((nb, t, d), F32),
                   jax.ShapeDtypeStruct((hist, D_CONV), F32),
                   jax.ShapeDtypeStruct((hist, D_LRU), F32),
                   jax.ShapeDtypeStruct((nb, D_LRU), F32)],
        scratch_shapes=[pltpu.VMEM((2, tt, nb, d), F32), pltpu.VMEM((2, tt, nb, d), F32),
                        pltpu.SemaphoreType.DMA((2,)), pltpu.SemaphoreType.DMA((2,)),
                        pltpu.VMEM((hist + rows, D_CONV), F32), pltpu.VMEM((hist + rows, D_LRU), F32),
                        pltpu.VMEM((nb, D_LRU), F32),
                        pltpu.VMEM((rows, D_LRU), F32), pltpu.VMEM((rows, D_LRU), F32)],
        compiler_params=_cparams("arbitrary"),
        name="mixer_prompt",
    )(x, *map(_param_array, small))


def _mixer_sample_kernel(x_ref, sc0_ref, sc1_ref, lc0_ref, lc1_ref, lc2_ref, h0_ref,
                         gmix_ref, win_ref, wsc_ref, wlc_ref, blc_ref, wg_ref, ba_ref, bx_ref,
                         lam_ref, gco_ref, glo_ref, wout_ref, gx_ref, wq_ref,
                         xo_ref, q_ref, pnew_ref, xbnew_ref, hout_ref):
    x = x_ref[...]
    z = _dot(_rms(x, gmix_ref[...]).astype(BF16), win_ref[...])
    gb = z[:, 0:D_CONV]
    p = z[:, D_CONV:2 * D_CONV] * z[:, 2 * D_CONV:3 * D_CONV]
    xb = z[:, 3 * D_CONV:3 * D_CONV + D_LRU]
    gate = z[:, 3 * D_CONV + D_LRU:]

    ya = gb * (sc0_ref[...] * wsc_ref[0:1, :] + sc1_ref[...] * wsc_ref[1:2, :] + p * wsc_ref[2:3, :])
    xc = (lc0_ref[...] * wlc_ref[0:1, :] + lc1_ref[...] * wlc_ref[1:2, :] + lc2_ref[...] * wlc_ref[2:3, :]
          + xb * wlc_ref[3:4, :]) + blc_ref[...]
    a, u = _lru_gates(xc, wg_ref, ba_ref[...], bx_ref[...], lam_ref[...])
    h = a * h0_ref[...] + u
    yb = h * jax.nn.gelu(gate)
    x1 = _mix_out(x, ya, yb, gco_ref[...], glo_ref[...], wout_ref)
    xo_ref[...] = x1
    q_ref[...] = _dot(_rms(x1, gx_ref[...]).astype(BF16), wq_ref[...])
    pnew_ref[...] = p
    xbnew_ref[...] = xb
    hout_ref[...] = h


def _mixer_sample(x, sc, lc, h0, p):
    nb, d = x.shape
    args = [x, sc[:, 0], sc[:, 1], lc[:, 0], lc[:, 1], lc[:, 2], h0,
            p['gmix'], p['win'], p['wsc'], p['wlc'], p['blc'], p['wg'], p['ba'], p['bx'], p['lam'],
            p['gco'], p['glo'], p['wout'], p['gx'], p['wq']]
    return pl.pallas_call(
        _mixer_sample_kernel,
        grid=(1,),
        in_specs=[_param_spec(a) for a in args],
        out_specs=[_const_spec((nb, d)), _const_spec((nb, d)), _const_spec((nb, D_CONV)),
                   _const_spec((nb, D_LRU)), _const_spec((nb, D_LRU))],
        out_shape=[jax.ShapeDtypeStruct((nb, d), F32), jax.ShapeDtypeStruct((nb, d), F32),
                   jax.ShapeDtypeStruct((nb, D_CONV), F32), jax.ShapeDtypeStruct((nb, D_LRU), F32),
                   jax.ShapeDtypeStruct((nb, D_LRU), F32)],
        compiler_params=_cparams("arbitrary"),
        name="mixer_sample",
    )(*map(_param_array, args))


def _xattn_prompt_kernel(x_ref, g_ref, wq_ref, k_ref, v_ref, wo_ref, o_ref):
    x = x_ref[...]
    q = _dot(_rms(x, g_ref[...]).astype(BF16), wq_ref[...])
    heads = []
    for h in range(N_MEM_HEADS):
        sl = slice(h * MEM_HEAD_DIM, (h + 1) * MEM_HEAD_DIM)
        kh = k_ref[:, sl]
        vh = v_ref[:, sl]
        s = lax.dot_general(q[:, sl].astype(BF16), kh, (((1,), (1,)), ((), ())),
                            preferred_element_type=F32) * (MEM_HEAD_DIM ** -0.5)
        e = jnp.exp(s - jnp.max(s, axis=-1, keepdims=True))
        pr = e / jnp.sum(e, axis=-1, keepdims=True)
        heads.append(_dot(pr.astype(BF16), vh))
    o = jnp.concatenate(heads, axis=-1).astype(BF16)
    o_ref[...] = x + _dot(o, wo_ref[...])


def _xattn_prompt(x, g, wq, k, v, wo, layer, tq=XATTN_ROWS):
    nb, t, d = x.shape
    n_mem = k.shape[2]
    kv_spec = pl.BlockSpec((None, None, n_mem, d), lambda b, i: (layer, b, 0, 0))
    return pl.pallas_call(
        _xattn_prompt_kernel,
        grid=(nb, t // tq),
        in_specs=[pl.BlockSpec((None, tq, d), lambda b, i: (b, i, 0)), _param_spec(g),
                  _param_spec(wq, resident=True), kv_spec, kv_spec, _param_spec(wo, resident=True)],
        out_specs=pl.BlockSpec((None, tq, d), lambda b, i: (b, i, 0)),
        out_shape=jax.ShapeDtypeStruct((nb, t, d), F32),
        compiler_params=_cparams("parallel", "parallel"),
        name="xattn_prompt",
    )(x, _param_array(g), _param_array(wq), k, v, _param_array(wo))


def _kv_proj_kernel(mem_ref, g_ref, wk_ref, wv_ref, k5_ref, v5_ref, kb_ref, vb_ref):
    bb, n_mem, d = mem_ref.shape
    m = _rms(mem_ref[...].reshape(bb * n_mem, d), g_ref[...]).astype(BF16)
    for w_ref, o5_ref, ob_ref in ((wk_ref, k5_ref, kb_ref), (wv_ref, v5_ref, vb_ref)):
        y = _dot(m, w_ref[...]).reshape(bb, n_mem, d)
        ob_ref[...] = y.astype(BF16)
        for h in range(N_MEM_HEADS):
            o5_ref[:, :, h, :] = y[:, :, h * MEM_HEAD_DIM:(h + 1) * MEM_HEAD_DIM]


def _kv_proj(mem, gmem, wk, wv, bb=KV_PROJ_SEQS):
    nb, n_mem, d = mem.shape
    depth = wk.shape[0]
    w_spec = pl.BlockSpec((None, d, d), lambda l, i: (l, 0, 0))
    o5_spec = pl.BlockSpec((None, bb, n_mem, N_MEM_HEADS, MEM_HEAD_DIM), lambda l, i: (l, i, 0, 0, 0))
    ob_spec = pl.BlockSpec((None, bb, n_mem, d), lambda l, i: (l, i, 0, 0))
    o5_shape = jax.ShapeDtypeStruct((depth, nb, n_mem, N_MEM_HEADS, MEM_HEAD_DIM), F32)
    ob_shape = jax.ShapeDtypeStruct((depth, nb, n_mem, d), BF16)
    return pl.pallas_call(
        _kv_proj_kernel,
        grid=(depth, nb // bb),
        in_specs=[pl.BlockSpec((bb, n_mem, d), lambda l, i: (i, 0, 0)),
                  pl.BlockSpec((None, 1, d), lambda l, i: (l, 0, 0)), w_spec, w_spec],
        out_specs=[o5_spec, o5_spec, ob_spec, ob_spec],
        out_shape=[o5_shape, o5_shape, ob_shape, ob_shape],
        compiler_params=_cparams("parallel", "parallel"),
        name="kv_proj",
    )(mem, gmem, wk, wv)


LANE_TILES_PER_HEAD = MEM_HEAD_DIM // LANES
HEAD_ROWS = N_MEM_HEADS * LANE_TILES_PER_HEAD


def _to_head_rows(a):
    lead = a.shape[:-1]
    n = len(lead)
    a = a.reshape(*lead, N_MEM_HEADS, LANE_TILES_PER_HEAD, LANES)
    return jnp.swapaxes(a, n, n + 1).reshape(*lead, HEAD_ROWS, LANES)


def _from_head_rows(a):
    lead = a.shape[:-2]
    n = len(lead)
    a = a.reshape(*lead, LANE_TILES_PER_HEAD, N_MEM_HEADS, LANES)
    return jnp.swapaxes(a, n, n + 1).reshape(*lead, N_MEM_HEADS * MEM_HEAD_DIM)


def _xattn_sample_kernel(q_ref, k_ref, v_ref, o_ref):
    n_mem = k_ref.shape[1]
    ones = jnp.ones((LANES, LANES), BF16)
    for b in range(q_ref.shape[0]):
        prod = (k_ref[b] * q_ref[b][None]).reshape(n_mem * HEAD_ROWS, LANES).astype(BF16)
        part = _dot(prod, ones).reshape(n_mem, HEAD_ROWS, LANES)
        s = (part + pltpu.roll(part, N_MEM_HEADS, axis=1)) * (MEM_HEAD_DIM ** -0.5)
        e = jnp.exp(s - jnp.max(s, axis=0, keepdims=True))
        pr = e / jnp.sum(e, axis=0, keepdims=True)
        o_ref[b] = jnp.sum(pr * v_ref[b], axis=0)


def _xattn_sample(q, k8, v8, layer, bb=SAMPLE_ATTN_SEQS):
    nb, d = q.shape
    n_mem = k8.shape[2]
    kv_spec = pl.BlockSpec((None, bb, n_mem, HEAD_ROWS, LANES), lambda i: (layer, i, 0, 0, 0))
    out = pl.pallas_call(
        _xattn_sample_kernel,
        grid=(nb // bb,),
        in_specs=[pl.BlockSpec((bb, HEAD_ROWS, LANES), lambda i: (i, 0, 0)), kv_spec, kv_spec],
        out_specs=pl.BlockSpec((bb, HEAD_ROWS, LANES), lambda i: (i, 0, 0)),
        out_shape=jax.ShapeDtypeStruct((nb, HEAD_ROWS, LANES), F32),
        compiler_params=_cparams("parallel"),
        name="xattn_sample",
    )(_to_head_rows(q), k8, v8)
    return _from_head_rows(out)


def _split_bf16(a):
    hi = a.astype(BF16)
    return hi, (a - hi.astype(F32)).astype(BF16)


def _top2(logits):
    n = logits.shape[-1]
    lane = lax.broadcasted_iota(jnp.int32, logits.shape, 1)
    m1 = jnp.max(logits, axis=-1, keepdims=True)
    i1 = jnp.min(jnp.where(logits == m1, lane, n), axis=-1, keepdims=True)
    rest = jnp.where(lane == i1, -jnp.inf, logits)
    m2 = jnp.max(rest, axis=-1, keepdims=True)
    i2 = jnp.min(jnp.where(rest == m2, lane, n), axis=-1, keepdims=True)
    e2 = jnp.exp(m2 - m1)
    denom = 1.0 + e2
    return i1, i2, 1.0 / denom, e2 / denom


def _add_attn_proj(x, rest, proj):
    if not proj:
        return x, rest
    a_ref, wo_ref = rest[:2]
    return x + _dot(a_ref[...].astype(BF16), wo_ref[...]), rest[2:]


def _ffn_dense_kernel(x_ref, g_ref, *rest, final, proj):
    x, rest = _add_attn_proj(x_ref[...], rest, proj)
    if final:
        gfin_ref, rest = rest[0], rest[1:]
    wg_ref, wu_ref, wd_ref, o_ref = rest
    h = _rms(x, g_ref[...]).astype(BF16)
    act = (jax.nn.silu(_dot(h, wg_ref[...])) * _dot(h, wu_ref[...])).astype(BF16)
    out = x + _dot(act, wd_ref[...])
    o_ref[...] = _rms(out, gfin_ref[...]) if final else out


def _ffn_dense(x, g, wg, wu, wd, tm, g_final=None, attn=None):
    r, d = x.shape
    final, proj = g_final is not None, attn is not None
    args, specs = [x, g], [pl.BlockSpec((tm, d), lambda i: (i, 0)), _param_spec(g)]
    if proj:
        args += list(attn)
        specs += [pl.BlockSpec((tm, attn[0].shape[1]), lambda i: (i, 0)), _param_spec(attn[1], resident=True)]
    if final:
        args.append(g_final)
        specs.append(_param_spec(g_final))
    args += [wg, wu, wd]
    specs += [_param_spec(w, resident=True) for w in (wg, wu, wd)]
    return pl.pallas_call(
        functools.partial(_ffn_dense_kernel, final=final, proj=proj),
        grid=(r // tm,),
        in_specs=specs,
        out_specs=pl.BlockSpec((tm, d), lambda i: (i, 0)),
        out_shape=jax.ShapeDtypeStruct((r, d), F32),
        compiler_params=_cparams("parallel"),
        name="ffn_dense",
    )(*map(_param_array, args))


def _moe_all_experts_kernel(x_ref, g_ref, wr_ref, *rest, final, proj):
    if proj:
        attn_refs, rest = rest[:2], rest[2:]
    if final:
        gfin_ref, rest = rest[0], rest[1:]
    wg_ref, wu_ref, wd_ref, o_ref, h_ref, acc_ref, i1_ref, i2_ref, g1_ref, g2_ref = rest
    e = pl.program_id(1)

    @pl.when(e == 0)
    def _():
        x = x_ref[...]
        if proj:
            x, _ = _add_attn_proj(x, attn_refs, proj)
        h = _rms(x, g_ref[...])
        h_ref[...] = h.astype(BF16)
        acc_ref[...] = x
        h_hi, h_lo = _split_bf16(h)
        w_hi, w_lo = _split_bf16(wr_ref[...])
        logits = _dot(h_hi, w_hi) + (_dot(h_hi, w_lo) + _dot(h_lo, w_hi))
        lane = lax.broadcasted_iota(jnp.int32, logits.shape, 1)
        logits = jnp.where(lane < N_EXPERTS, logits, -jnp.inf)
        i1_ref[...], i2_ref[...], g1_ref[...], g2_ref[...] = _top2(logits)

    gt, up = _gate_up(h_ref[...], wg_ref, wu_ref)
    y = _dot((jax.nn.silu(gt) * up).astype(BF16), wd_ref[...])
    c = jnp.where(i1_ref[...] == e, g1_ref[...], 0.0) + jnp.where(i2_ref[...] == e, g2_ref[...], 0.0)
    acc_ref[...] += c * y

    @pl.when(e == pl.num_programs(1) - 1)
    def _():
        out = acc_ref[...]
        o_ref[...] = _rms(out, gfin_ref[...]) if final else out


def _moe_all_experts(x, g, w_router, wg, wu, wd, tm, g_final=None, attn=None):
    r, d = x.shape
    n_e, f, _ = wd.shape
    final, proj = g_final is not None, attn is not None
    args = [x, g, w_router]
    specs = [pl.BlockSpec((tm, d), lambda i, e: (i, 0)), _param_spec(g), _param_spec(w_router)]
    if proj:
        args += list(attn)
        specs += [pl.BlockSpec((tm, attn[0].shape[1]), lambda i, e: (i, 0)), _param_spec(attn[1])]
    if final:
        args.append(g_final)
        specs.append(_param_spec(g_final))
    args += [wg, wu, wd]
    specs += [pl.BlockSpec((None, d, f), lambda i, e: (e, 0, 0)),
              pl.BlockSpec((None, d, f), lambda i, e: (e, 0, 0)),
              pl.BlockSpec((None, f, d), lambda i, e: (e, 0, 0))]
    return pl.pallas_call(
        functools.partial(_moe_all_experts_kernel, final=final, proj=proj),
        grid=(r // tm, n_e),
        in_specs=specs,
        out_specs=pl.BlockSpec((tm, d), lambda i, e: (i, 0)),
        out_shape=jax.ShapeDtypeStruct((r, d), F32),
        scratch_shapes=[pltpu.VMEM((tm, d), BF16), pltpu.VMEM((tm, d), F32),
                        pltpu.VMEM((tm, 1), jnp.int32), pltpu.VMEM((tm, 1), jnp.int32),
                        pltpu.VMEM((tm, 1), F32), pltpu.VMEM((tm, 1), F32)],
        compiler_params=_cparams("parallel", "arbitrary"),
        name="moe_all_experts",
    )(*map(_param_array, args))


MOE_TILE = 512
TOP_K = 2
ROW_ALIGN = SUBLANES
COMPACT_ROWS = -(-(TOP_K * MOE_TILE + N_EXPERTS * (ROW_ALIGN - 1)) // (2 * SUBLANES)) * (2 * SUBLANES)
SEGMENT_BITS = (MOE_TILE // ROW_ALIGN).bit_length()


def _route_kernel(x_ref, g_ref, wrt_ref, earlier_ref, lp_ref, col_ref, cnt_ref):
    tk = x_ref.shape[0]
    n_e = wrt_ref.shape[0]
    h = _rms(x_ref[...], g_ref[...])
    h_hi, h_lo = _split_bf16(h)
    w_hi, w_lo = _split_bf16(wrt_ref[...])
    nt = (((1,), (1,)), ((), ()))
    dg = lambda a, b: lax.dot_general(a, b, nt, preferred_element_type=F32)
    logits = dg(w_hi, h_hi) + (dg(w_hi, h_lo) + dg(w_lo, h_hi))
    sub = lax.broadcasted_iota(jnp.int32, logits.shape, 0)
    m1 = jnp.max(logits, axis=0, keepdims=True)
    i1 = jnp.min(jnp.where(logits == m1, sub, n_e), axis=0, keepdims=True)
    rest = jnp.where(sub == i1, -jnp.inf, logits)
    m2 = jnp.max(rest, axis=0, keepdims=True)
    i2 = jnp.min(jnp.where(rest == m2, sub, n_e), axis=0, keepdims=True)
    e2 = jnp.exp(m2 - m1)
    denom = 1.0 + e2

    pick1 = jnp.where(sub == i1, 1.0, 0.0)
    pick2 = jnp.where(sub == i2, 1.0, 0.0)
    cnt = pick1 + pick2
    rank = _dot(cnt.astype(BF16), earlier_ref[...])
    n = jnp.sum(cnt, axis=1, keepdims=True)
    n_pad = jnp.floor((n + (ROW_ALIGN - 1)) * (1.0 / ROW_ALIGN)) * ROW_ALIGN
    base, acc = [], jnp.zeros((1, 1), F32)
    for e in range(n_e):
        base.append(acc)
        acc = acc + n_pad[e:e + 1, :]
    pos = rank + jnp.concatenate(base, axis=0)
    lp1 = jnp.sum(pick1 * pos, axis=0, keepdims=True)
    lp2 = jnp.sum(pick2 * pos, axis=0, keepdims=True)
    lp_ref[0:1, :] = lp1.astype(jnp.int32)
    lp_ref[1:2, :] = lp2.astype(jnp.int32)
    info = jnp.concatenate([lp1, lp2, 1.0 / denom, e2 / denom, jnp.zeros((SUBLANES - 4, tk), F32)], axis=0)
    col_ref[...] = info.T
    cnt_ref[...] = jnp.broadcast_to(n, cnt_ref.shape)


def _route(x, g, wrt):
    t, d = x.shape
    n_e = wrt.shape[0]
    n_tiles = t // MOE_TILE
    idx = jnp.arange(MOE_TILE)
    earlier = (idx[:, None] < idx[None, :]).astype(BF16)
    return pl.pallas_call(
        _route_kernel,
        grid=(n_tiles,),
        in_specs=[pl.BlockSpec((MOE_TILE, d), lambda j: (j, 0)), _param_spec(g), _const_spec(wrt.shape),
                  _const_spec(earlier.shape)],
        out_specs=[pl.BlockSpec((None, TOP_K, MOE_TILE), lambda j: (j, 0, 0)),
                   pl.BlockSpec((MOE_TILE, SUBLANES), lambda j: (j, 0)),
                   pl.BlockSpec((None, n_e, LANES), lambda j: (j, 0, 0))],
        out_shape=[jax.ShapeDtypeStruct((n_tiles, TOP_K, MOE_TILE), jnp.int32),
                   jax.ShapeDtypeStruct((t, SUBLANES), F32),
                   jax.ShapeDtypeStruct((n_tiles, n_e, LANES), F32)],
        compiler_params=_cparams("parallel"),
        name="moe_route",
    )(x, _param_array(g), wrt, earlier)


def _segment_copies(src_ref, src_row, dst_ref, dst_row, groups, sem):
    out = []
    for bit in range(SEGMENT_BITS):
        rows = ROW_ALIGN << bit
        before = ((groups >> (bit + 1)) << (bit + 1)) * ROW_ALIGN
        cp = pltpu.make_async_copy(
            src_ref.at[pl.ds(pl.multiple_of(src_row + before, ROW_ALIGN), rows), :],
            dst_ref.at[pl.ds(pl.multiple_of(dst_row + before, ROW_ALIGN), rows), :], sem)
        out.append((((groups >> bit) & 1) == 1, cp))
    return out


def _start_all(copies):
    for pred, cp in copies:
        pl.when(pred)(cp.start)


def _wait_all(copies):
    for pred, cp in copies:
        pl.when(pred)(cp.wait)


def _dispatch_kernel(grp_ref, off_ref, dst_ref, pad_ref, x_ref, g_ref, lp_ref, xs_ref, comp_ref, sem):
    j = pl.program_id(0)
    n_steps = pl.num_programs(0)
    n_e = grp_ref.shape[1]
    slot = j % 2

    def copies(step, s):
        out = []
        for e in range(n_e):
            out += _segment_copies(comp_ref.at[s], off_ref[step, e], xs_ref, dst_ref[step, e],
                                   grp_ref[step, e], sem.at[s])
        return out

    h = _rms(x_ref[...], g_ref[...]).astype(BF16)
    row = lax.broadcasted_iota(jnp.int32, (COMPACT_ROWS, h.shape[0]), 0)
    onehot = jnp.where(row == lp_ref[0:1, :], 1.0, jnp.where(row == lp_ref[1:2, :], 1.0, 0.0)).astype(BF16)
    compact = _dot(onehot, h)

    @pl.when(j >= 2)
    def _():
        _wait_all(copies(j - 2, slot))

    comp_ref[slot] = compact
    _start_all(copies(j, slot))

    @pl.when(j == n_steps - 1)
    def _():
        _wait_all(copies(j, slot))

        @pl.when(j >= 1)
        def _():
            _wait_all(copies(j - 1, 1 - slot))

        comp_ref[slot, 0:MOE_TILE, :] = jnp.zeros((MOE_TILE, comp_ref.shape[2]), F32)
        tail = []
        for e in range(n_e):
            tail += _segment_copies(comp_ref.at[slot], 0, xs_ref, pad_ref[0, e], pad_ref[1, e], sem.at[slot])
        tile_groups = MOE_TILE // ROW_ALIGN
        groups_left = (xs_ref.shape[0] - pad_ref[2, 0]) // ROW_ALIGN
        min_total = TOP_K * x_ref.shape[0] * grp_ref.shape[0]
        for c in range((xs_ref.shape[0] - min_total) // MOE_TILE):
            tail += _segment_copies(comp_ref.at[slot], 0, xs_ref, pad_ref[2, 0] + c * MOE_TILE,
                                    jnp.clip(groups_left - c * tile_groups, 0, tile_groups), sem.at[slot])
        _start_all(tail)
        _wait_all(tail)


def _dispatch(x, g, lp, groups, off, dst, pad, n_rows):
    t, d = x.shape
    n_tiles = t // MOE_TILE
    grid_spec = pltpu.PrefetchScalarGridSpec(
        num_scalar_prefetch=4,
        grid=(n_tiles,),
        in_specs=[pl.BlockSpec((MOE_TILE, d), lambda j, *_: (j, 0)),
                  _param_spec(g),
                  pl.BlockSpec((None, TOP_K, MOE_TILE), lambda j, *_: (j, 0, 0))],
        out_specs=pl.BlockSpec(memory_space=pl.ANY),
        scratch_shapes=[pltpu.VMEM((2, COMPACT_ROWS, d), F32), pltpu.SemaphoreType.DMA((2,))])
    return pl.pallas_call(
        _dispatch_kernel,
        grid_spec=grid_spec,
        out_shape=jax.ShapeDtypeStruct((n_rows, d), F32),
        compiler_params=_cparams("arbitrary"),
        name="moe_dispatch",
    )(groups, off, dst, pad, x, _param_array(g), lp)


STEP_FFN, STEP_ZERO = 1, 2


def _grouped_ffn_kernel(grp_ref, mode_ref, xtile_ref, xs_ref, wg_ref, wu_ref, wd_ref, ys_ref):
    w = pl.program_id(0)

    @pl.when(mode_ref[w] == STEP_ZERO)
    def _():
        ys_ref[...] = jnp.zeros_like(ys_ref)

    @pl.when(mode_ref[w] == STEP_FFN)
    def _():
        gt, up = _gate_up(xs_ref[...].astype(BF16), wg_ref, wu_ref)
        ys_ref[...] = _dot((jax.nn.silu(gt) * up).astype(BF16), wd_ref[...])


def _grouped_ffn(xs, wg, wu, wd, grp, mode, xtile):
    n_e, f, d = wd.shape
    grid_spec = pltpu.PrefetchScalarGridSpec(
        num_scalar_prefetch=3,
        grid=(xs.shape[0] // MOE_TILE,),
        in_specs=[pl.BlockSpec((MOE_TILE, d), lambda w, grp, mode, xtile: (xtile[w], 0)),
                  pl.BlockSpec((None, d, f), lambda w, grp, *_: (grp[w], 0, 0)),
                  pl.BlockSpec((None, d, f), lambda w, grp, *_: (grp[w], 0, 0)),
                  pl.BlockSpec((None, f, d), lambda w, grp, *_: (grp[w], 0, 0))],
        out_specs=pl.BlockSpec((MOE_TILE, d), lambda w, *_: (w, 0)))
    return pl.pallas_call(
        _grouped_ffn_kernel,
        grid_spec=grid_spec,
        out_shape=jax.ShapeDtypeStruct(xs.shape, F32),
        compiler_params=_cparams("arbitrary"),
        name="moe_grouped_ffn",
    )(grp, mode, xtile, xs, wg, wu, wd)


def _combine_kernel(grp_ref, off_ref, dst_ref, x_ref, col_ref, *rest, final):
    if final:
        gfin_ref, rest = rest[0], rest[1:]
    ys_ref, o_ref, comp_ref, sem = rest
    j = pl.program_id(0)
    n_steps = pl.num_programs(0)
    n_e = grp_ref.shape[1]
    slot = j % 2

    def copies(step, s):
        out = []
        for e in range(n_e):
            out += _segment_copies(ys_ref, dst_ref[step, e], comp_ref.at[s], off_ref[step, e],
                                   grp_ref[step, e], sem.at[s])
        return out

    @pl.when(j == 0)
    def _():
        comp_ref[...] = jnp.zeros_like(comp_ref)
        _start_all(copies(0, 0))

    @pl.when(j + 1 < n_steps)
    def _():
        _start_all(copies(j + 1, 1 - slot))

    _wait_all(copies(j, slot))
    y = comp_ref[slot].astype(BF16)
    col = col_ref[...]
    row = lax.broadcasted_iota(jnp.int32, (col.shape[0], COMPACT_ROWS), 1)
    weights = jnp.zeros(row.shape, F32)
    for k in range(TOP_K):
        weights = jnp.where(row == col[:, k:k + 1].astype(jnp.int32), col[:, TOP_K + k:TOP_K + k + 1], weights)
    out = x_ref[...] + _dot(weights.astype(BF16), y)
    o_ref[...] = _rms(out, gfin_ref[...]) if final else out


def _combine(x, col, ys, groups, off, dst, g_final):
    t, d = x.shape
    n_tiles = t // MOE_TILE
    final = g_final is not None
    args = [x, col]
    specs = [pl.BlockSpec((MOE_TILE, d), lambda j, *_: (j, 0)),
             pl.BlockSpec((MOE_TILE, SUBLANES), lambda j, *_: (j, 0))]
    if final:
        args.append(g_final)
        specs.append(_param_spec(g_final))
    args.append(ys)
    specs.append(pl.BlockSpec(memory_space=pl.ANY))
    grid_spec = pltpu.PrefetchScalarGridSpec(
        num_scalar_prefetch=3,
        grid=(n_tiles,),
        in_specs=specs,
        out_specs=pl.BlockSpec((MOE_TILE, d), lambda j, *_: (j, 0)),
        scratch_shapes=[pltpu.VMEM((2, COMPACT_ROWS, d), F32), pltpu.SemaphoreType.DMA((2,))])
    return pl.pallas_call(
        functools.partial(_combine_kernel, final=final),
        grid_spec=grid_spec,
        out_shape=jax.ShapeDtypeStruct((t, d), F32),
        compiler_params=_cparams("arbitrary"),
        name="moe_combine",
    )(groups, off, dst, *map(_param_array, args))


def _moe_routed(x, g, w_router, wg, wu, wd, g_final):
    t, _ = x.shape
    n_e = wd.shape[0]
    n_tiles = t // MOE_TILE
    lp, col, cnt = _route(x, g, w_router.T)
    groups = (cnt[:, :, 0].astype(jnp.int32) + (ROW_ALIGN - 1)) // ROW_ALIGN
    n = groups * ROW_ALIGN
    off = jnp.cumsum(n, axis=1) - n
    per_e = jnp.sum(n, axis=0)
    tiles_e = (per_e + (MOE_TILE - 1)) // MOE_TILE
    tile_end = jnp.cumsum(tiles_e)
    starts = (tile_end - tiles_e) * MOE_TILE
    dst = starts[None, :] + jnp.cumsum(n, axis=0) - n
    pad = jnp.stack([starts + per_e, (tiles_e * MOE_TILE - per_e) // ROW_ALIGN,
                     jnp.broadcast_to(tile_end[-1] * MOE_TILE, per_e.shape)])
    max_rows = TOP_K * t + n_tiles * n_e * (ROW_ALIGN - 1)
    n_row_tiles = -(-max_rows // MOE_TILE) + n_e
    xs = _dispatch(x, g, lp, groups, off, dst, pad, n_row_tiles * MOE_TILE)

    w = jnp.arange(n_row_tiles, dtype=jnp.int32)
    used = w < tile_end[-1]
    grp = jnp.minimum(jnp.sum(w[:, None] >= tile_end[None, :], axis=1), n_e - 1).astype(jnp.int32)
    mode = jnp.where(used, STEP_FFN, STEP_ZERO).astype(jnp.int32)
    xtile = jnp.minimum(w, tile_end[-1] - 1)
    ys = _grouped_ffn(xs, wg, wu, wd, grp, mode, xtile)
    return _combine(x, col, ys, groups, off, dst, g_final)


def _block_diag_gates(w_a, w_x):
    per_group = LRU_GROUP // LRU_HEAD_DIM
    eye = jnp.eye(per_group, dtype=w_a.dtype)

    def bd(w):
        w = w.reshape(D_LRU // LRU_GROUP, per_group, LRU_HEAD_DIM, LRU_HEAD_DIM)
        return jnp.einsum('ghij,hk->ghikj', w, eye).reshape(D_LRU // LRU_GROUP, LRU_GROUP, LRU_GROUP)

    return jnp.concatenate([bd(w_a), bd(w_x)], axis=-1).astype(BF16)


def kernel(x_prompt, x_sample, state_shortconv, state_lru_conv, state_lru_h, cache_mem_k, cache_mem_v,
           mem_prompt, norm_mix, w_in, w_short_conv, w_lru_conv, b_lru_conv, w_lru_a, b_lru_a, w_lru_x,
           b_lru_x, lru_lambda, norm_conv_out, norm_lru_out, w_mix_out, norm_xattn, norm_mem, w_q, w_k,
           w_v, w_o, norm_ffn, w_ff_gate, w_ff_up, w_ff_down, w_router, w_moe_gate, w_moe_up, w_moe_down,
           norm_final):
    depth = w_in.shape[0]
    nb, t, d = x_prompt.shape
    ns = x_sample.shape[0]
    n_mem = mem_prompt.shape[1]
    rows = lambda a: a.reshape(a.shape[0], 1, -1)
    stacks = dict(
        gmix=rows(norm_mix), win=w_in.astype(BF16), wsc=w_short_conv, wlc=w_lru_conv, blc=rows(b_lru_conv),
        wg=jax.vmap(_block_diag_gates)(w_lru_a, w_lru_x), ba=rows(b_lru_a), bx=rows(b_lru_x),
        lam=rows(lru_lambda), gco=rows(norm_conv_out), glo=rows(norm_lru_out), wout=w_mix_out.astype(BF16),
        gx=rows(norm_xattn), wq=w_q.astype(BF16), wo=w_o.astype(BF16), gffn=rows(norm_ffn))
    g_final = norm_final.reshape(1, -1)
    ff_dense = (w_ff_gate.astype(BF16), w_ff_up.astype(BF16), w_ff_down.astype(BF16))
    ff_moe = (w_moe_gate.astype(BF16), w_moe_up.astype(BF16), w_moe_down.astype(BF16))

    layers = []
    for l in range(depth):
        lp = {name: (stack, l) for name, stack in stacks.items()}
        lp['gfin'] = g_final if l == depth - 1 else None
        j = l // 2
        if l % 2 == 0:
            lp['ffw'] = tuple((w, j) for w in ff_dense)
            lp['wr'] = None
        else:
            lp['ffw'] = tuple(w[j] for w in ff_moe)
            lp['wr'] = jnp.pad(w_router[j], ((0, 0), (0, LANES - N_EXPERTS)))
            lp['wr_raw'] = w_router[j]
        layers.append(lp)

    x = x_prompt
    p_mem_k, p_mem_v, kb, vb = _kv_proj(mem_prompt, norm_mem.reshape(depth, 1, d), w_k.astype(BF16),
                                        w_v.astype(BF16))
    pc, plc, ph = [], [], []
    for l, lp in enumerate(layers):
        x, tail_p, tail_x, h_last = _mixer_prompt(x, lp)
        x = _xattn_prompt(x, lp['gx'], lp['wq'], kb, vb, lp['wo'], l)
        if lp['wr'] is None:
            x = _ffn_dense(x.reshape(nb * t, d), lp['gffn'], *lp['ffw'], tm=FFN_ROWS, g_final=lp['gfin'])
        else:
            x = _moe_routed(x.reshape(nb * t, d), lp['gffn'], lp['wr_raw'], *lp['ffw'], lp['gfin'])
        x = x.reshape(nb, t, d)
        tail_p = jnp.swapaxes(tail_p.reshape(MAX_CONV_HISTORY, nb, D_CONV), 0, 1)
        tail_x = jnp.swapaxes(tail_x.reshape(MAX_CONV_HISTORY, nb, D_LRU), 0, 1)
        pc.append(tail_p[:, MAX_CONV_HISTORY - (w_short_conv.shape[1] - 1):])
        plc.append(tail_x[:, MAX_CONV_HISTORY - (w_lru_conv.shape[1] - 1):])
        ph.append(h_last)
    y_prompt = x

    x = x_sample.reshape(ns, d)
    k8 = _to_head_rows(cache_mem_k.reshape(depth, ns, n_mem, d))
    v8 = _to_head_rows(cache_mem_v.reshape(depth, ns, n_mem, d))
    sc, slc, sh = [], [], []
    for l, lp in enumerate(layers):
        x, q, p_new, xb_new, h_new = _mixer_sample(x, state_shortconv[l], state_lru_conv[l], state_lru_h[l], lp)
        attn = (_xattn_sample(q, k8, v8, l), lp['wo'])
        if lp['wr'] is None:
            x = _ffn_dense(x, lp['gffn'], *lp['ffw'], tm=ns, g_final=lp['gfin'], attn=attn)
        else:
            x = _moe_all_experts(x, lp['gffn'], lp['wr'], *lp['ffw'], tm=ns, g_final=lp['gfin'], attn=attn)
        sc.append(jnp.stack([state_shortconv[l][:, 1], p_new], axis=1))
        slc.append(jnp.stack([state_lru_conv[l][:, 1], state_lru_conv[l][:, 2], xb_new], axis=1))
        sh.append(h_new)
    y_sample = x.reshape(ns, 1, d)

    return (y_prompt, y_sample, jnp.stack(pc), jnp.stack(plc), jnp.stack(ph), p_mem_k, p_mem_v,
            jnp.stack(sc), jnp.stack(slc), jnp.stack(sh))
```

```python
import functools

import jax
import jax.numpy as jnp
from jax import lax
from jax.experimental import pallas as pl
from jax.experimental.pallas import tpu as pltpu

F32 = jnp.float32
BF16 = jnp.bfloat16

EPS = 1e-6
LRU_C = 8.0
D_CONV = 512
D_LRU = 512
N_LRU_HEADS = 8
LRU_HEAD_DIM = 64
LRU_GROUP = 256
N_MEM_HEADS = 4
MEM_HEAD_DIM = 256
N_EXPERTS = 8
SUBLANES = 8
LANES = 128
MXU_COLS = 256
VMEM_LIMIT_BYTES = 56 * 1024 * 1024

MIXER_TIME_CHUNK = 128
XATTN_ROWS = 1024
FFN_ROWS = 512
KV_PROJ_SEQS = 2
SAMPLE_ATTN_SEQS = 8


def _cparams(*sem):
    return pltpu.CompilerParams(dimension_semantics=sem, vmem_limit_bytes=VMEM_LIMIT_BYTES)


def _rms(x, g):
    return x * lax.rsqrt(jnp.mean(x * x, axis=-1, keepdims=True) + EPS) * g


def _dot(a, b):
    return jnp.dot(a, b, preferred_element_type=F32)


def _gate_up(x, wg_ref, wu_ref):
    f = wg_ref.shape[1]
    main = f // MXU_COLS * MXU_COLS
    if main == f or 2 * (f - main) != MXU_COLS:
        return _dot(x, wg_ref[...]), _dot(x, wu_ref[...])
    rest = _dot(x, jnp.concatenate([wg_ref[:, main:], wu_ref[:, main:]], axis=1))
    gt = jnp.concatenate([_dot(x, wg_ref[:, :main]), rest[:, :f - main]], axis=1)
    up = jnp.concatenate([_dot(x, wu_ref[:, :main]), rest[:, f - main:]], axis=1)
    return gt, up


def _lru_gates(xc, wg_ref, ba, bx, lam):
    neg_lam = -lam
    softplus = jnp.maximum(neg_lam, 0.0) + jnp.log1p(jnp.exp(-jnp.abs(neg_lam)))
    a_parts, u_parts = [], []
    for g in range(D_LRU // LRU_GROUP):
        sl = slice(g * LRU_GROUP, (g + 1) * LRU_GROUP)
        xg = xc[:, sl]
        gates = _dot(xg.astype(BF16), wg_ref[g])
        r = jax.nn.sigmoid(gates[:, :LRU_GROUP] + ba[:, sl])
        i = jax.nn.sigmoid(gates[:, LRU_GROUP:] + bx[:, sl])
        log_a = (-LRU_C) * r * softplus[:, sl]
        th = jnp.tanh(log_a)
        one_minus_a2 = (-2.0 * th) / (1.0 - th)
        a_parts.append(jnp.exp(log_a))
        u_parts.append(jnp.sqrt(one_minus_a2) * (i * xg))
    return jnp.concatenate(a_parts, axis=-1), jnp.concatenate(u_parts, axis=-1)


def _mix_out(x, ya, yb, gco, glo, wout_ref):
    y = jnp.concatenate([_rms(ya, gco), _rms(yb, glo)], axis=-1).astype(BF16)
    return x + _dot(y, wout_ref[...])


MAX_CONV_HISTORY = 3


def _conv_taps(ext_ref, w_ref, rows, nb):
    width = w_ref.shape[0]
    hist = MAX_CONV_HISTORY * nb
    acc = ext_ref[hist:hist + rows, :] * w_ref[width - 1:width, :]
    for s in range(1, width):
        acc = acc + ext_ref[hist - s * nb:hist - s * nb + rows, :] * w_ref[width - 1 - s:width - s, :]
    return acc


def _mixer_prompt_kernel(x_hbm, gmix_ref, win_ref, wsc_ref, wlc_ref, blc_ref, wg_ref, ba_ref, bx_ref,
                         lam_ref, gco_ref, glo_ref, wout_ref,
                         xo_hbm, tailp_ref, tailx_ref, hout_ref,
                         xin_ref, xout_ref, in_sem, out_sem, pext_ref, xext_ref, h_ref, a_ref, u_ref):
    _, tt, nb, d = xin_ref.shape
    rows = tt * nb
    hist = MAX_CONV_HISTORY * nb
    i = pl.program_id(0)
    n_steps = pl.num_programs(0)
    slot = i % 2

    def in_copies(step, s):
        return [pltpu.make_async_copy(x_hbm.at[b, pl.ds(step * tt, tt), :], xin_ref.at[s, :, b, :],
                                      in_sem.at[s]) for b in range(nb)]

    def out_copies(step, s):
        return [pltpu.make_async_copy(xout_ref.at[s, :, b, :], xo_hbm.at[b, pl.ds(step * tt, tt), :],
                                      out_sem.at[s]) for b in range(nb)]

    @pl.when(i == 0)
    def _():
        pext_ref[0:hist, :] = jnp.zeros((hist, pext_ref.shape[1]), F32)
        xext_ref[0:hist, :] = jnp.zeros((hist, xext_ref.shape[1]), F32)
        h_ref[...] = jnp.zeros_like(h_ref)
        for cp in in_copies(0, 0):
            cp.start()

    @pl.when(i + 1 < n_steps)
    def _():
        for cp in in_copies(i + 1, 1 - slot):
            cp.start()

    for cp in in_copies(i, slot):
        cp.wait()
    x = xin_ref[slot].reshape(rows, d)
    z = _dot(_rms(x, gmix_ref[...]).astype(BF16), win_ref[...])
    gb = z[:, 0:D_CONV]
    gate = z[:, 3 * D_CONV + D_LRU:]
    pext_ref[hist:hist + rows, :] = z[:, D_CONV:2 * D_CONV] * z[:, 2 * D_CONV:3 * D_CONV]
    xext_ref[hist:hist + rows, :] = z[:, 3 * D_CONV:3 * D_CONV + D_LRU]

    ya = gb * _conv_taps(pext_ref, wsc_ref, rows, nb)
    xc = _conv_taps(xext_ref, wlc_ref, rows, nb) + blc_ref[...]
    for ext_ref, tail_ref in ((pext_ref, tailp_ref), (xext_ref, tailx_ref)):
        tail = ext_ref[rows:rows + hist, :]
        ext_ref[0:hist, :] = tail
        tail_ref[...] = tail

    a, u = _lru_gates(xc, wg_ref, ba_ref[...], bx_ref[...], lam_ref[...])
    a_ref[...] = a
    u_ref[...] = u

    def step(t, h):
        grp = pl.ds(pl.multiple_of(t * nb, nb), nb)
        h = a_ref[grp, :] * h + u_ref[grp, :]
        u_ref[grp, :] = h
        return h

    h_last = lax.fori_loop(0, tt, step, h_ref[...], unroll=8)
    h_ref[...] = h_last
    hout_ref[...] = h_last

    yb = u_ref[...] * jax.nn.gelu(gate)
    out = _mix_out(x, ya, yb, gco_ref[...], glo_ref[...], wout_ref)

    @pl.when(i >= 2)
    def _():
        for cp in out_copies(i - 2, slot):
            cp.wait()

    xout_ref[slot] = out.reshape(tt, nb, d)
    for cp in out_copies(i, slot):
        cp.start()

    @pl.when(i == n_steps - 1)
    def _():
        for cp in out_copies(i, slot):
            cp.wait()

        @pl.when(i >= 1)
        def _():
            for cp in out_copies(i - 1, 1 - slot):
                cp.wait()


def _const_spec(shape):
    return pl.BlockSpec(shape, lambda *_: (0,) * len(shape))


def _param_array(p):
    return p[0] if isinstance(p, tuple) else p


def _param_spec(p, resident=False):
    a, layer = p if isinstance(p, tuple) else (p, None)
    mode = dict(pipeline_mode=pl.Buffered(1)) if resident else {}
    if layer is None:
        return pl.BlockSpec(a.shape, lambda *_: (0,) * a.ndim, **mode)
    return pl.BlockSpec((None,) + a.shape[1:], lambda *_: (layer,) + (0,) * (a.ndim - 1), **mode)


def _mixer_prompt(x, p, tt=MIXER_TIME_CHUNK):
    nb, t, d = x.shape
    small = [p['gmix'], p['win'], p['wsc'], p['wlc'], p['blc'], p['wg'], p['ba'], p['bx'], p['lam'],
             p['gco'], p['glo'], p['wout']]
    rows = nb * tt
    hist = MAX_CONV_HISTORY * nb
    return pl.pallas_call(
        _mixer_prompt_kernel,
        grid=(t // tt,),
        in_specs=[pl.BlockSpec(memory_space=pl.ANY)] + [_param_spec(a, resident=True) for a in small],
        out_specs=[pl.BlockSpec(memory_space=pl.ANY),
                   _const_spec((hist, D_CONV)), _const_spec((hist, D_LRU)), _const_spec((nb, D_LRU))],
        out_shape=[jax.ShapeDtypeStruct((nb, t, d), F32),
                   jax.ShapeDtypeStruct((hist, D_CONV), F32),
                   jax.ShapeDtypeStruct((hist, D_LRU), F32),
                   jax.ShapeDtypeStruct((nb, D_LRU), F32)],
        scratch_shapes=[pltpu.VMEM((2, tt, nb, d), F32), pltpu.VMEM((2, tt, nb, d), F32),
                        pltpu.SemaphoreType.DMA((2,)), pltpu.SemaphoreType.DMA((2,)),
                        pltpu.VMEM((hist + rows, D_CONV), F32), pltpu.VMEM((hist + rows, D_LRU), F32),
                        pltpu.VMEM((nb, D_LRU), F32),
                        pltpu.VMEM((rows, D_LRU), F32), pltpu.VMEM((rows, D_LRU), F32)],
        compiler_params=_cparams("arbitrary"),
        name="mixer_prompt",
    )(x, *map(_param_array, small))


def _mixer_sample_kernel(x_ref, sc0_ref, sc1_ref, lc0_ref, lc1_ref, lc2_ref, h0_ref,
                         gmix_ref, win_ref, wsc_ref, wlc_ref, blc_ref, wg_ref, ba_ref, bx_ref,
                         lam_ref, gco_ref, glo_ref, wout_ref, gx_ref, wq_ref,
                         xo_ref, q_ref, pnew_ref, xbnew_ref, hout_ref):
    x = x_ref[...]
    z = _dot(_rms(x, gmix_ref[...]).astype(BF16), win_ref[...])
    gb = z[:, 0:D_CONV]
    p = z[:, D_CONV:2 * D_CONV] * z[:, 2 * D_CONV:3 * D_CONV]
    xb = z[:, 3 * D_CONV:3 * D_CONV + D_LRU]
    gate = z[:, 3 * D_CONV + D_LRU:]

    ya = gb * (sc0_ref[...] * wsc_ref[0:1, :] + sc1_ref[...] * wsc_ref[1:2, :] + p * wsc_ref[2:3, :])
    xc = (lc0_ref[...] * wlc_ref[0:1, :] + lc1_ref[...] * wlc_ref[1:2, :] + lc2_ref[...] * wlc_ref[2:3, :]
          + xb * wlc_ref[3:4, :]) + blc_ref[...]
    a, u = _lru_gates(xc, wg_ref, ba_ref[...], bx_ref[...], lam_ref[...])
    h = a * h0_ref[...] + u
    yb = h * jax.nn.gelu(gate)
    x1 = _mix_out(x, ya, yb, gco_ref[...], glo_ref[...], wout_ref)
    xo_ref[...] = x1
    q_ref[...] = _dot(_rms(x1, gx_ref[...]).astype(BF16), wq_ref[...])
    pnew_ref[...] = p
    xbnew_ref[...] = xb
    hout_ref[...] = h


def _mixer_sample(x, sc, lc, h0, p):
    nb, d = x.shape
    args = [x, sc[:, 0], sc[:, 1], lc[:, 0], lc[:, 1], lc[:, 2], h0,
            p['gmix'], p['win'], p['wsc'], p['wlc'], p['blc'], p['wg'], p['ba'], p['bx'], p['lam'],
            p['gco'], p['glo'], p['wout'], p['gx'], p['wq']]
    return pl.pallas_call(
        _mixer_sample_kernel,
        grid=(1,),
        in_specs=[_param_spec(a) for a in args],
        out_specs=[_const_spec((nb, d)), _const_spec((nb, d)), _const_spec((nb, D_CONV)),
                   _const_spec((nb, D_LRU)), _const_spec((nb, D_LRU))],
        out_shape=[jax.ShapeDtypeStruct((nb, d), F32), jax.ShapeDtypeStruct((nb, d), F32),
                   jax.ShapeDtypeStruct((nb, D_CONV), F32), jax.ShapeDtypeStruct((nb, D_LRU), F32),
                   jax.ShapeDtypeStruct((nb, D_LRU), F32)],
        compiler_params=_cparams("arbitrary"),
        name="mixer_sample",
    )(*map(_param_array, args))


def _xattn_prompt_kernel(x_ref, g_ref, wq_ref, k_ref, v_ref, wo_ref, o_ref):
    x = x_ref[...]
    q = _dot(_rms(x, g_ref[...]).astype(BF16), wq_ref[...])
    heads = []
    for h in range(N_MEM_HEADS):
        sl = slice(h * MEM_HEAD_DIM, (h + 1) * MEM_HEAD_DIM)
        kh = k_ref[:, sl]
        vh = v_ref[:, sl]
        s = lax.dot_general(q[:, sl].astype(BF16), kh, (((1,), (1,)), ((), ())),
                            preferred_element_type=F32) * (MEM_HEAD_DIM ** -0.5)
        e = jnp.exp(s - jnp.max(s, axis=-1, keepdims=True))
        pr = e / jnp.sum(e, axis=-1, keepdims=True)
        heads.append(_dot(pr.astype(BF16), vh))
    o = jnp.concatenate(heads, axis=-1).astype(BF16)
    o_ref[...] = x + _dot(o, wo_ref[...])


def _xattn_prompt(x, g, wq, k, v, wo, layer, tq=XATTN_ROWS):
    nb, t, d = x.shape
    n_mem = k.shape[2]
    kv_spec = pl.BlockSpec((None, None, n_mem, d), lambda b, i: (layer, b, 0, 0))
    return pl.pallas_call(
        _xattn_prompt_kernel,
        grid=(nb, t // tq),
        in_specs=[pl.BlockSpec((None, tq, d), lambda b, i: (b, i, 0)), _param_spec(g),
                  _param_spec(wq, resident=True), kv_spec, kv_spec, _param_spec(wo, resident=True)],
        out_specs=pl.BlockSpec((None, tq, d), lambda b, i: (b, i, 0)),
        out_shape=jax.ShapeDtypeStruct((nb, t, d), F32),
        compiler_params=_cparams("parallel", "parallel"),
        name="xattn_prompt",
    )(x, _param_array(g), _param_array(wq), k, v, _param_array(wo))


def _kv_proj_kernel(mem_ref, g_ref, wk_ref, wv_ref, k5_ref, v5_ref, kb_ref, vb_ref):
    bb, n_mem, d = mem_ref.shape
    m = _rms(mem_ref[...].reshape(bb * n_mem, d), g_ref[...]).astype(BF16)
    for w_ref, o5_ref, ob_ref in ((wk_ref, k5_ref, kb_ref), (wv_ref, v5_ref, vb_ref)):
        y = _dot(m, w_ref[...]).reshape(bb, n_mem, d)
        ob_ref[...] = y.astype(BF16)
        for h in range(N_MEM_HEADS):
            o5_ref[:, :, h, :] = y[:, :, h * MEM_HEAD_DIM:(h + 1) * MEM_HEAD_DIM]


def _kv_proj(mem, gmem, wk, wv, bb=KV_PROJ_SEQS):
    nb, n_mem, d = mem.shape
    depth = wk.shape[0]
    w_spec = pl.BlockSpec((None, d, d), lambda l, i: (l, 0, 0))
    o5_spec = pl.BlockSpec((None, bb, n_mem, N_MEM_HEADS, MEM_HEAD_DIM), lambda l, i: (l, i, 0, 0, 0))
    ob_spec = pl.BlockSpec((None, bb, n_mem, d), lambda l, i: (l, i, 0, 0))
    o5_shape = jax.ShapeDtypeStruct((depth, nb, n_mem, N_MEM_HEADS, MEM_HEAD_DIM), F32)
    ob_shape = jax.ShapeDtypeStruct((depth, nb, n_mem, d), BF16)
    return pl.pallas_call(
        _kv_proj_kernel,
        grid=(depth, nb // bb),
        in_specs=[pl.BlockSpec((bb, n_mem, d), lambda l, i: (i, 0, 0)),
                  pl.BlockSpec((None, 1, d), lambda l, i: (l, 0, 0)), w_spec, w_spec],
        out_specs=[o5_spec, o5_spec, ob_spec, ob_spec],
        out_shape=[o5_shape, o5_shape, ob_shape, ob_shape],
        compiler_params=_cparams("parallel", "parallel"),
        name="kv_proj",
    )(mem, gmem, wk, wv)


LANE_TILES_PER_HEAD = MEM_HEAD_DIM // LANES
HEAD_ROWS = N_MEM_HEADS * LANE_TILES_PER_HEAD


def _to_head_rows(a):
    lead = a.shape[:-1]
    n = len(lead)
    a = a.reshape(*lead, N_MEM_HEADS, LANE_TILES_PER_HEAD, LANES)
    return jnp.swapaxes(a, n, n + 1).reshape(*lead, HEAD_ROWS, LANES)


def _from_head_rows(a):
    lead = a.shape[:-2]
    n = len(lead)
    a = a.reshape(*lead, LANE_TILES_PER_HEAD, N_MEM_HEADS, LANES)
    return jnp.swapaxes(a, n, n + 1).reshape(*lead, N_MEM_HEADS * MEM_HEAD_DIM)


def _xattn_sample_kernel(q_ref, k_ref, v_ref, o_ref):
    n_mem = k_ref.shape[1]
    ones = jnp.ones((LANES, LANES), BF16)
    for b in range(q_ref.shape[0]):
        prod = (k_ref[b] * q_ref[b][None]).reshape(n_mem * HEAD_ROWS, LANES).astype(BF16)
        part = _dot(prod, ones).reshape(n_mem, HEAD_ROWS, LANES)
        s = (part + pltpu.roll(part, N_MEM_HEADS, axis=1)) * (MEM_HEAD_DIM ** -0.5)
        e = jnp.exp(s - jnp.max(s, axis=0, keepdims=True))
        pr = e / jnp.sum(e, axis=0, keepdims=True)
        o_ref[b] = jnp.sum(pr * v_ref[b], axis=0)


def _xattn_sample(q, k8, v8, layer, bb=SAMPLE_ATTN_SEQS):
    nb, d = q.shape
    n_mem = k8.shape[2]
    kv_spec = pl.BlockSpec((None, bb, n_mem, HEAD_ROWS, LANES), lambda i: (layer, i, 0, 0, 0))
    out = pl.pallas_call(
        _xattn_sample_kernel,
        grid=(nb // bb,),
        in_specs=[pl.BlockSpec((bb, HEAD_ROWS, LANES), lambda i: (i, 0, 0)), kv_spec, kv_spec],
        out_specs=pl.BlockSpec((bb, HEAD_ROWS, LANES), lambda i: (i, 0, 0)),
        out_shape=jax.ShapeDtypeStruct((nb, HEAD_ROWS, LANES), F32),
        compiler_params=_cparams("parallel"),
        name="xattn_sample",
    )(_to_head_rows(q), k8, v8)
    return _from_head_rows(out)


def _split_bf16(a):
    hi = a.astype(BF16)
    return hi, (a - hi.astype(F32)).astype(BF16)


def _top2(logits):
    n = logits.shape[-1]
    lane = lax.broadcasted_iota(jnp.int32, logits.shape, 1)
    m1 = jnp.max(logits, axis=-1, keepdims=True)
    i1 = jnp.min(jnp.where(logits == m1, lane, n), axis=-1, keepdims=True)
    rest = jnp.where(lane == i1, -jnp.inf, logits)
    m2 = jnp.max(rest, axis=-1, keepdims=True)
    i2 = jnp.min(jnp.where(rest == m2, lane, n), axis=-1, keepdims=True)
    e2 = jnp.exp(m2 - m1)
    denom = 1.0 + e2
    return i1, i2, 1.0 / denom, e2 / denom


def _add_attn_proj(x, rest, proj):
    if not proj:
        return x, rest
    a_ref, wo_ref = rest[:2]
    return x + _dot(a_ref[...].astype(BF16), wo_ref[...]), rest[2:]


def _ffn_dense_kernel(x_ref, g_ref, *rest, final, proj):
    x, rest = _add_attn_proj(x_ref[...], rest, proj)
    if final:
        gfin_ref, rest = rest[0], rest[1:]
    wg_ref, wu_ref, wd_ref, o_ref = rest
    h = _rms(x, g_ref[...]).astype(BF16)
    act = (jax.nn.silu(_dot(h, wg_ref[...])) * _dot(h, wu_ref[...])).astype(BF16)
    out = x + _dot(act, wd_ref[...])
    o_ref[...] = _rms(out, gfin_ref[...]) if final else out


def _ffn_dense(x, g, wg, wu, wd, tm, g_final=None, attn=None):
    r, d = x.shape
    final, proj = g_final is not None, attn is not None
    args, specs = [x, g], [pl.BlockSpec((tm, d), lambda i: (i, 0)), _param_spec(g)]
    if proj:
        args += list(attn)
        specs += [pl.BlockSpec((tm, attn[0].shape[1]), lambda i: (i, 0)), _param_spec(attn[1], resident=True)]
    if final:
        args.append(g_final)
        specs.append(_param_spec(g_final))
    args += [wg, wu, wd]
    specs += [_param_spec(w, resident=True) for w in (wg, wu, wd)]
    return pl.pallas_call(
        functools.partial(_ffn_dense_kernel, final=final, proj=proj),
        grid=(r // tm,),
        in_specs=specs,
        out_specs=pl.BlockSpec((tm, d), lambda i: (i, 0)),
        out_shape=jax.ShapeDtypeStruct((r, d), F32),
        compiler_params=_cparams("parallel"),
        name="ffn_dense",
    )(*map(_param_array, args))


def _moe_all_experts_kernel(x_ref, g_ref, wr_ref, *rest, final, proj):
    if proj:
        attn_refs, rest = rest[:2], rest[2:]
    if final:
        gfin_ref, rest = rest[0], rest[1:]
    wg_ref, wu_ref, wd_ref, o_ref, h_ref, acc_ref, i1_ref, i2_ref, g1_ref, g2_ref = rest
    e = pl.program_id(1)

    @pl.when(e == 0)
    def _():
        x = x_ref[...]
        if proj:
            x, _ = _add_attn_proj(x, attn_refs, proj)
        h = _rms(x, g_ref[...])
        h_ref[...] = h.astype(BF16)
        acc_ref[...] = x
        h_hi, h_lo = _split_bf16(h)
        w_hi, w_lo = _split_bf16(wr_ref[...])
        logits = _dot(h_hi, w_hi) + (_dot(h_hi, w_lo) + _dot(h_lo, w_hi))
        lane = lax.broadcasted_iota(jnp.int32, logits.shape, 1)
        logits = jnp.where(lane < N_EXPERTS, logits, -jnp.inf)
        i1_ref[...], i2_ref[...], g1_ref[...], g2_ref[...] = _top2(logits)

    gt, up = _gate_up(h_ref[...], wg_ref, wu_ref)
    y = _dot((jax.nn.silu(gt) * up).astype(BF16), wd_ref[...])
    c = jnp.where(i1_ref[...] == e, g1_ref[...], 0.0) + jnp.where(i2_ref[...] == e, g2_ref[...], 0.0)
    acc_ref[...] += c * y

    @pl.when(e == pl.num_programs(1) - 1)
    def _():
        out = acc_ref[...]
        o_ref[...] = _rms(out, gfin_ref[...]) if final else out


def _moe_all_experts(x, g, w_router, wg, wu, wd, tm, g_final=None, attn=None):
    r, d = x.shape
    n_e, f, _ = wd.shape
    final, proj = g_final is not None, attn is not None
    args = [x, g, w_router]
    specs = [pl.BlockSpec((tm, d), lambda i, e: (i, 0)), _param_spec(g), _param_spec(w_router)]
    if proj:
        args += list(attn)
        specs += [pl.BlockSpec((tm, attn[0].shape[1]), lambda i, e: (i, 0)), _param_spec(attn[1])]
    if final:
        args.append(g_final)
        specs.append(_param_spec(g_final))
    args += [wg, wu, wd]
    specs += [pl.BlockSpec((None, d, f), lambda i, e: (e, 0, 0)),
              pl.BlockSpec((None, d, f), lambda i, e: (e, 0, 0)),
              pl.BlockSpec((None, f, d), lambda i, e: (e, 0, 0))]
    return pl.pallas_call(
        functools.partial(_moe_all_experts_kernel, final=final, proj=proj),
        grid=(r // tm, n_e),
        in_specs=specs,
        out_specs=pl.BlockSpec((tm, d), lambda i, e: (i, 0)),
        out_shape=jax.ShapeDtypeStruct((r, d), F32),
        scratch_shapes=[pltpu.VMEM((tm, d), BF16), pltpu.VMEM((tm, d), F32),
                        pltpu.VMEM((tm, 1), jnp.int32), pltpu.VMEM((tm, 1), jnp.int32),
                        pltpu.VMEM((tm, 1), F32), pltpu.VMEM((tm, 1), F32)],
        compiler_params=_cparams("parallel", "arbitrary"),
        name="moe_all_experts",
    )(*map(_param_array, args))


MOE_TILE = 512
TOP_K = 2
ROUTE_TILES_PER_STEP = 4
ROW_ALIGN = SUBLANES
COMPACT_ROWS = -(-(TOP_K * MOE_TILE + N_EXPERTS * (ROW_ALIGN - 1)) // (2 * SUBLANES)) * (2 * SUBLANES)
SEGMENT_BITS = (MOE_TILE // ROW_ALIGN).bit_length()


def _route_kernel(x_ref, g_ref, wrt_ref, earlier_ref, lp_ref, col_ref, cnt_ref):
    n_e = wrt_ref.shape[0]
    w_hi, w_lo = _split_bf16(wrt_ref[...])
    nt = (((1,), (1,)), ((), ()))
    dg = lambda a, b: lax.dot_general(a, b, nt, preferred_element_type=F32)
    for k in range(lp_ref.shape[0]):
        tile = slice(k * MOE_TILE, (k + 1) * MOE_TILE)
        h = _rms(x_ref[tile, :], g_ref[...])
        h_hi, h_lo = _split_bf16(h)
        logits = dg(w_hi, h_hi) + (dg(w_hi, h_lo) + dg(w_lo, h_hi))
        sub = lax.broadcasted_iota(jnp.int32, logits.shape, 0)
        m1 = jnp.max(logits, axis=0, keepdims=True)
        i1 = jnp.min(jnp.where(logits == m1, sub, n_e), axis=0, keepdims=True)
        rest = jnp.where(sub == i1, -jnp.inf, logits)
        m2 = jnp.max(rest, axis=0, keepdims=True)
        i2 = jnp.min(jnp.where(rest == m2, sub, n_e), axis=0, keepdims=True)
        e2 = jnp.exp(m2 - m1)
        denom = 1.0 + e2

        pick1 = jnp.where(sub == i1, 1.0, 0.0)
        pick2 = jnp.where(sub == i2, 1.0, 0.0)
        cnt = pick1 + pick2
        rank = _dot(cnt.astype(BF16), earlier_ref[...])
        n = jnp.sum(cnt, axis=1, keepdims=True)
        n_pad = jnp.floor((n + (ROW_ALIGN - 1)) * (1.0 / ROW_ALIGN)) * ROW_ALIGN
        base, acc = [], jnp.zeros((1, 1), F32)
        for e in range(n_e):
            base.append(acc)
            acc = acc + n_pad[e:e + 1, :]
        pos = rank + jnp.concatenate(base, axis=0)
        lp1 = jnp.sum(pick1 * pos, axis=0, keepdims=True)
        lp2 = jnp.sum(pick2 * pos, axis=0, keepdims=True)
        lp_ref[k, 0:1, :] = lp1.astype(jnp.int32)
        lp_ref[k, 1:2, :] = lp2.astype(jnp.int32)
        info = jnp.concatenate([lp1, lp2, 1.0 / denom, e2 / denom,
                                jnp.zeros((SUBLANES - 4, MOE_TILE), F32)], axis=0)
        col_ref[tile, :] = info.T
        cnt_ref[k] = jnp.broadcast_to(n, cnt_ref.shape[1:])


def _route(x, g, wrt):
    t, d = x.shape
    n_e = wrt.shape[0]
    n_tiles = t // MOE_TILE
    idx = jnp.arange(MOE_TILE)
    earlier = (idx[:, None] < idx[None, :]).astype(BF16)
    return pl.pallas_call(
        _route_kernel,
        grid=(n_tiles // ROUTE_TILES_PER_STEP,),
        in_specs=[pl.BlockSpec((ROUTE_TILES_PER_STEP * MOE_TILE, d), lambda j: (j, 0)), _param_spec(g),
                  _const_spec(wrt.shape), _const_spec(earlier.shape)],
        out_specs=[pl.BlockSpec((ROUTE_TILES_PER_STEP, TOP_K, MOE_TILE), lambda j: (j, 0, 0)),
                   pl.BlockSpec((ROUTE_TILES_PER_STEP * MOE_TILE, SUBLANES), lambda j: (j, 0)),
                   pl.BlockSpec((ROUTE_TILES_PER_STEP, n_e, LANES), lambda j: (j, 0, 0))],
        out_shape=[jax.ShapeDtypeStruct((n_tiles, TOP_K, MOE_TILE), jnp.int32),
                   jax.ShapeDtypeStruct((t, SUBLANES), F32),
                   jax.ShapeDtypeStruct((n_tiles, n_e, LANES), F32)],
        compiler_params=_cparams("parallel"),
        name="moe_route",
    )(x, _param_array(g), wrt, earlier)


def _segment_copies(src_ref, src_row, dst_ref, dst_row, groups, sem):
    out = []
    for bit in range(SEGMENT_BITS):
        rows = ROW_ALIGN << bit
        before = ((groups >> (bit + 1)) << (bit + 1)) * ROW_ALIGN
        cp = pltpu.make_async_copy(
            src_ref.at[pl.ds(pl.multiple_of(src_row + before, ROW_ALIGN), rows), :],
            dst_ref.at[pl.ds(pl.multiple_of(dst_row + before, ROW_ALIGN), rows), :], sem)
        out.append((((groups >> bit) & 1) == 1, cp))
    return out


def _start_all(copies):
    for pred, cp in copies:
        pl.when(pred)(cp.start)


def _wait_all(copies):
    for pred, cp in copies:
        pl.when(pred)(cp.wait)


def _dispatch_kernel(grp_ref, off_ref, dst_ref, pad_ref, x_ref, g_ref, lp_ref, xs_ref, comp_ref, sem):
    j = pl.program_id(0)
    n_steps = pl.num_programs(0)
    n_e = grp_ref.shape[1]
    slot = j % 2

    def copies(step, s):
        out = []
        for e in range(n_e):
            out += _segment_copies(comp_ref.at[s], off_ref[step, e], xs_ref, dst_ref[step, e],
                                   grp_ref[step, e], sem.at[s])
        return out

    h = _rms(x_ref[...], g_ref[...]).astype(BF16)
    row = lax.broadcasted_iota(jnp.int32, (COMPACT_ROWS, h.shape[0]), 0)
    onehot = jnp.where(row == lp_ref[0:1, :], 1.0, jnp.where(row == lp_ref[1:2, :], 1.0, 0.0)).astype(BF16)
    compact = _dot(onehot, h)

    @pl.when(j >= 2)
    def _():
        _wait_all(copies(j - 2, slot))

    comp_ref[slot] = compact
    _start_all(copies(j, slot))

    @pl.when(j == n_steps - 1)
    def _():
        _wait_all(copies(j, slot))

        @pl.when(j >= 1)
        def _():
            _wait_all(copies(j - 1, 1 - slot))

        comp_ref[slot, 0:MOE_TILE, :] = jnp.zeros((MOE_TILE, comp_ref.shape[2]), F32)
        tail = []
        for e in range(n_e):
            tail += _segment_copies(comp_ref.at[slot], 0, xs_ref, pad_ref[0, e], pad_ref[1, e], sem.at[slot])
        tile_groups = MOE_TILE // ROW_ALIGN
        groups_left = (xs_ref.shape[0] - pad_ref[2, 0]) // ROW_ALIGN
        min_total = TOP_K * x_ref.shape[0] * grp_ref.shape[0]
        for c in range((xs_ref.shape[0] - min_total) // MOE_TILE):
            tail += _segment_copies(comp_ref.at[slot], 0, xs_ref, pad_ref[2, 0] + c * MOE_TILE,
                                    jnp.clip(groups_left - c * tile_groups, 0, tile_groups), sem.at[slot])
        _start_all(tail)
        _wait_all(tail)


def _dispatch(x, g, lp, groups, off, dst, pad, n_rows):
    t, d = x.shape
    n_tiles = t // MOE_TILE
    grid_spec = pltpu.PrefetchScalarGridSpec(
        num_scalar_prefetch=4,
        grid=(n_tiles,),
        in_specs=[pl.BlockSpec((MOE_TILE, d), lambda j, *_: (j, 0)),
                  _param_spec(g),
                  pl.BlockSpec((None, TOP_K, MOE_TILE), lambda j, *_: (j, 0, 0))],
        out_specs=pl.BlockSpec(memory_space=pl.ANY),
        scratch_shapes=[pltpu.VMEM((2, COMPACT_ROWS, d), F32), pltpu.SemaphoreType.DMA((2,))])
    return pl.pallas_call(
        _dispatch_kernel,
        grid_spec=grid_spec,
        out_shape=jax.ShapeDtypeStruct((n_rows, d), F32),
        compiler_params=_cparams("arbitrary"),
        name="moe_dispatch",
    )(groups, off, dst, pad, x, _param_array(g), lp)


STEP_FFN, STEP_ZERO = 1, 2


def _grouped_ffn_kernel(grp_ref, mode_ref, xtile_ref, xs_ref, wg_ref, wu_ref, wd_ref, ys_ref):
    w = pl.program_id(0)

    @pl.when(mode_ref[w] == STEP_ZERO)
    def _():
        ys_ref[...] = jnp.zeros_like(ys_ref)

    @pl.when(mode_ref[w] == STEP_FFN)
    def _():
        gt, up = _gate_up(xs_ref[...].astype(BF16), wg_ref, wu_ref)
        ys_ref[...] = _dot((jax.nn.silu(gt) * up).astype(BF16), wd_ref[...])


def _grouped_ffn(xs, wg, wu, wd, grp, mode, xtile):
    n_e, f, d = wd.shape
    grid_spec = pltpu.PrefetchScalarGridSpec(
        num_scalar_prefetch=3,
        grid=(xs.shape[0] // MOE_TILE,),
        in_specs=[pl.BlockSpec((MOE_TILE, d), lambda w, grp, mode, xtile: (xtile[w], 0)),
                  pl.BlockSpec((None, d, f), lambda w, grp, *_: (grp[w], 0, 0)),
                  pl.BlockSpec((None, d, f), lambda w, grp, *_: (grp[w], 0, 0)),
                  pl.BlockSpec((None, f, d), lambda w, grp, *_: (grp[w], 0, 0))],
        out_specs=pl.BlockSpec((MOE_TILE, d), lambda w, *_: (w, 0)))
    return pl.pallas_call(
        _grouped_ffn_kernel,
        grid_spec=grid_spec,
        out_shape=jax.ShapeDtypeStruct(xs.shape, F32),
        compiler_params=_cparams("arbitrary"),
        name="moe_grouped_ffn",
    )(grp, mode, xtile, xs, wg, wu, wd)


def _combine_kernel(grp_ref, off_ref, dst_ref, x_ref, col_ref, *rest, final):
    if final:
        gfin_ref, rest = rest[0], rest[1:]
    ys_ref, o_ref, comp_ref, sem = rest
    j = pl.program_id(0)
    n_steps = pl.num_programs(0)
    n_e = grp_ref.shape[1]
    slot = j % 2

    def copies(step, s):
        out = []
        for e in range(n_e):
            out += _segment_copies(ys_ref, dst_ref[step, e], comp_ref.at[s], off_ref[step, e],
                                   grp_ref[step, e], sem.at[s])
        return out

    @pl.when(j == 0)
    def _():
        comp_ref[...] = jnp.zeros_like(comp_ref)
        _start_all(copies(0, 0))

    @pl.when(j + 1 < n_steps)
    def _():
        _start_all(copies(j + 1, 1 - slot))

    _wait_all(copies(j, slot))
    y = comp_ref[slot].astype(BF16)
    col = col_ref[...]
    row = lax.broadcasted_iota(jnp.int32, (col.shape[0], COMPACT_ROWS), 1)
    weights = jnp.zeros(row.shape, F32)
    for k in range(TOP_K):
        weights = jnp.where(row == col[:, k:k + 1].astype(jnp.int32), col[:, TOP_K + k:TOP_K + k + 1], weights)
    out = x_ref[...] + _dot(weights.astype(BF16), y)
    o_ref[...] = _rms(out, gfin_ref[...]) if final else out


def _combine(x, col, ys, groups, off, dst, g_final):
    t, d = x.shape
    n_tiles = t // MOE_TILE
    final = g_final is not None
    args = [x, col]
    specs = [pl.BlockSpec((MOE_TILE, d), lambda j, *_: (j, 0)),
             pl.BlockSpec((MOE_TILE, SUBLANES), lambda j, *_: (j, 0))]
    if final:
        args.append(g_final)
        specs.append(_param_spec(g_final))
    args.append(ys)
    specs.append(pl.BlockSpec(memory_space=pl.ANY))
    grid_spec = pltpu.PrefetchScalarGridSpec(
        num_scalar_prefetch=3,
        grid=(n_tiles,),
        in_specs=specs,
        out_specs=pl.BlockSpec((MOE_TILE, d), lambda j, *_: (j, 0)),
        scratch_shapes=[pltpu.VMEM((2, COMPACT_ROWS, d), F32), pltpu.SemaphoreType.DMA((2,))])
    return pl.pallas_call(
        functools.partial(_combine_kernel, final=final),
        grid_spec=grid_spec,
        out_shape=jax.ShapeDtypeStruct((t, d), F32),
        compiler_params=_cparams("arbitrary"),
        name="moe_combine",
    )(groups, off, dst, *map(_param_array, args))


def _moe_routed(x, g, w_router, wg, wu, wd, g_final):
    t, _ = x.shape
    n_e = wd.shape[0]
    n_tiles = t // MOE_TILE
    lp, col, cnt = _route(x, g, w_router.T)
    groups = (cnt[:, :, 0].astype(jnp.int32) + (ROW_ALIGN - 1)) // ROW_ALIGN
    n = groups * ROW_ALIGN
    off = jnp.cumsum(n, axis=1) - n
    per_e = jnp.sum(n, axis=0)
    tiles_e = (per_e + (MOE_TILE - 1)) // MOE_TILE
    tile_end = jnp.cumsum(tiles_e)
    starts = (tile_end - tiles_e) * MOE_TILE
    dst = starts[None, :] + jnp.cumsum(n, axis=0) - n
    pad = jnp.stack([starts + per_e, (tiles_e * MOE_TILE - per_e) // ROW_ALIGN,
                     jnp.broadcast_to(tile_end[-1] * MOE_TILE, per_e.shape)])
    max_rows = TOP_K * t + n_tiles * n_e * (ROW_ALIGN - 1)
    n_row_tiles = -(-max_rows // MOE_TILE) + n_e
    xs = _dispatch(x, g, lp, groups, off, dst, pad, n_row_tiles * MOE_TILE)

    w = jnp.arange(n_row_tiles, dtype=jnp.int32)
    used = w < tile_end[-1]
    grp = jnp.minimum(jnp.sum(w[:, None] >= tile_end[None, :], axis=1), n_e - 1).astype(jnp.int32)
    mode = jnp.where(used, STEP_FFN, STEP_ZERO).astype(jnp.int32)
    xtile = jnp.minimum(w, tile_end[-1] - 1)
    ys = _grouped_ffn(xs, wg, wu, wd, grp, mode, xtile)
    return _combine(x, col, ys, groups, off, dst, g_final)


def _block_diag_gates(w_a, w_x):
    per_group = LRU_GROUP // LRU_HEAD_DIM
    eye = jnp.eye(per_group, dtype=w_a.dtype)

    def bd(w):
        w = w.reshape(D_LRU // LRU_GROUP, per_group, LRU_HEAD_DIM, LRU_HEAD_DIM)
        return jnp.einsum('ghij,hk->ghikj', w, eye).reshape(D_LRU // LRU_GROUP, LRU_GROUP, LRU_GROUP)

    return jnp.concatenate([bd(w_a), bd(w_x)], axis=-1).astype(BF16)


def kernel(x_prompt, x_sample, state_shortconv, state_lru_conv, state_lru_h, cache_mem_k, cache_mem_v,
           mem_prompt, norm_mix, w_in, w_short_conv, w_lru_conv, b_lru_conv, w_lru_a, b_lru_a, w_lru_x,
           b_lru_x, lru_lambda, norm_conv_out, norm_lru_out, w_mix_out, norm_xattn, norm_mem, w_q, w_k,
           w_v, w_o, norm_ffn, w_ff_gate, w_ff_up, w_ff_down, w_router, w_moe_gate, w_moe_up, w_moe_down,
           norm_final):
    depth = w_in.shape[0]
    nb, t, d = x_prompt.shape
    ns = x_sample.shape[0]
    n_mem = mem_prompt.shape[1]
    rows = lambda a: a.reshape(a.shape[0], 1, -1)
    stacks = dict(
        gmix=rows(norm_mix), win=w_in.astype(BF16), wsc=w_short_conv, wlc=w_lru_conv, blc=rows(b_lru_conv),
        wg=jax.vmap(_block_diag_gates)(w_lru_a, w_lru_x), ba=rows(b_lru_a), bx=rows(b_lru_x),
        lam=rows(lru_lambda), gco=rows(norm_conv_out), glo=rows(norm_lru_out), wout=w_mix_out.astype(BF16),
        gx=rows(norm_xattn), wq=w_q.astype(BF16), wo=w_o.astype(BF16), gffn=rows(norm_ffn))
    g_final = norm_final.reshape(1, -1)
    ff_dense = (w_ff_gate.astype(BF16), w_ff_up.astype(BF16), w_ff_down.astype(BF16))
    ff_moe = (w_moe_gate.astype(BF16), w_moe_up.astype(BF16), w_moe_down.astype(BF16))

    layers = []
    for l in range(depth):
        lp = {name: (stack, l) for name, stack in stacks.items()}
        lp['gfin'] = g_final if l == depth - 1 else None
        j = l // 2
        if l % 2 == 0:
            lp['ffw'] = tuple((w, j) for w in ff_dense)
            lp['wr'] = None
        else:
            lp['ffw'] = tuple(w[j] for w in ff_moe)
            lp['wr'] = jnp.pad(w_router[j], ((0, 0), (0, LANES - N_EXPERTS)))
            lp['wr_raw'] = w_router[j]
        layers.append(lp)

    x = x_prompt
    p_mem_k, p_mem_v, kb, vb = _kv_proj(mem_prompt, norm_mem.reshape(depth, 1, d), w_k.astype(BF16),
                                        w_v.astype(BF16))
    pc, plc, ph = [], [], []
    for l, lp in enumerate(layers):
        x, tail_p, tail_x, h_last = _mixer_prompt(x, lp)
        x = _xattn_prompt(x, lp['gx'], lp['wq'], kb, vb, lp['wo'], l)
        if lp['wr'] is None:
            x = _ffn_dense(x.reshape(nb * t, d), lp['gffn'], *lp['ffw'], tm=FFN_ROWS, g_final=lp['gfin'])
        else:
            x = _moe_routed(x.reshape(nb * t, d), lp['gffn'], lp['wr_raw'], *lp['ffw'], lp['gfin'])
        x = x.reshape(nb, t, d)
        tail_p = jnp.swapaxes(tail_p.reshape(MAX_CONV_HISTORY, nb, D_CONV), 0, 1)
        tail_x = jnp.swapaxes(tail_x.reshape(MAX_CONV_HISTORY, nb, D_LRU), 0, 1)
        pc.append(tail_p[:, MAX_CONV_HISTORY - (w_short_conv.shape[1] - 1):])
        plc.append(tail_x[:, MAX_CONV_HISTORY - (w_lru_conv.shape[1] - 1):])
        ph.append(h_last)
    y_prompt = x

    x = x_sample.reshape(ns, d)
    k8 = _to_head_rows(cache_mem_k.reshape(depth, ns, n_mem, d))
    v8 = _to_head_rows(cache_mem_v.reshape(depth, ns, n_mem, d))
    sc, slc, sh = [], [], []
    for l, lp in enumerate(layers):
        x, q, p_new, xb_new, h_new = _mixer_sample(x, state_shortconv[l], state_lru_conv[l], state_lru_h[l], lp)
        attn = (_xattn_sample(q, k8, v8, l), lp['wo'])
        if lp['wr'] is None:
            x = _ffn_dense(x, lp['gffn'], *lp['ffw'], tm=ns, g_final=lp['gfin'], attn=attn)
        else:
            x = _moe_all_experts(x, lp['gffn'], lp['wr'], *lp['ffw'], tm=ns, g_final=lp['gfin'], attn=attn)
        sc.append(jnp.stack([state_shortconv[l][:, 1], p_new], axis=1))
        slc.append(jnp.stack([state_lru_conv[l][:, 1], state_lru_conv[l][:, 2], xb_new], axis=1))
        sh.append(h_new)
    y_sample = x.reshape(ns, 1, d)

    return (y_prompt, y_sample, jnp.stack(pc), jnp.stack(plc), jnp.stack(ph), p_mem_k, p_mem_v,
            jnp.stack(sc), jnp.stack(slc), jnp.stack(sh))
```

```python
import functools

import jax
import jax.numpy as jnp
from jax import lax
from jax.experimental import pallas as pl
from jax.experimental.pallas import tpu as pltpu

F32 = jnp.float32
BF16 = jnp.bfloat16

EPS = 1e-6
LRU_C = 8.0
D_CONV = 512
D_LRU = 512
LRU_HEAD_DIM = 64
LRU_GROUP = 256
N_MEM_HEADS = 4
MEM_HEAD_DIM = 256
N_EXPERTS = 8
SUBLANES = 8
LANES = 128
MXU_COLS = 256
VMEM_LIMIT_BYTES = 56 * 1024 * 1024

MIXER_TIME_CHUNK = 128
XATTN_ROWS = 1024
FFN_ROWS = 512
KV_PROJ_SEQS = 2
SAMPLE_ATTN_SEQS = 8


def _cparams(*sem):
    return pltpu.CompilerParams(dimension_semantics=sem, vmem_limit_bytes=VMEM_LIMIT_BYTES)


def _rms(x, g):
    return x * lax.rsqrt(jnp.mean(x * x, axis=-1, keepdims=True) + EPS) * g


def _dot(a, b):
    return jnp.dot(a, b, preferred_element_type=F32)


def _gate_up(x, wg_ref, wu_ref):
    f = wg_ref.shape[1]
    main = f // MXU_COLS * MXU_COLS
    if main == f or 2 * (f - main) != MXU_COLS:
        return _dot(x, wg_ref[...]), _dot(x, wu_ref[...])
    rest = _dot(x, jnp.concatenate([wg_ref[:, main:], wu_ref[:, main:]], axis=1))
    gt = jnp.concatenate([_dot(x, wg_ref[:, :main]), rest[:, :f - main]], axis=1)
    up = jnp.concatenate([_dot(x, wu_ref[:, :main]), rest[:, f - main:]], axis=1)
    return gt, up


def _lru_gates(xc, wg_ref, ba, bx, lam):
    neg_lam = -lam
    softplus = jnp.maximum(neg_lam, 0.0) + jnp.log1p(jnp.exp(-jnp.abs(neg_lam)))
    a_parts, u_parts = [], []
    for g in range(D_LRU // LRU_GROUP):
        sl = slice(g * LRU_GROUP, (g + 1) * LRU_GROUP)
        xg = xc[:, sl]
        gates = _dot(xg.astype(BF16), wg_ref[g])
        r = jax.nn.sigmoid(gates[:, :LRU_GROUP] + ba[:, sl])
        i = jax.nn.sigmoid(gates[:, LRU_GROUP:] + bx[:, sl])
        log_a = (-LRU_C) * r * softplus[:, sl]
        th = jnp.tanh(log_a)
        one_minus_a2 = (-2.0 * th) / (1.0 - th)
        a_parts.append(jnp.exp(log_a))
        u_parts.append(jnp.sqrt(one_minus_a2) * (i * xg))
    return jnp.concatenate(a_parts, axis=-1), jnp.concatenate(u_parts, axis=-1)


def _mix_out(x, ya, yb, gco, glo, wout_ref):
    y = jnp.concatenate([_rms(ya, gco), _rms(yb, glo)], axis=-1).astype(BF16)
    return x + _dot(y, wout_ref[...])


MAX_CONV_HISTORY = 3


def _conv_taps(ext_ref, w_ref, rows, nb):
    width = w_ref.shape[0]
    hist = MAX_CONV_HISTORY * nb
    acc = ext_ref[hist:hist + rows, :] * w_ref[width - 1:width, :]
    for s in range(1, width):
        acc = acc + ext_ref[hist - s * nb:hist - s * nb + rows, :] * w_ref[width - 1 - s:width - s, :]
    return acc


def _mixer_prompt_kernel(x_hbm, gmix_ref, win_ref, wsc_ref, wlc_ref, blc_ref, wg_ref, ba_ref, bx_ref,
                         lam_ref, gco_ref, glo_ref, wout_ref,
                         xo_hbm, tailp_ref, tailx_ref, hout_ref,
                         xin_ref, xout_ref, in_sem, out_sem, pext_ref, xext_ref, h_ref, a_ref, u_ref):
    _, tt, nb, d = xin_ref.shape
    rows = tt * nb
    hist = MAX_CONV_HISTORY * nb
    i = pl.program_id(0)
    n_steps = pl.num_programs(0)
    slot = i % 2

    def in_copies(step, s):
        return [pltpu.make_async_copy(x_hbm.at[b, pl.ds(step * tt, tt), :], xin_ref.at[s, :, b, :],
                                      in_sem.at[s]) for b in range(nb)]

    def out_copies(step, s):
        return [pltpu.make_async_copy(xout_ref.at[s, :, b, :], xo_hbm.at[b, pl.ds(step * tt, tt), :],
                                      out_sem.at[s]) for b in range(nb)]

    @pl.when(i == 0)
    def _():
        pext_ref[0:hist, :] = jnp.zeros((hist, pext_ref.shape[1]), F32)
        xext_ref[0:hist, :] = jnp.zeros((hist, xext_ref.shape[1]), F32)
        h_ref[...] = jnp.zeros_like(h_ref)
        for cp in in_copies(0, 0):
            cp.start()

    @pl.when(i + 1 < n_steps)
    def _():
        for cp in in_copies(i + 1, 1 - slot):
            cp.start()

    for cp in in_copies(i, slot):
        cp.wait()
    x = xin_ref[slot].reshape(rows, d)
    z = _dot(_rms(x, gmix_ref[...]).astype(BF16), win_ref[...])
    gb = z[:, 0:D_CONV]
    gate = z[:, 3 * D_CONV + D_LRU:]
    pext_ref[hist:hist + rows, :] = z[:, D_CONV:2 * D_CONV] * z[:, 2 * D_CONV:3 * D_CONV]
    xext_ref[hist:hist + rows, :] = z[:, 3 * D_CONV:3 * D_CONV + D_LRU]

    ya = gb * _conv_taps(pext_ref, wsc_ref, rows, nb)
    xc = _conv_taps(xext_ref, wlc_ref, rows, nb) + blc_ref[...]
    for ext_ref, tail_ref in ((pext_ref, tailp_ref), (xext_ref, tailx_ref)):
        tail = ext_ref[rows:rows + hist, :]
        ext_ref[0:hist, :] = tail
        tail_ref[...] = tail

    a, u = _lru_gates(xc, wg_ref, ba_ref[...], bx_ref[...], lam_ref[...])
    a_ref[...] = a
    u_ref[...] = u

    def step(t, h):
        grp = pl.ds(pl.multiple_of(t * nb, nb), nb)
        h = a_ref[grp, :] * h + u_ref[grp, :]
        u_ref[grp, :] = h
        return h

    h_last = lax.fori_loop(0, tt, step, h_ref[...], unroll=8)
    h_ref[...] = h_last
    hout_ref[...] = h_last

    yb = u_ref[...] * jax.nn.gelu(gate)
    out = _mix_out(x, ya, yb, gco_ref[...], glo_ref[...], wout_ref)

    @pl.when(i >= 2)
    def _():
        for cp in out_copies(i - 2, slot):
            cp.wait()

    xout_ref[slot] = out.reshape(tt, nb, d)
    for cp in out_copies(i, slot):
        cp.start()

    @pl.when(i == n_steps - 1)
    def _():
        for cp in out_copies(i, slot):
            cp.wait()

        @pl.when(i >= 1)
        def _():
            for cp in out_copies(i - 1, 1 - slot):
                cp.wait()


def _const_spec(shape):
    return pl.BlockSpec(shape, lambda *_: (0,) * len(shape))


def _param_array(p):
    return p[0] if isinstance(p, tuple) else p


def _param_spec(p, resident=False):
    a, layer = p if isinstance(p, tuple) else (p, None)
    mode = dict(pipeline_mode=pl.Buffered(1)) if resident else {}
    if layer is None:
        return pl.BlockSpec(a.shape, lambda *_: (0,) * a.ndim, **mode)
    return pl.BlockSpec((None,) + a.shape[1:], lambda *_: (layer,) + (0,) * (a.ndim - 1), **mode)


def _cast_specs(w, n_steps):
    rows = w.shape[0] // n_steps
    assert rows * n_steps == w.shape[0] and rows % (2 * SUBLANES) == 0
    any_spec = pl.BlockSpec(memory_space=pl.ANY)
    scratch = [pltpu.VMEM((2, rows, w.shape[1]), F32), pltpu.VMEM((2, rows, w.shape[1]), BF16),
               pltpu.SemaphoreType.DMA((2,)), pltpu.SemaphoreType.DMA((2,))]
    return any_spec, any_spec, jax.ShapeDtypeStruct(w.shape, BF16), scratch


def _cast_rows(step, n_steps, src_hbm, dst_hbm, in_buf, out_buf, in_sem, out_sem):
    rows = in_buf.shape[1]
    slot = step % 2

    def in_copy(s, sl):
        return pltpu.make_async_copy(src_hbm.at[pl.ds(pl.multiple_of(s * rows, rows), rows), :],
                                     in_buf.at[sl], in_sem.at[sl])

    def out_copy(s, sl):
        return pltpu.make_async_copy(out_buf.at[sl],
                                     dst_hbm.at[pl.ds(pl.multiple_of(s * rows, rows), rows), :], out_sem.at[sl])

    pl.when(step == 0)(in_copy(0, 0).start)
    pl.when(step + 1 < n_steps)(in_copy(step + 1, 1 - slot).start)
    in_copy(step, slot).wait()
    pl.when(step >= 2)(out_copy(step - 2, slot).wait)
    out_buf[slot] = in_buf[slot].astype(BF16)
    out_copy(step, slot).start()

    @pl.when(step == n_steps - 1)
    def _():
        out_copy(step, slot).wait()
        pl.when(step >= 1)(out_copy(step - 1, 1 - slot).wait)


def _mixer_prompt(x, p, tt=MIXER_TIME_CHUNK):
    nb, t, d = x.shape
    small = [p['gmix'], p['win'], p['wsc'], p['wlc'], p['blc'], p['wg'], p['ba'], p['bx'], p['lam'],
             p['gco'], p['glo'], p['wout']]
    rows = nb * tt
    hist = MAX_CONV_HISTORY * nb
    return pl.pallas_call(
        _mixer_prompt_kernel,
        grid=(t // tt,),
        in_specs=[pl.BlockSpec(memory_space=pl.ANY)] + [_param_spec(a, resident=True) for a in small],
        out_specs=[pl.BlockSpec(memory_space=pl.ANY),
                   _const_spec((hist, D_CONV)), _const_spec((hist, D_LRU)), _const_spec((nb, D_LRU))],
        out_shape=[jax.ShapeDtypeStruct((nb, t, d), F32),
                   jax.ShapeDtypeStruct((hist, D_CONV), F32),
                   jax.ShapeDtypeStruct((hist, D_LRU), F32),
                   jax.ShapeDtypeStruct((nb, D_LRU), F32)],
        scratch_shapes=[pltpu.VMEM((2, tt, nb, d), F32), pltpu.VMEM((2, tt, nb, d), F32),
                        pltpu.SemaphoreType.DMA((2,)), pltpu.SemaphoreType.DMA((2,)),
                        pltpu.VMEM((hist + rows, D_CONV), F32), pltpu.VMEM((hist + rows, D_LRU), F32),
                        pltpu.VMEM((nb, D_LRU), F32),
                        pltpu.VMEM((rows, D_LRU), F32), pltpu.VMEM((rows, D_LRU), F32)],
        compiler_params=_cparams("arbitrary"),
        name="mixer_prompt",
    )(x, *map(_param_array, small))


def _mixer_sample_kernel(x_ref, sc0_ref, sc1_ref, lc0_ref, lc1_ref, lc2_ref, h0_ref,
                         gmix_ref, win_ref, wsc_ref, wlc_ref, blc_ref, wg_ref, ba_ref, bx_ref,
                         lam_ref, gco_ref, glo_ref, wout_ref, gx_ref, wq_ref,
                         xo_ref, q_ref, pnew_ref, xbnew_ref, hout_ref):
    x = x_ref[...]
    z = _dot(_rms(x, gmix_ref[...]).astype(BF16), win_ref[...])
    gb = z[:, 0:D_CONV]
    p = z[:, D_CONV:2 * D_CONV] * z[:, 2 * D_CONV:3 * D_CONV]
    xb = z[:, 3 * D_CONV:3 * D_CONV + D_LRU]
    gate = z[:, 3 * D_CONV + D_LRU:]

    ya = gb * (sc0_ref[...] * wsc_ref[0:1, :] + sc1_ref[...] * wsc_ref[1:2, :] + p * wsc_ref[2:3, :])
    xc = (lc0_ref[...] * wlc_ref[0:1, :] + lc1_ref[...] * wlc_ref[1:2, :] + lc2_ref[...] * wlc_ref[2:3, :]
          + xb * wlc_ref[3:4, :]) + blc_ref[...]
    a, u = _lru_gates(xc, wg_ref, ba_ref[...], bx_ref[...], lam_ref[...])
    h = a * h0_ref[...] + u
    yb = h * jax.nn.gelu(gate)
    x1 = _mix_out(x, ya, yb, gco_ref[...], glo_ref[...], wout_ref)
    xo_ref[...] = x1
    q_ref[...] = _dot(_rms(x1, gx_ref[...]).astype(BF16), wq_ref[...])
    pnew_ref[...] = p
    xbnew_ref[...] = xb
    hout_ref[...] = h


def _mixer_sample(x, sc, lc, h0, p):
    nb, d = x.shape
    args = [x, sc[:, 0], sc[:, 1], lc[:, 0], lc[:, 1], lc[:, 2], h0,
            p['gmix'], p['win'], p['wsc'], p['wlc'], p['blc'], p['wg'], p['ba'], p['bx'], p['lam'],
            p['gco'], p['glo'], p['wout'], p['gx'], p['wq']]
    return pl.pallas_call(
        _mixer_sample_kernel,
        grid=(1,),
        in_specs=[_param_spec(a) for a in args],
        out_specs=[_const_spec((nb, d)), _const_spec((nb, d)), _const_spec((nb, D_CONV)),
                   _const_spec((nb, D_LRU)), _const_spec((nb, D_LRU))],
        out_shape=[jax.ShapeDtypeStruct((nb, d), F32), jax.ShapeDtypeStruct((nb, d), F32),
                   jax.ShapeDtypeStruct((nb, D_CONV), F32), jax.ShapeDtypeStruct((nb, D_LRU), F32),
                   jax.ShapeDtypeStruct((nb, D_LRU), F32)],
        compiler_params=_cparams("arbitrary"),
        name="mixer_sample",
    )(*map(_param_array, args))


def _xattn_prompt_kernel(x_ref, g_ref, wq_ref, k_ref, v_ref, wo_ref, *rest, cast):
    if cast:
        src_hbm, o_ref, dst_hbm, *cast_scratch = rest
        step = pl.program_id(0) * pl.num_programs(1) + pl.program_id(1)
        _cast_rows(step, pl.num_programs(0) * pl.num_programs(1), src_hbm, dst_hbm, *cast_scratch)
    else:
        o_ref, = rest
    x = x_ref[...]
    q = _dot(_rms(x, g_ref[...]).astype(BF16), wq_ref[...])
    heads = []
    for h in range(N_MEM_HEADS):
        sl = slice(h * MEM_HEAD_DIM, (h + 1) * MEM_HEAD_DIM)
        kh = k_ref[:, sl]
        vh = v_ref[:, sl]
        s = lax.dot_general(q[:, sl].astype(BF16), kh, (((1,), (1,)), ((), ())),
                            preferred_element_type=F32) * (MEM_HEAD_DIM ** -0.5)
        e = jnp.exp(s - jnp.max(s, axis=-1, keepdims=True))
        pr = e / jnp.sum(e, axis=-1, keepdims=True)
        heads.append(_dot(pr.astype(BF16), vh))
    o = jnp.concatenate(heads, axis=-1).astype(BF16)
    o_ref[...] = x + _dot(o, wo_ref[...])


def _xattn_prompt(x, g, wq, k, v, wo, layer, tq=XATTN_ROWS, cast=None):
    nb, t, d = x.shape
    n_mem = k.shape[2]
    kv_spec = pl.BlockSpec((None, None, n_mem, d), lambda b, i: (layer, b, 0, 0))
    args = [x, _param_array(g), _param_array(wq), k, v, _param_array(wo)]
    in_specs = [pl.BlockSpec((None, tq, d), lambda b, i: (b, i, 0)), _param_spec(g),
                _param_spec(wq, resident=True), kv_spec, kv_spec, _param_spec(wo, resident=True)]
    out_specs = [pl.BlockSpec((None, tq, d), lambda b, i: (b, i, 0))]
    out_shape = [jax.ShapeDtypeStruct((nb, t, d), F32)]
    scratch = []
    if cast is not None:
        in_spec, out_spec, shape, scratch = _cast_specs(cast, nb * (t // tq))
        args.append(cast)
        in_specs.append(in_spec)
        out_specs.append(out_spec)
        out_shape.append(shape)
    out = pl.pallas_call(
        functools.partial(_xattn_prompt_kernel, cast=cast is not None),
        grid=(nb, t // tq),
        in_specs=in_specs,
        out_specs=out_specs,
        out_shape=out_shape,
        scratch_shapes=scratch,
        compiler_params=_cparams("arbitrary", "arbitrary"),
        name="xattn_prompt",
    )(*args)
    return out if cast is not None else out[0]


def _kv_proj_kernel(mem_ref, g_ref, wk_ref, wv_ref, k5_ref, v5_ref, kb_ref, vb_ref):
    bb, n_mem, d = mem_ref.shape
    m = _rms(mem_ref[...].reshape(bb * n_mem, d), g_ref[...]).astype(BF16)
    for w_ref, o5_ref, ob_ref in ((wk_ref, k5_ref, kb_ref), (wv_ref, v5_ref, vb_ref)):
        y = _dot(m, w_ref[...]).reshape(bb, n_mem, d)
        ob_ref[...] = y.astype(BF16)
        for h in range(N_MEM_HEADS):
            o5_ref[:, :, h, :] = y[:, :, h * MEM_HEAD_DIM:(h + 1) * MEM_HEAD_DIM]


def _kv_proj(mem, gmem, wk, wv, bb=KV_PROJ_SEQS):
    nb, n_mem, d = mem.shape
    depth = wk.shape[0]
    w_spec = pl.BlockSpec((None, d, d), lambda l, i: (l, 0, 0))
    o5_spec = pl.BlockSpec((None, bb, n_mem, N_MEM_HEADS, MEM_HEAD_DIM), lambda l, i: (l, i, 0, 0, 0))
    ob_spec = pl.BlockSpec((None, bb, n_mem, d), lambda l, i: (l, i, 0, 0))
    o5_shape = jax.ShapeDtypeStruct((depth, nb, n_mem, N_MEM_HEADS, MEM_HEAD_DIM), F32)
    ob_shape = jax.ShapeDtypeStruct((depth, nb, n_mem, d), BF16)
    return pl.pallas_call(
        _kv_proj_kernel,
        grid=(depth, nb // bb),
        in_specs=[pl.BlockSpec((bb, n_mem, d), lambda l, i: (i, 0, 0)),
                  pl.BlockSpec((None, 1, d), lambda l, i: (l, 0, 0)), w_spec, w_spec],
        out_specs=[o5_spec, o5_spec, ob_spec, ob_spec],
        out_shape=[o5_shape, o5_shape, ob_shape, ob_shape],
        compiler_params=_cparams("parallel", "parallel"),
        name="kv_proj",
    )(mem, gmem, wk, wv)


LANE_TILES_PER_HEAD = MEM_HEAD_DIM // LANES
HEAD_ROWS = N_MEM_HEADS * LANE_TILES_PER_HEAD


def _to_head_rows(a):
    lead = a.shape[:-1]
    n = len(lead)
    a = a.reshape(*lead, N_MEM_HEADS, LANE_TILES_PER_HEAD, LANES)
    return jnp.swapaxes(a, n, n + 1).reshape(*lead, HEAD_ROWS, LANES)


def _from_head_rows(a):
    lead = a.shape[:-2]
    n = len(lead)
    a = a.reshape(*lead, LANE_TILES_PER_HEAD, N_MEM_HEADS, LANES)
    return jnp.swapaxes(a, n, n + 1).reshape(*lead, N_MEM_HEADS * MEM_HEAD_DIM)


def _xattn_sample_kernel(q_ref, k_ref, v_ref, o_ref):
    n_mem = k_ref.shape[1]
    ones = jnp.ones((LANES, LANES), BF16)
    for b in range(q_ref.shape[0]):
        prod = (k_ref[b] * q_ref[b][None]).reshape(n_mem * HEAD_ROWS, LANES).astype(BF16)
        part = _dot(prod, ones).reshape(n_mem, HEAD_ROWS, LANES)
        s = (part + pltpu.roll(part, N_MEM_HEADS, axis=1)) * (MEM_HEAD_DIM ** -0.5)
        e = jnp.exp(s - jnp.max(s, axis=0, keepdims=True))
        pr = e / jnp.sum(e, axis=0, keepdims=True)
        o_ref[b] = jnp.sum(pr * v_ref[b], axis=0)


def _xattn_sample(q, k8, v8, layer, bb=SAMPLE_ATTN_SEQS):
    nb, d = q.shape
    n_mem = k8.shape[2]
    kv_spec = pl.BlockSpec((None, bb, n_mem, HEAD_ROWS, LANES), lambda i: (layer, i, 0, 0, 0))
    out = pl.pallas_call(
        _xattn_sample_kernel,
        grid=(nb // bb,),
        in_specs=[pl.BlockSpec((bb, HEAD_ROWS, LANES), lambda i: (i, 0, 0)), kv_spec, kv_spec],
        out_specs=pl.BlockSpec((bb, HEAD_ROWS, LANES), lambda i: (i, 0, 0)),
        out_shape=jax.ShapeDtypeStruct((nb, HEAD_ROWS, LANES), F32),
        compiler_params=_cparams("parallel"),
        name="xattn_sample",
    )(_to_head_rows(q), k8, v8)
    return _from_head_rows(out)


def _split_bf16(a):
    hi = a.astype(BF16)
    return hi, (a - hi.astype(F32)).astype(BF16)


def _top2(logits):
    n = logits.shape[-1]
    lane = lax.broadcasted_iota(jnp.int32, logits.shape, 1)
    m1 = jnp.max(logits, axis=-1, keepdims=True)
    i1 = jnp.min(jnp.where(logits == m1, lane, n), axis=-1, keepdims=True)
    rest = jnp.where(lane == i1, -jnp.inf, logits)
    m2 = jnp.max(rest, axis=-1, keepdims=True)
    i2 = jnp.min(jnp.where(rest == m2, lane, n), axis=-1, keepdims=True)
    e2 = jnp.exp(m2 - m1)
    denom = 1.0 + e2
    return i1, i2, 1.0 / denom, e2 / denom


def _add_attn_proj(x, rest, proj):
    if not proj:
        return x, rest
    a_ref, wo_ref = rest[:2]
    return x + _dot(a_ref[...].astype(BF16), wo_ref[...]), rest[2:]


def _ffn_dense_kernel(x_ref, g_ref, *rest, final, proj, cast):
    x, rest = _add_attn_proj(x_ref[...], rest, proj)
    if final:
        gfin_ref, rest = rest[0], rest[1:]
    if cast:
        wg_ref, wu_ref, wd_ref, src_hbm, o_ref, dst_hbm, *cast_scratch = rest
        _cast_rows(pl.program_id(0), pl.num_programs(0), src_hbm, dst_hbm, *cast_scratch)
    else:
        wg_ref, wu_ref, wd_ref, o_ref = rest
    h = _rms(x, g_ref[...]).astype(BF16)
    act = (jax.nn.silu(_dot(h, wg_ref[...])) * _dot(h, wu_ref[...])).astype(BF16)
    out = x + _dot(act, wd_ref[...])
    o_ref[...] = _rms(out, gfin_ref[...]) if final else out


def _ffn_dense(x, g, wg, wu, wd, tm, g_final=None, attn=None, cast=None):
    r, d = x.shape
    final, proj = g_final is not None, attn is not None
    args, specs = [x, g], [pl.BlockSpec((tm, d), lambda i: (i, 0)), _param_spec(g)]
    if proj:
        args += list(attn)
        specs += [pl.BlockSpec((tm, attn[0].shape[1]), lambda i: (i, 0)), _param_spec(attn[1], resident=True)]
    if final:
        args.append(g_final)
        specs.append(_param_spec(g_final))
    args += [wg, wu, wd]
    specs += [_param_spec(w, resident=True) for w in (wg, wu, wd)]
    out_specs = [pl.BlockSpec((tm, d), lambda i: (i, 0))]
    out_shape = [jax.ShapeDtypeStruct((r, d), F32)]
    scratch = []
    if cast is not None:
        in_spec, out_spec, shape, scratch = _cast_specs(cast, r // tm)
        args.append(cast)
        specs.append(in_spec)
        out_specs.append(out_spec)
        out_shape.append(shape)
    out = pl.pallas_call(
        functools.partial(_ffn_dense_kernel, final=final, proj=proj, cast=cast is not None),
        grid=(r // tm,),
        in_specs=specs,
        out_specs=out_specs,
        out_shape=out_shape,
        scratch_shapes=scratch,
        compiler_params=_cparams("arbitrary"),
        name="ffn_dense",
    )(*map(_param_array, args))
    return out if cast is not None else out[0]


def _moe_all_experts_kernel(x_ref, g_ref, wr_ref, *rest, final, proj):
    if proj:
        attn_refs, rest = rest[:2], rest[2:]
    if final:
        gfin_ref, rest = rest[0], rest[1:]
    wg_ref, wu_ref, wd_ref, o_ref, h_ref, acc_ref, i1_ref, i2_ref, g1_ref, g2_ref = rest
    e = pl.program_id(1)

    @pl.when(e == 0)
    def _():
        x = x_ref[...]
        if proj:
            x, _ = _add_attn_proj(x, attn_refs, proj)
        h = _rms(x, g_ref[...])
        h_ref[...] = h.astype(BF16)
        acc_ref[...] = x
        h_hi, h_lo = _split_bf16(h)
        w_hi, w_lo = _split_bf16(wr_ref[...])
        logits = _dot(h_hi, w_hi) + (_dot(h_hi, w_lo) + _dot(h_lo, w_hi))
        lane = lax.broadcasted_iota(jnp.int32, logits.shape, 1)
        logits = jnp.where(lane < N_EXPERTS, logits, -jnp.inf)
        i1_ref[...], i2_ref[...], g1_ref[...], g2_ref[...] = _top2(logits)

    gt, up = _gate_up(h_ref[...], wg_ref, wu_ref)
    y = _dot((jax.nn.silu(gt) * up).astype(BF16), wd_ref[...])
    c = jnp.where(i1_ref[...] == e, g1_ref[...], 0.0) + jnp.where(i2_ref[...] == e, g2_ref[...], 0.0)
    acc_ref[...] += c * y

    @pl.when(e == pl.num_programs(1) - 1)
    def _():
        out = acc_ref[...]
        o_ref[...] = _rms(out, gfin_ref[...]) if final else out


def _moe_all_experts(x, g, w_router, wg, wu, wd, tm, g_final=None, attn=None):
    r, d = x.shape
    n_e, f, _ = wd.shape
    final, proj = g_final is not None, attn is not None
    args = [x, g, w_router]
    specs = [pl.BlockSpec((tm, d), lambda i, e: (i, 0)), _param_spec(g), _param_spec(w_router)]
    if proj:
        args += list(attn)
        specs += [pl.BlockSpec((tm, attn[0].shape[1]), lambda i, e: (i, 0)), _param_spec(attn[1])]
    if final:
        args.append(g_final)
        specs.append(_param_spec(g_final))
    args += [wg, wu, wd]
    specs += [pl.BlockSpec((None, d, f), lambda i, e: (e, 0, 0)),
              pl.BlockSpec((None, d, f), lambda i, e: (e, 0, 0)),
              pl.BlockSpec((None, f, d), lambda i, e: (e, 0, 0))]
    return pl.pallas_call(
        functools.partial(_moe_all_experts_kernel, final=final, proj=proj),
        grid=(r // tm, n_e),
        in_specs=specs,
        out_specs=pl.BlockSpec((tm, d), lambda i, e: (i, 0)),
        out_shape=jax.ShapeDtypeStruct((r, d), F32),
        scratch_shapes=[pltpu.VMEM((tm, d), BF16), pltpu.VMEM((tm, d), F32),
                        pltpu.VMEM((tm, 1), jnp.int32), pltpu.VMEM((tm, 1), jnp.int32),
                        pltpu.VMEM((tm, 1), F32), pltpu.VMEM((tm, 1), F32)],
        compiler_params=_cparams("parallel", "arbitrary"),
        name="moe_all_experts",
    )(*map(_param_array, args))


MOE_TILE = 512
TOP_K = 2
ROUTE_TILES_PER_STEP = 4
ROW_ALIGN = SUBLANES
COMPACT_ROWS = -(-(TOP_K * MOE_TILE + N_EXPERTS * (ROW_ALIGN - 1)) // (2 * SUBLANES)) * (2 * SUBLANES)
SEGMENT_BITS = (MOE_TILE // ROW_ALIGN).bit_length()


def _route_kernel(x_ref, g_ref, wrt_ref, earlier_ref, lp_ref, col_ref, cnt_ref):
    n_e = wrt_ref.shape[0]
    w_hi, w_lo = _split_bf16(wrt_ref[...])
    nt = (((1,), (1,)), ((), ()))
    dg = lambda a, b: lax.dot_general(a, b, nt, preferred_element_type=F32)
    for k in range(lp_ref.shape[0]):
        tile = slice(k * MOE_TILE, (k + 1) * MOE_TILE)
        h = _rms(x_ref[tile, :], g_ref[...])
        h_hi, h_lo = _split_bf16(h)
        logits = dg(w_hi, h_hi) + (dg(w_hi, h_lo) + dg(w_lo, h_hi))
        sub = lax.broadcasted_iota(jnp.int32, logits.shape, 0)
        m1 = jnp.max(logits, axis=0, keepdims=True)
        i1 = jnp.min(jnp.where(logits == m1, sub, n_e), axis=0, keepdims=True)
        rest = jnp.where(sub == i1, -jnp.inf, logits)
        m2 = jnp.max(rest, axis=0, keepdims=True)
        i2 = jnp.min(jnp.where(rest == m2, sub, n_e), axis=0, keepdims=True)
        e2 = jnp.exp(m2 - m1)
        denom = 1.0 + e2

        pick1 = jnp.where(sub == i1, 1.0, 0.0)
        pick2 = jnp.where(sub == i2, 1.0, 0.0)
        cnt = pick1 + pick2
        rank = _dot(cnt.astype(BF16), earlier_ref[...])
        n = jnp.sum(cnt, axis=1, keepdims=True)
        n_pad = jnp.floor((n + (ROW_ALIGN - 1)) * (1.0 / ROW_ALIGN)) * ROW_ALIGN
        base, acc = [], jnp.zeros((1, 1), F32)
        for e in range(n_e):
            base.append(acc)
            acc = acc + n_pad[e:e + 1, :]
        pos = rank + jnp.concatenate(base, axis=0)
        lp1 = jnp.sum(pick1 * pos, axis=0, keepdims=True)
        lp2 = jnp.sum(pick2 * pos, axis=0, keepdims=True)
        lp_ref[k, 0:1, :] = lp1.astype(jnp.int32)
        lp_ref[k, 1:2, :] = lp2.astype(jnp.int32)
        info = jnp.concatenate([lp1, lp2, 1.0 / denom, e2 / denom,
                                jnp.zeros((SUBLANES - 4, MOE_TILE), F32)], axis=0)
        col_ref[tile, :] = info.T
        cnt_ref[k] = jnp.broadcast_to(n, cnt_ref.shape[1:])


def _route(x, g, wrt):
    t, d = x.shape
    n_e = wrt.shape[0]
    n_tiles = t // MOE_TILE
    idx = jnp.arange(MOE_TILE)
    earlier = (idx[:, None] < idx[None, :]).astype(BF16)
    return pl.pallas_call(
        _route_kernel,
        grid=(n_tiles // ROUTE_TILES_PER_STEP,),
        in_specs=[pl.BlockSpec((ROUTE_TILES_PER_STEP * MOE_TILE, d), lambda j: (j, 0)), _param_spec(g),
                  _const_spec(wrt.shape), _const_spec(earlier.shape)],
        out_specs=[pl.BlockSpec((ROUTE_TILES_PER_STEP, TOP_K, MOE_TILE), lambda j: (j, 0, 0)),
                   pl.BlockSpec((ROUTE_TILES_PER_STEP * MOE_TILE, SUBLANES), lambda j: (j, 0)),
                   pl.BlockSpec((ROUTE_TILES_PER_STEP, n_e, LANES), lambda j: (j, 0, 0))],
        out_shape=[jax.ShapeDtypeStruct((n_tiles, TOP_K, MOE_TILE), jnp.int32),
                   jax.ShapeDtypeStruct((t, SUBLANES), F32),
                   jax.ShapeDtypeStruct((n_tiles, n_e, LANES), F32)],
        compiler_params=_cparams("parallel"),
        name="moe_route",
    )(x, _param_array(g), wrt, earlier)


def _segment_copies(src_ref, src_row, dst_ref, dst_row, groups, sem):
    out = []
    for bit in range(SEGMENT_BITS):
        rows = ROW_ALIGN << bit
        before = ((groups >> (bit + 1)) << (bit + 1)) * ROW_ALIGN
        cp = pltpu.make_async_copy(
            src_ref.at[pl.ds(pl.multiple_of(src_row + before, ROW_ALIGN), rows), :],
            dst_ref.at[pl.ds(pl.multiple_of(dst_row + before, ROW_ALIGN), rows), :], sem)
        out.append((((groups >> bit) & 1) == 1, cp))
    return out


def _start_all(copies):
    for pred, cp in copies:
        pl.when(pred)(cp.start)


def _wait_all(copies):
    for pred, cp in copies:
        pl.when(pred)(cp.wait)


def _dispatch_kernel(grp_ref, off_ref, dst_ref, pad_ref, x_ref, g_ref, lp_ref, xs_ref, comp_ref, sem):
    j = pl.program_id(0)
    n_steps = pl.num_programs(0)
    n_e = grp_ref.shape[1]
    slot = j % 2

    def copies(step, s):
        out = []
        for e in range(n_e):
            out += _segment_copies(comp_ref.at[s], off_ref[step, e], xs_ref, dst_ref[step, e],
                                   grp_ref[step, e], sem.at[s])
        return out

    h = _rms(x_ref[...], g_ref[...]).astype(BF16)
    row = lax.broadcasted_iota(jnp.int32, (COMPACT_ROWS, h.shape[0]), 0)
    onehot = jnp.where(row == lp_ref[0:1, :], 1.0, jnp.where(row == lp_ref[1:2, :], 1.0, 0.0)).astype(BF16)
    compact = _dot(onehot, h)

    @pl.when(j >= 2)
    def _():
        _wait_all(copies(j - 2, slot))

    comp_ref[slot] = compact
    _start_all(copies(j, slot))

    @pl.when(j == n_steps - 1)
    def _():
        _wait_all(copies(j, slot))

        @pl.when(j >= 1)
        def _():
            _wait_all(copies(j - 1, 1 - slot))

        comp_ref[slot, 0:MOE_TILE, :] = jnp.zeros((MOE_TILE, comp_ref.shape[2]), F32)
        tail = []
        for e in range(n_e):
            tail += _segment_copies(comp_ref.at[slot], 0, xs_ref, pad_ref[0, e], pad_ref[1, e], sem.at[slot])
        tile_groups = MOE_TILE // ROW_ALIGN
        groups_left = (xs_ref.shape[0] - pad_ref[2, 0]) // ROW_ALIGN
        min_total = TOP_K * x_ref.shape[0] * grp_ref.shape[0]
        for c in range((xs_ref.shape[0] - min_total) // MOE_TILE):
            tail += _segment_copies(comp_ref.at[slot], 0, xs_ref, pad_ref[2, 0] + c * MOE_TILE,
                                    jnp.clip(groups_left - c * tile_groups, 0, tile_groups), sem.at[slot])
        _start_all(tail)
        _wait_all(tail)


def _dispatch(x, g, lp, groups, off, dst, pad, n_rows):
    t, d = x.shape
    n_tiles = t // MOE_TILE
    grid_spec = pltpu.PrefetchScalarGridSpec(
        num_scalar_prefetch=4,
        grid=(n_tiles,),
        in_specs=[pl.BlockSpec((MOE_TILE, d), lambda j, *_: (j, 0)),
                  _param_spec(g),
                  pl.BlockSpec((None, TOP_K, MOE_TILE), lambda j, *_: (j, 0, 0))],
        out_specs=pl.BlockSpec(memory_space=pl.ANY),
        scratch_shapes=[pltpu.VMEM((2, COMPACT_ROWS, d), F32), pltpu.SemaphoreType.DMA((2,))])
    return pl.pallas_call(
        _dispatch_kernel,
        grid_spec=grid_spec,
        out_shape=jax.ShapeDtypeStruct((n_rows, d), F32),
        compiler_params=_cparams("arbitrary"),
        name="moe_dispatch",
    )(groups, off, dst, pad, x, _param_array(g), lp)


STEP_FFN, STEP_ZERO = 1, 2


def _grouped_ffn_kernel(grp_ref, mode_ref, xtile_ref, xs_ref, wg_ref, wu_ref, wd_ref, ys_ref):
    w = pl.program_id(0)

    @pl.when(mode_ref[w] == STEP_ZERO)
    def _():
        ys_ref[...] = jnp.zeros_like(ys_ref)

    @pl.when(mode_ref[w] == STEP_FFN)
    def _():
        gt, up = _gate_up(xs_ref[...].astype(BF16), wg_ref, wu_ref)
        ys_ref[...] = _dot((jax.nn.silu(gt) * up).astype(BF16), wd_ref[...])


def _grouped_ffn(xs, wg, wu, wd, grp, mode, xtile):
    n_e, f, d = wd.shape
    grid_spec = pltpu.PrefetchScalarGridSpec(
        num_scalar_prefetch=3,
        grid=(xs.shape[0] // MOE_TILE,),
        in_specs=[pl.BlockSpec((MOE_TILE, d), lambda w, grp, mode, xtile: (xtile[w], 0)),
                  pl.BlockSpec((None, d, f), lambda w, grp, *_: (grp[w], 0, 0)),
                  pl.BlockSpec((None, d, f), lambda w, grp, *_: (grp[w], 0, 0)),
                  pl.BlockSpec((None, f, d), lambda w, grp, *_: (grp[w], 0, 0))],
        out_specs=pl.BlockSpec((MOE_TILE, d), lambda w, *_: (w, 0)))
    return pl.pallas_call(
        _grouped_ffn_kernel,
        grid_spec=grid_spec,
        out_shape=jax.ShapeDtypeStruct(xs.shape, F32),
        compiler_params=_cparams("arbitrary"),
        name="moe_grouped_ffn",
    )(grp, mode, xtile, xs, wg, wu, wd)


def _combine_kernel(grp_ref, off_ref, dst_ref, x_ref, col_ref, *rest, final):
    if final:
        gfin_ref, rest = rest[0], rest[1:]
    ys_ref, o_ref, comp_ref, sem = rest
    j = pl.program_id(0)
    n_steps = pl.num_programs(0)
    n_e = grp_ref.shape[1]
    slot = j % 2

    def copies(step, s):
        out = []
        for e in range(n_e):
            out += _segment_copies(ys_ref, dst_ref[step, e], comp_ref.at[s], off_ref[step, e],
                                   grp_ref[step, e], sem.at[s])
        return out

    @pl.when(j == 0)
    def _():
        comp_ref[...] = jnp.zeros_like(comp_ref)
        _start_all(copies(0, 0))

    @pl.when(j + 1 < n_steps)
    def _():
        _start_all(copies(j + 1, 1 - slot))

    _wait_all(copies(j, slot))
    y = comp_ref[slot].astype(BF16)
    col = col_ref[...]
    row = lax.broadcasted_iota(jnp.int32, (col.shape[0], COMPACT_ROWS), 1)
    weights = jnp.zeros(row.shape, F32)
    for k in range(TOP_K):
        weights = jnp.where(row == col[:, k:k + 1].astype(jnp.int32), col[:, TOP_K + k:TOP_K + k + 1], weights)
    out = x_ref[...] + _dot(weights.astype(BF16), y)
    o_ref[...] = _rms(out, gfin_ref[...]) if final else out


def _combine(x, col, ys, groups, off, dst, g_final):
    t, d = x.shape
    n_tiles = t // MOE_TILE
    final = g_final is not None
    args = [x, col]
    specs = [pl.BlockSpec((MOE_TILE, d), lambda j, *_: (j, 0)),
             pl.BlockSpec((MOE_TILE, SUBLANES), lambda j, *_: (j, 0))]
    if final:
        args.append(g_final)
        specs.append(_param_spec(g_final))
    args.append(ys)
    specs.append(pl.BlockSpec(memory_space=pl.ANY))
    grid_spec = pltpu.PrefetchScalarGridSpec(
        num_scalar_prefetch=3,
        grid=(n_tiles,),
        in_specs=specs,
        out_specs=pl.BlockSpec((MOE_TILE, d), lambda j, *_: (j, 0)),
        scratch_shapes=[pltpu.VMEM((2, COMPACT_ROWS, d), F32), pltpu.SemaphoreType.DMA((2,))])
    return pl.pallas_call(
        functools.partial(_combine_kernel, final=final),
        grid_spec=grid_spec,
        out_shape=jax.ShapeDtypeStruct((t, d), F32),
        compiler_params=_cparams("arbitrary"),
        name="moe_combine",
    )(groups, off, dst, *map(_param_array, args))


def _moe_routed(x, g, w_router, wg, wu, wd, g_final):
    t, _ = x.shape
    n_e = wd.shape[0]
    n_tiles = t // MOE_TILE
    lp, col, cnt = _route(x, g, w_router.T)
    groups = (cnt[:, :, 0].astype(jnp.int32) + (ROW_ALIGN - 1)) // ROW_ALIGN
    n = groups * ROW_ALIGN
    off = jnp.cumsum(n, axis=1) - n
    per_e = jnp.sum(n, axis=0)
    tiles_e = (per_e + (MOE_TILE - 1)) // MOE_TILE
    tile_end = jnp.cumsum(tiles_e)
    starts = (tile_end - tiles_e) * MOE_TILE
    dst = starts[None, :] + jnp.cumsum(n, axis=0) - n
    pad = jnp.stack([starts + per_e, (tiles_e * MOE_TILE - per_e) // ROW_ALIGN,
                     jnp.broadcast_to(tile_end[-1] * MOE_TILE, per_e.shape)])
    max_rows = TOP_K * t + n_tiles * n_e * (ROW_ALIGN - 1)
    n_row_tiles = -(-max_rows // MOE_TILE) + n_e
    xs = _dispatch(x, g, lp, groups, off, dst, pad, n_row_tiles * MOE_TILE)

    w = jnp.arange(n_row_tiles, dtype=jnp.int32)
    used = w < tile_end[-1]
    grp = jnp.minimum(jnp.sum(w[:, None] >= tile_end[None, :], axis=1), n_e - 1).astype(jnp.int32)
    mode = jnp.where(used, STEP_FFN, STEP_ZERO).astype(jnp.int32)
    xtile = jnp.minimum(w, tile_end[-1] - 1)
    ys = _grouped_ffn(xs, wg, wu, wd, grp, mode, xtile)
    return _combine(x, col, ys, groups, off, dst, g_final)


def _block_diag_gates(w_a, w_x):
    per_group = LRU_GROUP // LRU_HEAD_DIM
    eye = jnp.eye(per_group, dtype=w_a.dtype)

    def bd(w):
        w = w.reshape(D_LRU // LRU_GROUP, per_group, LRU_HEAD_DIM, LRU_HEAD_DIM)
        return jnp.einsum('ghij,hk->ghikj', w, eye).reshape(D_LRU // LRU_GROUP, LRU_GROUP, LRU_GROUP)

    return jnp.concatenate([bd(w_a), bd(w_x)], axis=-1).astype(BF16)


def kernel(x_prompt, x_sample, state_shortconv, state_lru_conv, state_lru_h, cache_mem_k, cache_mem_v,
           mem_prompt, norm_mix, w_in, w_short_conv, w_lru_conv, b_lru_conv, w_lru_a, b_lru_a, w_lru_x,
           b_lru_x, lru_lambda, norm_conv_out, norm_lru_out, w_mix_out, norm_xattn, norm_mem, w_q, w_k,
           w_v, w_o, norm_ffn, w_ff_gate, w_ff_up, w_ff_down, w_router, w_moe_gate, w_moe_up, w_moe_down,
           norm_final):
    depth = w_in.shape[0]
    nb, t, d = x_prompt.shape
    ns = x_sample.shape[0]
    n_mem = mem_prompt.shape[1]
    rows = lambda a: a.reshape(a.shape[0], 1, -1)
    stacks = dict(
        gmix=rows(norm_mix), win=w_in.astype(BF16), wsc=w_short_conv, wlc=w_lru_conv, blc=rows(b_lru_conv),
        wg=jax.vmap(_block_diag_gates)(w_lru_a, w_lru_x), ba=rows(b_lru_a), bx=rows(b_lru_x),
        lam=rows(lru_lambda), gco=rows(norm_conv_out), glo=rows(norm_lru_out), wout=w_mix_out.astype(BF16),
        gx=rows(norm_xattn), wq=w_q.astype(BF16), wo=w_o.astype(BF16), gffn=rows(norm_ffn))
    g_final = norm_final.reshape(1, -1)
    ff_dense = (w_ff_gate.astype(BF16), w_ff_up.astype(BF16), w_ff_down.astype(BF16))
    ff_moe_f32 = (w_moe_gate, w_moe_up, w_moe_down)

    layers = []
    for l in range(depth):
        lp = {name: (stack, l) for name, stack in stacks.items()}
        lp['gfin'] = g_final if l == depth - 1 else None
        j = l // 2
        if l % 2 == 0:
            lp['ffw'] = tuple((w, j) for w in ff_dense)
            lp['wr'] = None
        else:
            lp['ffw'] = None
            lp['wr'] = jnp.pad(w_router[j], ((0, 0), (0, LANES - N_EXPERTS)))
            lp['wr_raw'] = w_router[j]
        layers.append(lp)

    x = x_prompt
    p_mem_k, p_mem_v, kb, vb = _kv_proj(mem_prompt, norm_mem.reshape(depth, 1, d), w_k.astype(BF16),
                                        w_v.astype(BF16))
    pc, plc, ph = [], [], []
    moe_bf16 = {}

    def cast_for(j, k):
        w = ff_moe_f32[k][j]
        return w.reshape(-1, w.shape[-1])

    for l, lp in enumerate(layers):
        x, tail_p, tail_x, h_last = _mixer_prompt(x, lp)
        if l + 1 < depth and lp['wr'] is None:
            x, moe_bf16[(l + 1) // 2, 0] = _xattn_prompt(x, lp['gx'], lp['wq'], kb, vb, lp['wo'], l,
                                                         cast=cast_for((l + 1) // 2, 0))
        elif lp['wr'] is not None:
            x, moe_bf16[l // 2, 1] = _xattn_prompt(x, lp['gx'], lp['wq'], kb, vb, lp['wo'], l,
                                                   cast=cast_for(l // 2, 1))
        else:
            x = _xattn_prompt(x, lp['gx'], lp['wq'], kb, vb, lp['wo'], l)
        if lp['wr'] is None:
            if l + 1 < depth:
                x, moe_bf16[(l + 1) // 2, 2] = _ffn_dense(x.reshape(nb * t, d), lp['gffn'], *lp['ffw'], tm=FFN_ROWS,
                                                          g_final=lp['gfin'], cast=cast_for((l + 1) // 2, 2))
            else:
                x = _ffn_dense(x.reshape(nb * t, d), lp['gffn'], *lp['ffw'], tm=FFN_ROWS, g_final=lp['gfin'])
        else:
            j = l // 2
            lp['ffw'] = tuple(moe_bf16[j, k].reshape(ff_moe_f32[k][j].shape) for k in range(3))
            x = _moe_routed(x.reshape(nb * t, d), lp['gffn'], lp['wr_raw'], *lp['ffw'], lp['gfin'])
        x = x.reshape(nb, t, d)
        tail_p = jnp.swapaxes(tail_p.reshape(MAX_CONV_HISTORY, nb, D_CONV), 0, 1)
        tail_x = jnp.swapaxes(tail_x.reshape(MAX_CONV_HISTORY, nb, D_LRU), 0, 1)
        pc.append(tail_p[:, MAX_CONV_HISTORY - (w_short_conv.shape[1] - 1):])
        plc.append(tail_x[:, MAX_CONV_HISTORY - (w_lru_conv.shape[1] - 1):])
        ph.append(h_last)
    y_prompt = x

    x = x_sample.reshape(ns, d)
    k8 = _to_head_rows(cache_mem_k.reshape(depth, ns, n_mem, d))
    v8 = _to_head_rows(cache_mem_v.reshape(depth, ns, n_mem, d))
    sc, slc, sh = [], [], []
    for l, lp in enumerate(layers):
        x, q, p_new, xb_new, h_new = _mixer_sample(x, state_shortconv[l], state_lru_conv[l], state_lru_h[l], lp)
        attn = (_xattn_sample(q, k8, v8, l), lp['wo'])
        if lp['wr'] is None:
            x = _ffn_dense(x, lp['gffn'], *lp['ffw'], tm=ns, g_final=lp['gfin'], attn=attn)
        else:
            x = _moe_all_experts(x, lp['gffn'], lp['wr'], *lp['ffw'], tm=ns, g_final=lp['gfin'], attn=attn)
        sc.append(jnp.stack([state_shortconv[l][:, 1], p_new], axis=1))
        slc.append(jnp.stack([state_lru_conv[l][:, 1], state_lru_conv[l][:, 2], xb_new], axis=1))
        sh.append(h_new)
    y_sample = x.reshape(ns, 1, d)

    return (y_prompt, y_sample, jnp.stack(pc), jnp.stack(plc), jnp.stack(ph), p_mem_k, p_mem_v,
            jnp.stack(sc), jnp.stack(slc), jnp.stack(sh))
```

```python
import functools

import jax
import jax.numpy as jnp
from jax import lax
from jax.experimental import pallas as pl
from jax.experimental.pallas import tpu as pltpu

F32 = jnp.float32
BF16 = jnp.bfloat16

EPS = 1e-6
LRU_C = 8.0
D_CONV = 512
D_LRU = 512
LRU_HEAD_DIM = 64
LRU_GROUP = 256
N_MEM_HEADS = 4
MEM_HEAD_DIM = 256
N_EXPERTS = 8
SUBLANES = 8
LANES = 128
MXU_COLS = 256
VMEM_LIMIT_BYTES = 56 * 1024 * 1024

MIXER_TIME_CHUNK = 128
XATTN_ROWS = 1024
FFN_ROWS = 512
KV_PROJ_SEQS = 2
SAMPLE_ATTN_SEQS = 8


def _cparams(*sem):
    return pltpu.CompilerParams(dimension_semantics=sem, vmem_limit_bytes=VMEM_LIMIT_BYTES)


def _rms(x, g):
    return x * lax.rsqrt(jnp.mean(x * x, axis=-1, keepdims=True) + EPS) * g


def _dot(a, b):
    return jnp.dot(a, b, preferred_element_type=F32)


def _gate_up(x, wg_ref, wu_ref):
    f = wg_ref.shape[1]
    main = f // MXU_COLS * MXU_COLS
    if main == f or 2 * (f - main) != MXU_COLS:
        return _dot(x, wg_ref[...]), _dot(x, wu_ref[...])
    rest = _dot(x, jnp.concatenate([wg_ref[:, main:], wu_ref[:, main:]], axis=1))
    gt = jnp.concatenate([_dot(x, wg_ref[:, :main]), rest[:, :f - main]], axis=1)
    up = jnp.concatenate([_dot(x, wu_ref[:, :main]), rest[:, f - main:]], axis=1)
    return gt, up


def _lru_gates(xc, wg_ref, ba, bx, lam):
    neg_lam = -lam
    softplus = jnp.maximum(neg_lam, 0.0) + jnp.log1p(jnp.exp(-jnp.abs(neg_lam)))
    a_parts, u_parts = [], []
    for g in range(D_LRU // LRU_GROUP):
        sl = slice(g * LRU_GROUP, (g + 1) * LRU_GROUP)
        xg = xc[:, sl]
        gates = _dot(xg.astype(BF16), wg_ref[g])
        r = jax.nn.sigmoid(gates[:, :LRU_GROUP] + ba[:, sl])
        i = jax.nn.sigmoid(gates[:, LRU_GROUP:] + bx[:, sl])
        log_a = (-LRU_C) * r * softplus[:, sl]
        th = jnp.tanh(log_a)
        one_minus_a2 = (-2.0 * th) / (1.0 - th)
        a_parts.append(jnp.exp(log_a))
        u_parts.append(jnp.sqrt(one_minus_a2) * (i * xg))
    return jnp.concatenate(a_parts, axis=-1), jnp.concatenate(u_parts, axis=-1)


def _mix_out(x, ya, yb, gco, glo, wout_ref):
    y = jnp.concatenate([_rms(ya, gco), _rms(yb, glo)], axis=-1).astype(BF16)
    return x + _dot(y, wout_ref[...])


MAX_CONV_HISTORY = 3


def _conv_taps(ext_ref, w_ref, rows, nb):
    width = w_ref.shape[0]
    hist = MAX_CONV_HISTORY * nb
    acc = ext_ref[hist:hist + rows, :] * w_ref[width - 1:width, :]
    for s in range(1, width):
        acc = acc + ext_ref[hist - s * nb:hist - s * nb + rows, :] * w_ref[width - 1 - s:width - s, :]
    return acc


def _mixer_prompt_kernel(x_hbm, gmix_ref, win_ref, wsc_ref, wlc_ref, blc_ref, wg_ref, ba_ref, bx_ref,
                         lam_ref, gco_ref, glo_ref, wout_ref,
                         xo_hbm, tailp_ref, tailx_ref, hout_ref,
                         xin_ref, xout_ref, in_sem, out_sem, pext_ref, xext_ref, h_ref, a_ref, u_ref):
    _, tt, nb, d = xin_ref.shape
    rows = tt * nb
    hist = MAX_CONV_HISTORY * nb
    i = pl.program_id(0)
    n_steps = pl.num_programs(0)
    slot = i % 2

    def in_copies(step, s):
        return [pltpu.make_async_copy(x_hbm.at[b, pl.ds(step * tt, tt), :], xin_ref.at[s, :, b, :],
                                      in_sem.at[s]) for b in range(nb)]

    def out_copies(step, s):
        return [pltpu.make_async_copy(xout_ref.at[s, :, b, :], xo_hbm.at[b, pl.ds(step * tt, tt), :],
                                      out_sem.at[s]) for b in range(nb)]

    @pl.when(i == 0)
    def _():
        pext_ref[0:hist, :] = jnp.zeros((hist, pext_ref.shape[1]), F32)
        xext_ref[0:hist, :] = jnp.zeros((hist, xext_ref.shape[1]), F32)
        h_ref[...] = jnp.zeros_like(h_ref)
        for cp in in_copies(0, 0):
            cp.start()

    @pl.when(i + 1 < n_steps)
    def _():
        for cp in in_copies(i + 1, 1 - slot):
            cp.start()

    for cp in in_copies(i, slot):
        cp.wait()
    x = xin_ref[slot].reshape(rows, d)
    z = _dot(_rms(x, gmix_ref[...]).astype(BF16), win_ref[...])
    gb = z[:, 0:D_CONV]
    gate = z[:, 3 * D_CONV + D_LRU:]
    pext_ref[hist:hist + rows, :] = z[:, D_CONV:2 * D_CONV] * z[:, 2 * D_CONV:3 * D_CONV]
    xext_ref[hist:hist + rows, :] = z[:, 3 * D_CONV:3 * D_CONV + D_LRU]

    ya = gb * _conv_taps(pext_ref, wsc_ref, rows, nb)
    xc = _conv_taps(xext_ref, wlc_ref, rows, nb) + blc_ref[...]
    for ext_ref, tail_ref in ((pext_ref, tailp_ref), (xext_ref, tailx_ref)):
        tail = ext_ref[rows:rows + hist, :]
        ext_ref[0:hist, :] = tail
        keep = tail_ref.shape[1]
        for k in range(keep):
            step0 = (MAX_CONV_HISTORY - keep + k) * nb
            tail_ref[:, k, :] = tail[step0:step0 + nb, :]

    a, u = _lru_gates(xc, wg_ref, ba_ref[...], bx_ref[...], lam_ref[...])
    a_ref[...] = a
    u_ref[...] = u

    def step(t, h):
        grp = pl.ds(pl.multiple_of(t * nb, nb), nb)
        h = a_ref[grp, :] * h + u_ref[grp, :]
        u_ref[grp, :] = h
        return h

    h_last = lax.fori_loop(0, tt, step, h_ref[...], unroll=8)
    h_ref[...] = h_last
    hout_ref[...] = h_last

    yb = u_ref[...] * jax.nn.gelu(gate)
    out = _mix_out(x, ya, yb, gco_ref[...], glo_ref[...], wout_ref)

    @pl.when(i >= 2)
    def _():
        for cp in out_copies(i - 2, slot):
            cp.wait()

    xout_ref[slot] = out.reshape(tt, nb, d)
    for cp in out_copies(i, slot):
        cp.start()

    @pl.when(i == n_steps - 1)
    def _():
        for cp in out_copies(i, slot):
            cp.wait()

        @pl.when(i >= 1)
        def _():
            for cp in out_copies(i - 1, 1 - slot):
                cp.wait()


def _const_spec(shape):
    return pl.BlockSpec(shape, lambda *_: (0,) * len(shape))


def _param_array(p):
    return p[0] if isinstance(p, tuple) else p


def _param_spec(p, resident=False):
    a, layer = p if isinstance(p, tuple) else (p, None)
    mode = dict(pipeline_mode=pl.Buffered(1)) if resident else {}
    if layer is None:
        return pl.BlockSpec(a.shape, lambda *_: (0,) * a.ndim, **mode)
    return pl.BlockSpec((None,) + a.shape[1:], lambda *_: (layer,) + (0,) * (a.ndim - 1), **mode)


def _cast_specs(w, n_steps):
    rows = w.shape[0] // n_steps
    assert rows * n_steps == w.shape[0] and rows % (2 * SUBLANES) == 0
    any_spec = pl.BlockSpec(memory_space=pl.ANY)
    scratch = [pltpu.VMEM((2, rows, w.shape[1]), F32), pltpu.VMEM((2, rows, w.shape[1]), BF16),
               pltpu.SemaphoreType.DMA((2,)), pltpu.SemaphoreType.DMA((2,))]
    return any_spec, any_spec, jax.ShapeDtypeStruct(w.shape, BF16), scratch


def _cast_rows(step, n_steps, src_hbm, dst_hbm, in_buf, out_buf, in_sem, out_sem):
    rows = in_buf.shape[1]
    slot = step % 2

    def in_copy(s, sl):
        return pltpu.make_async_copy(src_hbm.at[pl.ds(pl.multiple_of(s * rows, rows), rows), :],
                                     in_buf.at[sl], in_sem.at[sl])

    def out_copy(s, sl):
        return pltpu.make_async_copy(out_buf.at[sl],
                                     dst_hbm.at[pl.ds(pl.multiple_of(s * rows, rows), rows), :], out_sem.at[sl])

    pl.when(step == 0)(in_copy(0, 0).start)
    pl.when(step + 1 < n_steps)(in_copy(step + 1, 1 - slot).start)
    in_copy(step, slot).wait()
    pl.when(step >= 2)(out_copy(step - 2, slot).wait)
    out_buf[slot] = in_buf[slot].astype(BF16)
    out_copy(step, slot).start()

    @pl.when(step == n_steps - 1)
    def _():
        out_copy(step, slot).wait()
        pl.when(step >= 1)(out_copy(step - 1, 1 - slot).wait)


def _mixer_prompt(x, p, tt=MIXER_TIME_CHUNK):
    nb, t, d = x.shape
    small = [p['gmix'], p['win'], p['wsc'], p['wlc'], p['blc'], p['wg'], p['ba'], p['bx'], p['lam'],
             p['gco'], p['glo'], p['wout']]
    rows = nb * tt
    hist = MAX_CONV_HISTORY * nb
    conv_state = (nb, _param_array(p['wsc']).shape[1] - 1, D_CONV)
    lru_state = (nb, _param_array(p['wlc']).shape[1] - 1, D_LRU)
    return pl.pallas_call(
        _mixer_prompt_kernel,
        grid=(t // tt,),
        in_specs=[pl.BlockSpec(memory_space=pl.ANY)] + [_param_spec(a, resident=True) for a in small],
        out_specs=[pl.BlockSpec(memory_space=pl.ANY),
                   _const_spec(conv_state), _const_spec(lru_state), _const_spec((nb, D_LRU))],
        out_shape=[jax.ShapeDtypeStruct((nb, t, d), F32),
                   jax.ShapeDtypeStruct(conv_state, F32),
                   jax.ShapeDtypeStruct(lru_state, F32),
                   jax.ShapeDtypeStruct((nb, D_LRU), F32)],
        scratch_shapes=[pltpu.VMEM((2, tt, nb, d), F32), pltpu.VMEM((2, tt, nb, d), F32),
                        pltpu.SemaphoreType.DMA((2,)), pltpu.SemaphoreType.DMA((2,)),
                        pltpu.VMEM((hist + rows, D_CONV), F32), pltpu.VMEM((hist + rows, D_LRU), F32),
                        pltpu.VMEM((nb, D_LRU), F32),
                        pltpu.VMEM((rows, D_LRU), F32), pltpu.VMEM((rows, D_LRU), F32)],
        compiler_params=_cparams("arbitrary"),
        name="mixer_prompt",
    )(x, *map(_param_array, small))


def _mixer_sample_kernel(x_ref, sc_ref, lc_ref, h0_ref,
                         gmix_ref, win_ref, wsc_ref, wlc_ref, blc_ref, wg_ref, ba_ref, bx_ref,
                         lam_ref, gco_ref, glo_ref, wout_ref, gx_ref, wq_ref,
                         xo_ref, q_ref, sc_new_ref, lc_new_ref, hout_ref):
    x = x_ref[...]
    z = _dot(_rms(x, gmix_ref[...]).astype(BF16), win_ref[...])
    gb = z[:, 0:D_CONV]
    p = z[:, D_CONV:2 * D_CONV] * z[:, 2 * D_CONV:3 * D_CONV]
    xb = z[:, 3 * D_CONV:3 * D_CONV + D_LRU]
    gate = z[:, 3 * D_CONV + D_LRU:]

    sc = [sc_ref[:, k, :] for k in range(sc_ref.shape[1])] + [p]
    lc = [lc_ref[:, k, :] for k in range(lc_ref.shape[1])] + [xb]
    ya = gb * sum(v * wsc_ref[k:k + 1, :] for k, v in enumerate(sc))
    xc = sum(v * wlc_ref[k:k + 1, :] for k, v in enumerate(lc)) + blc_ref[...]
    for k, v in enumerate(sc[1:]):
        sc_new_ref[:, k, :] = v
    for k, v in enumerate(lc[1:]):
        lc_new_ref[:, k, :] = v
    a, u = _lru_gates(xc, wg_ref, ba_ref[...], bx_ref[...], lam_ref[...])
    h = a * h0_ref[...] + u
    yb = h * jax.nn.gelu(gate)
    x1 = _mix_out(x, ya, yb, gco_ref[...], glo_ref[...], wout_ref)
    xo_ref[...] = x1
    q = _dot(_rms(x1, gx_ref[...]).astype(BF16), wq_ref[...])
    for r in range(HEAD_ROWS):
        col0 = (r % N_MEM_HEADS) * MEM_HEAD_DIM + (r // N_MEM_HEADS) * LANES
        q_ref[:, r, :] = q[:, col0:col0 + LANES]
    hout_ref[...] = h


def _mixer_sample(x, sc, lc, h0, p):
    nb, d = x.shape
    sc_shape, lc_shape = _param_array(sc).shape[-3:], _param_array(lc).shape[-3:]
    args = [x, sc, lc, h0,
            p['gmix'], p['win'], p['wsc'], p['wlc'], p['blc'], p['wg'], p['ba'], p['bx'], p['lam'],
            p['gco'], p['glo'], p['wout'], p['gx'], p['wq']]
    return pl.pallas_call(
        _mixer_sample_kernel,
        grid=(1,),
        in_specs=[_param_spec(a) for a in args],
        out_specs=[_const_spec((nb, d)), _const_spec((nb, HEAD_ROWS, LANES)), _const_spec(sc_shape),
                   _const_spec(lc_shape), _const_spec((nb, D_LRU))],
        out_shape=[jax.ShapeDtypeStruct((nb, d), F32), jax.ShapeDtypeStruct((nb, HEAD_ROWS, LANES), F32),
                   jax.ShapeDtypeStruct(sc_shape, F32), jax.ShapeDtypeStruct(lc_shape, F32),
                   jax.ShapeDtypeStruct((nb, D_LRU), F32)],
        compiler_params=_cparams("arbitrary"),
        name="mixer_sample",
    )(*map(_param_array, args))


def _xattn_prompt_kernel(x_ref, g_ref, wq_ref, k_ref, v_ref, wo_ref, *rest, cast):
    if cast:
        src_hbm, o_ref, dst_hbm, *cast_scratch = rest
        step = pl.program_id(0) * pl.num_programs(1) + pl.program_id(1)
        _cast_rows(step, pl.num_programs(0) * pl.num_programs(1), src_hbm, dst_hbm, *cast_scratch)
    else:
        o_ref, = rest
    x = x_ref[...]
    q = _dot(_rms(x, g_ref[...]).astype(BF16), wq_ref[...])
    heads = []
    for h in range(N_MEM_HEADS):
        sl = slice(h * MEM_HEAD_DIM, (h + 1) * MEM_HEAD_DIM)
        kh = k_ref[:, sl]
        vh = v_ref[:, sl]
        s = lax.dot_general(q[:, sl].astype(BF16), kh, (((1,), (1,)), ((), ())),
                            preferred_element_type=F32) * (MEM_HEAD_DIM ** -0.5)
        e = jnp.exp(s - jnp.max(s, axis=-1, keepdims=True))
        pr = e / jnp.sum(e, axis=-1, keepdims=True)
        heads.append(_dot(pr.astype(BF16), vh))
    o = jnp.concatenate(heads, axis=-1).astype(BF16)
    o_ref[...] = x + _dot(o, wo_ref[...])


def _xattn_prompt(x, g, wq, k, v, wo, layer, tq=XATTN_ROWS, cast=None):
    nb, t, d = x.shape
    n_mem = k.shape[2]
    kv_spec = pl.BlockSpec((None, None, n_mem, d), lambda b, i: (layer, b, 0, 0))
    args = [x, _param_array(g), _param_array(wq), k, v, _param_array(wo)]
    in_specs = [pl.BlockSpec((None, tq, d), lambda b, i: (b, i, 0)), _param_spec(g),
                _param_spec(wq, resident=True), kv_spec, kv_spec, _param_spec(wo, resident=True)]
    out_specs = [pl.BlockSpec((None, tq, d), lambda b, i: (b, i, 0))]
    out_shape = [jax.ShapeDtypeStruct((nb, t, d), F32)]
    scratch = []
    if cast is not None:
        in_spec, out_spec, shape, scratch = _cast_specs(cast, nb * (t // tq))
        args.append(cast)
        in_specs.append(in_spec)
        out_specs.append(out_spec)
        out_shape.append(shape)
    out = pl.pallas_call(
        functools.partial(_xattn_prompt_kernel, cast=cast is not None),
        grid=(nb, t // tq),
        in_specs=in_specs,
        out_specs=out_specs,
        out_shape=out_shape,
        scratch_shapes=scratch,
        compiler_params=_cparams("arbitrary", "arbitrary"),
        name="xattn_prompt",
    )(*args)
    return out if cast is not None else out[0]


def _kv_proj_kernel(mem_ref, g_ref, wk_ref, wv_ref, k5_ref, v5_ref, kb_ref, vb_ref):
    bb, n_mem, d = mem_ref.shape
    m = _rms(mem_ref[...].reshape(bb * n_mem, d), g_ref[...]).astype(BF16)
    for w_ref, o5_ref, ob_ref in ((wk_ref, k5_ref, kb_ref), (wv_ref, v5_ref, vb_ref)):
        y = _dot(m, w_ref[...]).reshape(bb, n_mem, d)
        ob_ref[...] = y.astype(BF16)
        for h in range(N_MEM_HEADS):
            o5_ref[:, :, h, :] = y[:, :, h * MEM_HEAD_DIM:(h + 1) * MEM_HEAD_DIM]


def _kv_proj(mem, gmem, wk, wv, bb=KV_PROJ_SEQS):
    nb, n_mem, d = mem.shape
    depth = wk.shape[0]
    w_spec = pl.BlockSpec((None, d, d), lambda l, i: (l, 0, 0))
    o5_spec = pl.BlockSpec((None, bb, n_mem, N_MEM_HEADS, MEM_HEAD_DIM), lambda l, i: (l, i, 0, 0, 0))
    ob_spec = pl.BlockSpec((None, bb, n_mem, d), lambda l, i: (l, i, 0, 0))
    o5_shape = jax.ShapeDtypeStruct((depth, nb, n_mem, N_MEM_HEADS, MEM_HEAD_DIM), F32)
    ob_shape = jax.ShapeDtypeStruct((depth, nb, n_mem, d), BF16)
    return pl.pallas_call(
        _kv_proj_kernel,
        grid=(depth, nb // bb),
        in_specs=[pl.BlockSpec((bb, n_mem, d), lambda l, i: (i, 0, 0)),
                  pl.BlockSpec((None, 1, d), lambda l, i: (l, 0, 0)), w_spec, w_spec],
        out_specs=[o5_spec, o5_spec, ob_spec, ob_spec],
        out_shape=[o5_shape, o5_shape, ob_shape, ob_shape],
        compiler_params=_cparams("parallel", "parallel"),
        name="kv_proj",
    )(mem, gmem, wk, wv)


LANE_TILES_PER_HEAD = MEM_HEAD_DIM // LANES
HEAD_ROWS = N_MEM_HEADS * LANE_TILES_PER_HEAD


def _to_head_rows(a):
    lead = a.shape[:-1]
    n = len(lead)
    a = a.reshape(*lead, N_MEM_HEADS, LANE_TILES_PER_HEAD, LANES)
    return jnp.swapaxes(a, n, n + 1).reshape(*lead, HEAD_ROWS, LANES)


def _xattn_sample_kernel(q_ref, k_ref, v_ref, o_ref):
    n_mem = k_ref.shape[1]
    ones = jnp.ones((LANES, LANES), BF16)
    for b in range(q_ref.shape[0]):
        prod = (k_ref[b] * q_ref[b][None]).reshape(n_mem * HEAD_ROWS, LANES).astype(BF16)
        part = _dot(prod, ones).reshape(n_mem, HEAD_ROWS, LANES)
        s = (part + pltpu.roll(part, N_MEM_HEADS, axis=1)) * (MEM_HEAD_DIM ** -0.5)
        e = jnp.exp(s - jnp.max(s, axis=0, keepdims=True))
        pr = e / jnp.sum(e, axis=0, keepdims=True)
        o_ref[b] = jnp.sum(pr * v_ref[b], axis=0)


def _xattn_sample(q, k8, v8, layer, bb=SAMPLE_ATTN_SEQS):
    nb = q.shape[0]
    n_mem = k8.shape[2]
    kv_spec = pl.BlockSpec((None, bb, n_mem, HEAD_ROWS, LANES), lambda i: (layer, i, 0, 0, 0))
    out = pl.pallas_call(
        _xattn_sample_kernel,
        grid=(nb // bb,),
        in_specs=[pl.BlockSpec((bb, HEAD_ROWS, LANES), lambda i: (i, 0, 0)), kv_spec, kv_spec],
        out_specs=pl.BlockSpec((bb, HEAD_ROWS, LANES), lambda i: (i, 0, 0)),
        out_shape=jax.ShapeDtypeStruct((nb, HEAD_ROWS, LANES), F32),
        compiler_params=_cparams("parallel"),
        name="xattn_sample",
    )(q, k8, v8)
    return out


def _split_bf16(a):
    hi = a.astype(BF16)
    return hi, (a - hi.astype(F32)).astype(BF16)


def _top2(logits):
    n = logits.shape[-1]
    lane = lax.broadcasted_iota(jnp.int32, logits.shape, 1)
    m1 = jnp.max(logits, axis=-1, keepdims=True)
    i1 = jnp.min(jnp.where(logits == m1, lane, n), axis=-1, keepdims=True)
    rest = jnp.where(lane == i1, -jnp.inf, logits)
    m2 = jnp.max(rest, axis=-1, keepdims=True)
    i2 = jnp.min(jnp.where(rest == m2, lane, n), axis=-1, keepdims=True)
    e2 = jnp.exp(m2 - m1)
    denom = 1.0 + e2
    return i1, i2, 1.0 / denom, e2 / denom


def _add_attn_proj(x, rest, proj):
    if not proj:
        return x, rest
    a_ref, wo_ref = rest[:2]
    heads = [a_ref[:, dt * N_MEM_HEADS + h, :] for h in range(N_MEM_HEADS) for dt in range(LANE_TILES_PER_HEAD)]
    a = jnp.concatenate(heads, axis=-1).astype(BF16)
    return x + _dot(a, wo_ref[...]), rest[2:]


def _ffn_dense_kernel(x_ref, g_ref, *rest, final, proj, cast):
    x, rest = _add_attn_proj(x_ref[...], rest, proj)
    if final:
        gfin_ref, rest = rest[0], rest[1:]
    if cast:
        wg_ref, wu_ref, wd_ref, src_hbm, o_ref, dst_hbm, *cast_scratch = rest
        _cast_rows(pl.program_id(0), pl.num_programs(0), src_hbm, dst_hbm, *cast_scratch)
    else:
        wg_ref, wu_ref, wd_ref, o_ref = rest
    h = _rms(x, g_ref[...]).astype(BF16)
    act = (jax.nn.silu(_dot(h, wg_ref[...])) * _dot(h, wu_ref[...])).astype(BF16)
    out = x + _dot(act, wd_ref[...])
    o_ref[...] = _rms(out, gfin_ref[...]) if final else out


def _ffn_dense(x, g, wg, wu, wd, tm, g_final=None, attn=None, cast=None):
    r, d = x.shape
    final, proj = g_final is not None, attn is not None
    args, specs = [x, g], [pl.BlockSpec((tm, d), lambda i: (i, 0)), _param_spec(g)]
    if proj:
        args += list(attn)
        specs += [pl.BlockSpec((tm,) + attn[0].shape[1:], lambda i: (i, 0, 0)), _param_spec(attn[1], resident=True)]
    if final:
        args.append(g_final)
        specs.append(_param_spec(g_final))
    args += [wg, wu, wd]
    specs += [_param_spec(w, resident=True) for w in (wg, wu, wd)]
    out_specs = [pl.BlockSpec((tm, d), lambda i: (i, 0))]
    out_shape = [jax.ShapeDtypeStruct((r, d), F32)]
    scratch = []
    if cast is not None:
        in_spec, out_spec, shape, scratch = _cast_specs(cast, r // tm)
        args.append(cast)
        specs.append(in_spec)
        out_specs.append(out_spec)
        out_shape.append(shape)
    out = pl.pallas_call(
        functools.partial(_ffn_dense_kernel, final=final, proj=proj, cast=cast is not None),
        grid=(r // tm,),
        in_specs=specs,
        out_specs=out_specs,
        out_shape=out_shape,
        scratch_shapes=scratch,
        compiler_params=_cparams("arbitrary"),
        name="ffn_dense",
    )(*map(_param_array, args))
    return out if cast is not None else out[0]


def _moe_all_experts_kernel(x_ref, g_ref, wr_ref, *rest, final, proj):
    if proj:
        attn_refs, rest = rest[:2], rest[2:]
    if final:
        gfin_ref, rest = rest[0], rest[1:]
    wg_ref, wu_ref, wd_ref, o_ref, h_ref, acc_ref, i1_ref, i2_ref, g1_ref, g2_ref = rest
    e = pl.program_id(1)

    @pl.when(e == 0)
    def _():
        x = x_ref[...]
        if proj:
            x, _ = _add_attn_proj(x, attn_refs, proj)
        h = _rms(x, g_ref[...])
        h_ref[...] = h.astype(BF16)
        acc_ref[...] = x
        h_hi, h_lo = _split_bf16(h)
        w_hi, w_lo = _split_bf16(wr_ref[...])
        logits = _dot(h_hi, w_hi) + (_dot(h_hi, w_lo) + _dot(h_lo, w_hi))
        lane = lax.broadcasted_iota(jnp.int32, logits.shape, 1)
        logits = jnp.where(lane < N_EXPERTS, logits, -jnp.inf)
        i1_ref[...], i2_ref[...], g1_ref[...], g2_ref[...] = _top2(logits)

    gt, up = _gate_up(h_ref[...], wg_ref, wu_ref)
    y = _dot((jax.nn.silu(gt) * up).astype(BF16), wd_ref[...])
    c = jnp.where(i1_ref[...] == e, g1_ref[...], 0.0) + jnp.where(i2_ref[...] == e, g2_ref[...], 0.0)
    acc_ref[...] += c * y

    @pl.when(e == pl.num_programs(1) - 1)
    def _():
        out = acc_ref[...]
        o_ref[...] = _rms(out, gfin_ref[...]) if final else out


def _moe_all_experts(x, g, w_router, wg, wu, wd, tm, g_final=None, attn=None):
    r, d = x.shape
    n_e, f, _ = wd.shape
    final, proj = g_final is not None, attn is not None
    args = [x, g, w_router]
    specs = [pl.BlockSpec((tm, d), lambda i, e: (i, 0)), _param_spec(g), _param_spec(w_router)]
    if proj:
        args += list(attn)
        specs += [pl.BlockSpec((tm,) + attn[0].shape[1:], lambda i, e: (i, 0, 0)), _param_spec(attn[1])]
    if final:
        args.append(g_final)
        specs.append(_param_spec(g_final))
    args += [wg, wu, wd]
    specs += [pl.BlockSpec((None, d, f), lambda i, e: (e, 0, 0)),
              pl.BlockSpec((None, d, f), lambda i, e: (e, 0, 0)),
              pl.BlockSpec((None, f, d), lambda i, e: (e, 0, 0))]
    return pl.pallas_call(
        functools.partial(_moe_all_experts_kernel, final=final, proj=proj),
        grid=(r // tm, n_e),
        in_specs=specs,
        out_specs=pl.BlockSpec((tm, d), lambda i, e: (i, 0)),
        out_shape=jax.ShapeDtypeStruct((r, d), F32),
        scratch_shapes=[pltpu.VMEM((tm, d), BF16), pltpu.VMEM((tm, d), F32),
                        pltpu.VMEM((tm, 1), jnp.int32), pltpu.VMEM((tm, 1), jnp.int32),
                        pltpu.VMEM((tm, 1), F32), pltpu.VMEM((tm, 1), F32)],
        compiler_params=_cparams("parallel", "arbitrary"),
        name="moe_all_experts",
    )(*map(_param_array, args))


MOE_TILE = 512
TOP_K = 2
ROUTE_TILES_PER_STEP = 4
ROW_ALIGN = SUBLANES
COMPACT_ROWS = -(-(TOP_K * MOE_TILE + N_EXPERTS * (ROW_ALIGN - 1)) // (2 * SUBLANES)) * (2 * SUBLANES)
SEGMENT_BITS = (MOE_TILE // ROW_ALIGN).bit_length()


def _route_kernel(x_ref, g_ref, wrt_ref, earlier_ref, lp_ref, col_ref, cnt_ref):
    n_e = wrt_ref.shape[0]
    w_hi, w_lo = _split_bf16(wrt_ref[...])
    nt = (((1,), (1,)), ((), ()))
    dg = lambda a, b: lax.dot_general(a, b, nt, preferred_element_type=F32)
    for k in range(lp_ref.shape[0]):
        tile = slice(k * MOE_TILE, (k + 1) * MOE_TILE)
        h = _rms(x_ref[tile, :], g_ref[...])
        h_hi, h_lo = _split_bf16(h)
        logits = dg(w_hi, h_hi) + (dg(w_hi, h_lo) + dg(w_lo, h_hi))
        sub = lax.broadcasted_iota(jnp.int32, logits.shape, 0)
        m1 = jnp.max(logits, axis=0, keepdims=True)
        i1 = jnp.min(jnp.where(logits == m1, sub, n_e), axis=0, keepdims=True)
        rest = jnp.where(sub == i1, -jnp.inf, logits)
        m2 = jnp.max(rest, axis=0, keepdims=True)
        i2 = jnp.min(jnp.where(rest == m2, sub, n_e), axis=0, keepdims=True)
        e2 = jnp.exp(m2 - m1)
        denom = 1.0 + e2

        pick1 = jnp.where(sub == i1, 1.0, 0.0)
        pick2 = jnp.where(sub == i2, 1.0, 0.0)
        cnt = pick1 + pick2
        rank = _dot(cnt.astype(BF16), earlier_ref[...])
        n = jnp.sum(cnt, axis=1, keepdims=True)
        n_pad = jnp.floor((n + (ROW_ALIGN - 1)) * (1.0 / ROW_ALIGN)) * ROW_ALIGN
        base, acc = [], jnp.zeros((1, 1), F32)
        for e in range(n_e):
            base.append(acc)
            acc = acc + n_pad[e:e + 1, :]
        pos = rank + jnp.concatenate(base, axis=0)
        lp1 = jnp.sum(pick1 * pos, axis=0, keepdims=True)
        lp2 = jnp.sum(pick2 * pos, axis=0, keepdims=True)
        lp_ref[k, 0:1, :] = lp1.astype(jnp.int32)
        lp_ref[k, 1:2, :] = lp2.astype(jnp.int32)
        info = jnp.concatenate([lp1, lp2, 1.0 / denom, e2 / denom,
                                jnp.zeros((SUBLANES - 4, MOE_TILE), F32)], axis=0)
        col_ref[tile, :] = info.T
        cnt_ref[k] = jnp.broadcast_to(n, cnt_ref.shape[1:])


def _route(x, g, wrt):
    t, d = x.shape
    n_e = wrt.shape[0]
    n_tiles = t // MOE_TILE
    idx = jnp.arange(MOE_TILE)
    earlier = (idx[:, None] < idx[None, :]).astype(BF16)
    return pl.pallas_call(
        _route_kernel,
        grid=(n_tiles // ROUTE_TILES_PER_STEP,),
        in_specs=[pl.BlockSpec((ROUTE_TILES_PER_STEP * MOE_TILE, d), lambda j: (j, 0)), _param_spec(g),
                  _const_spec(wrt.shape), _const_spec(earlier.shape)],
        out_specs=[pl.BlockSpec((ROUTE_TILES_PER_STEP, TOP_K, MOE_TILE), lambda j: (j, 0, 0)),
                   pl.BlockSpec((ROUTE_TILES_PER_STEP * MOE_TILE, SUBLANES), lambda j: (j, 0)),
                   pl.BlockSpec((ROUTE_TILES_PER_STEP, n_e, LANES), lambda j: (j, 0, 0))],
        out_shape=[jax.ShapeDtypeStruct((n_tiles, TOP_K, MOE_TILE), jnp.int32),
                   jax.ShapeDtypeStruct((t, SUBLANES), F32),
                   jax.ShapeDtypeStruct((n_tiles, n_e, LANES), F32)],
        compiler_params=_cparams("parallel"),
        name="moe_route",
    )(x, _param_array(g), wrt, earlier)


def _segment_copies(src_ref, src_row, dst_ref, dst_row, groups, sem):
    out = []
    for bit in range(SEGMENT_BITS):
        rows = ROW_ALIGN << bit
        before = ((groups >> (bit + 1)) << (bit + 1)) * ROW_ALIGN
        cp = pltpu.make_async_copy(
            src_ref.at[pl.ds(pl.multiple_of(src_row + before, ROW_ALIGN), rows), :],
            dst_ref.at[pl.ds(pl.multiple_of(dst_row + before, ROW_ALIGN), rows), :], sem)
        out.append((((groups >> bit) & 1) == 1, cp))
    return out


def _start_all(copies):
    for pred, cp in copies:
        pl.when(pred)(cp.start)


def _wait_all(copies):
    for pred, cp in copies:
        pl.when(pred)(cp.wait)


def _dispatch_kernel(grp_ref, off_ref, dst_ref, pad_ref, x_ref, g_ref, lp_ref, xs_ref, comp_ref, sem):
    j = pl.program_id(0)
    n_steps = pl.num_programs(0)
    n_e = grp_ref.shape[1]
    slot = j % 2

    def copies(step, s):
        out = []
        for e in range(n_e):
            out += _segment_copies(comp_ref.at[s], off_ref[step, e], xs_ref, dst_ref[step, e],
                                   grp_ref[step, e], sem.at[s])
        return out

    h = _rms(x_ref[...], g_ref[...]).astype(BF16)
    row = lax.broadcasted_iota(jnp.int32, (COMPACT_ROWS, h.shape[0]), 0)
    onehot = jnp.where(row == lp_ref[0:1, :], 1.0, jnp.where(row == lp_ref[1:2, :], 1.0, 0.0)).astype(BF16)
    compact = _dot(onehot, h)

    @pl.when(j >= 2)
    def _():
        _wait_all(copies(j - 2, slot))

    comp_ref[slot] = compact
    _start_all(copies(j, slot))

    @pl.when(j == n_steps - 1)
    def _():
        _wait_all(copies(j, slot))

        @pl.when(j >= 1)
        def _():
            _wait_all(copies(j - 1, 1 - slot))

        comp_ref[slot, 0:MOE_TILE, :] = jnp.zeros((MOE_TILE, comp_ref.shape[2]), F32)
        tail = []
        for e in range(n_e):
            tail += _segment_copies(comp_ref.at[slot], 0, xs_ref, pad_ref[0, e], pad_ref[1, e], sem.at[slot])
        tile_groups = MOE_TILE // ROW_ALIGN
        groups_left = (xs_ref.shape[0] - pad_ref[2, 0]) // ROW_ALIGN
        min_total = TOP_K * x_ref.shape[0] * grp_ref.shape[0]
        for c in range((xs_ref.shape[0] - min_total) // MOE_TILE):
            tail += _segment_copies(comp_ref.at[slot], 0, xs_ref, pad_ref[2, 0] + c * MOE_TILE,
                                    jnp.clip(groups_left - c * tile_groups, 0, tile_groups), sem.at[slot])
        _start_all(tail)
        _wait_all(tail)


def _dispatch(x, g, lp, groups, off, dst, pad, n_rows):
    t, d = x.shape
    n_tiles = t // MOE_TILE
    grid_spec = pltpu.PrefetchScalarGridSpec(
        num_scalar_prefetch=4,
        grid=(n_tiles,),
        in_specs=[pl.BlockSpec((MOE_TILE, d), lambda j, *_: (j, 0)),
                  _param_spec(g),
                  pl.BlockSpec((None, TOP_K, MOE_TILE), lambda j, *_: (j, 0, 0))],
        out_specs=pl.BlockSpec(memory_space=pl.ANY),
        scratch_shapes=[pltpu.VMEM((2, COMPACT_ROWS, d), F32), pltpu.SemaphoreType.DMA((2,))])
    return pl.pallas_call(
        _dispatch_kernel,
        grid_spec=grid_spec,
        out_shape=jax.ShapeDtypeStruct((n_rows, d), F32),
        compiler_params=_cparams("arbitrary"),
        name="moe_dispatch",
    )(groups, off, dst, pad, x, _param_array(g), lp)


STEP_FFN, STEP_ZERO = 1, 2


def _grouped_ffn_kernel(grp_ref, mode_ref, xtile_ref, xs_ref, wg_ref, wu_ref, wd_ref, ys_ref):
    w = pl.program_id(0)

    @pl.when(mode_ref[w] == STEP_ZERO)
    def _():
        ys_ref[...] = jnp.zeros_like(ys_ref)

    @pl.when(mode_ref[w] == STEP_FFN)
    def _():
        gt, up = _gate_up(xs_ref[...].astype(BF16), wg_ref, wu_ref)
        ys_ref[...] = _dot((jax.nn.silu(gt) * up).astype(BF16), wd_ref[...])


def _grouped_ffn(xs, wg, wu, wd, grp, mode, xtile):
    n_e, f, d = wd.shape
    grid_spec = pltpu.PrefetchScalarGridSpec(
        num_scalar_prefetch=3,
        grid=(xs.shape[0] // MOE_TILE,),
        in_specs=[pl.BlockSpec((MOE_TILE, d), lambda w, grp, mode, xtile: (xtile[w], 0)),
                  pl.BlockSpec((None, d, f), lambda w, grp, *_: (grp[w], 0, 0)),
                  pl.BlockSpec((None, d, f), lambda w, grp, *_: (grp[w], 0, 0)),
                  pl.BlockSpec((None, f, d), lambda w, grp, *_: (grp[w], 0, 0))],
        out_specs=pl.BlockSpec((MOE_TILE, d), lambda w, *_: (w, 0)))
    return pl.pallas_call(
        _grouped_ffn_kernel,
        grid_spec=grid_spec,
        out_shape=jax.ShapeDtypeStruct(xs.shape, F32),
        compiler_params=_cparams("arbitrary"),
        name="moe_grouped_ffn",
    )(grp, mode, xtile, xs, wg, wu, wd)


def _combine_kernel(grp_ref, off_ref, dst_ref, x_ref, col_ref, *rest, final):
    if final:
        gfin_ref, rest = rest[0], rest[1:]
    ys_ref, o_ref, comp_ref, sem = rest
    j = pl.program_id(0)
    n_steps = pl.num_programs(0)
    n_e = grp_ref.shape[1]
    slot = j % 2

    def copies(step, s):
        out = []
        for e in range(n_e):
            out += _segment_copies(ys_ref, dst_ref[step, e], comp_ref.at[s], off_ref[step, e],
                                   grp_ref[step, e], sem.at[s])
        return out

    @pl.when(j == 0)
    def _():
        comp_ref[...] = jnp.zeros_like(comp_ref)
        _start_all(copies(0, 0))

    @pl.when(j + 1 < n_steps)
    def _():
        _start_all(copies(j + 1, 1 - slot))

    _wait_all(copies(j, slot))
    y = comp_ref[slot].astype(BF16)
    col = col_ref[...]
    row = lax.broadcasted_iota(jnp.int32, (col.shape[0], COMPACT_ROWS), 1)
    weights = jnp.zeros(row.shape, F32)
    for k in range(TOP_K):
        weights = jnp.where(row == col[:, k:k + 1].astype(jnp.int32), col[:, TOP_K + k:TOP_K + k + 1], weights)
    out = x_ref[...] + _dot(weights.astype(BF16), y)
    o_ref[...] = _rms(out, gfin_ref[...]) if final else out


def _combine(x, col, ys, groups, off, dst, g_final):
    t, d = x.shape
    n_tiles = t // MOE_TILE
    final = g_final is not None
    args = [x, col]
    specs = [pl.BlockSpec((MOE_TILE, d), lambda j, *_: (j, 0)),
             pl.BlockSpec((MOE_TILE, SUBLANES), lambda j, *_: (j, 0))]
    if final:
        args.append(g_final)
        specs.append(_param_spec(g_final))
    args.append(ys)
    specs.append(pl.BlockSpec(memory_space=pl.ANY))
    grid_spec = pltpu.PrefetchScalarGridSpec(
        num_scalar_prefetch=3,
        grid=(n_tiles,),
        in_specs=specs,
        out_specs=pl.BlockSpec((MOE_TILE, d), lambda j, *_: (j, 0)),
        scratch_shapes=[pltpu.VMEM((2, COMPACT_ROWS, d), F32), pltpu.SemaphoreType.DMA((2,))])
    return pl.pallas_call(
        functools.partial(_combine_kernel, final=final),
        grid_spec=grid_spec,
        out_shape=jax.ShapeDtypeStruct((t, d), F32),
        compiler_params=_cparams("arbitrary"),
        name="moe_combine",
    )(groups, off, dst, *map(_param_array, args))


def _moe_routed(x, g, w_router, wg, wu, wd, g_final):
    t, _ = x.shape
    n_e = wd.shape[0]
    n_tiles = t // MOE_TILE
    lp, col, cnt = _route(x, g, w_router.T)
    groups = (cnt[:, :, 0].astype(jnp.int32) + (ROW_ALIGN - 1)) // ROW_ALIGN
    n = groups * ROW_ALIGN
    off = jnp.cumsum(n, axis=1) - n
    per_e = jnp.sum(n, axis=0)
    tiles_e = (per_e + (MOE_TILE - 1)) // MOE_TILE
    tile_end = jnp.cumsum(tiles_e)
    starts = (tile_end - tiles_e) * MOE_TILE
    dst = starts[None, :] + jnp.cumsum(n, axis=0) - n
    pad = jnp.stack([starts + per_e, (tiles_e * MOE_TILE - per_e) // ROW_ALIGN,
                     jnp.broadcast_to(tile_end[-1] * MOE_TILE, per_e.shape)])
    max_rows = TOP_K * t + n_tiles * n_e * (ROW_ALIGN - 1)
    n_row_tiles = -(-max_rows // MOE_TILE) + n_e
    xs = _dispatch(x, g, lp, groups, off, dst, pad, n_row_tiles * MOE_TILE)

    w = jnp.arange(n_row_tiles, dtype=jnp.int32)
    used = w < tile_end[-1]
    grp = jnp.minimum(jnp.sum(w[:, None] >= tile_end[None, :], axis=1), n_e - 1).astype(jnp.int32)
    mode = jnp.where(used, STEP_FFN, STEP_ZERO).astype(jnp.int32)
    xtile = jnp.minimum(w, tile_end[-1] - 1)
    ys = _grouped_ffn(xs, wg, wu, wd, grp, mode, xtile)
    return _combine(x, col, ys, groups, off, dst, g_final)


def _block_diag_gates(w_a, w_x):
    per_group = LRU_GROUP // LRU_HEAD_DIM
    eye = jnp.eye(per_group, dtype=w_a.dtype)

    def bd(w):
        w = w.reshape(D_LRU // LRU_GROUP, per_group, LRU_HEAD_DIM, LRU_HEAD_DIM)
        return jnp.einsum('ghij,hk->ghikj', w, eye).reshape(D_LRU // LRU_GROUP, LRU_GROUP, LRU_GROUP)

    return jnp.concatenate([bd(w_a), bd(w_x)], axis=-1).astype(BF16)


def kernel(x_prompt, x_sample, state_shortconv, state_lru_conv, state_lru_h, cache_mem_k, cache_mem_v,
           mem_prompt, norm_mix, w_in, w_short_conv, w_lru_conv, b_lru_conv, w_lru_a, b_lru_a, w_lru_x,
           b_lru_x, lru_lambda, norm_conv_out, norm_lru_out, w_mix_out, norm_xattn, norm_mem, w_q, w_k,
           w_v, w_o, norm_ffn, w_ff_gate, w_ff_up, w_ff_down, w_router, w_moe_gate, w_moe_up, w_moe_down,
           norm_final):
    depth = w_in.shape[0]
    nb, t, d = x_prompt.shape
    ns = x_sample.shape[0]
    n_mem = mem_prompt.shape[1]
    rows = lambda a: a.reshape(a.shape[0], 1, -1)
    stacks = dict(
        gmix=rows(norm_mix), win=w_in.astype(BF16), wsc=w_short_conv, wlc=w_lru_conv, blc=rows(b_lru_conv),
        wg=jax.vmap(_block_diag_gates)(w_lru_a, w_lru_x), ba=rows(b_lru_a), bx=rows(b_lru_x),
        lam=rows(lru_lambda), gco=rows(norm_conv_out), glo=rows(norm_lru_out), wout=w_mix_out.astype(BF16),
        gx=rows(norm_xattn), wq=w_q.astype(BF16), wo=w_o.astype(BF16), gffn=rows(norm_ffn))
    g_final = norm_final.reshape(1, -1)
    ff_dense = (w_ff_gate.astype(BF16), w_ff_up.astype(BF16), w_ff_down.astype(BF16))
    ff_moe_f32 = (w_moe_gate, w_moe_up, w_moe_down)

    layers = []
    for l in range(depth):
        lp = {name: (stack, l) for name, stack in stacks.items()}
        lp['gfin'] = g_final if l == depth - 1 else None
        j = l // 2
        if l % 2 == 0:
            lp['ffw'] = tuple((w, j) for w in ff_dense)
            lp['wr'] = None
        else:
            lp['ffw'] = None
            lp['wr'] = jnp.pad(w_router[j], ((0, 0), (0, LANES - N_EXPERTS)))
            lp['wr_raw'] = w_router[j]
        layers.append(lp)

    x = x_prompt
    p_mem_k, p_mem_v, kb, vb = _kv_proj(mem_prompt, norm_mem.reshape(depth, 1, d), w_k.astype(BF16),
                                        w_v.astype(BF16))
    pc, plc, ph = [], [], []
    moe_bf16 = {}

    def cast_for(j, k):
        w = ff_moe_f32[k][j]
        return w.reshape(-1, w.shape[-1])

    for l, lp in enumerate(layers):
        x, tail_p, tail_x, h_last = _mixer_prompt(x, lp)
        if l + 1 < depth and lp['wr'] is None:
            x, moe_bf16[(l + 1) // 2, 0] = _xattn_prompt(x, lp['gx'], lp['wq'], kb, vb, lp['wo'], l,
                                                         cast=cast_for((l + 1) // 2, 0))
        elif lp['wr'] is not None:
            x, moe_bf16[l // 2, 1] = _xattn_prompt(x, lp['gx'], lp['wq'], kb, vb, lp['wo'], l,
                                                   cast=cast_for(l // 2, 1))
        else:
            x = _xattn_prompt(x, lp['gx'], lp['wq'], kb, vb, lp['wo'], l)
        if lp['wr'] is None:
            if l + 1 < depth:
                x, moe_bf16[(l + 1) // 2, 2] = _ffn_dense(x.reshape(nb * t, d), lp['gffn'], *lp['ffw'], tm=FFN_ROWS,
                                                          g_final=lp['gfin'], cast=cast_for((l + 1) // 2, 2))
            else:
                x = _ffn_dense(x.reshape(nb * t, d), lp['gffn'], *lp['ffw'], tm=FFN_ROWS, g_final=lp['gfin'])
        else:
            j = l // 2
            lp['ffw'] = tuple(moe_bf16[j, k].reshape(ff_moe_f32[k][j].shape) for k in range(3))
            x = _moe_routed(x.reshape(nb * t, d), lp['gffn'], lp['wr_raw'], *lp['ffw'], lp['gfin'])
        x = x.reshape(nb, t, d)
        pc.append(tail_p)
        plc.append(tail_x)
        ph.append(h_last)
    y_prompt = x

    x = x_sample.reshape(ns, d)
    k8 = _to_head_rows(cache_mem_k.reshape(depth, ns, n_mem, d))
    v8 = _to_head_rows(cache_mem_v.reshape(depth, ns, n_mem, d))
    sc, slc, sh = [], [], []
    for l, lp in enumerate(layers):
        x, q, sc_new, lc_new, h_new = _mixer_sample(x, (state_shortconv, l), (state_lru_conv, l),
                                                    (state_lru_h, l), lp)
        attn = (_xattn_sample(q, k8, v8, l), lp['wo'])
        if lp['wr'] is None:
            x = _ffn_dense(x, lp['gffn'], *lp['ffw'], tm=ns, g_final=lp['gfin'], attn=attn)
        else:
            x = _moe_all_experts(x, lp['gffn'], lp['wr'], *lp['ffw'], tm=ns, g_final=lp['gfin'], attn=attn)
        sc.append(sc_new)
        slc.append(lc_new)
        sh.append(h_new)
    y_sample = x.reshape(ns, 1, d)

    return (y_prompt, y_sample, jnp.stack(pc), jnp.stack(plc), jnp.stack(ph), p_mem_k, p_mem_v,
            jnp.stack(sc), jnp.stack(slc), jnp.stack(sh))
```

```python
import functools

import jax
import jax.numpy as jnp
from jax import lax
from jax.experimental import pallas as pl
from jax.experimental.pallas import tpu as pltpu

F32 = jnp.float32
BF16 = jnp.bfloat16

EPS = 1e-6
LRU_C = 8.0
D_CONV = 512
D_LRU = 512
LRU_HEAD_DIM = 64
LRU_GROUP = 256
N_MEM_HEADS = 4
MEM_HEAD_DIM = 256
N_EXPERTS = 8
SUBLANES = 8
LANES = 128
MXU_COLS = 256
VMEM_LIMIT_BYTES = 56 * 1024 * 1024

MIXER_TIME_CHUNK = 128
XATTN_ROWS = 1024
FFN_ROWS = 512
KV_PROJ_SEQS = 2
SAMPLE_ATTN_SEQS = 8


def _cparams(*sem):
    return pltpu.CompilerParams(dimension_semantics=sem, vmem_limit_bytes=VMEM_LIMIT_BYTES)


def _rms(x, g):
    return x * lax.rsqrt(jnp.mean(x * x, axis=-1, keepdims=True) + EPS) * g


def _dot(a, b):
    return jnp.dot(a, b, preferred_element_type=F32)


def _sigmoid(x):
    return 0.5 * jnp.tanh(0.5 * x) + 0.5


def _gate_up(x, wg_ref, wu_ref):
    f = wg_ref.shape[1]
    main = f // MXU_COLS * MXU_COLS
    if main == f or 2 * (f - main) != MXU_COLS:
        return _dot(x, wg_ref[...]), _dot(x, wu_ref[...])
    rest = _dot(x, jnp.concatenate([wg_ref[:, main:], wu_ref[:, main:]], axis=1))
    gt = jnp.concatenate([_dot(x, wg_ref[:, :main]), rest[:, :f - main]], axis=1)
    up = jnp.concatenate([_dot(x, wu_ref[:, :main]), rest[:, f - main:]], axis=1)
    return gt, up


def _lru_gates(xc, wg_ref, ba, bx, lam):
    neg_lam = -lam
    softplus = jnp.maximum(neg_lam, 0.0) + jnp.log1p(jnp.exp(-jnp.abs(neg_lam)))
    a_parts, u_parts = [], []
    for g in range(D_LRU // LRU_GROUP):
        sl = slice(g * LRU_GROUP, (g + 1) * LRU_GROUP)
        xg = xc[:, sl]
        gates = _dot(xg.astype(BF16), wg_ref[g])
        r = _sigmoid(gates[:, :LRU_GROUP] + ba[:, sl])
        i = _sigmoid(gates[:, LRU_GROUP:] + bx[:, sl])
        log_a = (-LRU_C) * r * softplus[:, sl]
        th = jnp.tanh(log_a)
        one_minus_a2 = (-2.0 * th) / (1.0 - th)
        a_parts.append(jnp.exp(log_a))
        u_parts.append(jnp.sqrt(one_minus_a2) * (i * xg))
    return jnp.concatenate(a_parts, axis=-1), jnp.concatenate(u_parts, axis=-1)


def _mix_out(x, ya, yb, gco, glo, wout_ref):
    y = jnp.concatenate([_rms(ya, gco), _rms(yb, glo)], axis=-1).astype(BF16)
    return x + _dot(y, wout_ref[...])


MAX_CONV_HISTORY = 3


def _conv_taps(ext_ref, w_ref, rows, nb):
    width = w_ref.shape[0]
    hist = MAX_CONV_HISTORY * nb
    acc = ext_ref[hist:hist + rows, :] * w_ref[width - 1:width, :]
    for s in range(1, width):
        acc = acc + ext_ref[hist - s * nb:hist - s * nb + rows, :] * w_ref[width - 1 - s:width - s, :]
    return acc


def _mixer_prompt_kernel(x_hbm, gmix_ref, win_ref, wsc_ref, wlc_ref, blc_ref, wg_ref, ba_ref, bx_ref,
                         lam_ref, gco_ref, glo_ref, wout_ref,
                         xo_hbm, tailp_ref, tailx_ref, hout_ref,
                         xin_ref, xout_ref, in_sem, out_sem, pext_ref, xext_ref, h_ref, a_ref, u_ref):
    _, tt, nb, d = xin_ref.shape
    rows = tt * nb
    hist = MAX_CONV_HISTORY * nb
    i = pl.program_id(0)
    n_steps = pl.num_programs(0)
    slot = i % 2

    def in_copies(step, s):
        return [pltpu.make_async_copy(x_hbm.at[b, pl.ds(step * tt, tt), :], xin_ref.at[s, :, b, :],
                                      in_sem.at[s]) for b in range(nb)]

    def out_copies(step, s):
        return [pltpu.make_async_copy(xout_ref.at[s, :, b, :], xo_hbm.at[b, pl.ds(step * tt, tt), :],
                                      out_sem.at[s]) for b in range(nb)]

    @pl.when(i == 0)
    def _():
        pext_ref[0:hist, :] = jnp.zeros((hist, pext_ref.shape[1]), F32)
        xext_ref[0:hist, :] = jnp.zeros((hist, xext_ref.shape[1]), F32)
        h_ref[...] = jnp.zeros_like(h_ref)
        for cp in in_copies(0, 0):
            cp.start()

    @pl.when(i + 1 < n_steps)
    def _():
        for cp in in_copies(i + 1, 1 - slot):
            cp.start()

    for cp in in_copies(i, slot):
        cp.wait()
    x = xin_ref[slot].reshape(rows, d)
    z = _dot(_rms(x, gmix_ref[...]).astype(BF16), win_ref[...])
    gb = z[:, 0:D_CONV]
    gate = z[:, 3 * D_CONV + D_LRU:]
    pext_ref[hist:hist + rows, :] = z[:, D_CONV:2 * D_CONV] * z[:, 2 * D_CONV:3 * D_CONV]
    xext_ref[hist:hist + rows, :] = z[:, 3 * D_CONV:3 * D_CONV + D_LRU]

    ya = gb * _conv_taps(pext_ref, wsc_ref, rows, nb)
    xc = _conv_taps(xext_ref, wlc_ref, rows, nb) + blc_ref[...]
    for ext_ref, tail_ref in ((pext_ref, tailp_ref), (xext_ref, tailx_ref)):
        tail = ext_ref[rows:rows + hist, :]
        ext_ref[0:hist, :] = tail
        keep = tail_ref.shape[1]
        for k in range(keep):
            step0 = (MAX_CONV_HISTORY - keep + k) * nb
            tail_ref[:, k, :] = tail[step0:step0 + nb, :]

    a, u = _lru_gates(xc, wg_ref, ba_ref[...], bx_ref[...], lam_ref[...])
    a_ref[...] = a
    u_ref[...] = u

    def step(t, h):
        grp = pl.ds(pl.multiple_of(t * nb, nb), nb)
        h = a_ref[grp, :] * h + u_ref[grp, :]
        u_ref[grp, :] = h
        return h

    h_last = lax.fori_loop(0, tt, step, h_ref[...], unroll=8)
    h_ref[...] = h_last
    hout_ref[...] = h_last

    yb = u_ref[...] * jax.nn.gelu(gate)
    out = _mix_out(x, ya, yb, gco_ref[...], glo_ref[...], wout_ref)

    @pl.when(i >= 2)
    def _():
        for cp in out_copies(i - 2, slot):
            cp.wait()

    xout_ref[slot] = out.reshape(tt, nb, d)
    for cp in out_copies(i, slot):
        cp.start()

    @pl.when(i == n_steps - 1)
    def _():
        for cp in out_copies(i, slot):
            cp.wait()

        @pl.when(i >= 1)
        def _():
            for cp in out_copies(i - 1, 1 - slot):
                cp.wait()


def _const_spec(shape):
    return pl.BlockSpec(shape, lambda *_: (0,) * len(shape))


def _param_array(p):
    return p[0] if isinstance(p, tuple) else p


def _param_spec(p, resident=False):
    a, layer = p if isinstance(p, tuple) else (p, None)
    mode = dict(pipeline_mode=pl.Buffered(1)) if resident else {}
    if layer is None:
        return pl.BlockSpec(a.shape, lambda *_: (0,) * a.ndim, **mode)
    return pl.BlockSpec((None,) + a.shape[1:], lambda *_: (layer,) + (0,) * (a.ndim - 1), **mode)


def _cast_specs(w, n_steps):
    rows = w.shape[0] // n_steps
    assert rows * n_steps == w.shape[0] and rows % (2 * SUBLANES) == 0
    any_spec = pl.BlockSpec(memory_space=pl.ANY)
    scratch = [pltpu.VMEM((2, rows, w.shape[1]), F32), pltpu.VMEM((2, rows, w.shape[1]), BF16),
               pltpu.SemaphoreType.DMA((2,)), pltpu.SemaphoreType.DMA((2,))]
    return any_spec, any_spec, jax.ShapeDtypeStruct(w.shape, BF16), scratch


def _cast_rows(step, n_steps, src_hbm, dst_hbm, in_buf, out_buf, in_sem, out_sem):
    rows = in_buf.shape[1]
    slot = step % 2

    def in_copy(s, sl):
        return pltpu.make_async_copy(src_hbm.at[pl.ds(pl.multiple_of(s * rows, rows), rows), :],
                                     in_buf.at[sl], in_sem.at[sl])

    def out_copy(s, sl):
        return pltpu.make_async_copy(out_buf.at[sl],
                                     dst_hbm.at[pl.ds(pl.multiple_of(s * rows, rows), rows), :], out_sem.at[sl])

    pl.when(step == 0)(in_copy(0, 0).start)
    pl.when(step + 1 < n_steps)(in_copy(step + 1, 1 - slot).start)
    in_copy(step, slot).wait()
    pl.when(step >= 2)(out_copy(step - 2, slot).wait)
    out_buf[slot] = in_buf[slot].astype(BF16)
    out_copy(step, slot).start()

    @pl.when(step == n_steps - 1)
    def _():
        out_copy(step, slot).wait()
        pl.when(step >= 1)(out_copy(step - 1, 1 - slot).wait)


def _mixer_prompt(x, p, tt=MIXER_TIME_CHUNK):
    nb, t, d = x.shape
    small = [p['gmix'], p['win'], p['wsc'], p['wlc'], p['blc'], p['wg'], p['ba'], p['bx'], p['lam'],
             p['gco'], p['glo'], p['wout']]
    rows = nb * tt
    hist = MAX_CONV_HISTORY * nb
    conv_state = (nb, _param_array(p['wsc']).shape[1] - 1, D_CONV)
    lru_state = (nb, _param_array(p['wlc']).shape[1] - 1, D_LRU)
    return pl.pallas_call(
        _mixer_prompt_kernel,
        grid=(t // tt,),
        in_specs=[pl.BlockSpec(memory_space=pl.ANY)] + [_param_spec(a, resident=True) for a in small],
        out_specs=[pl.BlockSpec(memory_space=pl.ANY),
                   _const_spec(conv_state), _const_spec(lru_state), _const_spec((nb, D_LRU))],
        out_shape=[jax.ShapeDtypeStruct((nb, t, d), F32),
                   jax.ShapeDtypeStruct(conv_state, F32),
                   jax.ShapeDtypeStruct(lru_state, F32),
                   jax.ShapeDtypeStruct((nb, D_LRU), F32)],
        scratch_shapes=[pltpu.VMEM((2, tt, nb, d), F32), pltpu.VMEM((2, tt, nb, d), F32),
                        pltpu.SemaphoreType.DMA((2,)), pltpu.SemaphoreType.DMA((2,)),
                        pltpu.VMEM((hist + rows, D_CONV), F32), pltpu.VMEM((hist + rows, D_LRU), F32),
                        pltpu.VMEM((nb, D_LRU), F32),
                        pltpu.VMEM((rows, D_LRU), F32), pltpu.VMEM((rows, D_LRU), F32)],
        compiler_params=_cparams("arbitrary"),
        name="mixer_prompt",
    )(x, *map(_param_array, small))


def _mixer_sample_kernel(x_ref, sc_ref, lc_ref, h0_ref,
                         gmix_ref, win_ref, wsc_ref, wlc_ref, blc_ref, wg_ref, ba_ref, bx_ref,
                         lam_ref, gco_ref, glo_ref, wout_ref, gx_ref, wq_ref,
                         xo_ref, q_ref, sc_new_ref, lc_new_ref, hout_ref):
    x = x_ref[...]
    z = _dot(_rms(x, gmix_ref[...]).astype(BF16), win_ref[...])
    gb = z[:, 0:D_CONV]
    p = z[:, D_CONV:2 * D_CONV] * z[:, 2 * D_CONV:3 * D_CONV]
    xb = z[:, 3 * D_CONV:3 * D_CONV + D_LRU]
    gate = z[:, 3 * D_CONV + D_LRU:]

    sc = [sc_ref[:, k, :] for k in range(sc_ref.shape[1])] + [p]
    lc = [lc_ref[:, k, :] for k in range(lc_ref.shape[1])] + [xb]
    ya = gb * sum(v * wsc_ref[k:k + 1, :] for k, v in enumerate(sc))
    xc = sum(v * wlc_ref[k:k + 1, :] for k, v in enumerate(lc)) + blc_ref[...]
    for k, v in enumerate(sc[1:]):
        sc_new_ref[:, k, :] = v
    for k, v in enumerate(lc[1:]):
        lc_new_ref[:, k, :] = v
    a, u = _lru_gates(xc, wg_ref, ba_ref[...], bx_ref[...], lam_ref[...])
    h = a * h0_ref[...] + u
    yb = h * jax.nn.gelu(gate)
    x1 = _mix_out(x, ya, yb, gco_ref[...], glo_ref[...], wout_ref)
    xo_ref[...] = x1
    q = _dot(_rms(x1, gx_ref[...]).astype(BF16), wq_ref[...])
    for r in range(HEAD_ROWS):
        col0 = (r % N_MEM_HEADS) * MEM_HEAD_DIM + (r // N_MEM_HEADS) * LANES
        q_ref[:, r, :] = q[:, col0:col0 + LANES]
    hout_ref[...] = h


def _mixer_sample(x, sc, lc, h0, p):
    nb, d = x.shape
    sc_shape, lc_shape = _param_array(sc).shape[-3:], _param_array(lc).shape[-3:]
    args = [x, sc, lc, h0,
            p['gmix'], p['win'], p['wsc'], p['wlc'], p['blc'], p['wg'], p['ba'], p['bx'], p['lam'],
            p['gco'], p['glo'], p['wout'], p['gx'], p['wq']]
    return pl.pallas_call(
        _mixer_sample_kernel,
        grid=(1,),
        in_specs=[_param_spec(a) for a in args],
        out_specs=[_const_spec((nb, d)), _const_spec((nb, HEAD_ROWS, LANES)), _const_spec(sc_shape),
                   _const_spec(lc_shape), _const_spec((nb, D_LRU))],
        out_shape=[jax.ShapeDtypeStruct((nb, d), F32), jax.ShapeDtypeStruct((nb, HEAD_ROWS, LANES), F32),
                   jax.ShapeDtypeStruct(sc_shape, F32), jax.ShapeDtypeStruct(lc_shape, F32),
                   jax.ShapeDtypeStruct((nb, D_LRU), F32)],
        compiler_params=_cparams("arbitrary"),
        name="mixer_sample",
    )(*map(_param_array, args))


def _xattn_prompt_kernel(x_ref, g_ref, wq_ref, k_ref, v_ref, wo_ref, *rest, cast):
    if cast:
        src_hbm, o_ref, dst_hbm, *cast_scratch = rest
        step = pl.program_id(0) * pl.num_programs(1) + pl.program_id(1)
        _cast_rows(step, pl.num_programs(0) * pl.num_programs(1), src_hbm, dst_hbm, *cast_scratch)
    else:
        o_ref, = rest
    x = x_ref[...]
    q = _dot(_rms(x, g_ref[...]).astype(BF16), wq_ref[...])
    heads = []
    for h in range(N_MEM_HEADS):
        sl = slice(h * MEM_HEAD_DIM, (h + 1) * MEM_HEAD_DIM)
        kh = k_ref[:, sl]
        vh = v_ref[:, sl]
        s = lax.dot_general(q[:, sl].astype(BF16), kh, (((1,), (1,)), ((), ())),
                            preferred_element_type=F32) * (MEM_HEAD_DIM ** -0.5)
        e = jnp.exp(s - jnp.max(s, axis=-1, keepdims=True))
        pr = e / jnp.sum(e, axis=-1, keepdims=True)
        heads.append(_dot(pr.astype(BF16), vh))
    o = jnp.concatenate(heads, axis=-1).astype(BF16)
    o_ref[...] = x + _dot(o, wo_ref[...])


def _xattn_prompt(x, g, wq, k, v, wo, layer, tq=XATTN_ROWS, cast=None):
    nb, t, d = x.shape
    n_mem = k.shape[2]
    kv_spec = pl.BlockSpec((None, None, n_mem, d), lambda b, i: (layer, b, 0, 0))
    args = [x, _param_array(g), _param_array(wq), k, v, _param_array(wo)]
    in_specs = [pl.BlockSpec((None, tq, d), lambda b, i: (b, i, 0)), _param_spec(g),
                _param_spec(wq, resident=True), kv_spec, kv_spec, _param_spec(wo, resident=True)]
    out_specs = [pl.BlockSpec((None, tq, d), lambda b, i: (b, i, 0))]
    out_shape = [jax.ShapeDtypeStruct((nb, t, d), F32)]
    scratch = []
    if cast is not None:
        in_spec, out_spec, shape, scratch = _cast_specs(cast, nb * (t // tq))
        args.append(cast)
        in_specs.append(in_spec)
        out_specs.append(out_spec)
        out_shape.append(shape)
    out = pl.pallas_call(
        functools.partial(_xattn_prompt_kernel, cast=cast is not None),
        grid=(nb, t // tq),
        in_specs=in_specs,
        out_specs=out_specs,
        out_shape=out_shape,
        scratch_shapes=scratch,
        compiler_params=_cparams("arbitrary", "arbitrary"),
        name="xattn_prompt",
    )(*args)
    return out if cast is not None else out[0]


def _kv_proj_kernel(mem_ref, g_ref, wk_ref, wv_ref, k5_ref, v5_ref, kb_ref, vb_ref):
    bb, n_mem, d = mem_ref.shape
    m = _rms(mem_ref[...].reshape(bb * n_mem, d), g_ref[...]).astype(BF16)
    for w_ref, o5_ref, ob_ref in ((wk_ref, k5_ref, kb_ref), (wv_ref, v5_ref, vb_ref)):
        y = _dot(m, w_ref[...]).reshape(bb, n_mem, d)
        ob_ref[...] = y.astype(BF16)
        for h in range(N_MEM_HEADS):
            o5_ref[:, :, h, :] = y[:, :, h * MEM_HEAD_DIM:(h + 1) * MEM_HEAD_DIM]


def _kv_proj(mem, gmem, wk, wv, bb=KV_PROJ_SEQS):
    nb, n_mem, d = mem.shape
    depth = wk.shape[0]
    w_spec = pl.BlockSpec((None, d, d), lambda l, i: (l, 0, 0))
    o5_spec = pl.BlockSpec((None, bb, n_mem, N_MEM_HEADS, MEM_HEAD_DIM), lambda l, i: (l, i, 0, 0, 0))
    ob_spec = pl.BlockSpec((None, bb, n_mem, d), lambda l, i: (l, i, 0, 0))
    o5_shape = jax.ShapeDtypeStruct((depth, nb, n_mem, N_MEM_HEADS, MEM_HEAD_DIM), F32)
    ob_shape = jax.ShapeDtypeStruct((depth, nb, n_mem, d), BF16)
    return pl.pallas_call(
        _kv_proj_kernel,
        grid=(depth, nb // bb),
        in_specs=[pl.BlockSpec((bb, n_mem, d), lambda l, i: (i, 0, 0)),
                  pl.BlockSpec((None, 1, d), lambda l, i: (l, 0, 0)), w_spec, w_spec],
        out_specs=[o5_spec, o5_spec, ob_spec, ob_spec],
        out_shape=[o5_shape, o5_shape, ob_shape, ob_shape],
        compiler_params=_cparams("parallel", "parallel"),
        name="kv_proj",
    )(mem, gmem, wk, wv)


LANE_TILES_PER_HEAD = MEM_HEAD_DIM // LANES
HEAD_ROWS = N_MEM_HEADS * LANE_TILES_PER_HEAD


def _to_head_rows(a):
    lead = a.shape[:-1]
    n = len(lead)
    a = a.reshape(*lead, N_MEM_HEADS, LANE_TILES_PER_HEAD, LANES)
    return jnp.swapaxes(a, n, n + 1).reshape(*lead, HEAD_ROWS, LANES)


def _xattn_sample_kernel(q_ref, k_ref, v_ref, o_ref):
    n_mem = k_ref.shape[1]
    ones = jnp.ones((LANES, LANES), BF16)
    for b in range(q_ref.shape[0]):
        prod = (k_ref[b] * q_ref[b][None]).reshape(n_mem * HEAD_ROWS, LANES).astype(BF16)
        part = _dot(prod, ones).reshape(n_mem, HEAD_ROWS, LANES)
        s = (part + pltpu.roll(part, N_MEM_HEADS, axis=1)) * (MEM_HEAD_DIM ** -0.5)
        e = jnp.exp(s - jnp.max(s, axis=0, keepdims=True))
        pr = e / jnp.sum(e, axis=0, keepdims=True)
        o_ref[b] = jnp.sum(pr * v_ref[b], axis=0)


def _xattn_sample(q, k8, v8, layer, bb=SAMPLE_ATTN_SEQS):
    nb = q.shape[0]
    n_mem = k8.shape[2]
    kv_spec = pl.BlockSpec((None, bb, n_mem, HEAD_ROWS, LANES), lambda i: (layer, i, 0, 0, 0))
    out = pl.pallas_call(
        _xattn_sample_kernel,
        grid=(nb // bb,),
        in_specs=[pl.BlockSpec((bb, HEAD_ROWS, LANES), lambda i: (i, 0, 0)), kv_spec, kv_spec],
        out_specs=pl.BlockSpec((bb, HEAD_ROWS, LANES), lambda i: (i, 0, 0)),
        out_shape=jax.ShapeDtypeStruct((nb, HEAD_ROWS, LANES), F32),
        compiler_params=_cparams("parallel"),
        name="xattn_sample",
    )(q, k8, v8)
    return out


def _split_bf16(a):
    hi = a.astype(BF16)
    return hi, (a - hi.astype(F32)).astype(BF16)


def _top2(logits):
    n = logits.shape[-1]
    lane = lax.broadcasted_iota(jnp.int32, logits.shape, 1)
    m1 = jnp.max(logits, axis=-1, keepdims=True)
    i1 = jnp.min(jnp.where(logits == m1, lane, n), axis=-1, keepdims=True)
    rest = jnp.where(lane == i1, -jnp.inf, logits)
    m2 = jnp.max(rest, axis=-1, keepdims=True)
    i2 = jnp.min(jnp.where(rest == m2, lane, n), axis=-1, keepdims=True)
    e2 = jnp.exp(m2 - m1)
    denom = 1.0 + e2
    return i1, i2, 1.0 / denom, e2 / denom


def _add_attn_proj(x, rest, proj):
    if not proj:
        return x, rest
    a_ref, wo_ref = rest[:2]
    heads = [a_ref[:, dt * N_MEM_HEADS + h, :] for h in range(N_MEM_HEADS) for dt in range(LANE_TILES_PER_HEAD)]
    a = jnp.concatenate(heads, axis=-1).astype(BF16)
    return x + _dot(a, wo_ref[...]), rest[2:]


def _ffn_dense_kernel(x_ref, g_ref, *rest, final, proj, cast):
    x, rest = _add_attn_proj(x_ref[...], rest, proj)
    if final:
        gfin_ref, rest = rest[0], rest[1:]
    if cast:
        wg_ref, wu_ref, wd_ref, src_hbm, o_ref, dst_hbm, *cast_scratch = rest
        _cast_rows(pl.program_id(0), pl.num_programs(0), src_hbm, dst_hbm, *cast_scratch)
    else:
        wg_ref, wu_ref, wd_ref, o_ref = rest
    h = _rms(x, g_ref[...]).astype(BF16)
    act = (jax.nn.silu(_dot(h, wg_ref[...])) * _dot(h, wu_ref[...])).astype(BF16)
    out = x + _dot(act, wd_ref[...])
    o_ref[...] = _rms(out, gfin_ref[...]) if final else out


def _ffn_dense(x, g, wg, wu, wd, tm, g_final=None, attn=None, cast=None):
    r, d = x.shape
    final, proj = g_final is not None, attn is not None
    args, specs = [x, g], [pl.BlockSpec((tm, d), lambda i: (i, 0)), _param_spec(g)]
    if proj:
        args += list(attn)
        specs += [pl.BlockSpec((tm,) + attn[0].shape[1:], lambda i: (i, 0, 0)), _param_spec(attn[1], resident=True)]
    if final:
        args.append(g_final)
        specs.append(_param_spec(g_final))
    args += [wg, wu, wd]
    specs += [_param_spec(w, resident=True) for w in (wg, wu, wd)]
    out_specs = [pl.BlockSpec((tm, d), lambda i: (i, 0))]
    out_shape = [jax.ShapeDtypeStruct((r, d), F32)]
    scratch = []
    if cast is not None:
        in_spec, out_spec, shape, scratch = _cast_specs(cast, r // tm)
        args.append(cast)
        specs.append(in_spec)
        out_specs.append(out_spec)
        out_shape.append(shape)
    out = pl.pallas_call(
        functools.partial(_ffn_dense_kernel, final=final, proj=proj, cast=cast is not None),
        grid=(r // tm,),
        in_specs=specs,
        out_specs=out_specs,
        out_shape=out_shape,
        scratch_shapes=scratch,
        compiler_params=_cparams("arbitrary"),
        name="ffn_dense",
    )(*map(_param_array, args))
    return out if cast is not None else out[0]


def _moe_all_experts_kernel(x_ref, g_ref, wr_ref, *rest, final, proj):
    if proj:
        attn_refs, rest = rest[:2], rest[2:]
    if final:
        gfin_ref, rest = rest[0], rest[1:]
    wg_ref, wu_ref, wd_ref, o_ref, h_ref, acc_ref, i1_ref, i2_ref, g1_ref, g2_ref = rest
    e = pl.program_id(1)

    @pl.when(e == 0)
    def _():
        x = x_ref[...]
        if proj:
            x, _ = _add_attn_proj(x, attn_refs, proj)
        h = _rms(x, g_ref[...])
        h_ref[...] = h.astype(BF16)
        acc_ref[...] = x
        h_hi, h_lo = _split_bf16(h)
        w_hi, w_lo = _split_bf16(wr_ref[...])
        logits = _dot(h_hi, w_hi) + (_dot(h_hi, w_lo) + _dot(h_lo, w_hi))
        lane = lax.broadcasted_iota(jnp.int32, logits.shape, 1)
        logits = jnp.where(lane < N_EXPERTS, logits, -jnp.inf)
        i1_ref[...], i2_ref[...], g1_ref[...], g2_ref[...] = _top2(logits)

    gt, up = _gate_up(h_ref[...], wg_ref, wu_ref)
    y = _dot((jax.nn.silu(gt) * up).astype(BF16), wd_ref[...])
    c = jnp.where(i1_ref[...] == e, g1_ref[...], 0.0) + jnp.where(i2_ref[...] == e, g2_ref[...], 0.0)
    acc_ref[...] += c * y

    @pl.when(e == pl.num_programs(1) - 1)
    def _():
        out = acc_ref[...]
        o_ref[...] = _rms(out, gfin_ref[...]) if final else out


def _moe_all_experts(x, g, w_router, wg, wu, wd, tm, g_final=None, attn=None):
    r, d = x.shape
    n_e, f, _ = wd.shape
    final, proj = g_final is not None, attn is not None
    args = [x, g, w_router]
    specs = [pl.BlockSpec((tm, d), lambda i, e: (i, 0)), _param_spec(g), _param_spec(w_router)]
    if proj:
        args += list(attn)
        specs += [pl.BlockSpec((tm,) + attn[0].shape[1:], lambda i, e: (i, 0, 0)), _param_spec(attn[1])]
    if final:
        args.append(g_final)
        specs.append(_param_spec(g_final))
    args += [wg, wu, wd]
    specs += [pl.BlockSpec((None, d, f), lambda i, e: (e, 0, 0)),
              pl.BlockSpec((None, d, f), lambda i, e: (e, 0, 0)),
              pl.BlockSpec((None, f, d), lambda i, e: (e, 0, 0))]
    return pl.pallas_call(
        functools.partial(_moe_all_experts_kernel, final=final, proj=proj),
        grid=(r // tm, n_e),
        in_specs=specs,
        out_specs=pl.BlockSpec((tm, d), lambda i, e: (i, 0)),
        out_shape=jax.ShapeDtypeStruct((r, d), F32),
        scratch_shapes=[pltpu.VMEM((tm, d), BF16), pltpu.VMEM((tm, d), F32),
                        pltpu.VMEM((tm, 1), jnp.int32), pltpu.VMEM((tm, 1), jnp.int32),
                        pltpu.VMEM((tm, 1), F32), pltpu.VMEM((tm, 1), F32)],
        compiler_params=_cparams("parallel", "arbitrary"),
        name="moe_all_experts",
    )(*map(_param_array, args))


MOE_TILE = 512
TOP_K = 2
ROUTE_TILES_PER_STEP = 4
ROW_ALIGN = SUBLANES
COMPACT_ROWS = -(-(TOP_K * MOE_TILE + N_EXPERTS * (ROW_ALIGN - 1)) // (2 * SUBLANES)) * (2 * SUBLANES)
SEGMENT_BITS = (MOE_TILE // ROW_ALIGN).bit_length()


def _route_kernel(x_ref, g_ref, wrt_ref, earlier_ref, lp_ref, col_ref, cnt_ref):
    n_e = wrt_ref.shape[0]
    w_hi, w_lo = _split_bf16(wrt_ref[...])
    nt = (((1,), (1,)), ((), ()))
    dg = lambda a, b: lax.dot_general(a, b, nt, preferred_element_type=F32)
    for k in range(lp_ref.shape[0]):
        tile = slice(k * MOE_TILE, (k + 1) * MOE_TILE)
        h = _rms(x_ref[tile, :], g_ref[...])
        h_hi, h_lo = _split_bf16(h)
        logits = dg(w_hi, h_hi) + (dg(w_hi, h_lo) + dg(w_lo, h_hi))
        sub = lax.broadcasted_iota(jnp.int32, logits.shape, 0)
        m1 = jnp.max(logits, axis=0, keepdims=True)
        i1 = jnp.min(jnp.where(logits == m1, sub, n_e), axis=0, keepdims=True)
        rest = jnp.where(sub == i1, -jnp.inf, logits)
        m2 = jnp.max(rest, axis=0, keepdims=True)
        i2 = jnp.min(jnp.where(rest == m2, sub, n_e), axis=0, keepdims=True)
        e2 = jnp.exp(m2 - m1)
        denom = 1.0 + e2

        pick1 = jnp.where(sub == i1, 1.0, 0.0)
        pick2 = jnp.where(sub == i2, 1.0, 0.0)
        cnt = pick1 + pick2
        rank = _dot(cnt.astype(BF16), earlier_ref[...])
        n = jnp.sum(cnt, axis=1, keepdims=True)
        n_pad = jnp.floor((n + (ROW_ALIGN - 1)) * (1.0 / ROW_ALIGN)) * ROW_ALIGN
        base, acc = [], jnp.zeros((1, 1), F32)
        for e in range(n_e):
            base.append(acc)
            acc = acc + n_pad[e:e + 1, :]
        pos = rank + jnp.concatenate(base, axis=0)
        lp1 = jnp.sum(pick1 * pos, axis=0, keepdims=True)
        lp2 = jnp.sum(pick2 * pos, axis=0, keepdims=True)
        lp_ref[k, 0:1, :] = lp1.astype(jnp.int32)
        lp_ref[k, 1:2, :] = lp2.astype(jnp.int32)
        info = jnp.concatenate([lp1, lp2, 1.0 / denom, e2 / denom,
                                jnp.zeros((SUBLANES - 4, MOE_TILE), F32)], axis=0)
        col_ref[tile, :] = info.T
        cnt_ref[k] = jnp.broadcast_to(n, cnt_ref.shape[1:])


def _route(x, g, wrt):
    t, d = x.shape
    n_e = wrt.shape[0]
    n_tiles = t // MOE_TILE
    idx = jnp.arange(MOE_TILE)
    earlier = (idx[:, None] < idx[None, :]).astype(BF16)
    return pl.pallas_call(
        _route_kernel,
        grid=(n_tiles // ROUTE_TILES_PER_STEP,),
        in_specs=[pl.BlockSpec((ROUTE_TILES_PER_STEP * MOE_TILE, d), lambda j: (j, 0)), _param_spec(g),
                  _const_spec(wrt.shape), _const_spec(earlier.shape)],
        out_specs=[pl.BlockSpec((ROUTE_TILES_PER_STEP, TOP_K, MOE_TILE), lambda j: (j, 0, 0)),
                   pl.BlockSpec((ROUTE_TILES_PER_STEP * MOE_TILE, SUBLANES), lambda j: (j, 0)),
                   pl.BlockSpec((ROUTE_TILES_PER_STEP, n_e, LANES), lambda j: (j, 0, 0))],
        out_shape=[jax.ShapeDtypeStruct((n_tiles, TOP_K, MOE_TILE), jnp.int32),
                   jax.ShapeDtypeStruct((t, SUBLANES), F32),
                   jax.ShapeDtypeStruct((n_tiles, n_e, LANES), F32)],
        compiler_params=_cparams("parallel"),
        name="moe_route",
    )(x, _param_array(g), wrt, earlier)


def _segment_copies(src_ref, src_row, dst_ref, dst_row, groups, sem):
    out = []
    for bit in range(SEGMENT_BITS):
        rows = ROW_ALIGN << bit
        before = ((groups >> (bit + 1)) << (bit + 1)) * ROW_ALIGN
        cp = pltpu.make_async_copy(
            src_ref.at[pl.ds(pl.multiple_of(src_row + before, ROW_ALIGN), rows), :],
            dst_ref.at[pl.ds(pl.multiple_of(dst_row + before, ROW_ALIGN), rows), :], sem)
        out.append((((groups >> bit) & 1) == 1, cp))
    return out


def _start_all(copies):
    for pred, cp in copies:
        pl.when(pred)(cp.start)


def _wait_all(copies):
    for pred, cp in copies:
        pl.when(pred)(cp.wait)


def _dispatch_kernel(grp_ref, off_ref, dst_ref, pad_ref, x_ref, g_ref, lp_ref, xs_ref, comp_ref, sem):
    j = pl.program_id(0)
    n_steps = pl.num_programs(0)
    n_e = grp_ref.shape[1]
    slot = j % 2

    def copies(step, s):
        out = []
        for e in range(n_e):
            out += _segment_copies(comp_ref.at[s], off_ref[step, e], xs_ref, dst_ref[step, e],
                                   grp_ref[step, e], sem.at[s])
        return out

    h = _rms(x_ref[...], g_ref[...]).astype(BF16)
    row = lax.broadcasted_iota(jnp.int32, (COMPACT_ROWS, h.shape[0]), 0)
    onehot = jnp.where(row == lp_ref[0:1, :], 1.0, jnp.where(row == lp_ref[1:2, :], 1.0, 0.0)).astype(BF16)
    compact = _dot(onehot, h)

    @pl.when(j >= 2)
    def _():
        _wait_all(copies(j - 2, slot))

    comp_ref[slot] = compact
    _start_all(copies(j, slot))

    @pl.when(j == n_steps - 1)
    def _():
        _wait_all(copies(j, slot))

        @pl.when(j >= 1)
        def _():
            _wait_all(copies(j - 1, 1 - slot))

        comp_ref[slot, 0:MOE_TILE, :] = jnp.zeros((MOE_TILE, comp_ref.shape[2]), F32)
        tail = []
        for e in range(n_e):
            tail += _segment_copies(comp_ref.at[slot], 0, xs_ref, pad_ref[0, e], pad_ref[1, e], sem.at[slot])
        tile_groups = MOE_TILE // ROW_ALIGN
        groups_left = (xs_ref.shape[0] - pad_ref[2, 0]) // ROW_ALIGN
        min_total = TOP_K * x_ref.shape[0] * grp_ref.shape[0]
        for c in range((xs_ref.shape[0] - min_total) // MOE_TILE):
            tail += _segment_copies(comp_ref.at[slot], 0, xs_ref, pad_ref[2, 0] + c * MOE_TILE,
                                    jnp.clip(groups_left - c * tile_groups, 0, tile_groups), sem.at[slot])
        _start_all(tail)
        _wait_all(tail)


def _dispatch(x, g, lp, groups, off, dst, pad, n_rows):
    t, d = x.shape
    n_tiles = t // MOE_TILE
    grid_spec = pltpu.PrefetchScalarGridSpec(
        num_scalar_prefetch=4,
        grid=(n_tiles,),
        in_specs=[pl.BlockSpec((MOE_TILE, d), lambda j, *_: (j, 0)),
                  _param_spec(g),
                  pl.BlockSpec((None, TOP_K, MOE_TILE), lambda j, *_: (j, 0, 0))],
        out_specs=pl.BlockSpec(memory_space=pl.ANY),
        scratch_shapes=[pltpu.VMEM((2, COMPACT_ROWS, d), F32), pltpu.SemaphoreType.DMA((2,))])
    return pl.pallas_call(
        _dispatch_kernel,
        grid_spec=grid_spec,
        out_shape=jax.ShapeDtypeStruct((n_rows, d), F32),
        compiler_params=_cparams("arbitrary"),
        name="moe_dispatch",
    )(groups, off, dst, pad, x, _param_array(g), lp)


STEP_FFN, STEP_ZERO = 1, 2


def _grouped_ffn_kernel(grp_ref, mode_ref, xtile_ref, xs_ref, wg_ref, wu_ref, wd_ref, ys_ref):
    w = pl.program_id(0)

    @pl.when(mode_ref[w] == STEP_ZERO)
    def _():
        ys_ref[...] = jnp.zeros_like(ys_ref)

    @pl.when(mode_ref[w] == STEP_FFN)
    def _():
        gt, up = _gate_up(xs_ref[...].astype(BF16), wg_ref, wu_ref)
        ys_ref[...] = _dot((jax.nn.silu(gt) * up).astype(BF16), wd_ref[...])


def _grouped_ffn(xs, wg, wu, wd, grp, mode, xtile):
    n_e, f, d = wd.shape
    grid_spec = pltpu.PrefetchScalarGridSpec(
        num_scalar_prefetch=3,
        grid=(xs.shape[0] // MOE_TILE,),
        in_specs=[pl.BlockSpec((MOE_TILE, d), lambda w, grp, mode, xtile: (xtile[w], 0)),
                  pl.BlockSpec((None, d, f), lambda w, grp, *_: (grp[w], 0, 0)),
                  pl.BlockSpec((None, d, f), lambda w, grp, *_: (grp[w], 0, 0)),
                  pl.BlockSpec((None, f, d), lambda w, grp, *_: (grp[w], 0, 0))],
        out_specs=pl.BlockSpec((MOE_TILE, d), lambda w, *_: (w, 0)))
    return pl.pallas_call(
        _grouped_ffn_kernel,
        grid_spec=grid_spec,
        out_shape=jax.ShapeDtypeStruct(xs.shape, F32),
        compiler_params=_cparams("arbitrary"),
        name="moe_grouped_ffn",
    )(grp, mode, xtile, xs, wg, wu, wd)


def _combine_kernel(grp_ref, off_ref, dst_ref, x_ref, col_ref, *rest, final):
    if final:
        gfin_ref, rest = rest[0], rest[1:]
    ys_ref, o_ref, comp_ref, sem = rest
    j = pl.program_id(0)
    n_steps = pl.num_programs(0)
    n_e = grp_ref.shape[1]
    slot = j % 2

    def copies(step, s):
        out = []
        for e in range(n_e):
            out += _segment_copies(ys_ref, dst_ref[step, e], comp_ref.at[s], off_ref[step, e],
                                   grp_ref[step, e], sem.at[s])
        return out

    @pl.when(j == 0)
    def _():
        comp_ref[...] = jnp.zeros_like(comp_ref)
        _start_all(copies(0, 0))

    @pl.when(j + 1 < n_steps)
    def _():
        _start_all(copies(j + 1, 1 - slot))

    _wait_all(copies(j, slot))
    y = comp_ref[slot].astype(BF16)
    col = col_ref[...]
    row = lax.broadcasted_iota(jnp.int32, (col.shape[0], COMPACT_ROWS), 1)
    weights = jnp.zeros(row.shape, F32)
    for k in range(TOP_K):
        weights = jnp.where(row == col[:, k:k + 1].astype(jnp.int32), col[:, TOP_K + k:TOP_K + k + 1], weights)
    out = x_ref[...] + _dot(weights.astype(BF16), y)
    o_ref[...] = _rms(out, gfin_ref[...]) if final else out


def _combine(x, col, ys, groups, off, dst, g_final):
    t, d = x.shape
    n_tiles = t // MOE_TILE
    final = g_final is not None
    args = [x, col]
    specs = [pl.BlockSpec((MOE_TILE, d), lambda j, *_: (j, 0)),
             pl.BlockSpec((MOE_TILE, SUBLANES), lambda j, *_: (j, 0))]
    if final:
        args.append(g_final)
        specs.append(_param_spec(g_final))
    args.append(ys)
    specs.append(pl.BlockSpec(memory_space=pl.ANY))
    grid_spec = pltpu.PrefetchScalarGridSpec(
        num_scalar_prefetch=3,
        grid=(n_tiles,),
        in_specs=specs,
        out_specs=pl.BlockSpec((MOE_TILE, d), lambda j, *_: (j, 0)),
        scratch_shapes=[pltpu.VMEM((2, COMPACT_ROWS, d), F32), pltpu.SemaphoreType.DMA((2,))])
    return pl.pallas_call(
        functools.partial(_combine_kernel, final=final),
        grid_spec=grid_spec,
        out_shape=jax.ShapeDtypeStruct((t, d), F32),
        compiler_params=_cparams("arbitrary"),
        name="moe_combine",
    )(groups, off, dst, *map(_param_array, args))


def _moe_routed(x, g, w_router, wg, wu, wd, g_final):
    t, _ = x.shape
    n_e = wd.shape[0]
    n_tiles = t // MOE_TILE
    lp, col, cnt = _route(x, g, w_router.T)
    groups = (cnt[:, :, 0].astype(jnp.int32) + (ROW_ALIGN - 1)) // ROW_ALIGN
    n = groups * ROW_ALIGN
    off = jnp.cumsum(n, axis=1) - n
    per_e = jnp.sum(n, axis=0)
    tiles_e = (per_e + (MOE_TILE - 1)) // MOE_TILE
    tile_end = jnp.cumsum(tiles_e)
    starts = (tile_end - tiles_e) * MOE_TILE
    dst = starts[None, :] + jnp.cumsum(n, axis=0) - n
    pad = jnp.stack([starts + per_e, (tiles_e * MOE_TILE - per_e) // ROW_ALIGN,
                     jnp.broadcast_to(tile_end[-1] * MOE_TILE, per_e.shape)])
    max_rows = TOP_K * t + n_tiles * n_e * (ROW_ALIGN - 1)
    n_row_tiles = -(-max_rows // MOE_TILE) + n_e
    xs = _dispatch(x, g, lp, groups, off, dst, pad, n_row_tiles * MOE_TILE)

    w = jnp.arange(n_row_tiles, dtype=jnp.int32)
    used = w < tile_end[-1]
    grp = jnp.minimum(jnp.sum(w[:, None] >= tile_end[None, :], axis=1), n_e - 1).astype(jnp.int32)
    mode = jnp.where(used, STEP_FFN, STEP_ZERO).astype(jnp.int32)
    xtile = jnp.minimum(w, tile_end[-1] - 1)
    ys = _grouped_ffn(xs, wg, wu, wd, grp, mode, xtile)
    return _combine(x, col, ys, groups, off, dst, g_final)


def _block_diag_gates(w_a, w_x):
    per_group = LRU_GROUP // LRU_HEAD_DIM
    eye = jnp.eye(per_group, dtype=w_a.dtype)

    def bd(w):
        w = w.reshape(D_LRU // LRU_GROUP, per_group, LRU_HEAD_DIM, LRU_HEAD_DIM)
        return jnp.einsum('ghij,hk->ghikj', w, eye).reshape(D_LRU // LRU_GROUP, LRU_GROUP, LRU_GROUP)

    return jnp.concatenate([bd(w_a), bd(w_x)], axis=-1).astype(BF16)


def kernel(x_prompt, x_sample, state_shortconv, state_lru_conv, state_lru_h, cache_mem_k, cache_mem_v,
           mem_prompt, norm_mix, w_in, w_short_conv, w_lru_conv, b_lru_conv, w_lru_a, b_lru_a, w_lru_x,
           b_lru_x, lru_lambda, norm_conv_out, norm_lru_out, w_mix_out, norm_xattn, norm_mem, w_q, w_k,
           w_v, w_o, norm_ffn, w_ff_gate, w_ff_up, w_ff_down, w_router, w_moe_gate, w_moe_up, w_moe_down,
           norm_final):
    depth = w_in.shape[0]
    nb, t, d = x_prompt.shape
    ns = x_sample.shape[0]
    n_mem = mem_prompt.shape[1]
    rows = lambda a: a.reshape(a.shape[0], 1, -1)
    stacks = dict(
        gmix=rows(norm_mix), win=w_in.astype(BF16), wsc=w_short_conv, wlc=w_lru_conv, blc=rows(b_lru_conv),
        wg=jax.vmap(_block_diag_gates)(w_lru_a, w_lru_x), ba=rows(b_lru_a), bx=rows(b_lru_x),
        lam=rows(lru_lambda), gco=rows(norm_conv_out), glo=rows(norm_lru_out), wout=w_mix_out.astype(BF16),
        gx=rows(norm_xattn), wq=w_q.astype(BF16), wo=w_o.astype(BF16), gffn=rows(norm_ffn))
    g_final = norm_final.reshape(1, -1)
    ff_dense = (w_ff_gate.astype(BF16), w_ff_up.astype(BF16), w_ff_down.astype(BF16))
    ff_moe_f32 = (w_moe_gate, w_moe_up, w_moe_down)

    layers = []
    for l in range(depth):
        lp = {name: (stack, l) for name, stack in stacks.items()}
        lp['gfin'] = g_final if l == depth - 1 else None
        j = l // 2
        if l % 2 == 0:
            lp['ffw'] = tuple((w, j) for w in ff_dense)
            lp['wr'] = None
        else:
            lp['ffw'] = None
            lp['wr'] = jnp.pad(w_router[j], ((0, 0), (0, LANES - N_EXPERTS)))
            lp['wr_raw'] = w_router[j]
        layers.append(lp)

    x = x_prompt
    p_mem_k, p_mem_v, kb, vb = _kv_proj(mem_prompt, norm_mem.reshape(depth, 1, d), w_k.astype(BF16),
                                        w_v.astype(BF16))
    pc, plc, ph = [], [], []
    moe_bf16 = {}

    def cast_for(j, k):
        w = ff_moe_f32[k][j]
        return w.reshape(-1, w.shape[-1])

    for l, lp in enumerate(layers):
        x, tail_p, tail_x, h_last = _mixer_prompt(x, lp)
        if l + 1 < depth and lp['wr'] is None:
            x, moe_bf16[(l + 1) // 2, 0] = _xattn_prompt(x, lp['gx'], lp['wq'], kb, vb, lp['wo'], l,
                                                         cast=cast_for((l + 1) // 2, 0))
        elif lp['wr'] is not None:
            x, moe_bf16[l // 2, 1] = _xattn_prompt(x, lp['gx'], lp['wq'], kb, vb, lp['wo'], l,
                                                   cast=cast_for(l // 2, 1))
        else:
            x = _xattn_prompt(x, lp['gx'], lp['wq'], kb, vb, lp['wo'], l)
        if lp['wr'] is None:
            if l + 1 < depth:
                x, moe_bf16[(l + 1) // 2, 2] = _ffn_dense(x.reshape(nb * t, d), lp['gffn'], *lp['ffw'], tm=FFN_ROWS,
                                                          g_final=lp['gfin'], cast=cast_for((l + 1) // 2, 2))
            else:
                x = _ffn_dense(x.reshape(nb * t, d), lp['gffn'], *lp['ffw'], tm=FFN_ROWS, g_final=lp['gfin'])
        else:
            j = l // 2
            lp['ffw'] = tuple(moe_bf16[j, k].reshape(ff_moe_f32[k][j].shape) for k in range(3))
            x = _moe_routed(x.reshape(nb * t, d), lp['gffn'], lp['wr_raw'], *lp['ffw'], lp['gfin'])
        x = x.reshape(nb, t, d)
        pc.append(tail_p)
        plc.append(tail_x)
        ph.append(h_last)
    y_prompt = x

    x = x_sample.reshape(ns, d)
    k8 = _to_head_rows(cache_mem_k.reshape(depth, ns, n_mem, d))
    v8 = _to_head_rows(cache_mem_v.reshape(depth, ns, n_mem, d))
    sc, slc, sh = [], [], []
    for l, lp in enumerate(layers):
        x, q, sc_new, lc_new, h_new = _mixer_sample(x, (state_shortconv, l), (state_lru_conv, l),
                                                    (state_lru_h, l), lp)
        attn = (_xattn_sample(q, k8, v8, l), lp['wo'])
        if lp['wr'] is None:
            x = _ffn_dense(x, lp['gffn'], *lp['ffw'], tm=ns, g_final=lp['gfin'], attn=attn)
        else:
            x = _moe_all_experts(x, lp['gffn'], lp['wr'], *lp['ffw'], tm=ns, g_final=lp['gfin'], attn=attn)
        sc.append(sc_new)
        slc.append(lc_new)
        sh.append(h_new)
    y_sample = x.reshape(ns, 1, d)

    return (y_prompt, y_sample, jnp.stack(pc), jnp.stack(plc), jnp.stack(ph), p_mem_k, p_mem_v,
            jnp.stack(sc), jnp.stack(slc), jnp.stack(sh))
```

```python
import functools

import jax
import jax.numpy as jnp
from jax import lax
from jax.experimental import pallas as pl
from jax.experimental.pallas import tpu as pltpu

F32 = jnp.float32
BF16 = jnp.bfloat16

EPS = 1e-6
LRU_C = 8.0
D_CONV = 512
D_LRU = 512
LRU_HEAD_DIM = 64
LRU_GROUP = 256
N_MEM_HEADS = 4
MEM_HEAD_DIM = 256
N_EXPERTS = 8
SUBLANES = 8
LANES = 128
MXU_COLS = 256
VMEM_LIMIT_BYTES = 56 * 1024 * 1024

MIXER_TIME_CHUNK = 128
XATTN_ROWS = 1024
FFN_ROWS = 512
KV_PROJ_SEQS = 2
SAMPLE_ATTN_SEQS = 8


def _cparams(*sem):
    return pltpu.CompilerParams(dimension_semantics=sem, vmem_limit_bytes=VMEM_LIMIT_BYTES)


def _rms(x, g):
    return x * lax.rsqrt(jnp.mean(x * x, axis=-1, keepdims=True) + EPS) * g


def _dot(a, b):
    return jnp.dot(a, b, preferred_element_type=F32)


def _sigmoid(x):
    return 0.5 * jnp.tanh(0.5 * x) + 0.5


def _gate_up(x, wg_ref, wu_ref):
    f = wg_ref.shape[1]
    main = f // MXU_COLS * MXU_COLS
    if main == f or 2 * (f - main) != MXU_COLS:
        return _dot(x, wg_ref[...]), _dot(x, wu_ref[...])
    rest = _dot(x, jnp.concatenate([wg_ref[:, main:], wu_ref[:, main:]], axis=1))
    gt = jnp.concatenate([_dot(x, wg_ref[:, :main]), rest[:, :f - main]], axis=1)
    up = jnp.concatenate([_dot(x, wu_ref[:, :main]), rest[:, f - main:]], axis=1)
    return gt, up


def _lru_gates(xc, wg_ref, ba, bx, lam):
    neg_lam = -lam
    softplus = jnp.maximum(neg_lam, 0.0) + jnp.log1p(jnp.exp(-jnp.abs(neg_lam)))
    a_parts, u_parts = [], []
    for g in range(D_LRU // LRU_GROUP):
        sl = slice(g * LRU_GROUP, (g + 1) * LRU_GROUP)
        xg = xc[:, sl]
        gates = _dot(xg.astype(BF16), wg_ref[g])
        r = _sigmoid(gates[:, :LRU_GROUP] + ba[:, sl])
        i = _sigmoid(gates[:, LRU_GROUP:] + bx[:, sl])
        log_a = (-LRU_C) * r * softplus[:, sl]
        th = jnp.tanh(log_a)
        one_minus_a2 = (-2.0 * th) / (1.0 - th)
        a_parts.append(jnp.exp(log_a))
        u_parts.append(jnp.sqrt(one_minus_a2) * (i * xg))
    return jnp.concatenate(a_parts, axis=-1), jnp.concatenate(u_parts, axis=-1)


def _mix_out(x, ya, yb, gco, glo, wout_ref):
    y = jnp.concatenate([_rms(ya, gco), _rms(yb, glo)], axis=-1).astype(BF16)
    return x + _dot(y, wout_ref[...])


MAX_CONV_HISTORY = 3


def _conv_taps(ext_ref, w_ref, rows, nb):
    width = w_ref.shape[0]
    hist = MAX_CONV_HISTORY * nb
    acc = ext_ref[hist:hist + rows, :] * w_ref[width - 1:width, :]
    for s in range(1, width):
        acc = acc + ext_ref[hist - s * nb:hist - s * nb + rows, :] * w_ref[width - 1 - s:width - s, :]
    return acc


def _mixer_prompt_kernel(x_hbm, gmix_ref, win_ref, wsc_ref, wlc_ref, blc_ref, wg_ref, ba_ref, bx_ref,
                         lam_ref, gco_ref, glo_ref, wout_ref,
                         xo_hbm, tailp_ref, tailx_ref, hout_ref,
                         xin_ref, xout_ref, in_sem, out_sem, pext_ref, xext_ref, h_ref, a_ref, u_ref):
    _, tt, nb, d = xin_ref.shape
    rows = tt * nb
    hist = MAX_CONV_HISTORY * nb
    i = pl.program_id(0)
    n_steps = pl.num_programs(0)
    slot = i % 2

    def in_copies(step, s):
        return [pltpu.make_async_copy(x_hbm.at[b, pl.ds(step * tt, tt), :], xin_ref.at[s, :, b, :],
                                      in_sem.at[s]) for b in range(nb)]

    def out_copies(step, s):
        return [pltpu.make_async_copy(xout_ref.at[s, :, b, :], xo_hbm.at[b, pl.ds(step * tt, tt), :],
                                      out_sem.at[s]) for b in range(nb)]

    @pl.when(i == 0)
    def _():
        pext_ref[0:hist, :] = jnp.zeros((hist, pext_ref.shape[1]), F32)
        xext_ref[0:hist, :] = jnp.zeros((hist, xext_ref.shape[1]), F32)
        h_ref[...] = jnp.zeros_like(h_ref)
        for cp in in_copies(0, 0):
            cp.start()

    @pl.when(i + 1 < n_steps)
    def _():
        for cp in in_copies(i + 1, 1 - slot):
            cp.start()

    for cp in in_copies(i, slot):
        cp.wait()
    x = xin_ref[slot].reshape(rows, d)
    z = _dot(_rms(x, gmix_ref[...]).astype(BF16), win_ref[...])
    gb = z[:, 0:D_CONV]
    gate = z[:, 3 * D_CONV + D_LRU:]
    pext_ref[hist:hist + rows, :] = z[:, D_CONV:2 * D_CONV] * z[:, 2 * D_CONV:3 * D_CONV]
    xext_ref[hist:hist + rows, :] = z[:, 3 * D_CONV:3 * D_CONV + D_LRU]

    ya = gb * _conv_taps(pext_ref, wsc_ref, rows, nb)
    xc = _conv_taps(xext_ref, wlc_ref, rows, nb) + blc_ref[...]
    for ext_ref, tail_ref in ((pext_ref, tailp_ref), (xext_ref, tailx_ref)):
        tail = ext_ref[rows:rows + hist, :]
        ext_ref[0:hist, :] = tail
        keep = tail_ref.shape[1]
        for k in range(keep):
            step0 = (MAX_CONV_HISTORY - keep + k) * nb
            tail_ref[:, k, :] = tail[step0:step0 + nb, :]

    a, u = _lru_gates(xc, wg_ref, ba_ref[...], bx_ref[...], lam_ref[...])
    a_ref[...] = a
    u_ref[...] = u

    def step(t, h):
        grp = pl.ds(pl.multiple_of(t * nb, nb), nb)
        h = a_ref[grp, :] * h + u_ref[grp, :]
        u_ref[grp, :] = h
        return h

    h_last = lax.fori_loop(0, tt, step, h_ref[...], unroll=8)
    h_ref[...] = h_last
    hout_ref[...] = h_last

    yb = u_ref[...] * jax.nn.gelu(gate)
    out = _mix_out(x, ya, yb, gco_ref[...], glo_ref[...], wout_ref)

    @pl.when(i >= 2)
    def _():
        for cp in out_copies(i - 2, slot):
            cp.wait()

    xout_ref[slot] = out.reshape(tt, nb, d)
    for cp in out_copies(i, slot):
        cp.start()

    @pl.when(i == n_steps - 1)
    def _():
        for cp in out_copies(i, slot):
            cp.wait()

        @pl.when(i >= 1)
        def _():
            for cp in out_copies(i - 1, 1 - slot):
                cp.wait()


def _const_spec(shape):
    return pl.BlockSpec(shape, lambda *_: (0,) * len(shape))


def _param_array(p):
    return p[0] if isinstance(p, tuple) else p


def _param_spec(p, resident=False):
    a, layer = p if isinstance(p, tuple) else (p, None)
    mode = dict(pipeline_mode=pl.Buffered(1)) if resident else {}
    if layer is None:
        return pl.BlockSpec(a.shape, lambda *_: (0,) * a.ndim, **mode)
    return pl.BlockSpec((None,) + a.shape[1:], lambda *_: (layer,) + (0,) * (a.ndim - 1), **mode)


def _cast_specs(w, n_steps):
    rows = w.shape[0] // n_steps
    assert rows * n_steps == w.shape[0] and rows % (2 * SUBLANES) == 0
    any_spec = pl.BlockSpec(memory_space=pl.ANY)
    scratch = [pltpu.VMEM((2, rows, w.shape[1]), F32), pltpu.VMEM((2, rows, w.shape[1]), BF16),
               pltpu.SemaphoreType.DMA((2,)), pltpu.SemaphoreType.DMA((2,))]
    return any_spec, any_spec, jax.ShapeDtypeStruct(w.shape, BF16), scratch


def _cast_rows(step, n_steps, src_hbm, dst_hbm, in_buf, out_buf, in_sem, out_sem):
    rows = in_buf.shape[1]
    slot = step % 2

    def in_copy(s, sl):
        return pltpu.make_async_copy(src_hbm.at[pl.ds(pl.multiple_of(s * rows, rows), rows), :],
                                     in_buf.at[sl], in_sem.at[sl])

    def out_copy(s, sl):
        return pltpu.make_async_copy(out_buf.at[sl],
                                     dst_hbm.at[pl.ds(pl.multiple_of(s * rows, rows), rows), :], out_sem.at[sl])

    pl.when(step == 0)(in_copy(0, 0).start)
    pl.when(step + 1 < n_steps)(in_copy(step + 1, 1 - slot).start)
    in_copy(step, slot).wait()
    pl.when(step >= 2)(out_copy(step - 2, slot).wait)
    out_buf[slot] = in_buf[slot].astype(BF16)
    out_copy(step, slot).start()

    @pl.when(step == n_steps - 1)
    def _():
        out_copy(step, slot).wait()
        pl.when(step >= 1)(out_copy(step - 1, 1 - slot).wait)


def _mixer_prompt(x, p, tt=MIXER_TIME_CHUNK):
    nb, t, d = x.shape
    small = [p['gmix'], p['win'], p['wsc'], p['wlc'], p['blc'], p['wg'], p['ba'], p['bx'], p['lam'],
             p['gco'], p['glo'], p['wout']]
    rows = nb * tt
    hist = MAX_CONV_HISTORY * nb
    conv_state = (nb, _param_array(p['wsc']).shape[1] - 1, D_CONV)
    lru_state = (nb, _param_array(p['wlc']).shape[1] - 1, D_LRU)
    return pl.pallas_call(
        _mixer_prompt_kernel,
        grid=(t // tt,),
        in_specs=[pl.BlockSpec(memory_space=pl.ANY)] + [_param_spec(a, resident=True) for a in small],
        out_specs=[pl.BlockSpec(memory_space=pl.ANY),
                   _const_spec(conv_state), _const_spec(lru_state), _const_spec((nb, D_LRU))],
        out_shape=[jax.ShapeDtypeStruct((nb, t, d), F32),
                   jax.ShapeDtypeStruct(conv_state, F32),
                   jax.ShapeDtypeStruct(lru_state, F32),
                   jax.ShapeDtypeStruct((nb, D_LRU), F32)],
        scratch_shapes=[pltpu.VMEM((2, tt, nb, d), F32), pltpu.VMEM((2, tt, nb, d), F32),
                        pltpu.SemaphoreType.DMA((2,)), pltpu.SemaphoreType.DMA((2,)),
                        pltpu.VMEM((hist + rows, D_CONV), F32), pltpu.VMEM((hist + rows, D_LRU), F32),
                        pltpu.VMEM((nb, D_LRU), F32),
                        pltpu.VMEM((rows, D_LRU), F32), pltpu.VMEM((rows, D_LRU), F32)],
        compiler_params=_cparams("arbitrary"),
        name="mixer_prompt",
    )(x, *map(_param_array, small))


def _mixer_sample_kernel(x_ref, sc_ref, lc_ref, h0_ref,
                         gmix_ref, win_ref, wsc_ref, wlc_ref, blc_ref, wg_ref, ba_ref, bx_ref,
                         lam_ref, gco_ref, glo_ref, wout_ref, gx_ref, wq_ref,
                         xo_ref, q_ref, sc_new_ref, lc_new_ref, hout_ref):
    x = x_ref[...]
    z = _dot(_rms(x, gmix_ref[...]).astype(BF16), win_ref[...])
    gb = z[:, 0:D_CONV]
    p = z[:, D_CONV:2 * D_CONV] * z[:, 2 * D_CONV:3 * D_CONV]
    xb = z[:, 3 * D_CONV:3 * D_CONV + D_LRU]
    gate = z[:, 3 * D_CONV + D_LRU:]

    sc = [sc_ref[:, k, :] for k in range(sc_ref.shape[1])] + [p]
    lc = [lc_ref[:, k, :] for k in range(lc_ref.shape[1])] + [xb]
    ya = gb * sum(v * wsc_ref[k:k + 1, :] for k, v in enumerate(sc))
    xc = sum(v * wlc_ref[k:k + 1, :] for k, v in enumerate(lc)) + blc_ref[...]
    for k, v in enumerate(sc[1:]):
        sc_new_ref[:, k, :] = v
    for k, v in enumerate(lc[1:]):
        lc_new_ref[:, k, :] = v
    a, u = _lru_gates(xc, wg_ref, ba_ref[...], bx_ref[...], lam_ref[...])
    h = a * h0_ref[...] + u
    yb = h * jax.nn.gelu(gate)
    x1 = _mix_out(x, ya, yb, gco_ref[...], glo_ref[...], wout_ref)
    xo_ref[...] = x1
    q = _dot(_rms(x1, gx_ref[...]).astype(BF16), wq_ref[...])
    for r in range(HEAD_ROWS):
        col0 = (r % N_MEM_HEADS) * MEM_HEAD_DIM + (r // N_MEM_HEADS) * LANES
        q_ref[:, r, :] = q[:, col0:col0 + LANES]
    hout_ref[...] = h


def _mixer_sample(x, sc, lc, h0, p):
    nb, d = x.shape
    sc_shape, lc_shape = _param_array(sc).shape[-3:], _param_array(lc).shape[-3:]
    args = [x, sc, lc, h0,
            p['gmix'], p['win'], p['wsc'], p['wlc'], p['blc'], p['wg'], p['ba'], p['bx'], p['lam'],
            p['gco'], p['glo'], p['wout'], p['gx'], p['wq']]
    return pl.pallas_call(
        _mixer_sample_kernel,
        grid=(1,),
        in_specs=[_param_spec(a) for a in args],
        out_specs=[_const_spec((nb, d)), _const_spec((nb, HEAD_ROWS, LANES)), _const_spec(sc_shape),
                   _const_spec(lc_shape), _const_spec((nb, D_LRU))],
        out_shape=[jax.ShapeDtypeStruct((nb, d), F32), jax.ShapeDtypeStruct((nb, HEAD_ROWS, LANES), F32),
                   jax.ShapeDtypeStruct(sc_shape, F32), jax.ShapeDtypeStruct(lc_shape, F32),
                   jax.ShapeDtypeStruct((nb, D_LRU), F32)],
        compiler_params=_cparams("arbitrary"),
        name="mixer_sample",
    )(*map(_param_array, args))


def _xattn_prompt_kernel(x_ref, g_ref, wq_ref, k_ref, v_ref, wo_ref, *rest, n_cast):
    o_ref = rest[n_cast]
    step = pl.program_id(0) * pl.num_programs(1) + pl.program_id(1)
    for c in range(n_cast):
        _cast_rows(step, pl.num_programs(0) * pl.num_programs(1), rest[c], rest[n_cast + 1 + c],
                   *rest[2 * n_cast + 1 + 4 * c:2 * n_cast + 5 + 4 * c])
    x = x_ref[...]
    q = _dot(_rms(x, g_ref[...]).astype(BF16), wq_ref[...])
    heads = []
    for h in range(N_MEM_HEADS):
        sl = slice(h * MEM_HEAD_DIM, (h + 1) * MEM_HEAD_DIM)
        kh = k_ref[:, sl]
        vh = v_ref[:, sl]
        s = lax.dot_general(q[:, sl].astype(BF16), kh, (((1,), (1,)), ((), ())),
                            preferred_element_type=F32) * (MEM_HEAD_DIM ** -0.5)
        e = jnp.exp(s - jnp.max(s, axis=-1, keepdims=True))
        pr = e / jnp.sum(e, axis=-1, keepdims=True)
        heads.append(_dot(pr.astype(BF16), vh))
    o = jnp.concatenate(heads, axis=-1).astype(BF16)
    o_ref[...] = x + _dot(o, wo_ref[...])


def _xattn_prompt(x, g, wq, k, v, wo, layer, tq=XATTN_ROWS, casts=()):
    nb, t, d = x.shape
    n_mem = k.shape[2]
    kv_spec = pl.BlockSpec((None, None, n_mem, d), lambda b, i: (layer, b, 0, 0))
    args = [x, _param_array(g), _param_array(wq), k, v, _param_array(wo)]
    in_specs = [pl.BlockSpec((None, tq, d), lambda b, i: (b, i, 0)), _param_spec(g),
                _param_spec(wq, resident=True), kv_spec, kv_spec, _param_spec(wo, resident=True)]
    out_specs = [pl.BlockSpec((None, tq, d), lambda b, i: (b, i, 0))]
    out_shape = [jax.ShapeDtypeStruct((nb, t, d), F32)]
    scratch = []
    for w in casts:
        in_spec, out_spec, shape, job_scratch = _cast_specs(w, nb * (t // tq))
        args.append(w)
        in_specs.append(in_spec)
        out_specs.append(out_spec)
        out_shape.append(shape)
        scratch += job_scratch
    out = pl.pallas_call(
        functools.partial(_xattn_prompt_kernel, n_cast=len(casts)),
        grid=(nb, t // tq),
        in_specs=in_specs,
        out_specs=out_specs,
        out_shape=out_shape,
        scratch_shapes=scratch,
        compiler_params=_cparams("arbitrary", "arbitrary"),
        name="xattn_prompt",
    )(*args)
    return out[0], out[1:]


def _kv_proj_kernel(mem_ref, g_ref, wk_ref, wv_ref, k5_ref, v5_ref, kb_ref, vb_ref):
    bb, n_mem, d = mem_ref.shape
    m = _rms(mem_ref[...].reshape(bb * n_mem, d), g_ref[...]).astype(BF16)
    for w_ref, o5_ref, ob_ref in ((wk_ref, k5_ref, kb_ref), (wv_ref, v5_ref, vb_ref)):
        y = _dot(m, w_ref[...]).reshape(bb, n_mem, d)
        ob_ref[...] = y.astype(BF16)
        for h in range(N_MEM_HEADS):
            o5_ref[:, :, h, :] = y[:, :, h * MEM_HEAD_DIM:(h + 1) * MEM_HEAD_DIM]


def _kv_proj(mem, gmem, wk, wv, bb=KV_PROJ_SEQS):
    nb, n_mem, d = mem.shape
    depth = wk.shape[0]
    w_spec = pl.BlockSpec((None, d, d), lambda l, i: (l, 0, 0))
    o5_spec = pl.BlockSpec((None, bb, n_mem, N_MEM_HEADS, MEM_HEAD_DIM), lambda l, i: (l, i, 0, 0, 0))
    ob_spec = pl.BlockSpec((None, bb, n_mem, d), lambda l, i: (l, i, 0, 0))
    o5_shape = jax.ShapeDtypeStruct((depth, nb, n_mem, N_MEM_HEADS, MEM_HEAD_DIM), F32)
    ob_shape = jax.ShapeDtypeStruct((depth, nb, n_mem, d), BF16)
    return pl.pallas_call(
        _kv_proj_kernel,
        grid=(depth, nb // bb),
        in_specs=[pl.BlockSpec((bb, n_mem, d), lambda l, i: (i, 0, 0)),
                  pl.BlockSpec((None, 1, d), lambda l, i: (l, 0, 0)), w_spec, w_spec],
        out_specs=[o5_spec, o5_spec, ob_spec, ob_spec],
        out_shape=[o5_shape, o5_shape, ob_shape, ob_shape],
        compiler_params=_cparams("parallel", "parallel"),
        name="kv_proj",
    )(mem, gmem, wk, wv)


LANE_TILES_PER_HEAD = MEM_HEAD_DIM // LANES
HEAD_ROWS = N_MEM_HEADS * LANE_TILES_PER_HEAD


def _to_head_rows(a):
    lead = a.shape[:-1]
    n = len(lead)
    a = a.reshape(*lead, N_MEM_HEADS, LANE_TILES_PER_HEAD, LANES)
    return jnp.swapaxes(a, n, n + 1).reshape(*lead, HEAD_ROWS, LANES)


def _xattn_sample_kernel(q_ref, k_ref, v_ref, o_ref):
    n_mem = k_ref.shape[1]
    ones = jnp.ones((LANES, LANES), BF16)
    for b in range(q_ref.shape[0]):
        prod = (k_ref[b] * q_ref[b][None]).reshape(n_mem * HEAD_ROWS, LANES).astype(BF16)
        part = _dot(prod, ones).reshape(n_mem, HEAD_ROWS, LANES)
        s = (part + pltpu.roll(part, N_MEM_HEADS, axis=1)) * (MEM_HEAD_DIM ** -0.5)
        e = jnp.exp(s - jnp.max(s, axis=0, keepdims=True))
        pr = e / jnp.sum(e, axis=0, keepdims=True)
        o_ref[b] = jnp.sum(pr * v_ref[b], axis=0)


def _xattn_sample(q, k8, v8, layer, bb=SAMPLE_ATTN_SEQS):
    nb = q.shape[0]
    n_mem = k8.shape[2]
    kv_spec = pl.BlockSpec((None, bb, n_mem, HEAD_ROWS, LANES), lambda i: (layer, i, 0, 0, 0))
    out = pl.pallas_call(
        _xattn_sample_kernel,
        grid=(nb // bb,),
        in_specs=[pl.BlockSpec((bb, HEAD_ROWS, LANES), lambda i: (i, 0, 0)), kv_spec, kv_spec],
        out_specs=pl.BlockSpec((bb, HEAD_ROWS, LANES), lambda i: (i, 0, 0)),
        out_shape=jax.ShapeDtypeStruct((nb, HEAD_ROWS, LANES), F32),
        compiler_params=_cparams("parallel"),
        name="xattn_sample",
    )(q, k8, v8)
    return out


def _split_bf16(a):
    hi = a.astype(BF16)
    return hi, (a - hi.astype(F32)).astype(BF16)


def _top2(logits):
    n = logits.shape[-1]
    lane = lax.broadcasted_iota(jnp.int32, logits.shape, 1)
    m1 = jnp.max(logits, axis=-1, keepdims=True)
    i1 = jnp.min(jnp.where(logits == m1, lane, n), axis=-1, keepdims=True)
    rest = jnp.where(lane == i1, -jnp.inf, logits)
    m2 = jnp.max(rest, axis=-1, keepdims=True)
    i2 = jnp.min(jnp.where(rest == m2, lane, n), axis=-1, keepdims=True)
    e2 = jnp.exp(m2 - m1)
    denom = 1.0 + e2
    return i1, i2, 1.0 / denom, e2 / denom


def _add_attn_proj(x, rest, proj):
    if not proj:
        return x, rest
    a_ref, wo_ref = rest[:2]
    heads = [a_ref[:, dt * N_MEM_HEADS + h, :] for h in range(N_MEM_HEADS) for dt in range(LANE_TILES_PER_HEAD)]
    a = jnp.concatenate(heads, axis=-1).astype(BF16)
    return x + _dot(a, wo_ref[...]), rest[2:]


def _ffn_dense_kernel(x_ref, g_ref, *rest, final, proj, cast):
    x, rest = _add_attn_proj(x_ref[...], rest, proj)
    if final:
        gfin_ref, rest = rest[0], rest[1:]
    if cast:
        wg_ref, wu_ref, wd_ref, src_hbm, o_ref, dst_hbm, *cast_scratch = rest
        _cast_rows(pl.program_id(0), pl.num_programs(0), src_hbm, dst_hbm, *cast_scratch)
    else:
        wg_ref, wu_ref, wd_ref, o_ref = rest
    h = _rms(x, g_ref[...]).astype(BF16)
    act = (jax.nn.silu(_dot(h, wg_ref[...])) * _dot(h, wu_ref[...])).astype(BF16)
    out = x + _dot(act, wd_ref[...])
    o_ref[...] = _rms(out, gfin_ref[...]) if final else out


def _ffn_dense(x, g, wg, wu, wd, tm, g_final=None, attn=None, cast=None):
    r, d = x.shape
    final, proj = g_final is not None, attn is not None
    args, specs = [x, g], [pl.BlockSpec((tm, d), lambda i: (i, 0)), _param_spec(g)]
    if proj:
        args += list(attn)
        specs += [pl.BlockSpec((tm,) + attn[0].shape[1:], lambda i: (i, 0, 0)), _param_spec(attn[1], resident=True)]
    if final:
        args.append(g_final)
        specs.append(_param_spec(g_final))
    args += [wg, wu, wd]
    specs += [_param_spec(w, resident=True) for w in (wg, wu, wd)]
    out_specs = [pl.BlockSpec((tm, d), lambda i: (i, 0))]
    out_shape = [jax.ShapeDtypeStruct((r, d), F32)]
    scratch = []
    if cast is not None:
        in_spec, out_spec, shape, scratch = _cast_specs(cast, r // tm)
        args.append(cast)
        specs.append(in_spec)
        out_specs.append(out_spec)
        out_shape.append(shape)
    out = pl.pallas_call(
        functools.partial(_ffn_dense_kernel, final=final, proj=proj, cast=cast is not None),
        grid=(r // tm,),
        in_specs=specs,
        out_specs=out_specs,
        out_shape=out_shape,
        scratch_shapes=scratch,
        compiler_params=_cparams("arbitrary"),
        name="ffn_dense",
    )(*map(_param_array, args))
    return out if cast is not None else out[0]


def _moe_all_experts_kernel(x_ref, g_ref, wr_ref, *rest, final, proj):
    if proj:
        attn_refs, rest = rest[:2], rest[2:]
    if final:
        gfin_ref, rest = rest[0], rest[1:]
    wg_ref, wu_ref, wd_ref, o_ref, h_ref, acc_ref, i1_ref, i2_ref, g1_ref, g2_ref = rest
    e = pl.program_id(1)

    @pl.when(e == 0)
    def _():
        x = x_ref[...]
        if proj:
            x, _ = _add_attn_proj(x, attn_refs, proj)
        h = _rms(x, g_ref[...])
        h_ref[...] = h.astype(BF16)
        acc_ref[...] = x
        h_hi, h_lo = _split_bf16(h)
        w_hi, w_lo = _split_bf16(wr_ref[...])
        logits = _dot(h_hi, w_hi) + (_dot(h_hi, w_lo) + _dot(h_lo, w_hi))
        lane = lax.broadcasted_iota(jnp.int32, logits.shape, 1)
        logits = jnp.where(lane < N_EXPERTS, logits, -jnp.inf)
        i1_ref[...], i2_ref[...], g1_ref[...], g2_ref[...] = _top2(logits)

    gt, up = _gate_up(h_ref[...], wg_ref, wu_ref)
    y = _dot((jax.nn.silu(gt) * up).astype(BF16), wd_ref[...])
    c = jnp.where(i1_ref[...] == e, g1_ref[...], 0.0) + jnp.where(i2_ref[...] == e, g2_ref[...], 0.0)
    acc_ref[...] += c * y

    @pl.when(e == pl.num_programs(1) - 1)
    def _():
        out = acc_ref[...]
        o_ref[...] = _rms(out, gfin_ref[...]) if final else out


def _moe_all_experts(x, g, w_router, wg, wu, wd, tm, g_final=None, attn=None):
    r, d = x.shape
    n_e, f, _ = wd.shape
    final, proj = g_final is not None, attn is not None
    args = [x, g, w_router]
    specs = [pl.BlockSpec((tm, d), lambda i, e: (i, 0)), _param_spec(g), _param_spec(w_router)]
    if proj:
        args += list(attn)
        specs += [pl.BlockSpec((tm,) + attn[0].shape[1:], lambda i, e: (i, 0, 0)), _param_spec(attn[1])]
    if final:
        args.append(g_final)
        specs.append(_param_spec(g_final))
    args += [wg, wu, wd]
    specs += [pl.BlockSpec((None, d, f), lambda i, e: (e, 0, 0)),
              pl.BlockSpec((None, d, f), lambda i, e: (e, 0, 0)),
              pl.BlockSpec((None, f, d), lambda i, e: (e, 0, 0))]
    return pl.pallas_call(
        functools.partial(_moe_all_experts_kernel, final=final, proj=proj),
        grid=(r // tm, n_e),
        in_specs=specs,
        out_specs=pl.BlockSpec((tm, d), lambda i, e: (i, 0)),
        out_shape=jax.ShapeDtypeStruct((r, d), F32),
        scratch_shapes=[pltpu.VMEM((tm, d), BF16), pltpu.VMEM((tm, d), F32),
                        pltpu.VMEM((tm, 1), jnp.int32), pltpu.VMEM((tm, 1), jnp.int32),
                        pltpu.VMEM((tm, 1), F32), pltpu.VMEM((tm, 1), F32)],
        compiler_params=_cparams("parallel", "arbitrary"),
        name="moe_all_experts",
    )(*map(_param_array, args))


MOE_TILE = 512
TOP_K = 2
ROUTE_TILES_PER_STEP = 4
ROW_ALIGN = SUBLANES
COMPACT_ROWS = -(-(TOP_K * MOE_TILE + N_EXPERTS * (ROW_ALIGN - 1)) // (2 * SUBLANES)) * (2 * SUBLANES)
SEGMENT_BITS = (MOE_TILE // ROW_ALIGN).bit_length()


def _route_kernel(x_ref, g_ref, wrt_ref, earlier_ref, lp_ref, col_ref, cnt_ref):
    n_e = wrt_ref.shape[0]
    w_hi, w_lo = _split_bf16(wrt_ref[...])
    nt = (((1,), (1,)), ((), ()))
    dg = lambda a, b: lax.dot_general(a, b, nt, preferred_element_type=F32)
    for k in range(lp_ref.shape[0]):
        tile = slice(k * MOE_TILE, (k + 1) * MOE_TILE)
        h = _rms(x_ref[tile, :], g_ref[...])
        h_hi, h_lo = _split_bf16(h)
        logits = dg(w_hi, h_hi) + (dg(w_hi, h_lo) + dg(w_lo, h_hi))
        sub = lax.broadcasted_iota(jnp.int32, logits.shape, 0)
        m1 = jnp.max(logits, axis=0, keepdims=True)
        i1 = jnp.min(jnp.where(logits == m1, sub, n_e), axis=0, keepdims=True)
        rest = jnp.where(sub == i1, -jnp.inf, logits)
        m2 = jnp.max(rest, axis=0, keepdims=True)
        i2 = jnp.min(jnp.where(rest == m2, sub, n_e), axis=0, keepdims=True)
        e2 = jnp.exp(m2 - m1)
        denom = 1.0 + e2

        pick1 = jnp.where(sub == i1, 1.0, 0.0)
        pick2 = jnp.where(sub == i2, 1.0, 0.0)
        cnt = pick1 + pick2
        rank = _dot(cnt.astype(BF16), earlier_ref[...])
        n = jnp.sum(cnt, axis=1, keepdims=True)
        n_pad = jnp.floor((n + (ROW_ALIGN - 1)) * (1.0 / ROW_ALIGN)) * ROW_ALIGN
        base, acc = [], jnp.zeros((1, 1), F32)
        for e in range(n_e):
            base.append(acc)
            acc = acc + n_pad[e:e + 1, :]
        pos = rank + jnp.concatenate(base, axis=0)
        lp1 = jnp.sum(pick1 * pos, axis=0, keepdims=True)
        lp2 = jnp.sum(pick2 * pos, axis=0, keepdims=True)
        lp_ref[k, 0:1, :] = lp1.astype(jnp.int32)
        lp_ref[k, 1:2, :] = lp2.astype(jnp.int32)
        info = jnp.concatenate([lp1, lp2, 1.0 / denom, e2 / denom,
                                jnp.zeros((SUBLANES - 4, MOE_TILE), F32)], axis=0)
        col_ref[tile, :] = info.T
        cnt_ref[k] = jnp.broadcast_to(n, cnt_ref.shape[1:])


def _route(x, g, wrt):
    t, d = x.shape
    n_e = wrt.shape[0]
    n_tiles = t // MOE_TILE
    idx = jnp.arange(MOE_TILE)
    earlier = (idx[:, None] < idx[None, :]).astype(BF16)
    return pl.pallas_call(
        _route_kernel,
        grid=(n_tiles // ROUTE_TILES_PER_STEP,),
        in_specs=[pl.BlockSpec((ROUTE_TILES_PER_STEP * MOE_TILE, d), lambda j: (j, 0)), _param_spec(g),
                  _const_spec(wrt.shape), _const_spec(earlier.shape)],
        out_specs=[pl.BlockSpec((ROUTE_TILES_PER_STEP, TOP_K, MOE_TILE), lambda j: (j, 0, 0)),
                   pl.BlockSpec((ROUTE_TILES_PER_STEP * MOE_TILE, SUBLANES), lambda j: (j, 0)),
                   pl.BlockSpec((ROUTE_TILES_PER_STEP, n_e, LANES), lambda j: (j, 0, 0))],
        out_shape=[jax.ShapeDtypeStruct((n_tiles, TOP_K, MOE_TILE), jnp.int32),
                   jax.ShapeDtypeStruct((t, SUBLANES), F32),
                   jax.ShapeDtypeStruct((n_tiles, n_e, LANES), F32)],
        compiler_params=_cparams("parallel"),
        name="moe_route",
    )(x, _param_array(g), wrt, earlier)


def _segment_copies(src_ref, src_row, dst_ref, dst_row, groups, sem):
    out = []
    for bit in range(SEGMENT_BITS):
        rows = ROW_ALIGN << bit
        before = ((groups >> (bit + 1)) << (bit + 1)) * ROW_ALIGN
        cp = pltpu.make_async_copy(
            src_ref.at[pl.ds(pl.multiple_of(src_row + before, ROW_ALIGN), rows), :],
            dst_ref.at[pl.ds(pl.multiple_of(dst_row + before, ROW_ALIGN), rows), :], sem)
        out.append((((groups >> bit) & 1) == 1, cp))
    return out


def _start_all(copies):
    for pred, cp in copies:
        pl.when(pred)(cp.start)


def _wait_all(copies):
    for pred, cp in copies:
        pl.when(pred)(cp.wait)


def _dispatch_kernel(grp_ref, off_ref, dst_ref, pad_ref, x_ref, g_ref, lp_ref, xs_ref, comp_ref, sem):
    j = pl.program_id(0)
    n_steps = pl.num_programs(0)
    n_e = grp_ref.shape[1]
    slot = j % 2

    def copies(step, s):
        out = []
        for e in range(n_e):
            out += _segment_copies(comp_ref.at[s], off_ref[step, e], xs_ref, dst_ref[step, e],
                                   grp_ref[step, e], sem.at[s])
        return out

    h = _rms(x_ref[...], g_ref[...]).astype(BF16)
    row = lax.broadcasted_iota(jnp.int32, (COMPACT_ROWS, h.shape[0]), 0)
    onehot = jnp.where(row == lp_ref[0:1, :], 1.0, jnp.where(row == lp_ref[1:2, :], 1.0, 0.0)).astype(BF16)
    compact = _dot(onehot, h)

    @pl.when(j >= 2)
    def _():
        _wait_all(copies(j - 2, slot))

    comp_ref[slot] = compact
    _start_all(copies(j, slot))

    @pl.when(j == n_steps - 1)
    def _():
        _wait_all(copies(j, slot))

        @pl.when(j >= 1)
        def _():
            _wait_all(copies(j - 1, 1 - slot))

        comp_ref[slot, 0:MOE_TILE, :] = jnp.zeros((MOE_TILE, comp_ref.shape[2]), F32)
        tail = []
        for e in range(n_e):
            tail += _segment_copies(comp_ref.at[slot], 0, xs_ref, pad_ref[0, e], pad_ref[1, e], sem.at[slot])
        tile_groups = MOE_TILE // ROW_ALIGN
        groups_left = (xs_ref.shape[0] - pad_ref[2, 0]) // ROW_ALIGN
        min_total = TOP_K * x_ref.shape[0] * grp_ref.shape[0]
        for c in range((xs_ref.shape[0] - min_total) // MOE_TILE):
            tail += _segment_copies(comp_ref.at[slot], 0, xs_ref, pad_ref[2, 0] + c * MOE_TILE,
                                    jnp.clip(groups_left - c * tile_groups, 0, tile_groups), sem.at[slot])
        _start_all(tail)
        _wait_all(tail)


def _dispatch(x, g, lp, groups, off, dst, pad, n_rows):
    t, d = x.shape
    n_tiles = t // MOE_TILE
    grid_spec = pltpu.PrefetchScalarGridSpec(
        num_scalar_prefetch=4,
        grid=(n_tiles,),
        in_specs=[pl.BlockSpec((MOE_TILE, d), lambda j, *_: (j, 0)),
                  _param_spec(g),
                  pl.BlockSpec((None, TOP_K, MOE_TILE), lambda j, *_: (j, 0, 0))],
        out_specs=pl.BlockSpec(memory_space=pl.ANY),
        scratch_shapes=[pltpu.VMEM((2, COMPACT_ROWS, d), F32), pltpu.SemaphoreType.DMA((2,))])
    return pl.pallas_call(
        _dispatch_kernel,
        grid_spec=grid_spec,
        out_shape=jax.ShapeDtypeStruct((n_rows, d), F32),
        compiler_params=_cparams("arbitrary"),
        name="moe_dispatch",
    )(groups, off, dst, pad, x, _param_array(g), lp)


STEP_FFN, STEP_ZERO = 1, 2


def _grouped_ffn_kernel(grp_ref, mode_ref, xtile_ref, xs_ref, wg_ref, wu_ref, wd_ref, ys_ref):
    w = pl.program_id(0)

    @pl.when(mode_ref[w] == STEP_ZERO)
    def _():
        ys_ref[...] = jnp.zeros_like(ys_ref)

    @pl.when(mode_ref[w] == STEP_FFN)
    def _():
        gt, up = _gate_up(xs_ref[...].astype(BF16), wg_ref, wu_ref)
        ys_ref[...] = _dot((jax.nn.silu(gt) * up).astype(BF16), wd_ref[...])


def _grouped_ffn(xs, wg, wu, wd, grp, mode, xtile):
    n_e, f, d = wd.shape
    grid_spec = pltpu.PrefetchScalarGridSpec(
        num_scalar_prefetch=3,
        grid=(xs.shape[0] // MOE_TILE,),
        in_specs=[pl.BlockSpec((MOE_TILE, d), lambda w, grp, mode, xtile: (xtile[w], 0)),
                  pl.BlockSpec((None, d, f), lambda w, grp, *_: (grp[w], 0, 0)),
                  pl.BlockSpec((None, d, f), lambda w, grp, *_: (grp[w], 0, 0)),
                  pl.BlockSpec((None, f, d), lambda w, grp, *_: (grp[w], 0, 0))],
        out_specs=pl.BlockSpec((MOE_TILE, d), lambda w, *_: (w, 0)))
    return pl.pallas_call(
        _grouped_ffn_kernel,
        grid_spec=grid_spec,
        out_shape=jax.ShapeDtypeStruct(xs.shape, F32),
        compiler_params=_cparams("arbitrary"),
        name="moe_grouped_ffn",
    )(grp, mode, xtile, xs, wg, wu, wd)


def _combine_kernel(grp_ref, off_ref, dst_ref, x_ref, col_ref, *rest, final):
    if final:
        gfin_ref, rest = rest[0], rest[1:]
    ys_ref, o_ref, comp_ref, sem = rest
    j = pl.program_id(0)
    n_steps = pl.num_programs(0)
    n_e = grp_ref.shape[1]
    slot = j % 2

    def copies(step, s):
        out = []
        for e in range(n_e):
            out += _segment_copies(ys_ref, dst_ref[step, e], comp_ref.at[s], off_ref[step, e],
                                   grp_ref[step, e], sem.at[s])
        return out

    @pl.when(j == 0)
    def _():
        comp_ref[...] = jnp.zeros_like(comp_ref)
        _start_all(copies(0, 0))

    @pl.when(j + 1 < n_steps)
    def _():
        _start_all(copies(j + 1, 1 - slot))

    _wait_all(copies(j, slot))
    y = comp_ref[slot].astype(BF16)
    col = col_ref[...]
    row = lax.broadcasted_iota(jnp.int32, (col.shape[0], COMPACT_ROWS), 1)
    weights = jnp.zeros(row.shape, F32)
    for k in range(TOP_K):
        weights = jnp.where(row == col[:, k:k + 1].astype(jnp.int32), col[:, TOP_K + k:TOP_K + k + 1], weights)
    out = x_ref[...] + _dot(weights.astype(BF16), y)
    o_ref[...] = _rms(out, gfin_ref[...]) if final else out


def _combine(x, col, ys, groups, off, dst, g_final):
    t, d = x.shape
    n_tiles = t // MOE_TILE
    final = g_final is not None
    args = [x, col]
    specs = [pl.BlockSpec((MOE_TILE, d), lambda j, *_: (j, 0)),
             pl.BlockSpec((MOE_TILE, SUBLANES), lambda j, *_: (j, 0))]
    if final:
        args.append(g_final)
        specs.append(_param_spec(g_final))
    args.append(ys)
    specs.append(pl.BlockSpec(memory_space=pl.ANY))
    grid_spec = pltpu.PrefetchScalarGridSpec(
        num_scalar_prefetch=3,
        grid=(n_tiles,),
        in_specs=specs,
        out_specs=pl.BlockSpec((MOE_TILE, d), lambda j, *_: (j, 0)),
        scratch_shapes=[pltpu.VMEM((2, COMPACT_ROWS, d), F32), pltpu.SemaphoreType.DMA((2,))])
    return pl.pallas_call(
        functools.partial(_combine_kernel, final=final),
        grid_spec=grid_spec,
        out_shape=jax.ShapeDtypeStruct((t, d), F32),
        compiler_params=_cparams("arbitrary"),
        name="moe_combine",
    )(groups, off, dst, *map(_param_array, args))


def _moe_routed(x, g, w_router, wg, wu, wd, g_final):
    t, _ = x.shape
    n_e = wd.shape[0]
    n_tiles = t // MOE_TILE
    lp, col, cnt = _route(x, g, w_router.T)
    groups = (cnt[:, :, 0].astype(jnp.int32) + (ROW_ALIGN - 1)) // ROW_ALIGN
    n = groups * ROW_ALIGN
    off = jnp.cumsum(n, axis=1) - n
    per_e = jnp.sum(n, axis=0)
    tiles_e = (per_e + (MOE_TILE - 1)) // MOE_TILE
    tile_end = jnp.cumsum(tiles_e)
    starts = (tile_end - tiles_e) * MOE_TILE
    dst = starts[None, :] + jnp.cumsum(n, axis=0) - n
    pad = jnp.stack([starts + per_e, (tiles_e * MOE_TILE - per_e) // ROW_ALIGN,
                     jnp.broadcast_to(tile_end[-1] * MOE_TILE, per_e.shape)])
    max_rows = TOP_K * t + n_tiles * n_e * (ROW_ALIGN - 1)
    n_row_tiles = -(-max_rows // MOE_TILE) + n_e
    xs = _dispatch(x, g, lp, groups, off, dst, pad, n_row_tiles * MOE_TILE)

    w = jnp.arange(n_row_tiles, dtype=jnp.int32)
    used = w < tile_end[-1]
    grp = jnp.minimum(jnp.sum(w[:, None] >= tile_end[None, :], axis=1), n_e - 1).astype(jnp.int32)
    mode = jnp.where(used, STEP_FFN, STEP_ZERO).astype(jnp.int32)
    xtile = jnp.minimum(w, tile_end[-1] - 1)
    ys = _grouped_ffn(xs, wg, wu, wd, grp, mode, xtile)
    return _combine(x, col, ys, groups, off, dst, g_final)


def _block_diag_gates(w_a, w_x):
    per_group = LRU_GROUP // LRU_HEAD_DIM
    eye = jnp.eye(per_group, dtype=w_a.dtype)

    def bd(w):
        w = w.reshape(D_LRU // LRU_GROUP, per_group, LRU_HEAD_DIM, LRU_HEAD_DIM)
        return jnp.einsum('ghij,hk->ghikj', w, eye).reshape(D_LRU // LRU_GROUP, LRU_GROUP, LRU_GROUP)

    return jnp.concatenate([bd(w_a), bd(w_x)], axis=-1).astype(BF16)


def kernel(x_prompt, x_sample, state_shortconv, state_lru_conv, state_lru_h, cache_mem_k, cache_mem_v,
           mem_prompt, norm_mix, w_in, w_short_conv, w_lru_conv, b_lru_conv, w_lru_a, b_lru_a, w_lru_x,
           b_lru_x, lru_lambda, norm_conv_out, norm_lru_out, w_mix_out, norm_xattn, norm_mem, w_q, w_k,
           w_v, w_o, norm_ffn, w_ff_gate, w_ff_up, w_ff_down, w_router, w_moe_gate, w_moe_up, w_moe_down,
           norm_final):
    depth = w_in.shape[0]
    nb, t, d = x_prompt.shape
    ns = x_sample.shape[0]
    n_mem = mem_prompt.shape[1]
    rows = lambda a: a.reshape(a.shape[0], 1, -1)
    stacks = dict(
        gmix=rows(norm_mix), win=w_in.astype(BF16), wsc=w_short_conv, wlc=w_lru_conv, blc=rows(b_lru_conv),
        wg=jax.vmap(_block_diag_gates)(w_lru_a, w_lru_x), ba=rows(b_lru_a), bx=rows(b_lru_x),
        lam=rows(lru_lambda), gco=rows(norm_conv_out), glo=rows(norm_lru_out), wout=w_mix_out.astype(BF16),
        gx=rows(norm_xattn), wq=w_q.astype(BF16), wo=w_o.astype(BF16), gffn=rows(norm_ffn))
    g_final = norm_final.reshape(1, -1)
    ff_dense_f32 = (w_ff_gate, w_ff_up, w_ff_down)
    ff_moe_f32 = (w_moe_gate, w_moe_up, w_moe_down)

    layers = []
    for l in range(depth):
        lp = {name: (stack, l) for name, stack in stacks.items()}
        lp['gfin'] = g_final if l == depth - 1 else None
        j = l // 2
        if l % 2 == 0:
            lp['ffw'] = None
            lp['wr'] = None
        else:
            lp['ffw'] = None
            lp['wr'] = jnp.pad(w_router[j], ((0, 0), (0, LANES - N_EXPERTS)))
            lp['wr_raw'] = w_router[j]
        layers.append(lp)

    x = x_prompt
    p_mem_k, p_mem_v, kb, vb = _kv_proj(mem_prompt, norm_mem.reshape(depth, 1, d), w_k.astype(BF16),
                                        w_v.astype(BF16))
    pc, plc, ph = [], [], []
    moe_bf16 = {}

    def cast_for(j, k):
        w = ff_moe_f32[k][j]
        return w.reshape(-1, w.shape[-1])

    for l, lp in enumerate(layers):
        x, tail_p, tail_x, h_last = _mixer_prompt(x, lp)
        if lp['wr'] is None:
            casts = [w[l // 2] for w in ff_dense_f32]
            if l + 1 < depth:
                casts.append(cast_for((l + 1) // 2, 0))
            x, done = _xattn_prompt(x, lp['gx'], lp['wq'], kb, vb, lp['wo'], l, casts=casts)
            lp['ffw'] = tuple(done[:3])
            if l + 1 < depth:
                moe_bf16[(l + 1) // 2, 0] = done[3]
        else:
            x, (moe_bf16[l // 2, 1],) = _xattn_prompt(x, lp['gx'], lp['wq'], kb, vb, lp['wo'], l,
                                                      casts=[cast_for(l // 2, 1)])
        if lp['wr'] is None:
            if l + 1 < depth:
                x, moe_bf16[(l + 1) // 2, 2] = _ffn_dense(x.reshape(nb * t, d), lp['gffn'], *lp['ffw'], tm=FFN_ROWS,
                                                          g_final=lp['gfin'], cast=cast_for((l + 1) // 2, 2))
            else:
                x = _ffn_dense(x.reshape(nb * t, d), lp['gffn'], *lp['ffw'], tm=FFN_ROWS, g_final=lp['gfin'])
        else:
            j = l // 2
            lp['ffw'] = tuple(moe_bf16[j, k].reshape(ff_moe_f32[k][j].shape) for k in range(3))
            x = _moe_routed(x.reshape(nb * t, d), lp['gffn'], lp['wr_raw'], *lp['ffw'], lp['gfin'])
        x = x.reshape(nb, t, d)
        pc.append(tail_p)
        plc.append(tail_x)
        ph.append(h_last)
    y_prompt = x

    x = x_sample.reshape(ns, d)
    k8 = _to_head_rows(cache_mem_k.reshape(depth, ns, n_mem, d))
    v8 = _to_head_rows(cache_mem_v.reshape(depth, ns, n_mem, d))
    sc, slc, sh = [], [], []
    for l, lp in enumerate(layers):
        x, q, sc_new, lc_new, h_new = _mixer_sample(x, (state_shortconv, l), (state_lru_conv, l),
                                                    (state_lru_h, l), lp)
        attn = (_xattn_sample(q, k8, v8, l), lp['wo'])
        if lp['wr'] is None:
            x = _ffn_dense(x, lp['gffn'], *lp['ffw'], tm=ns, g_final=lp['gfin'], attn=attn)
        else:
            x = _moe_all_experts(x, lp['gffn'], lp['wr'], *lp['ffw'], tm=ns, g_final=lp['gfin'], attn=attn)
        sc.append(sc_new)
        slc.append(lc_new)
        sh.append(h_new)
    y_sample = x.reshape(ns, 1, d)

    return (y_prompt, y_sample, jnp.stack(pc), jnp.stack(plc), jnp.stack(ph), p_mem_k, p_mem_v,
            jnp.stack(sc), jnp.stack(slc), jnp.stack(sh))
```

```python
import functools

import jax
import jax.numpy as jnp
from jax import lax
from jax.experimental import pallas as pl
from jax.experimental.pallas import tpu as pltpu

F32 = jnp.float32
BF16 = jnp.bfloat16

EPS = 1e-6
LRU_C = 8.0
D_CONV = 512
D_LRU = 512
LRU_HEAD_DIM = 64
LRU_GROUP = 256
N_MEM_HEADS = 4
MEM_HEAD_DIM = 256
N_EXPERTS = 8
SUBLANES = 8
LANES = 128
MXU_COLS = 256
VMEM_LIMIT_BYTES = 56 * 1024 * 1024

MIXER_TIME_CHUNK = 128
XATTN_ROWS = 1024
FFN_ROWS = 512
KV_PROJ_SEQS = 4
SAMPLE_ATTN_SEQS = 4
SAMPLE_ATTN_BUFFERS = 3


def _cparams(*sem):
    return pltpu.CompilerParams(dimension_semantics=sem, vmem_limit_bytes=VMEM_LIMIT_BYTES)


def _rms(x, g):
    return x * lax.rsqrt(jnp.mean(x * x, axis=-1, keepdims=True) + EPS) * g


def _dot(a, b):
    return jnp.dot(a, b, preferred_element_type=F32)


def _sigmoid(x):
    return 0.5 * jnp.tanh(0.5 * x) + 0.5


def _gate_up(x, wg_ref, wu_ref):
    f = wg_ref.shape[1]
    main = f // MXU_COLS * MXU_COLS
    if main == f or 2 * (f - main) != MXU_COLS:
        return _dot(x, wg_ref[...]), _dot(x, wu_ref[...])
    rest = _dot(x, jnp.concatenate([wg_ref[:, main:], wu_ref[:, main:]], axis=1))
    gt = jnp.concatenate([_dot(x, wg_ref[:, :main]), rest[:, :f - main]], axis=1)
    up = jnp.concatenate([_dot(x, wu_ref[:, :main]), rest[:, f - main:]], axis=1)
    return gt, up


def _lru_gates(xc, wg_ref, ba, bx, lam):
    neg_lam = -lam
    softplus = jnp.maximum(neg_lam, 0.0) + jnp.log1p(jnp.exp(-jnp.abs(neg_lam)))
    a_parts, u_parts = [], []
    for g in range(D_LRU // LRU_GROUP):
        sl = slice(g * LRU_GROUP, (g + 1) * LRU_GROUP)
        xg = xc[:, sl]
        gates = _dot(xg.astype(BF16), wg_ref[g])
        r = _sigmoid(gates[:, :LRU_GROUP] + ba[:, sl])
        i = _sigmoid(gates[:, LRU_GROUP:] + bx[:, sl])
        log_a = (-LRU_C) * r * softplus[:, sl]
        th = jnp.tanh(log_a)
        one_minus_a2 = (-2.0 * th) / (1.0 - th)
        a_parts.append(jnp.exp(log_a))
        u_parts.append(jnp.sqrt(one_minus_a2) * (i * xg))
    return jnp.concatenate(a_parts, axis=-1), jnp.concatenate(u_parts, axis=-1)


def _mix_out(x, ya, yb, gco, glo, wout_ref):
    y = jnp.concatenate([_rms(ya, gco), _rms(yb, glo)], axis=-1).astype(BF16)
    return x + _dot(y, wout_ref[...])


MAX_CONV_HISTORY = 3


def _conv_taps(ext_ref, w_ref, rows, nb):
    width = w_ref.shape[0]
    hist = MAX_CONV_HISTORY * nb
    acc = ext_ref[hist:hist + rows, :] * w_ref[width - 1:width, :]
    for s in range(1, width):
        acc = acc + ext_ref[hist - s * nb:hist - s * nb + rows, :] * w_ref[width - 1 - s:width - s, :]
    return acc


def _mixer_prompt_kernel(x_hbm, gmix_ref, win_ref, wsc_ref, wlc_ref, blc_ref, wg_ref, ba_ref, bx_ref,
                         lam_ref, gco_ref, glo_ref, wout_ref,
                         xo_hbm, tailp_ref, tailx_ref, hout_ref,
                         xin_ref, xout_ref, in_sem, out_sem, pext_ref, xext_ref, h_ref, a_ref, u_ref):
    _, tt, nb, d = xin_ref.shape
    rows = tt * nb
    hist = MAX_CONV_HISTORY * nb
    i = pl.program_id(0)
    n_steps = pl.num_programs(0)
    slot = i % 2

    def in_copies(step, s):
        return [pltpu.make_async_copy(x_hbm.at[b, pl.ds(step * tt, tt), :], xin_ref.at[s, :, b, :],
                                      in_sem.at[s]) for b in range(nb)]

    def out_copies(step, s):
        return [pltpu.make_async_copy(xout_ref.at[s, :, b, :], xo_hbm.at[b, pl.ds(step * tt, tt), :],
                                      out_sem.at[s]) for b in range(nb)]

    @pl.when(i == 0)
    def _():
        pext_ref[0:hist, :] = jnp.zeros((hist, pext_ref.shape[1]), F32)
        xext_ref[0:hist, :] = jnp.zeros((hist, xext_ref.shape[1]), F32)
        h_ref[...] = jnp.zeros_like(h_ref)
        for cp in in_copies(0, 0):
            cp.start()

    @pl.when(i + 1 < n_steps)
    def _():
        for cp in in_copies(i + 1, 1 - slot):
            cp.start()

    for cp in in_copies(i, slot):
        cp.wait()
    x = xin_ref[slot].reshape(rows, d)
    z = _dot(_rms(x, gmix_ref[...]).astype(BF16), win_ref[...])
    gb = z[:, 0:D_CONV]
    gate = z[:, 3 * D_CONV + D_LRU:]
    pext_ref[hist:hist + rows, :] = z[:, D_CONV:2 * D_CONV] * z[:, 2 * D_CONV:3 * D_CONV]
    xext_ref[hist:hist + rows, :] = z[:, 3 * D_CONV:3 * D_CONV + D_LRU]

    ya = gb * _conv_taps(pext_ref, wsc_ref, rows, nb)
    xc = _conv_taps(xext_ref, wlc_ref, rows, nb) + blc_ref[...]
    for ext_ref, tail_ref in ((pext_ref, tailp_ref), (xext_ref, tailx_ref)):
        tail = ext_ref[rows:rows + hist, :]
        ext_ref[0:hist, :] = tail
        keep = tail_ref.shape[1]
        for k in range(keep):
            step0 = (MAX_CONV_HISTORY - keep + k) * nb
            tail_ref[:, k, :] = tail[step0:step0 + nb, :]

    a, u = _lru_gates(xc, wg_ref, ba_ref[...], bx_ref[...], lam_ref[...])
    a_ref[...] = a
    u_ref[...] = u

    def step(t, h):
        grp = pl.ds(pl.multiple_of(t * nb, nb), nb)
        h = a_ref[grp, :] * h + u_ref[grp, :]
        u_ref[grp, :] = h
        return h

    h_last = lax.fori_loop(0, tt, step, h_ref[...], unroll=8)
    h_ref[...] = h_last
    hout_ref[...] = h_last

    yb = u_ref[...] * jax.nn.gelu(gate)
    out = _mix_out(x, ya, yb, gco_ref[...], glo_ref[...], wout_ref)

    @pl.when(i >= 2)
    def _():
        for cp in out_copies(i - 2, slot):
            cp.wait()

    xout_ref[slot] = out.reshape(tt, nb, d)
    for cp in out_copies(i, slot):
        cp.start()

    @pl.when(i == n_steps - 1)
    def _():
        for cp in out_copies(i, slot):
            cp.wait()

        @pl.when(i >= 1)
        def _():
            for cp in out_copies(i - 1, 1 - slot):
                cp.wait()


def _const_spec(shape):
    return pl.BlockSpec(shape, lambda *_: (0,) * len(shape))


def _param_array(p):
    return p[0] if isinstance(p, tuple) else p


def _param_spec(p, resident=False):
    a, layer = p if isinstance(p, tuple) else (p, None)
    mode = dict(pipeline_mode=pl.Buffered(1)) if resident else {}
    if layer is None:
        return pl.BlockSpec(a.shape, lambda *_: (0,) * a.ndim, **mode)
    return pl.BlockSpec((None,) + a.shape[1:], lambda *_: (layer,) + (0,) * (a.ndim - 1), **mode)


def _cast_specs(w, n_steps):
    rows = w.shape[0] // n_steps
    assert rows * n_steps == w.shape[0] and rows % (2 * SUBLANES) == 0
    any_spec = pl.BlockSpec(memory_space=pl.ANY)
    scratch = [pltpu.VMEM((2, rows, w.shape[1]), F32), pltpu.VMEM((2, rows, w.shape[1]), BF16),
               pltpu.SemaphoreType.DMA((2,)), pltpu.SemaphoreType.DMA((2,))]
    return any_spec, any_spec, jax.ShapeDtypeStruct(w.shape, BF16), scratch


def _cast_rows(step, n_steps, src_hbm, dst_hbm, in_buf, out_buf, in_sem, out_sem):
    rows = in_buf.shape[1]
    slot = step % 2

    def in_copy(s, sl):
        return pltpu.make_async_copy(src_hbm.at[pl.ds(pl.multiple_of(s * rows, rows), rows), :],
                                     in_buf.at[sl], in_sem.at[sl])

    def out_copy(s, sl):
        return pltpu.make_async_copy(out_buf.at[sl],
                                     dst_hbm.at[pl.ds(pl.multiple_of(s * rows, rows), rows), :], out_sem.at[sl])

    pl.when(step == 0)(in_copy(0, 0).start)
    pl.when(step + 1 < n_steps)(in_copy(step + 1, 1 - slot).start)
    in_copy(step, slot).wait()
    pl.when(step >= 2)(out_copy(step - 2, slot).wait)
    out_buf[slot] = in_buf[slot].astype(BF16)
    out_copy(step, slot).start()

    @pl.when(step == n_steps - 1)
    def _():
        out_copy(step, slot).wait()
        pl.when(step >= 1)(out_copy(step - 1, 1 - slot).wait)


def _mixer_prompt(x, p, tt=MIXER_TIME_CHUNK):
    nb, t, d = x.shape
    small = [p['gmix'], p['win'], p['wsc'], p['wlc'], p['blc'], p['wg'], p['ba'], p['bx'], p['lam'],
             p['gco'], p['glo'], p['wout']]
    rows = nb * tt
    hist = MAX_CONV_HISTORY * nb
    conv_state = (nb, _param_array(p['wsc']).shape[1] - 1, D_CONV)
    lru_state = (nb, _param_array(p['wlc']).shape[1] - 1, D_LRU)
    return pl.pallas_call(
        _mixer_prompt_kernel,
        grid=(t // tt,),
        in_specs=[pl.BlockSpec(memory_space=pl.ANY)] + [_param_spec(a, resident=True) for a in small],
        out_specs=[pl.BlockSpec(memory_space=pl.ANY),
                   _const_spec(conv_state), _const_spec(lru_state), _const_spec((nb, D_LRU))],
        out_shape=[jax.ShapeDtypeStruct((nb, t, d), F32),
                   jax.ShapeDtypeStruct(conv_state, F32),
                   jax.ShapeDtypeStruct(lru_state, F32),
                   jax.ShapeDtypeStruct((nb, D_LRU), F32)],
        scratch_shapes=[pltpu.VMEM((2, tt, nb, d), F32), pltpu.VMEM((2, tt, nb, d), F32),
                        pltpu.SemaphoreType.DMA((2,)), pltpu.SemaphoreType.DMA((2,)),
                        pltpu.VMEM((hist + rows, D_CONV), F32), pltpu.VMEM((hist + rows, D_LRU), F32),
                        pltpu.VMEM((nb, D_LRU), F32),
                        pltpu.VMEM((rows, D_LRU), F32), pltpu.VMEM((rows, D_LRU), F32)],
        compiler_params=_cparams("arbitrary"),
        name="mixer_prompt",
    )(x, *map(_param_array, small))


def _mixer_sample_kernel(x_ref, sc_ref, lc_ref, h0_ref,
                         gmix_ref, win_ref, wsc_ref, wlc_ref, blc_ref, wg_ref, ba_ref, bx_ref,
                         lam_ref, gco_ref, glo_ref, wout_ref, gx_ref, wq_ref,
                         xo_ref, q_ref, sc_new_ref, lc_new_ref, hout_ref):
    x = x_ref[...]
    z = _dot(_rms(x, gmix_ref[...]).astype(BF16), win_ref[...])
    gb = z[:, 0:D_CONV]
    p = z[:, D_CONV:2 * D_CONV] * z[:, 2 * D_CONV:3 * D_CONV]
    xb = z[:, 3 * D_CONV:3 * D_CONV + D_LRU]
    gate = z[:, 3 * D_CONV + D_LRU:]

    sc = [sc_ref[:, k, :] for k in range(sc_ref.shape[1])] + [p]
    lc = [lc_ref[:, k, :] for k in range(lc_ref.shape[1])] + [xb]
    ya = gb * sum(v * wsc_ref[k:k + 1, :] for k, v in enumerate(sc))
    xc = sum(v * wlc_ref[k:k + 1, :] for k, v in enumerate(lc)) + blc_ref[...]
    for k, v in enumerate(sc[1:]):
        sc_new_ref[:, k, :] = v
    for k, v in enumerate(lc[1:]):
        lc_new_ref[:, k, :] = v
    a, u = _lru_gates(xc, wg_ref, ba_ref[...], bx_ref[...], lam_ref[...])
    h = a * h0_ref[...] + u
    yb = h * jax.nn.gelu(gate)
    x1 = _mix_out(x, ya, yb, gco_ref[...], glo_ref[...], wout_ref)
    xo_ref[...] = x1
    q = _dot(_rms(x1, gx_ref[...]).astype(BF16), wq_ref[...])
    for r in range(HEAD_ROWS):
        col0 = (r % N_MEM_HEADS) * MEM_HEAD_DIM + (r // N_MEM_HEADS) * LANES
        q_ref[:, r, :] = q[:, col0:col0 + LANES]
    hout_ref[...] = h


def _mixer_sample(x, sc, lc, h0, p):
    nb, d = x.shape
    sc_shape, lc_shape = _param_array(sc).shape[-3:], _param_array(lc).shape[-3:]
    args = [x, sc, lc, h0,
            p['gmix'], p['win'], p['wsc'], p['wlc'], p['blc'], p['wg'], p['ba'], p['bx'], p['lam'],
            p['gco'], p['glo'], p['wout'], p['gx'], p['wq']]
    return pl.pallas_call(
        _mixer_sample_kernel,
        grid=(1,),
        in_specs=[_param_spec(a) for a in args],
        out_specs=[_const_spec((nb, d)), _const_spec((nb, HEAD_ROWS, LANES)), _const_spec(sc_shape),
                   _const_spec(lc_shape), _const_spec((nb, D_LRU))],
        out_shape=[jax.ShapeDtypeStruct((nb, d), F32), jax.ShapeDtypeStruct((nb, HEAD_ROWS, LANES), F32),
                   jax.ShapeDtypeStruct(sc_shape, F32), jax.ShapeDtypeStruct(lc_shape, F32),
                   jax.ShapeDtypeStruct((nb, D_LRU), F32)],
        compiler_params=_cparams("arbitrary"),
        name="mixer_sample",
    )(*map(_param_array, args))


def _xattn_prompt_kernel(x_ref, g_ref, wq_ref, k_ref, v_ref, wo_ref, *rest, n_cast):
    o_ref = rest[n_cast]
    step = pl.program_id(0) * pl.num_programs(1) + pl.program_id(1)
    for c in range(n_cast):
        _cast_rows(step, pl.num_programs(0) * pl.num_programs(1), rest[c], rest[n_cast + 1 + c],
                   *rest[2 * n_cast + 1 + 4 * c:2 * n_cast + 5 + 4 * c])
    x = x_ref[...]
    q = _dot(_rms(x, g_ref[...]).astype(BF16), wq_ref[...])
    heads = []
    for h in range(N_MEM_HEADS):
        sl = slice(h * MEM_HEAD_DIM, (h + 1) * MEM_HEAD_DIM)
        kh = k_ref[:, sl]
        vh = v_ref[:, sl]
        s = lax.dot_general(q[:, sl].astype(BF16), kh, (((1,), (1,)), ((), ())),
                            preferred_element_type=F32) * (MEM_HEAD_DIM ** -0.5)
        e = jnp.exp(s - jnp.max(s, axis=-1, keepdims=True))
        pr = e / jnp.sum(e, axis=-1, keepdims=True)
        heads.append(_dot(pr.astype(BF16), vh))
    o = jnp.concatenate(heads, axis=-1).astype(BF16)
    o_ref[...] = x + _dot(o, wo_ref[...])


def _xattn_prompt(x, g, wq, k, v, wo, layer, tq=XATTN_ROWS, casts=()):
    nb, t, d = x.shape
    n_mem = k.shape[2]
    kv_spec = pl.BlockSpec((None, None, n_mem, d), lambda b, i: (layer, b, 0, 0))
    args = [x, _param_array(g), _param_array(wq), k, v, _param_array(wo)]
    in_specs = [pl.BlockSpec((None, tq, d), lambda b, i: (b, i, 0)), _param_spec(g),
                _param_spec(wq, resident=True), kv_spec, kv_spec, _param_spec(wo, resident=True)]
    out_specs = [pl.BlockSpec((None, tq, d), lambda b, i: (b, i, 0))]
    out_shape = [jax.ShapeDtypeStruct((nb, t, d), F32)]
    scratch = []
    for w in casts:
        in_spec, out_spec, shape, job_scratch = _cast_specs(w, nb * (t // tq))
        args.append(w)
        in_specs.append(in_spec)
        out_specs.append(out_spec)
        out_shape.append(shape)
        scratch += job_scratch
    out = pl.pallas_call(
        functools.partial(_xattn_prompt_kernel, n_cast=len(casts)),
        grid=(nb, t // tq),
        in_specs=in_specs,
        out_specs=out_specs,
        out_shape=out_shape,
        scratch_shapes=scratch,
        compiler_params=_cparams("arbitrary", "arbitrary"),
        name="xattn_prompt",
    )(*args)
    return out[0], out[1:]


def _kv_proj_kernel(mem_ref, g_ref, wk_ref, wv_ref, k5_ref, v5_ref, kb_ref, vb_ref):
    bb, n_mem, d = mem_ref.shape
    m = _rms(mem_ref[...].reshape(bb * n_mem, d), g_ref[...]).astype(BF16)
    for w_ref, o5_ref, ob_ref in ((wk_ref, k5_ref, kb_ref), (wv_ref, v5_ref, vb_ref)):
        y = _dot(m, w_ref[...]).reshape(bb, n_mem, d)
        ob_ref[...] = y.astype(BF16)
        for h in range(N_MEM_HEADS):
            o5_ref[:, :, h, :] = y[:, :, h * MEM_HEAD_DIM:(h + 1) * MEM_HEAD_DIM]


def _kv_proj(mem, gmem, wk, wv, bb=KV_PROJ_SEQS):
    nb, n_mem, d = mem.shape
    depth = wk.shape[0]
    w_spec = pl.BlockSpec((None, d, d), lambda l, i: (l, 0, 0))
    o5_spec = pl.BlockSpec((None, bb, n_mem, N_MEM_HEADS, MEM_HEAD_DIM), lambda l, i: (l, i, 0, 0, 0))
    ob_spec = pl.BlockSpec((None, bb, n_mem, d), lambda l, i: (l, i, 0, 0))
    o5_shape = jax.ShapeDtypeStruct((depth, nb, n_mem, N_MEM_HEADS, MEM_HEAD_DIM), F32)
    ob_shape = jax.ShapeDtypeStruct((depth, nb, n_mem, d), BF16)
    return pl.pallas_call(
        _kv_proj_kernel,
        grid=(depth, nb // bb),
        in_specs=[pl.BlockSpec((bb, n_mem, d), lambda l, i: (i, 0, 0)),
                  pl.BlockSpec((None, 1, d), lambda l, i: (l, 0, 0)), w_spec, w_spec],
        out_specs=[o5_spec, o5_spec, ob_spec, ob_spec],
        out_shape=[o5_shape, o5_shape, ob_shape, ob_shape],
        compiler_params=_cparams("parallel", "parallel"),
        name="kv_proj",
    )(mem, gmem, wk, wv)


LANE_TILES_PER_HEAD = MEM_HEAD_DIM // LANES
HEAD_ROWS = N_MEM_HEADS * LANE_TILES_PER_HEAD


def _to_head_rows(a):
    lead = a.shape[:-1]
    n = len(lead)
    a = a.reshape(*lead, N_MEM_HEADS, LANE_TILES_PER_HEAD, LANES)
    return jnp.swapaxes(a, n, n + 1).reshape(*lead, HEAD_ROWS, LANES)


def _xattn_sample_kernel(q_ref, k_hbm, v_hbm, o_ref, kbuf, vbuf, ksem, vsem, *, layer):
    n_buf, bb, n_mem = kbuf.shape[:3]
    i = pl.program_id(0)
    n_steps = pl.num_programs(0)

    def copies(step):
        s = step % n_buf
        seqs = pl.ds(step * bb, bb)
        return [pltpu.make_async_copy(k_hbm.at[layer, seqs], kbuf.at[s], ksem.at[s]),
                pltpu.make_async_copy(v_hbm.at[layer, seqs], vbuf.at[s], vsem.at[s])]

    @pl.when(i == 0)
    def _():
        for step in range(n_buf - 1):
            @pl.when(step < n_steps)
            def _():
                for cp in copies(step):
                    cp.start()

    @pl.when(i + n_buf - 1 < n_steps)
    def _():
        for cp in copies(i + n_buf - 1):
            cp.start()

    for cp in copies(i):
        cp.wait()
    slot = i % n_buf
    ones = jnp.ones((LANES, LANES), BF16)
    for b in range(bb):
        prod = (kbuf[slot, b] * q_ref[b][None]).reshape(n_mem * HEAD_ROWS, LANES).astype(BF16)
        part = _dot(prod, ones).reshape(n_mem, HEAD_ROWS, LANES)
        s = (part + pltpu.roll(part, N_MEM_HEADS, axis=1)) * (MEM_HEAD_DIM ** -0.5)
        e = jnp.exp(s - jnp.max(s, axis=0, keepdims=True))
        pr = e / jnp.sum(e, axis=0, keepdims=True)
        o_ref[b] = jnp.sum(pr * vbuf[slot, b], axis=0)


def _xattn_sample(q, k8, v8, layer, bb=SAMPLE_ATTN_SEQS, n_buf=SAMPLE_ATTN_BUFFERS):
    nb = q.shape[0]
    n_mem = k8.shape[2]
    any_spec = pl.BlockSpec(memory_space=pl.ANY)
    buf = pltpu.VMEM((n_buf, bb, n_mem, HEAD_ROWS, LANES), F32)
    return pl.pallas_call(
        functools.partial(_xattn_sample_kernel, layer=layer),
        grid=(nb // bb,),
        in_specs=[pl.BlockSpec((bb, HEAD_ROWS, LANES), lambda i: (i, 0, 0)), any_spec, any_spec],
        out_specs=pl.BlockSpec((bb, HEAD_ROWS, LANES), lambda i: (i, 0, 0)),
        out_shape=jax.ShapeDtypeStruct((nb, HEAD_ROWS, LANES), F32),
        scratch_shapes=[buf, buf, pltpu.SemaphoreType.DMA((n_buf,)), pltpu.SemaphoreType.DMA((n_buf,))],
        compiler_params=_cparams("arbitrary"),
        name="xattn_sample",
    )(q, k8, v8)


def _split_bf16(a):
    hi = a.astype(BF16)
    return hi, (a - hi.astype(F32)).astype(BF16)


def _top2(logits):
    n = logits.shape[-1]
    lane = lax.broadcasted_iota(jnp.int32, logits.shape, 1)
    m1 = jnp.max(logits, axis=-1, keepdims=True)
    i1 = jnp.min(jnp.where(logits == m1, lane, n), axis=-1, keepdims=True)
    rest = jnp.where(lane == i1, -jnp.inf, logits)
    m2 = jnp.max(rest, axis=-1, keepdims=True)
    i2 = jnp.min(jnp.where(rest == m2, lane, n), axis=-1, keepdims=True)
    e2 = jnp.exp(m2 - m1)
    denom = 1.0 + e2
    return i1, i2, 1.0 / denom, e2 / denom


def _add_attn_proj(x, rest, proj):
    if not proj:
        return x, rest
    a_ref, wo_ref = rest[:2]
    heads = [a_ref[:, dt * N_MEM_HEADS + h, :] for h in range(N_MEM_HEADS) for dt in range(LANE_TILES_PER_HEAD)]
    a = jnp.concatenate(heads, axis=-1).astype(BF16)
    return x + _dot(a, wo_ref[...]), rest[2:]


def _ffn_dense_kernel(x_ref, g_ref, *rest, final, proj, cast):
    x, rest = _add_attn_proj(x_ref[...], rest, proj)
    if final:
        gfin_ref, rest = rest[0], rest[1:]
    if cast:
        wg_ref, wu_ref, wd_ref, src_hbm, o_ref, dst_hbm, *cast_scratch = rest
        _cast_rows(pl.program_id(0), pl.num_programs(0), src_hbm, dst_hbm, *cast_scratch)
    else:
        wg_ref, wu_ref, wd_ref, o_ref = rest
    h = _rms(x, g_ref[...]).astype(BF16)
    act = (jax.nn.silu(_dot(h, wg_ref[...])) * _dot(h, wu_ref[...])).astype(BF16)
    out = x + _dot(act, wd_ref[...])
    o_ref[...] = _rms(out, gfin_ref[...]) if final else out


def _ffn_dense(x, g, wg, wu, wd, tm, g_final=None, attn=None, cast=None):
    r, d = x.shape
    final, proj = g_final is not None, attn is not None
    args, specs = [x, g], [pl.BlockSpec((tm, d), lambda i: (i, 0)), _param_spec(g)]
    if proj:
        args += list(attn)
        specs += [pl.BlockSpec((tm,) + attn[0].shape[1:], lambda i: (i, 0, 0)), _param_spec(attn[1], resident=True)]
    if final:
        args.append(g_final)
        specs.append(_param_spec(g_final))
    args += [wg, wu, wd]
    specs += [_param_spec(w, resident=True) for w in (wg, wu, wd)]
    out_specs = [pl.BlockSpec((tm, d), lambda i: (i, 0))]
    out_shape = [jax.ShapeDtypeStruct((r, d), F32)]
    scratch = []
    if cast is not None:
        in_spec, out_spec, shape, scratch = _cast_specs(cast, r // tm)
        args.append(cast)
        specs.append(in_spec)
        out_specs.append(out_spec)
        out_shape.append(shape)
    out = pl.pallas_call(
        functools.partial(_ffn_dense_kernel, final=final, proj=proj, cast=cast is not None),
        grid=(r // tm,),
        in_specs=specs,
        out_specs=out_specs,
        out_shape=out_shape,
        scratch_shapes=scratch,
        compiler_params=_cparams("arbitrary"),
        name="ffn_dense",
    )(*map(_param_array, args))
    return out if cast is not None else out[0]


def _moe_all_experts_kernel(x_ref, g_ref, wr_ref, *rest, final, proj):
    if proj:
        attn_refs, rest = rest[:2], rest[2:]
    if final:
        gfin_ref, rest = rest[0], rest[1:]
    wg_ref, wu_ref, wd_ref, o_ref, h_ref, acc_ref, i1_ref, i2_ref, g1_ref, g2_ref = rest
    e = pl.program_id(1)

    @pl.when(e == 0)
    def _():
        x = x_ref[...]
        if proj:
            x, _ = _add_attn_proj(x, attn_refs, proj)
        h = _rms(x, g_ref[...])
        h_ref[...] = h.astype(BF16)
        acc_ref[...] = x
        h_hi, h_lo = _split_bf16(h)
        w_hi, w_lo = _split_bf16(wr_ref[...])
        logits = _dot(h_hi, w_hi) + (_dot(h_hi, w_lo) + _dot(h_lo, w_hi))
        lane = lax.broadcasted_iota(jnp.int32, logits.shape, 1)
        logits = jnp.where(lane < N_EXPERTS, logits, -jnp.inf)
        i1_ref[...], i2_ref[...], g1_ref[...], g2_ref[...] = _top2(logits)

    gt, up = _gate_up(h_ref[...], wg_ref, wu_ref)
    y = _dot((jax.nn.silu(gt) * up).astype(BF16), wd_ref[...])
    c = jnp.where(i1_ref[...] == e, g1_ref[...], 0.0) + jnp.where(i2_ref[...] == e, g2_ref[...], 0.0)
    acc_ref[...] += c * y

    @pl.when(e == pl.num_programs(1) - 1)
    def _():
        out = acc_ref[...]
        o_ref[...] = _rms(out, gfin_ref[...]) if final else out


def _moe_all_experts(x, g, w_router, wg, wu, wd, tm, g_final=None, attn=None):
    r, d = x.shape
    n_e, f, _ = wd.shape
    final, proj = g_final is not None, attn is not None
    args = [x, g, w_router]
    specs = [pl.BlockSpec((tm, d), lambda i, e: (i, 0)), _param_spec(g), _param_spec(w_router)]
    if proj:
        args += list(attn)
        specs += [pl.BlockSpec((tm,) + attn[0].shape[1:], lambda i, e: (i, 0, 0)), _param_spec(attn[1])]
    if final:
        args.append(g_final)
        specs.append(_param_spec(g_final))
    args += [wg, wu, wd]
    specs += [pl.BlockSpec((None, d, f), lambda i, e: (e, 0, 0)),
              pl.BlockSpec((None, d, f), lambda i, e: (e, 0, 0)),
              pl.BlockSpec((None, f, d), lambda i, e: (e, 0, 0))]
    return pl.pallas_call(
        functools.partial(_moe_all_experts_kernel, final=final, proj=proj),
        grid=(r // tm, n_e),
        in_specs=specs,
        out_specs=pl.BlockSpec((tm, d), lambda i, e: (i, 0)),
        out_shape=jax.ShapeDtypeStruct((r, d), F32),
        scratch_shapes=[pltpu.VMEM((tm, d), BF16), pltpu.VMEM((tm, d), F32),
                        pltpu.VMEM((tm, 1), jnp.int32), pltpu.VMEM((tm, 1), jnp.int32),
                        pltpu.VMEM((tm, 1), F32), pltpu.VMEM((tm, 1), F32)],
        compiler_params=_cparams("parallel", "arbitrary"),
        name="moe_all_experts",
    )(*map(_param_array, args))


MOE_TILE = 512
TOP_K = 2
ROUTE_TILES_PER_STEP = 4
SCATTER_TILES_PER_STEP = 2
ROW_ALIGN = SUBLANES
COMPACT_ROWS = -(-(TOP_K * MOE_TILE + N_EXPERTS * (ROW_ALIGN - 1)) // (2 * SUBLANES)) * (2 * SUBLANES)
SEGMENT_BITS = (MOE_TILE // ROW_ALIGN).bit_length()


def _route_kernel(x_ref, g_ref, wrt_ref, earlier_ref, lp_ref, col_ref, cnt_ref):
    n_e = wrt_ref.shape[0]
    w_hi, w_lo = _split_bf16(wrt_ref[...])
    nt = (((1,), (1,)), ((), ()))
    dg = lambda a, b: lax.dot_general(a, b, nt, preferred_element_type=F32)
    for k in range(lp_ref.shape[0]):
        tile = slice(k * MOE_TILE, (k + 1) * MOE_TILE)
        h = _rms(x_ref[tile, :], g_ref[...])
        h_hi, h_lo = _split_bf16(h)
        logits = dg(w_hi, h_hi) + (dg(w_hi, h_lo) + dg(w_lo, h_hi))
        sub = lax.broadcasted_iota(jnp.int32, logits.shape, 0)
        m1 = jnp.max(logits, axis=0, keepdims=True)
        i1 = jnp.min(jnp.where(logits == m1, sub, n_e), axis=0, keepdims=True)
        rest = jnp.where(sub == i1, -jnp.inf, logits)
        m2 = jnp.max(rest, axis=0, keepdims=True)
        i2 = jnp.min(jnp.where(rest == m2, sub, n_e), axis=0, keepdims=True)
        e2 = jnp.exp(m2 - m1)
        denom = 1.0 + e2

        pick1 = jnp.where(sub == i1, 1.0, 0.0)
        pick2 = jnp.where(sub == i2, 1.0, 0.0)
        cnt = pick1 + pick2
        rank = _dot(cnt.astype(BF16), earlier_ref[...])
        n = jnp.sum(cnt, axis=1, keepdims=True)
        n_pad = jnp.floor((n + (ROW_ALIGN - 1)) * (1.0 / ROW_ALIGN)) * ROW_ALIGN
        base, acc = [], jnp.zeros((1, 1), F32)
        for e in range(n_e):
            base.append(acc)
            acc = acc + n_pad[e:e + 1, :]
        pos = rank + jnp.concatenate(base, axis=0)
        lp1 = jnp.sum(pick1 * pos, axis=0, keepdims=True)
        lp2 = jnp.sum(pick2 * pos, axis=0, keepdims=True)
        lp_ref[k, 0:1, :] = lp1.astype(jnp.int32)
        lp_ref[k, 1:2, :] = lp2.astype(jnp.int32)
        info = jnp.concatenate([lp1, lp2, 1.0 / denom, e2 / denom,
                                jnp.zeros((SUBLANES - 4, MOE_TILE), F32)], axis=0)
        col_ref[tile, :] = info.T
        cnt_ref[k] = jnp.broadcast_to(n, cnt_ref.shape[1:])


def _route(x, g, wrt):
    t, d = x.shape
    n_e = wrt.shape[0]
    n_tiles = t // MOE_TILE
    idx = jnp.arange(MOE_TILE)
    earlier = (idx[:, None] < idx[None, :]).astype(BF16)
    return pl.pallas_call(
        _route_kernel,
        grid=(n_tiles // ROUTE_TILES_PER_STEP,),
        in_specs=[pl.BlockSpec((ROUTE_TILES_PER_STEP * MOE_TILE, d), lambda j: (j, 0)), _param_spec(g),
                  _const_spec(wrt.shape), _const_spec(earlier.shape)],
        out_specs=[pl.BlockSpec((ROUTE_TILES_PER_STEP, TOP_K, MOE_TILE), lambda j: (j, 0, 0)),
                   pl.BlockSpec((ROUTE_TILES_PER_STEP * MOE_TILE, SUBLANES), lambda j: (j, 0)),
                   pl.BlockSpec((ROUTE_TILES_PER_STEP, n_e, LANES), lambda j: (j, 0, 0))],
        out_shape=[jax.ShapeDtypeStruct((n_tiles, TOP_K, MOE_TILE), jnp.int32),
                   jax.ShapeDtypeStruct((t, SUBLANES), F32),
                   jax.ShapeDtypeStruct((n_tiles, n_e, LANES), F32)],
        compiler_params=_cparams("parallel"),
        name="moe_route",
    )(x, _param_array(g), wrt, earlier)


def _segment_copies(src_ref, src_row, dst_ref, dst_row, groups, sem):
    out = []
    for bit in range(SEGMENT_BITS):
        rows = ROW_ALIGN << bit
        before = ((groups >> (bit + 1)) << (bit + 1)) * ROW_ALIGN
        cp = pltpu.make_async_copy(
            src_ref.at[pl.ds(pl.multiple_of(src_row + before, ROW_ALIGN), rows), :],
            dst_ref.at[pl.ds(pl.multiple_of(dst_row + before, ROW_ALIGN), rows), :], sem)
        out.append((((groups >> bit) & 1) == 1, cp))
    return out


def _start_all(copies):
    for pred, cp in copies:
        pl.when(pred)(cp.start)


def _wait_all(copies):
    for pred, cp in copies:
        pl.when(pred)(cp.wait)


def _dispatch_kernel(grp_ref, off_ref, dst_ref, pad_ref, x_ref, g_ref, lp_ref, xs_ref, comp_ref, sem):
    j = pl.program_id(0)
    n_steps = pl.num_programs(0)
    n_e = grp_ref.shape[1]
    per_step = lp_ref.shape[0]
    slot = j % 2

    def copies(step, s):
        out = []
        for k in range(per_step):
            tile = step * per_step + k
            for e in range(n_e):
                out += _segment_copies(comp_ref.at[s * per_step + k], off_ref[tile, e], xs_ref, dst_ref[tile, e],
                                       grp_ref[tile, e], sem.at[s])
        return out

    compact = []
    for k in range(per_step):
        h = _rms(x_ref[k * MOE_TILE:(k + 1) * MOE_TILE, :], g_ref[...]).astype(BF16)
        row = lax.broadcasted_iota(jnp.int32, (COMPACT_ROWS, MOE_TILE), 0)
        onehot = jnp.where(row == lp_ref[k, 0:1, :], 1.0,
                           jnp.where(row == lp_ref[k, 1:2, :], 1.0, 0.0)).astype(BF16)
        compact.append(_dot(onehot, h))

    @pl.when(j >= 2)
    def _():
        _wait_all(copies(j - 2, slot))

    for k in range(per_step):
        comp_ref[slot * per_step + k] = compact[k]
    _start_all(copies(j, slot))

    @pl.when(j == n_steps - 1)
    def _():
        _wait_all(copies(j, slot))

        @pl.when(j >= 1)
        def _():
            _wait_all(copies(j - 1, 1 - slot))

        zeros_ref = comp_ref.at[slot * per_step]
        zeros_ref[0:MOE_TILE, :] = jnp.zeros((MOE_TILE, comp_ref.shape[2]), F32)
        tail = []
        for e in range(n_e):
            tail += _segment_copies(zeros_ref, 0, xs_ref, pad_ref[0, e], pad_ref[1, e], sem.at[slot])
        tile_groups = MOE_TILE // ROW_ALIGN
        groups_left = (xs_ref.shape[0] - pad_ref[2, 0]) // ROW_ALIGN
        min_total = TOP_K * MOE_TILE * grp_ref.shape[0]
        for c in range((xs_ref.shape[0] - min_total) // MOE_TILE):
            tail += _segment_copies(zeros_ref, 0, xs_ref, pad_ref[2, 0] + c * MOE_TILE,
                                    jnp.clip(groups_left - c * tile_groups, 0, tile_groups), sem.at[slot])
        _start_all(tail)
        _wait_all(tail)


def _dispatch(x, g, lp, groups, off, dst, pad, n_rows):
    t, d = x.shape
    n_tiles = t // MOE_TILE
    grid_spec = pltpu.PrefetchScalarGridSpec(
        num_scalar_prefetch=4,
        grid=(n_tiles // SCATTER_TILES_PER_STEP,),
        in_specs=[pl.BlockSpec((SCATTER_TILES_PER_STEP * MOE_TILE, d), lambda j, *_: (j, 0)),
                  _param_spec(g),
                  pl.BlockSpec((SCATTER_TILES_PER_STEP, TOP_K, MOE_TILE), lambda j, *_: (j, 0, 0))],
        out_specs=pl.BlockSpec(memory_space=pl.ANY),
        scratch_shapes=[pltpu.VMEM((2 * SCATTER_TILES_PER_STEP, COMPACT_ROWS, d), F32),
                        pltpu.SemaphoreType.DMA((2,))])
    return pl.pallas_call(
        _dispatch_kernel,
        grid_spec=grid_spec,
        out_shape=jax.ShapeDtypeStruct((n_rows, d), F32),
        compiler_params=_cparams("arbitrary"),
        name="moe_dispatch",
    )(groups, off, dst, pad, x, _param_array(g), lp)


STEP_FFN, STEP_ZERO = 1, 2


def _grouped_ffn_kernel(grp_ref, mode_ref, xtile_ref, xs_ref, wg_ref, wu_ref, wd_ref, ys_ref):
    w = pl.program_id(0)

    @pl.when(mode_ref[w] == STEP_ZERO)
    def _():
        ys_ref[...] = jnp.zeros_like(ys_ref)

    @pl.when(mode_ref[w] == STEP_FFN)
    def _():
        gt, up = _gate_up(xs_ref[...].astype(BF16), wg_ref, wu_ref)
        ys_ref[...] = _dot((jax.nn.silu(gt) * up).astype(BF16), wd_ref[...])


def _grouped_ffn(xs, wg, wu, wd, grp, mode, xtile):
    n_e, f, d = wd.shape
    grid_spec = pltpu.PrefetchScalarGridSpec(
        num_scalar_prefetch=3,
        grid=(xs.shape[0] // MOE_TILE,),
        in_specs=[pl.BlockSpec((MOE_TILE, d), lambda w, grp, mode, xtile: (xtile[w], 0)),
                  pl.BlockSpec((None, d, f), lambda w, grp, *_: (grp[w], 0, 0)),
                  pl.BlockSpec((None, d, f), lambda w, grp, *_: (grp[w], 0, 0)),
                  pl.BlockSpec((None, f, d), lambda w, grp, *_: (grp[w], 0, 0))],
        out_specs=pl.BlockSpec((MOE_TILE, d), lambda w, *_: (w, 0)))
    return pl.pallas_call(
        _grouped_ffn_kernel,
        grid_spec=grid_spec,
        out_shape=jax.ShapeDtypeStruct(xs.shape, F32),
        compiler_params=_cparams("arbitrary"),
        name="moe_grouped_ffn",
    )(grp, mode, xtile, xs, wg, wu, wd)


def _combine_kernel(grp_ref, off_ref, dst_ref, x_ref, col_ref, *rest, final):
    if final:
        gfin_ref, rest = rest[0], rest[1:]
    ys_ref, o_ref, comp_ref, sem = rest
    j = pl.program_id(0)
    n_steps = pl.num_programs(0)
    n_e = grp_ref.shape[1]
    per_step = x_ref.shape[0] // MOE_TILE
    slot = j % 2

    def copies(step, s):
        out = []
        for k in range(per_step):
            tile = step * per_step + k
            for e in range(n_e):
                out += _segment_copies(ys_ref, dst_ref[tile, e], comp_ref.at[s * per_step + k], off_ref[tile, e],
                                       grp_ref[tile, e], sem.at[s])
        return out

    @pl.when(j == 0)
    def _():
        comp_ref[...] = jnp.zeros_like(comp_ref)
        _start_all(copies(0, 0))

    @pl.when(j + 1 < n_steps)
    def _():
        _start_all(copies(j + 1, 1 - slot))

    _wait_all(copies(j, slot))
    for t in range(per_step):
        rows = slice(t * MOE_TILE, (t + 1) * MOE_TILE)
        y = comp_ref[slot * per_step + t].astype(BF16)
        col = col_ref[rows, :]
        row = lax.broadcasted_iota(jnp.int32, (MOE_TILE, COMPACT_ROWS), 1)
        weights = jnp.zeros(row.shape, F32)
        for k in range(TOP_K):
            weights = jnp.where(row == col[:, k:k + 1].astype(jnp.int32), col[:, TOP_K + k:TOP_K + k + 1],
                                weights)
        out = x_ref[rows, :] + _dot(weights.astype(BF16), y)
        o_ref[rows, :] = _rms(out, gfin_ref[...]) if final else out


def _combine(x, col, ys, groups, off, dst, g_final):
    t, d = x.shape
    n_tiles = t // MOE_TILE
    final = g_final is not None
    rows = SCATTER_TILES_PER_STEP * MOE_TILE
    args = [x, col]
    specs = [pl.BlockSpec((rows, d), lambda j, *_: (j, 0)), pl.BlockSpec((rows, SUBLANES), lambda j, *_: (j, 0))]
    if final:
        args.append(g_final)
        specs.append(_param_spec(g_final))
    args.append(ys)
    specs.append(pl.BlockSpec(memory_space=pl.ANY))
    grid_spec = pltpu.PrefetchScalarGridSpec(
        num_scalar_prefetch=3,
        grid=(n_tiles // SCATTER_TILES_PER_STEP,),
        in_specs=specs,
        out_specs=pl.BlockSpec((rows, d), lambda j, *_: (j, 0)),
        scratch_shapes=[pltpu.VMEM((2 * SCATTER_TILES_PER_STEP, COMPACT_ROWS, d), F32),
                        pltpu.SemaphoreType.DMA((2,))])
    return pl.pallas_call(
        functools.partial(_combine_kernel, final=final),
        grid_spec=grid_spec,
        out_shape=jax.ShapeDtypeStruct((t, d), F32),
        compiler_params=_cparams("arbitrary"),
        name="moe_combine",
    )(groups, off, dst, *map(_param_array, args))


def _moe_routed(x, g, w_router, wg, wu, wd, g_final):
    t, _ = x.shape
    n_e = wd.shape[0]
    n_tiles = t // MOE_TILE
    lp, col, cnt = _route(x, g, w_router.T)
    groups = (cnt[:, :, 0].astype(jnp.int32) + (ROW_ALIGN - 1)) // ROW_ALIGN
    n = groups * ROW_ALIGN
    off = jnp.cumsum(n, axis=1) - n
    per_e = jnp.sum(n, axis=0)
    tiles_e = (per_e + (MOE_TILE - 1)) // MOE_TILE
    tile_end = jnp.cumsum(tiles_e)
    starts = (tile_end - tiles_e) * MOE_TILE
    dst = starts[None, :] + jnp.cumsum(n, axis=0) - n
    pad = jnp.stack([starts + per_e, (tiles_e * MOE_TILE - per_e) // ROW_ALIGN,
                     jnp.broadcast_to(tile_end[-1] * MOE_TILE, per_e.shape)])
    max_rows = TOP_K * t + n_tiles * n_e * (ROW_ALIGN - 1)
    n_row_tiles = -(-max_rows // MOE_TILE) + n_e
    xs = _dispatch(x, g, lp, groups, off, dst, pad, n_row_tiles * MOE_TILE)

    w = jnp.arange(n_row_tiles, dtype=jnp.int32)
    used = w < tile_end[-1]
    grp = jnp.minimum(jnp.sum(w[:, None] >= tile_end[None, :], axis=1), n_e - 1).astype(jnp.int32)
    mode = jnp.where(used, STEP_FFN, STEP_ZERO).astype(jnp.int32)
    xtile = jnp.minimum(w, tile_end[-1] - 1)
    ys = _grouped_ffn(xs, wg, wu, wd, grp, mode, xtile)
    return _combine(x, col, ys, groups, off, dst, g_final)


def _block_diag_gates(w_a, w_x):
    per_group = LRU_GROUP // LRU_HEAD_DIM
    eye = jnp.eye(per_group, dtype=w_a.dtype)

    def bd(w):
        w = w.reshape(D_LRU // LRU_GROUP, per_group, LRU_HEAD_DIM, LRU_HEAD_DIM)
        return jnp.einsum('ghij,hk->ghikj', w, eye).reshape(D_LRU // LRU_GROUP, LRU_GROUP, LRU_GROUP)

    return jnp.concatenate([bd(w_a), bd(w_x)], axis=-1).astype(BF16)


def kernel(x_prompt, x_sample, state_shortconv, state_lru_conv, state_lru_h, cache_mem_k, cache_mem_v,
           mem_prompt, norm_mix, w_in, w_short_conv, w_lru_conv, b_lru_conv, w_lru_a, b_lru_a, w_lru_x,
           b_lru_x, lru_lambda, norm_conv_out, norm_lru_out, w_mix_out, norm_xattn, norm_mem, w_q, w_k,
           w_v, w_o, norm_ffn, w_ff_gate, w_ff_up, w_ff_down, w_router, w_moe_gate, w_moe_up, w_moe_down,
           norm_final):
    depth = w_in.shape[0]
    nb, t, d = x_prompt.shape
    ns = x_sample.shape[0]
    n_mem = mem_prompt.shape[1]
    rows = lambda a: a.reshape(a.shape[0], 1, -1)
    stacks = dict(
        gmix=rows(norm_mix), win=w_in.astype(BF16), wsc=w_short_conv, wlc=w_lru_conv, blc=rows(b_lru_conv),
        wg=jax.vmap(_block_diag_gates)(w_lru_a, w_lru_x), ba=rows(b_lru_a), bx=rows(b_lru_x),
        lam=rows(lru_lambda), gco=rows(norm_conv_out), glo=rows(norm_lru_out), wout=w_mix_out.astype(BF16),
        gx=rows(norm_xattn), wq=w_q.astype(BF16), wo=w_o.astype(BF16), gffn=rows(norm_ffn))
    g_final = norm_final.reshape(1, -1)
    ff_dense_f32 = (w_ff_gate, w_ff_up, w_ff_down)
    ff_moe_f32 = (w_moe_gate, w_moe_up, w_moe_down)

    layers = []
    for l in range(depth):
        lp = {name: (stack, l) for name, stack in stacks.items()}
        lp['gfin'] = g_final if l == depth - 1 else None
        j = l // 2
        if l % 2 == 0:
            lp['ffw'] = None
            lp['wr'] = None
        else:
            lp['ffw'] = None
            lp['wr'] = jnp.pad(w_router[j], ((0, 0), (0, LANES - N_EXPERTS)))
            lp['wr_raw'] = w_router[j]
        layers.append(lp)

    x = x_prompt
    p_mem_k, p_mem_v, kb, vb = _kv_proj(mem_prompt, norm_mem.reshape(depth, 1, d), w_k.astype(BF16),
                                        w_v.astype(BF16))
    pc, plc, ph = [], [], []
    moe_bf16 = {}

    def cast_for(j, k):
        w = ff_moe_f32[k][j]
        return w.reshape(-1, w.shape[-1])

    for l, lp in enumerate(layers):
        x, tail_p, tail_x, h_last = _mixer_prompt(x, lp)
        if lp['wr'] is None:
            casts = [w[l // 2] for w in ff_dense_f32]
            if l + 1 < depth:
                casts.append(cast_for((l + 1) // 2, 0))
            x, done = _xattn_prompt(x, lp['gx'], lp['wq'], kb, vb, lp['wo'], l, casts=casts)
            lp['ffw'] = tuple(done[:3])
            if l + 1 < depth:
                moe_bf16[(l + 1) // 2, 0] = done[3]
        else:
            x, (moe_bf16[l // 2, 1],) = _xattn_prompt(x, lp['gx'], lp['wq'], kb, vb, lp['wo'], l,
                                                      casts=[cast_for(l // 2, 1)])
        if lp['wr'] is None:
            if l + 1 < depth:
                x, moe_bf16[(l + 1) // 2, 2] = _ffn_dense(x.reshape(nb * t, d), lp['gffn'], *lp['ffw'], tm=FFN_ROWS,
                                                          g_final=lp['gfin'], cast=cast_for((l + 1) // 2, 2))
            else:
                x = _ffn_dense(x.reshape(nb * t, d), lp['gffn'], *lp['ffw'], tm=FFN_ROWS, g_final=lp['gfin'])
        else:
            j = l // 2
            lp['ffw'] = tuple(moe_bf16[j, k].reshape(ff_moe_f32[k][j].shape) for k in range(3))
            x = _moe_routed(x.reshape(nb * t, d), lp['gffn'], lp['wr_raw'], *lp['ffw'], lp['gfin'])
        x = x.reshape(nb, t, d)
        pc.append(tail_p)
        plc.append(tail_x)
        ph.append(h_last)
    y_prompt = x

    x = x_sample.reshape(ns, d)
    k8 = _to_head_rows(cache_mem_k.reshape(depth, ns, n_mem, d))
    v8 = _to_head_rows(cache_mem_v.reshape(depth, ns, n_mem, d))
    sc, slc, sh = [], [], []
    for l, lp in enumerate(layers):
        x, q, sc_new, lc_new, h_new = _mixer_sample(x, (state_shortconv, l), (state_lru_conv, l),
                                                    (state_lru_h, l), lp)
        attn = (_xattn_sample(q, k8, v8, l), lp['wo'])
        if lp['wr'] is None:
            x = _ffn_dense(x, lp['gffn'], *lp['ffw'], tm=ns, g_final=lp['gfin'], attn=attn)
        else:
            x = _moe_all_experts(x, lp['gffn'], lp['wr'], *lp['ffw'], tm=ns, g_final=lp['gfin'], attn=attn)
        sc.append(sc_new)
        slc.append(lc_new)
        sh.append(h_new)
    y_sample = x.reshape(ns, 1, d)

    return (y_prompt, y_sample, jnp.stack(pc), jnp.stack(plc), jnp.stack(ph), p_mem_k, p_mem_v,
            jnp.stack(sc), jnp.stack(slc), jnp.stack(sh))
```

```python
import functools

import jax
import jax.numpy as jnp
from jax import lax
from jax.experimental import pallas as pl
from jax.experimental.pallas import tpu as pltpu

F32 = jnp.float32
BF16 = jnp.bfloat16

EPS = 1e-6
LRU_C = 8.0
D_CONV = 512
D_LRU = 512
LRU_HEAD_DIM = 64
LRU_GROUP = 256
N_MEM_HEADS = 4
MEM_HEAD_DIM = 256
N_EXPERTS = 8
SUBLANES = 8
LANES = 128
MXU_COLS = 256
VMEM_LIMIT_BYTES = 56 * 1024 * 1024

MIXER_TIME_CHUNK = 128
XATTN_ROWS = 1024
FFN_ROWS = 512
KV_PROJ_SEQS = 4
SAMPLE_ATTN_SEQS = 8


def _cparams(*sem):
    return pltpu.CompilerParams(dimension_semantics=sem, vmem_limit_bytes=VMEM_LIMIT_BYTES)


def _rms(x, g):
    return x * lax.rsqrt(jnp.mean(x * x, axis=-1, keepdims=True) + EPS) * g


def _dot(a, b):
    return jnp.dot(a, b, preferred_element_type=F32)


def _sigmoid(x):
    return 0.5 * jnp.tanh(0.5 * x) + 0.5


def _gate_up(x, wg_ref, wu_ref):
    f = wg_ref.shape[1]
    main = f // MXU_COLS * MXU_COLS
    if main == f or 2 * (f - main) != MXU_COLS:
        return _dot(x, wg_ref[...]), _dot(x, wu_ref[...])
    rest = _dot(x, jnp.concatenate([wg_ref[:, main:], wu_ref[:, main:]], axis=1))
    gt = jnp.concatenate([_dot(x, wg_ref[:, :main]), rest[:, :f - main]], axis=1)
    up = jnp.concatenate([_dot(x, wu_ref[:, :main]), rest[:, f - main:]], axis=1)
    return gt, up


def _lru_gates(xc, wg_ref, ba, bx, lam):
    neg_lam = -lam
    softplus = jnp.maximum(neg_lam, 0.0) + jnp.log1p(jnp.exp(-jnp.abs(neg_lam)))
    a_parts, u_parts = [], []
    for g in range(D_LRU // LRU_GROUP):
        sl = slice(g * LRU_GROUP, (g + 1) * LRU_GROUP)
        xg = xc[:, sl]
        gates = _dot(xg.astype(BF16), wg_ref[g])
        r = _sigmoid(gates[:, :LRU_GROUP] + ba[:, sl])
        i = _sigmoid(gates[:, LRU_GROUP:] + bx[:, sl])
        log_a = (-LRU_C) * r * softplus[:, sl]
        th = jnp.tanh(log_a)
        one_minus_a2 = (-2.0 * th) / (1.0 - th)
        a_parts.append(jnp.exp(log_a))
        u_parts.append(jnp.sqrt(one_minus_a2) * (i * xg))
    return jnp.concatenate(a_parts, axis=-1), jnp.concatenate(u_parts, axis=-1)


def _mix_out(x, ya, yb, gco, glo, wout_ref):
    y = jnp.concatenate([_rms(ya, gco), _rms(yb, glo)], axis=-1).astype(BF16)
    return x + _dot(y, wout_ref[...])


MAX_CONV_HISTORY = 3


def _conv_taps(ext_ref, w_ref, rows, nb):
    width = w_ref.shape[0]
    hist = MAX_CONV_HISTORY * nb
    acc = ext_ref[hist:hist + rows, :] * w_ref[width - 1:width, :]
    for s in range(1, width):
        acc = acc + ext_ref[hist - s * nb:hist - s * nb + rows, :] * w_ref[width - 1 - s:width - s, :]
    return acc


def _mixer_prompt_kernel(x_hbm, gmix_ref, win_ref, wsc_ref, wlc_ref, blc_ref, wg_ref, ba_ref, bx_ref,
                         lam_ref, gco_ref, glo_ref, wout_ref,
                         xo_hbm, tailp_ref, tailx_ref, hout_ref,
                         xin_ref, xout_ref, in_sem, out_sem, pext_ref, xext_ref, h_ref, a_ref, u_ref):
    _, tt, nb, d = xin_ref.shape
    rows = tt * nb
    hist = MAX_CONV_HISTORY * nb
    i = pl.program_id(0)
    n_steps = pl.num_programs(0)
    slot = i % 2

    def in_copies(step, s):
        return [pltpu.make_async_copy(x_hbm.at[b, pl.ds(step * tt, tt), :], xin_ref.at[s, :, b, :],
                                      in_sem.at[s]) for b in range(nb)]

    def out_copies(step, s):
        return [pltpu.make_async_copy(xout_ref.at[s, :, b, :], xo_hbm.at[b, pl.ds(step * tt, tt), :],
                                      out_sem.at[s]) for b in range(nb)]

    @pl.when(i == 0)
    def _():
        pext_ref[0:hist, :] = jnp.zeros((hist, pext_ref.shape[1]), F32)
        xext_ref[0:hist, :] = jnp.zeros((hist, xext_ref.shape[1]), F32)
        h_ref[...] = jnp.zeros_like(h_ref)
        for b, cp in enumerate(in_copies(0, 0)):
            cp.start(priority=b % 2)

    @pl.when(i + 1 < n_steps)
    def _():
        for b, cp in enumerate(in_copies(i + 1, 1 - slot)):
            cp.start(priority=b % 2)

    for cp in in_copies(i, slot):
        cp.wait()
    x = xin_ref[slot].reshape(rows, d)
    z = _dot(_rms(x, gmix_ref[...]).astype(BF16), win_ref[...])
    gb = z[:, 0:D_CONV]
    gate = z[:, 3 * D_CONV + D_LRU:]
    pext_ref[hist:hist + rows, :] = z[:, D_CONV:2 * D_CONV] * z[:, 2 * D_CONV:3 * D_CONV]
    xext_ref[hist:hist + rows, :] = z[:, 3 * D_CONV:3 * D_CONV + D_LRU]

    ya = gb * _conv_taps(pext_ref, wsc_ref, rows, nb)
    xc = _conv_taps(xext_ref, wlc_ref, rows, nb) + blc_ref[...]
    for ext_ref, tail_ref in ((pext_ref, tailp_ref), (xext_ref, tailx_ref)):
        tail = ext_ref[rows:rows + hist, :]
        ext_ref[0:hist, :] = tail
        keep = tail_ref.shape[1]
        for k in range(keep):
            step0 = (MAX_CONV_HISTORY - keep + k) * nb
            tail_ref[:, k, :] = tail[step0:step0 + nb, :]

    a, u = _lru_gates(xc, wg_ref, ba_ref[...], bx_ref[...], lam_ref[...])
    a_ref[...] = a
    u_ref[...] = u

    def step(t, h):
        grp = pl.ds(pl.multiple_of(t * nb, nb), nb)
        h = a_ref[grp, :] * h + u_ref[grp, :]
        u_ref[grp, :] = h
        return h

    h_last = lax.fori_loop(0, tt, step, h_ref[...], unroll=8)
    h_ref[...] = h_last
    hout_ref[...] = h_last

    yb = u_ref[...] * jax.nn.gelu(gate)
    out = _mix_out(x, ya, yb, gco_ref[...], glo_ref[...], wout_ref)

    @pl.when(i >= 2)
    def _():
        for cp in out_copies(i - 2, slot):
            cp.wait()

    xout_ref[slot] = out.reshape(tt, nb, d)
    for b, cp in enumerate(out_copies(i, slot)):
        cp.start(priority=b % 2)

    @pl.when(i == n_steps - 1)
    def _():
        for cp in out_copies(i, slot):
            cp.wait()

        @pl.when(i >= 1)
        def _():
            for cp in out_copies(i - 1, 1 - slot):
                cp.wait()


def _const_spec(shape):
    return pl.BlockSpec(shape, lambda *_: (0,) * len(shape))


def _param_array(p):
    return p[0] if isinstance(p, tuple) else p


def _param_spec(p, resident=False):
    a, layer = p if isinstance(p, tuple) else (p, None)
    mode = dict(pipeline_mode=pl.Buffered(1)) if resident else {}
    if layer is None:
        return pl.BlockSpec(a.shape, lambda *_: (0,) * a.ndim, **mode)
    return pl.BlockSpec((None,) + a.shape[1:], lambda *_: (layer,) + (0,) * (a.ndim - 1), **mode)


def _cast_specs(w, n_steps):
    rows = w.shape[0] // n_steps
    assert rows * n_steps == w.shape[0] and rows % (2 * SUBLANES) == 0
    any_spec = pl.BlockSpec(memory_space=pl.ANY)
    scratch = [pltpu.VMEM((2, rows, w.shape[1]), F32), pltpu.VMEM((2, rows, w.shape[1]), BF16),
               pltpu.SemaphoreType.DMA((2,)), pltpu.SemaphoreType.DMA((2,))]
    return any_spec, any_spec, jax.ShapeDtypeStruct(w.shape, BF16), scratch


def _cast_rows(step, n_steps, src_hbm, dst_hbm, in_buf, out_buf, in_sem, out_sem):
    rows = in_buf.shape[1]
    slot = step % 2

    def in_copy(s, sl):
        return pltpu.make_async_copy(src_hbm.at[pl.ds(pl.multiple_of(s * rows, rows), rows), :],
                                     in_buf.at[sl], in_sem.at[sl])

    def out_copy(s, sl):
        return pltpu.make_async_copy(out_buf.at[sl],
                                     dst_hbm.at[pl.ds(pl.multiple_of(s * rows, rows), rows), :], out_sem.at[sl])

    pl.when(step == 0)(in_copy(0, 0).start)
    pl.when(step + 1 < n_steps)(in_copy(step + 1, 1 - slot).start)
    in_copy(step, slot).wait()
    pl.when(step >= 2)(out_copy(step - 2, slot).wait)
    out_buf[slot] = in_buf[slot].astype(BF16)
    out_copy(step, slot).start()

    @pl.when(step == n_steps - 1)
    def _():
        out_copy(step, slot).wait()
        pl.when(step >= 1)(out_copy(step - 1, 1 - slot).wait)


def _mixer_prompt(x, p, tt=MIXER_TIME_CHUNK):
    nb, t, d = x.shape
    small = [p['gmix'], p['win'], p['wsc'], p['wlc'], p['blc'], p['wg'], p['ba'], p['bx'], p['lam'],
             p['gco'], p['glo'], p['wout']]
    rows = nb * tt
    hist = MAX_CONV_HISTORY * nb
    conv_state = (nb, _param_array(p['wsc']).shape[1] - 1, D_CONV)
    lru_state = (nb, _param_array(p['wlc']).shape[1] - 1, D_LRU)
    return pl.pallas_call(
        _mixer_prompt_kernel,
        grid=(t // tt,),
        in_specs=[pl.BlockSpec(memory_space=pl.ANY)] + [_param_spec(a, resident=True) for a in small],
        out_specs=[pl.BlockSpec(memory_space=pl.ANY),
                   _const_spec(conv_state), _const_spec(lru_state), _const_spec((nb, D_LRU))],
        out_shape=[jax.ShapeDtypeStruct((nb, t, d), F32),
                   jax.ShapeDtypeStruct(conv_state, F32),
                   jax.ShapeDtypeStruct(lru_state, F32),
                   jax.ShapeDtypeStruct((nb, D_LRU), F32)],
        scratch_shapes=[pltpu.VMEM((2, tt, nb, d), F32), pltpu.VMEM((2, tt, nb, d), F32),
                        pltpu.SemaphoreType.DMA((2,)), pltpu.SemaphoreType.DMA((2,)),
                        pltpu.VMEM((hist + rows, D_CONV), F32), pltpu.VMEM((hist + rows, D_LRU), F32),
                        pltpu.VMEM((nb, D_LRU), F32),
                        pltpu.VMEM((rows, D_LRU), F32), pltpu.VMEM((rows, D_LRU), F32)],
        compiler_params=_cparams("arbitrary"),
        name="mixer_prompt",
    )(x, *map(_param_array, small))


def _mixer_sample_kernel(x_ref, sc_ref, lc_ref, h0_ref,
                         gmix_ref, win_ref, wsc_ref, wlc_ref, blc_ref, wg_ref, ba_ref, bx_ref,
                         lam_ref, gco_ref, glo_ref, wout_ref, gx_ref, wq_ref,
                         xo_ref, q_ref, sc_new_ref, lc_new_ref, hout_ref):
    x = x_ref[...]
    z = _dot(_rms(x, gmix_ref[...]).astype(BF16), win_ref[...])
    gb = z[:, 0:D_CONV]
    p = z[:, D_CONV:2 * D_CONV] * z[:, 2 * D_CONV:3 * D_CONV]
    xb = z[:, 3 * D_CONV:3 * D_CONV + D_LRU]
    gate = z[:, 3 * D_CONV + D_LRU:]

    sc = [sc_ref[:, k, :] for k in range(sc_ref.shape[1])] + [p]
    lc = [lc_ref[:, k, :] for k in range(lc_ref.shape[1])] + [xb]
    ya = gb * sum(v * wsc_ref[k:k + 1, :] for k, v in enumerate(sc))
    xc = sum(v * wlc_ref[k:k + 1, :] for k, v in enumerate(lc)) + blc_ref[...]
    for k, v in enumerate(sc[1:]):
        sc_new_ref[:, k, :] = v
    for k, v in enumerate(lc[1:]):
        lc_new_ref[:, k, :] = v
    a, u = _lru_gates(xc, wg_ref, ba_ref[...], bx_ref[...], lam_ref[...])
    h = a * h0_ref[...] + u
    yb = h * jax.nn.gelu(gate)
    x1 = _mix_out(x, ya, yb, gco_ref[...], glo_ref[...], wout_ref)
    xo_ref[...] = x1
    q = _dot(_rms(x1, gx_ref[...]).astype(BF16), wq_ref[...])
    for r in range(HEAD_ROWS):
        col0 = (r % N_MEM_HEADS) * MEM_HEAD_DIM + (r // N_MEM_HEADS) * LANES
        q_ref[:, r, :] = q[:, col0:col0 + LANES]
    hout_ref[...] = h


def _mixer_sample(x, sc, lc, h0, p):
    nb, d = x.shape
    sc_shape, lc_shape = _param_array(sc).shape[-3:], _param_array(lc).shape[-3:]
    args = [x, sc, lc, h0,
            p['gmix'], p['win'], p['wsc'], p['wlc'], p['blc'], p['wg'], p['ba'], p['bx'], p['lam'],
            p['gco'], p['glo'], p['wout'], p['gx'], p['wq']]
    return pl.pallas_call(
        _mixer_sample_kernel,
        grid=(1,),
        in_specs=[_param_spec(a) for a in args],
        out_specs=[_const_spec((nb, d)), _const_spec((nb, HEAD_ROWS, LANES)), _const_spec(sc_shape),
                   _const_spec(lc_shape), _const_spec((nb, D_LRU))],
        out_shape=[jax.ShapeDtypeStruct((nb, d), F32), jax.ShapeDtypeStruct((nb, HEAD_ROWS, LANES), F32),
                   jax.ShapeDtypeStruct(sc_shape, F32), jax.ShapeDtypeStruct(lc_shape, F32),
                   jax.ShapeDtypeStruct((nb, D_LRU), F32)],
        compiler_params=_cparams("arbitrary"),
        name="mixer_sample",
    )(*map(_param_array, args))


def _xattn_prompt_kernel(x_ref, g_ref, wq_ref, k_ref, v_ref, wo_ref, *rest, n_cast):
    o_ref = rest[n_cast]
    step = pl.program_id(0) * pl.num_programs(1) + pl.program_id(1)
    for c in range(n_cast):
        _cast_rows(step, pl.num_programs(0) * pl.num_programs(1), rest[c], rest[n_cast + 1 + c],
                   *rest[2 * n_cast + 1 + 4 * c:2 * n_cast + 5 + 4 * c])
    x = x_ref[...]
    q = _dot(_rms(x, g_ref[...]).astype(BF16), wq_ref[...])
    heads = []
    for h in range(N_MEM_HEADS):
        sl = slice(h * MEM_HEAD_DIM, (h + 1) * MEM_HEAD_DIM)
        kh = k_ref[:, sl]
        vh = v_ref[:, sl]
        s = lax.dot_general(q[:, sl].astype(BF16), kh, (((1,), (1,)), ((), ())),
                            preferred_element_type=F32) * (MEM_HEAD_DIM ** -0.5)
        e = jnp.exp(s - jnp.max(s, axis=-1, keepdims=True))
        pr = e / jnp.sum(e, axis=-1, keepdims=True)
        heads.append(_dot(pr.astype(BF16), vh))
    o = jnp.concatenate(heads, axis=-1).astype(BF16)
    o_ref[...] = x + _dot(o, wo_ref[...])


def _xattn_prompt(x, g, wq, k, v, wo, layer, tq=XATTN_ROWS, casts=()):
    nb, t, d = x.shape
    n_mem = k.shape[2]
    kv_spec = pl.BlockSpec((None, None, n_mem, d), lambda b, i: (layer, b, 0, 0))
    args = [x, _param_array(g), _param_array(wq), k, v, _param_array(wo)]
    in_specs = [pl.BlockSpec((None, tq, d), lambda b, i: (b, i, 0)), _param_spec(g),
                _param_spec(wq, resident=True), kv_spec, kv_spec, _param_spec(wo, resident=True)]
    out_specs = [pl.BlockSpec((None, tq, d), lambda b, i: (b, i, 0))]
    out_shape = [jax.ShapeDtypeStruct((nb, t, d), F32)]
    scratch = []
    for w in casts:
        in_spec, out_spec, shape, job_scratch = _cast_specs(w, nb * (t // tq))
        args.append(w)
        in_specs.append(in_spec)
        out_specs.append(out_spec)
        out_shape.append(shape)
        scratch += job_scratch
    out = pl.pallas_call(
        functools.partial(_xattn_prompt_kernel, n_cast=len(casts)),
        grid=(nb, t // tq),
        in_specs=in_specs,
        out_specs=out_specs,
        out_shape=out_shape,
        scratch_shapes=scratch,
        compiler_params=_cparams("arbitrary", "arbitrary"),
        name="xattn_prompt",
    )(*args)
    return out[0], out[1:]


def _kv_proj_kernel(mem_ref, g_ref, wk_ref, wv_ref, k5_ref, v5_ref, kb_ref, vb_ref):
    bb, n_mem, d = mem_ref.shape
    m = _rms(mem_ref[...].reshape(bb * n_mem, d), g_ref[...]).astype(BF16)
    for w_ref, o5_ref, ob_ref in ((wk_ref, k5_ref, kb_ref), (wv_ref, v5_ref, vb_ref)):
        y = _dot(m, w_ref[...]).reshape(bb, n_mem, d)
        ob_ref[...] = y.astype(BF16)
        for h in range(N_MEM_HEADS):
            o5_ref[:, :, h, :] = y[:, :, h * MEM_HEAD_DIM:(h + 1) * MEM_HEAD_DIM]


def _kv_proj(mem, gmem, wk, wv, bb=KV_PROJ_SEQS):
    nb, n_mem, d = mem.shape
    depth = wk.shape[0]
    w_spec = pl.BlockSpec((None, d, d), lambda l, i: (l, 0, 0))
    o5_spec = pl.BlockSpec((None, bb, n_mem, N_MEM_HEADS, MEM_HEAD_DIM), lambda l, i: (l, i, 0, 0, 0))
    ob_spec = pl.BlockSpec((None, bb, n_mem, d), lambda l, i: (l, i, 0, 0))
    o5_shape = jax.ShapeDtypeStruct((depth, nb, n_mem, N_MEM_HEADS, MEM_HEAD_DIM), F32)
    ob_shape = jax.ShapeDtypeStruct((depth, nb, n_mem, d), BF16)
    return pl.pallas_call(
        _kv_proj_kernel,
        grid=(depth, nb // bb),
        in_specs=[pl.BlockSpec((bb, n_mem, d), lambda l, i: (i, 0, 0)),
                  pl.BlockSpec((None, 1, d), lambda l, i: (l, 0, 0)), w_spec, w_spec],
        out_specs=[o5_spec, o5_spec, ob_spec, ob_spec],
        out_shape=[o5_shape, o5_shape, ob_shape, ob_shape],
        compiler_params=_cparams("parallel", "parallel"),
        name="kv_proj",
    )(mem, gmem, wk, wv)


LANE_TILES_PER_HEAD = MEM_HEAD_DIM // LANES
HEAD_ROWS = N_MEM_HEADS * LANE_TILES_PER_HEAD


def _to_head_rows(a):
    lead = a.shape[:-1]
    n = len(lead)
    a = a.reshape(*lead, N_MEM_HEADS, LANE_TILES_PER_HEAD, LANES)
    return jnp.swapaxes(a, n, n + 1).reshape(*lead, HEAD_ROWS, LANES)


def _xattn_sample_kernel(q_ref, k_ref, v_ref, o_ref):
    n_mem = k_ref.shape[1]
    ones = jnp.ones((LANES, LANES), BF16)
    for b in range(q_ref.shape[0]):
        prod = (k_ref[b] * q_ref[b][None]).reshape(n_mem * HEAD_ROWS, LANES).astype(BF16)
        part = _dot(prod, ones).reshape(n_mem, HEAD_ROWS, LANES)
        s = (part + pltpu.roll(part, N_MEM_HEADS, axis=1)) * (MEM_HEAD_DIM ** -0.5)
        e = jnp.exp(s - jnp.max(s, axis=0, keepdims=True))
        pr = e / jnp.sum(e, axis=0, keepdims=True)
        o_ref[b] = jnp.sum(pr * v_ref[b], axis=0)


def _xattn_sample(q, k8, v8, layer, bb=SAMPLE_ATTN_SEQS):
    nb = q.shape[0]
    n_mem = k8.shape[2]
    kv_spec = pl.BlockSpec((None, bb, n_mem, HEAD_ROWS, LANES), lambda i: (layer, i, 0, 0, 0))
    out = pl.pallas_call(
        _xattn_sample_kernel,
        grid=(nb // bb,),
        in_specs=[pl.BlockSpec((bb, HEAD_ROWS, LANES), lambda i: (i, 0, 0)), kv_spec, kv_spec],
        out_specs=pl.BlockSpec((bb, HEAD_ROWS, LANES), lambda i: (i, 0, 0)),
        out_shape=jax.ShapeDtypeStruct((nb, HEAD_ROWS, LANES), F32),
        compiler_params=_cparams("parallel"),
        name="xattn_sample",
    )(q, k8, v8)
    return out


def _split_bf16(a):
    hi = a.astype(BF16)
    return hi, (a - hi.astype(F32)).astype(BF16)


def _top2(logits):
    n = logits.shape[-1]
    lane = lax.broadcasted_iota(jnp.int32, logits.shape, 1)
    m1 = jnp.max(logits, axis=-1, keepdims=True)
    i1 = jnp.min(jnp.where(logits == m1, lane, n), axis=-1, keepdims=True)
    rest = jnp.where(lane == i1, -jnp.inf, logits)
    m2 = jnp.max(rest, axis=-1, keepdims=True)
    i2 = jnp.min(jnp.where(rest == m2, lane, n), axis=-1, keepdims=True)
    e2 = jnp.exp(m2 - m1)
    denom = 1.0 + e2
    return i1, i2, 1.0 / denom, e2 / denom


def _add_attn_proj(x, rest, proj):
    if not proj:
        return x, rest
    a_ref, wo_ref = rest[:2]
    heads = [a_ref[:, dt * N_MEM_HEADS + h, :] for h in range(N_MEM_HEADS) for dt in range(LANE_TILES_PER_HEAD)]
    a = jnp.concatenate(heads, axis=-1).astype(BF16)
    return x + _dot(a, wo_ref[...]), rest[2:]


def _ffn_dense_kernel(x_ref, g_ref, *rest, final, proj, cast):
    x, rest = _add_attn_proj(x_ref[...], rest, proj)
    if final:
        gfin_ref, rest = rest[0], rest[1:]
    if cast:
        wg_ref, wu_ref, wd_ref, src_hbm, o_ref, dst_hbm, *cast_scratch = rest
        _cast_rows(pl.program_id(0), pl.num_programs(0), src_hbm, dst_hbm, *cast_scratch)
    else:
        wg_ref, wu_ref, wd_ref, o_ref = rest
    h = _rms(x, g_ref[...]).astype(BF16)
    act = (jax.nn.silu(_dot(h, wg_ref[...])) * _dot(h, wu_ref[...])).astype(BF16)
    out = x + _dot(act, wd_ref[...])
    o_ref[...] = _rms(out, gfin_ref[...]) if final else out


def _ffn_dense(x, g, wg, wu, wd, tm, g_final=None, attn=None, cast=None):
    r, d = x.shape
    final, proj = g_final is not None, attn is not None
    args, specs = [x, g], [pl.BlockSpec((tm, d), lambda i: (i, 0)), _param_spec(g)]
    if proj:
        args += list(attn)
        specs += [pl.BlockSpec((tm,) + attn[0].shape[1:], lambda i: (i, 0, 0)), _param_spec(attn[1], resident=True)]
    if final:
        args.append(g_final)
        specs.append(_param_spec(g_final))
    args += [wg, wu, wd]
    specs += [_param_spec(w, resident=True) for w in (wg, wu, wd)]
    out_specs = [pl.BlockSpec((tm, d), lambda i: (i, 0))]
    out_shape = [jax.ShapeDtypeStruct((r, d), F32)]
    scratch = []
    if cast is not None:
        in_spec, out_spec, shape, scratch = _cast_specs(cast, r // tm)
        args.append(cast)
        specs.append(in_spec)
        out_specs.append(out_spec)
        out_shape.append(shape)
    out = pl.pallas_call(
        functools.partial(_ffn_dense_kernel, final=final, proj=proj, cast=cast is not None),
        grid=(r // tm,),
        in_specs=specs,
        out_specs=out_specs,
        out_shape=out_shape,
        scratch_shapes=scratch,
        compiler_params=_cparams("arbitrary"),
        name="ffn_dense",
    )(*map(_param_array, args))
    return out if cast is not None else out[0]


def _moe_all_experts_kernel(x_ref, g_ref, wr_ref, *rest, final, proj):
    if proj:
        attn_refs, rest = rest[:2], rest[2:]
    if final:
        gfin_ref, rest = rest[0], rest[1:]
    wg_ref, wu_ref, wd_ref, o_ref, h_ref, acc_ref, i1_ref, i2_ref, g1_ref, g2_ref = rest
    e = pl.program_id(1)

    @pl.when(e == 0)
    def _():
        x = x_ref[...]
        if proj:
            x, _ = _add_attn_proj(x, attn_refs, proj)
        h = _rms(x, g_ref[...])
        h_ref[...] = h.astype(BF16)
        acc_ref[...] = x
        h_hi, h_lo = _split_bf16(h)
        w_hi, w_lo = _split_bf16(wr_ref[...])
        logits = _dot(h_hi, w_hi) + (_dot(h_hi, w_lo) + _dot(h_lo, w_hi))
        lane = lax.broadcasted_iota(jnp.int32, logits.shape, 1)
        logits = jnp.where(lane < N_EXPERTS, logits, -jnp.inf)
        i1_ref[...], i2_ref[...], g1_ref[...], g2_ref[...] = _top2(logits)

    gt, up = _gate_up(h_ref[...], wg_ref, wu_ref)
    y = _dot((jax.nn.silu(gt) * up).astype(BF16), wd_ref[...])
    c = jnp.where(i1_ref[...] == e, g1_ref[...], 0.0) + jnp.where(i2_ref[...] == e, g2_ref[...], 0.0)
    acc_ref[...] += c * y

    @pl.when(e == pl.num_programs(1) - 1)
    def _():
        out = acc_ref[...]
        o_ref[...] = _rms(out, gfin_ref[...]) if final else out


def _moe_all_experts(x, g, w_router, wg, wu, wd, tm, g_final=None, attn=None):
    r, d = x.shape
    n_e, f, _ = wd.shape
    final, proj = g_final is not None, attn is not None
    args = [x, g, w_router]
    specs = [pl.BlockSpec((tm, d), lambda i, e: (i, 0)), _param_spec(g), _param_spec(w_router)]
    if proj:
        args += list(attn)
        specs += [pl.BlockSpec((tm,) + attn[0].shape[1:], lambda i, e: (i, 0, 0)), _param_spec(attn[1])]
    if final:
        args.append(g_final)
        specs.append(_param_spec(g_final))
    args += [wg, wu, wd]
    specs += [pl.BlockSpec((None, d, f), lambda i, e: (e, 0, 0)),
              pl.BlockSpec((None, d, f), lambda i, e: (e, 0, 0)),
              pl.BlockSpec((None, f, d), lambda i, e: (e, 0, 0))]
    return pl.pallas_call(
        functools.partial(_moe_all_experts_kernel, final=final, proj=proj),
        grid=(r // tm, n_e),
        in_specs=specs,
        out_specs=pl.BlockSpec((tm, d), lambda i, e: (i, 0)),
        out_shape=jax.ShapeDtypeStruct((r, d), F32),
        scratch_shapes=[pltpu.VMEM((tm, d), BF16), pltpu.VMEM((tm, d), F32),
                        pltpu.VMEM((tm, 1), jnp.int32), pltpu.VMEM((tm, 1), jnp.int32),
                        pltpu.VMEM((tm, 1), F32), pltpu.VMEM((tm, 1), F32)],
        compiler_params=_cparams("parallel", "arbitrary"),
        name="moe_all_experts",
    )(*map(_param_array, args))


MOE_TILE = 512
TOP_K = 2
ROUTE_TILES_PER_STEP = 4
SCATTER_TILES_PER_STEP = 2
ROW_ALIGN = SUBLANES
COMPACT_ROWS = -(-(TOP_K * MOE_TILE + N_EXPERTS * (ROW_ALIGN - 1)) // (2 * SUBLANES)) * (2 * SUBLANES)
SEGMENT_BITS = (MOE_TILE // ROW_ALIGN).bit_length()


def _route_kernel(x_ref, g_ref, wrt_ref, earlier_ref, lp_ref, col_ref, cnt_ref):
    n_e = wrt_ref.shape[0]
    w_hi, w_lo = _split_bf16(wrt_ref[...])
    nt = (((1,), (1,)), ((), ()))
    dg = lambda a, b: lax.dot_general(a, b, nt, preferred_element_type=F32)
    for k in range(lp_ref.shape[0]):
        tile = slice(k * MOE_TILE, (k + 1) * MOE_TILE)
        h = _rms(x_ref[tile, :], g_ref[...])
        h_hi, h_lo = _split_bf16(h)
        logits = dg(w_hi, h_hi) + (dg(w_hi, h_lo) + dg(w_lo, h_hi))
        sub = lax.broadcasted_iota(jnp.int32, logits.shape, 0)
        m1 = jnp.max(logits, axis=0, keepdims=True)
        i1 = jnp.min(jnp.where(logits == m1, sub, n_e), axis=0, keepdims=True)
        rest = jnp.where(sub == i1, -jnp.inf, logits)
        m2 = jnp.max(rest, axis=0, keepdims=True)
        i2 = jnp.min(jnp.where(rest == m2, sub, n_e), axis=0, keepdims=True)
        e2 = jnp.exp(m2 - m1)
        denom = 1.0 + e2

        pick1 = jnp.where(sub == i1, 1.0, 0.0)
        pick2 = jnp.where(sub == i2, 1.0, 0.0)
        cnt = pick1 + pick2
        rank = _dot(cnt.astype(BF16), earlier_ref[...])
        n = jnp.sum(cnt, axis=1, keepdims=True)
        n_pad = jnp.floor((n + (ROW_ALIGN - 1)) * (1.0 / ROW_ALIGN)) * ROW_ALIGN
        base, acc = [], jnp.zeros((1, 1), F32)
        for e in range(n_e):
            base.append(acc)
            acc = acc + n_pad[e:e + 1, :]
        pos = rank + jnp.concatenate(base, axis=0)
        lp1 = jnp.sum(pick1 * pos, axis=0, keepdims=True)
        lp2 = jnp.sum(pick2 * pos, axis=0, keepdims=True)
        lp_ref[k, 0:1, :] = lp1.astype(jnp.int32)
        lp_ref[k, 1:2, :] = lp2.astype(jnp.int32)
        info = jnp.concatenate([lp1, lp2, 1.0 / denom, e2 / denom,
                                jnp.zeros((SUBLANES - 4, MOE_TILE), F32)], axis=0)
        col_ref[tile, :] = info.T
        cnt_ref[k] = jnp.broadcast_to(n, cnt_ref.shape[1:])


def _route(x, g, wrt):
    t, d = x.shape
    n_e = wrt.shape[0]
    n_tiles = t // MOE_TILE
    idx = jnp.arange(MOE_TILE)
    earlier = (idx[:, None] < idx[None, :]).astype(BF16)
    return pl.pallas_call(
        _route_kernel,
        grid=(n_tiles // ROUTE_TILES_PER_STEP,),
        in_specs=[pl.BlockSpec((ROUTE_TILES_PER_STEP * MOE_TILE, d), lambda j: (j, 0)), _param_spec(g),
                  _const_spec(wrt.shape), _const_spec(earlier.shape)],
        out_specs=[pl.BlockSpec((ROUTE_TILES_PER_STEP, TOP_K, MOE_TILE), lambda j: (j, 0, 0)),
                   pl.BlockSpec((ROUTE_TILES_PER_STEP * MOE_TILE, SUBLANES), lambda j: (j, 0)),
                   pl.BlockSpec((ROUTE_TILES_PER_STEP, n_e, LANES), lambda j: (j, 0, 0))],
        out_shape=[jax.ShapeDtypeStruct((n_tiles, TOP_K, MOE_TILE), jnp.int32),
                   jax.ShapeDtypeStruct((t, SUBLANES), F32),
                   jax.ShapeDtypeStruct((n_tiles, n_e, LANES), F32)],
        compiler_params=_cparams("parallel"),
        name="moe_route",
    )(x, _param_array(g), wrt, earlier)


def _segment_copies(src_ref, src_row, dst_ref, dst_row, groups, sem):
    out = []
    for bit in range(SEGMENT_BITS):
        rows = ROW_ALIGN << bit
        before = ((groups >> (bit + 1)) << (bit + 1)) * ROW_ALIGN
        cp = pltpu.make_async_copy(
            src_ref.at[pl.ds(pl.multiple_of(src_row + before, ROW_ALIGN), rows), :],
            dst_ref.at[pl.ds(pl.multiple_of(dst_row + before, ROW_ALIGN), rows), :], sem)
        out.append((((groups >> bit) & 1) == 1, cp))
    return out


def _start_all(copies):
    for pred, cp in copies:
        pl.when(pred)(cp.start)


def _wait_all(copies):
    for pred, cp in copies:
        pl.when(pred)(cp.wait)


def _dispatch_kernel(grp_ref, off_ref, dst_ref, pad_ref, x_ref, g_ref, lp_ref, xs_ref, comp_ref, sem):
    j = pl.program_id(0)
    n_steps = pl.num_programs(0)
    n_e = grp_ref.shape[1]
    per_step = lp_ref.shape[0]
    slot = j % 2

    def copies(step, s):
        out = []
        for k in range(per_step):
            tile = step * per_step + k
            for e in range(n_e):
                out += _segment_copies(comp_ref.at[s * per_step + k], off_ref[tile, e], xs_ref, dst_ref[tile, e],
                                       grp_ref[tile, e], sem.at[s])
        return out

    compact = []
    for k in range(per_step):
        h = _rms(x_ref[k * MOE_TILE:(k + 1) * MOE_TILE, :], g_ref[...]).astype(BF16)
        row = lax.broadcasted_iota(jnp.int32, (COMPACT_ROWS, MOE_TILE), 0)
        onehot = jnp.where(row == lp_ref[k, 0:1, :], 1.0,
                           jnp.where(row == lp_ref[k, 1:2, :], 1.0, 0.0)).astype(BF16)
        compact.append(_dot(onehot, h))

    @pl.when(j >= 2)
    def _():
        _wait_all(copies(j - 2, slot))

    for k in range(per_step):
        comp_ref[slot * per_step + k] = compact[k]
    _start_all(copies(j, slot))

    @pl.when(j == n_steps - 1)
    def _():
        _wait_all(copies(j, slot))

        @pl.when(j >= 1)
        def _():
            _wait_all(copies(j - 1, 1 - slot))

        zeros_ref = comp_ref.at[slot * per_step]
        zeros_ref[0:MOE_TILE, :] = jnp.zeros((MOE_TILE, comp_ref.shape[2]), F32)
        tail = []
        for e in range(n_e):
            tail += _segment_copies(zeros_ref, 0, xs_ref, pad_ref[0, e], pad_ref[1, e], sem.at[slot])
        tile_groups = MOE_TILE // ROW_ALIGN
        groups_left = (xs_ref.shape[0] - pad_ref[2, 0]) // ROW_ALIGN
        min_total = TOP_K * MOE_TILE * grp_ref.shape[0]
        for c in range((xs_ref.shape[0] - min_total) // MOE_TILE):
            tail += _segment_copies(zeros_ref, 0, xs_ref, pad_ref[2, 0] + c * MOE_TILE,
                                    jnp.clip(groups_left - c * tile_groups, 0, tile_groups), sem.at[slot])
        _start_all(tail)
        _wait_all(tail)


def _dispatch(x, g, lp, groups, off, dst, pad, n_rows):
    t, d = x.shape
    n_tiles = t // MOE_TILE
    grid_spec = pltpu.PrefetchScalarGridSpec(
        num_scalar_prefetch=4,
        grid=(n_tiles // SCATTER_TILES_PER_STEP,),
        in_specs=[pl.BlockSpec((SCATTER_TILES_PER_STEP * MOE_TILE, d), lambda j, *_: (j, 0)),
                  _param_spec(g),
                  pl.BlockSpec((SCATTER_TILES_PER_STEP, TOP_K, MOE_TILE), lambda j, *_: (j, 0, 0))],
        out_specs=pl.BlockSpec(memory_space=pl.ANY),
        scratch_shapes=[pltpu.VMEM((2 * SCATTER_TILES_PER_STEP, COMPACT_ROWS, d), F32),
                        pltpu.SemaphoreType.DMA((2,))])
    return pl.pallas_call(
        _dispatch_kernel,
        grid_spec=grid_spec,
        out_shape=jax.ShapeDtypeStruct((n_rows, d), F32),
        compiler_params=_cparams("arbitrary"),
        name="moe_dispatch",
    )(groups, off, dst, pad, x, _param_array(g), lp)


STEP_FFN, STEP_ZERO = 1, 2


def _grouped_ffn_kernel(grp_ref, mode_ref, xtile_ref, xs_ref, wg_ref, wu_ref, wd_ref, ys_ref):
    w = pl.program_id(0)

    @pl.when(mode_ref[w] == STEP_ZERO)
    def _():
        ys_ref[...] = jnp.zeros_like(ys_ref)

    @pl.when(mode_ref[w] == STEP_FFN)
    def _():
        gt, up = _gate_up(xs_ref[...].astype(BF16), wg_ref, wu_ref)
        ys_ref[...] = _dot((jax.nn.silu(gt) * up).astype(BF16), wd_ref[...])


def _grouped_ffn(xs, wg, wu, wd, grp, mode, xtile):
    n_e, f, d = wd.shape
    grid_spec = pltpu.PrefetchScalarGridSpec(
        num_scalar_prefetch=3,
        grid=(xs.shape[0] // MOE_TILE,),
        in_specs=[pl.BlockSpec((MOE_TILE, d), lambda w, grp, mode, xtile: (xtile[w], 0)),
                  pl.BlockSpec((None, d, f), lambda w, grp, *_: (grp[w], 0, 0)),
                  pl.BlockSpec((None, d, f), lambda w, grp, *_: (grp[w], 0, 0)),
                  pl.BlockSpec((None, f, d), lambda w, grp, *_: (grp[w], 0, 0))],
        out_specs=pl.BlockSpec((MOE_TILE, d), lambda w, *_: (w, 0)))
    return pl.pallas_call(
        _grouped_ffn_kernel,
        grid_spec=grid_spec,
        out_shape=jax.ShapeDtypeStruct(xs.shape, F32),
        compiler_params=_cparams("arbitrary"),
        name="moe_grouped_ffn",
    )(grp, mode, xtile, xs, wg, wu, wd)


def _combine_kernel(grp_ref, off_ref, dst_ref, x_ref, col_ref, *rest, final):
    if final:
        gfin_ref, rest = rest[0], rest[1:]
    ys_ref, o_ref, comp_ref, sem = rest
    j = pl.program_id(0)
    n_steps = pl.num_programs(0)
    n_e = grp_ref.shape[1]
    per_step = x_ref.shape[0] // MOE_TILE
    slot = j % 2

    def copies(step, s):
        out = []
        for k in range(per_step):
            tile = step * per_step + k
            for e in range(n_e):
                out += _segment_copies(ys_ref, dst_ref[tile, e], comp_ref.at[s * per_step + k], off_ref[tile, e],
                                       grp_ref[tile, e], sem.at[s])
        return out

    @pl.when(j == 0)
    def _():
        comp_ref[...] = jnp.zeros_like(comp_ref)
        _start_all(copies(0, 0))

    @pl.when(j + 1 < n_steps)
    def _():
        _start_all(copies(j + 1, 1 - slot))

    _wait_all(copies(j, slot))
    for t in range(per_step):
        rows = slice(t * MOE_TILE, (t + 1) * MOE_TILE)
        y = comp_ref[slot * per_step + t].astype(BF16)
        col = col_ref[rows, :]
        row = lax.broadcasted_iota(jnp.int32, (MOE_TILE, COMPACT_ROWS), 1)
        weights = jnp.zeros(row.shape, F32)
        for k in range(TOP_K):
            weights = jnp.where(row == col[:, k:k + 1].astype(jnp.int32), col[:, TOP_K + k:TOP_K + k + 1],
                                weights)
        out = x_ref[rows, :] + _dot(weights.astype(BF16), y)
        o_ref[rows, :] = _rms(out, gfin_ref[...]) if final else out


def _combine(x, col, ys, groups, off, dst, g_final):
    t, d = x.shape
    n_tiles = t // MOE_TILE
    final = g_final is not None
    rows = SCATTER_TILES_PER_STEP * MOE_TILE
    args = [x, col]
    specs = [pl.BlockSpec((rows, d), lambda j, *_: (j, 0)), pl.BlockSpec((rows, SUBLANES), lambda j, *_: (j, 0))]
    if final:
        args.append(g_final)
        specs.append(_param_spec(g_final))
    args.append(ys)
    specs.append(pl.BlockSpec(memory_space=pl.ANY))
    grid_spec = pltpu.PrefetchScalarGridSpec(
        num_scalar_prefetch=3,
        grid=(n_tiles // SCATTER_TILES_PER_STEP,),
        in_specs=specs,
        out_specs=pl.BlockSpec((rows, d), lambda j, *_: (j, 0)),
        scratch_shapes=[pltpu.VMEM((2 * SCATTER_TILES_PER_STEP, COMPACT_ROWS, d), F32),
                        pltpu.SemaphoreType.DMA((2,))])
    return pl.pallas_call(
        functools.partial(_combine_kernel, final=final),
        grid_spec=grid_spec,
        out_shape=jax.ShapeDtypeStruct((t, d), F32),
        compiler_params=_cparams("arbitrary"),
        name="moe_combine",
    )(groups, off, dst, *map(_param_array, args))


def _moe_routed(x, g, w_router, wg, wu, wd, g_final):
    t, _ = x.shape
    n_e = wd.shape[0]
    n_tiles = t // MOE_TILE
    lp, col, cnt = _route(x, g, w_router.T)
    groups = (cnt[:, :, 0].astype(jnp.int32) + (ROW_ALIGN - 1)) // ROW_ALIGN
    n = groups * ROW_ALIGN
    off = jnp.cumsum(n, axis=1) - n
    per_e = jnp.sum(n, axis=0)
    tiles_e = (per_e + (MOE_TILE - 1)) // MOE_TILE
    tile_end = jnp.cumsum(tiles_e)
    starts = (tile_end - tiles_e) * MOE_TILE
    dst = starts[None, :] + jnp.cumsum(n, axis=0) - n
    pad = jnp.stack([starts + per_e, (tiles_e * MOE_TILE - per_e) // ROW_ALIGN,
                     jnp.broadcast_to(tile_end[-1] * MOE_TILE, per_e.shape)])
    max_rows = TOP_K * t + n_tiles * n_e * (ROW_ALIGN - 1)
    n_row_tiles = -(-max_rows // MOE_TILE) + n_e
    xs = _dispatch(x, g, lp, groups, off, dst, pad, n_row_tiles * MOE_TILE)

    w = jnp.arange(n_row_tiles, dtype=jnp.int32)
    used = w < tile_end[-1]
    grp = jnp.minimum(jnp.sum(w[:, None] >= tile_end[None, :], axis=1), n_e - 1).astype(jnp.int32)
    mode = jnp.where(used, STEP_FFN, STEP_ZERO).astype(jnp.int32)
    xtile = jnp.minimum(w, tile_end[-1] - 1)
    ys = _grouped_ffn(xs, wg, wu, wd, grp, mode, xtile)
    return _combine(x, col, ys, groups, off, dst, g_final)


def _block_diag_gates(w_a, w_x):
    per_group = LRU_GROUP // LRU_HEAD_DIM
    eye = jnp.eye(per_group, dtype=w_a.dtype)

    def bd(w):
        w = w.reshape(D_LRU // LRU_GROUP, per_group, LRU_HEAD_DIM, LRU_HEAD_DIM)
        return jnp.einsum('ghij,hk->ghikj', w, eye).reshape(D_LRU // LRU_GROUP, LRU_GROUP, LRU_GROUP)

    return jnp.concatenate([bd(w_a), bd(w_x)], axis=-1).astype(BF16)


def kernel(x_prompt, x_sample, state_shortconv, state_lru_conv, state_lru_h, cache_mem_k, cache_mem_v,
           mem_prompt, norm_mix, w_in, w_short_conv, w_lru_conv, b_lru_conv, w_lru_a, b_lru_a, w_lru_x,
           b_lru_x, lru_lambda, norm_conv_out, norm_lru_out, w_mix_out, norm_xattn, norm_mem, w_q, w_k,
           w_v, w_o, norm_ffn, w_ff_gate, w_ff_up, w_ff_down, w_router, w_moe_gate, w_moe_up, w_moe_down,
           norm_final):
    depth = w_in.shape[0]
    nb, t, d = x_prompt.shape
    ns = x_sample.shape[0]
    n_mem = mem_prompt.shape[1]
    rows = lambda a: a.reshape(a.shape[0], 1, -1)
    stacks = dict(
        gmix=rows(norm_mix), win=w_in.astype(BF16), wsc=w_short_conv, wlc=w_lru_conv, blc=rows(b_lru_conv),
        wg=jax.vmap(_block_diag_gates)(w_lru_a, w_lru_x), ba=rows(b_lru_a), bx=rows(b_lru_x),
        lam=rows(lru_lambda), gco=rows(norm_conv_out), glo=rows(norm_lru_out), wout=w_mix_out.astype(BF16),
        gx=rows(norm_xattn), wq=w_q.astype(BF16), wo=w_o.astype(BF16), gffn=rows(norm_ffn))
    g_final = norm_final.reshape(1, -1)
    ff_dense_f32 = (w_ff_gate, w_ff_up, w_ff_down)
    ff_moe_f32 = (w_moe_gate, w_moe_up, w_moe_down)

    layers = []
    for l in range(depth):
        lp = {name: (stack, l) for name, stack in stacks.items()}
        lp['gfin'] = g_final if l == depth - 1 else None
        j = l // 2
        if l % 2 == 0:
            lp['ffw'] = None
            lp['wr'] = None
        else:
            lp['ffw'] = None
            lp['wr'] = jnp.pad(w_router[j], ((0, 0), (0, LANES - N_EXPERTS)))
            lp['wr_raw'] = w_router[j]
        layers.append(lp)

    x = x_prompt
    p_mem_k, p_mem_v, kb, vb = _kv_proj(mem_prompt, norm_mem.reshape(depth, 1, d), w_k.astype(BF16),
                                        w_v.astype(BF16))
    pc, plc, ph = [], [], []
    moe_bf16 = {}

    def cast_for(j, k):
        w = ff_moe_f32[k][j]
        return w.reshape(-1, w.shape[-1])

    for l, lp in enumerate(layers):
        x, tail_p, tail_x, h_last = _mixer_prompt(x, lp)
        if lp['wr'] is None:
            casts = [w[l // 2] for w in ff_dense_f32]
            if l + 1 < depth:
                casts.append(cast_for((l + 1) // 2, 0))
            x, done = _xattn_prompt(x, lp['gx'], lp['wq'], kb, vb, lp['wo'], l, casts=casts)
            lp['ffw'] = tuple(done[:3])
            if l + 1 < depth:
                moe_bf16[(l + 1) // 2, 0] = done[3]
        else:
            x, (moe_bf16[l // 2, 1],) = _xattn_prompt(x, lp['gx'], lp['wq'], kb, vb, lp['wo'], l,
                                                      casts=[cast_for(l // 2, 1)])
        if lp['wr'] is None:
            if l + 1 < depth:
                x, moe_bf16[(l + 1) // 2, 2] = _ffn_dense(x.reshape(nb * t, d), lp['gffn'], *lp['ffw'], tm=FFN_ROWS,
                                                          g_final=lp['gfin'], cast=cast_for((l + 1) // 2, 2))
            else:
                x = _ffn_dense(x.reshape(nb * t, d), lp['gffn'], *lp['ffw'], tm=FFN_ROWS, g_final=lp['gfin'])
        else:
            j = l // 2
            lp['ffw'] = tuple(moe_bf16[j, k].reshape(ff_moe_f32[k][j].shape) for k in range(3))
            x = _moe_routed(x.reshape(nb * t, d), lp['gffn'], lp['wr_raw'], *lp['ffw'], lp['gfin'])
        x = x.reshape(nb, t, d)
        pc.append(tail_p)
        plc.append(tail_x)
        ph.append(h_last)
    y_prompt = x

    x = x_sample.reshape(ns, d)
    k8 = _to_head_rows(cache_mem_k.reshape(depth, ns, n_mem, d))
    v8 = _to_head_rows(cache_mem_v.reshape(depth, ns, n_mem, d))
    sc, slc, sh = [], [], []
    for l, lp in enumerate(layers):
        x, q, sc_new, lc_new, h_new = _mixer_sample(x, (state_shortconv, l), (state_lru_conv, l),
                                                    (state_lru_h, l), lp)
        attn = (_xattn_sample(q, k8, v8, l), lp['wo'])
        if lp['wr'] is None:
            x = _ffn_dense(x, lp['gffn'], *lp['ffw'], tm=ns, g_final=lp['gfin'], attn=attn)
        else:
            x = _moe_all_experts(x, lp['gffn'], lp['wr'], *lp['ffw'], tm=ns, g_final=lp['gfin'], attn=attn)
        sc.append(sc_new)
        slc.append(lc_new)
        sh.append(h_new)
    y_sample = x.reshape(ns, 1, d)

    return (y_prompt, y_sample, jnp.stack(pc), jnp.stack(plc), jnp.stack(ph), p_mem_k, p_mem_v,
            jnp.stack(sc), jnp.stack(slc), jnp.stack(sh))
```
